```python
import jax, jax.numpy as jnp
from jax import lax
import numpy as np

D_MODEL = 1024
BATCH = 8
SEQ = 4096
DEPTH = 2

N_A_LAYERS = DEPTH // 2
N_B_LAYERS = DEPTH - N_A_LAYERS
HEAD_DIM = 64
N_HEADS = D_MODEL // HEAD_DIM
ATTN_DIM = N_HEADS * HEAD_DIM
FOX_Q_BLOCK = 128
NSA_KV_GROUPS = 4
NSA_HEADS_PER_GROUP = N_HEADS // NSA_KV_GROUPS
NSA_KV_DIM = NSA_KV_GROUPS * HEAD_DIM
NSA_N_KV = 6
CMP_BLOCK = 32
CMP_STRIDE = 16
CMP_HIDDEN = 128
SLC_BLOCK = 64
SLC_TOPK = 16
WINDOW = 512
NSA_Q_BLOCK = 32
N_EXPERTS = 256
EXPERT_TOPK = 8
EXPERT_GROUPS = 8
EXPERT_TOPK_GROUPS = 4
EXPERT_HIDDEN = 256
ROUTED_SCALE = 2.5
MOE_ROW_BLOCK = 128
PLE_DIM = 256
LN_EPS = 1e-5
DEEPNORM_ALPHA = (2.0 * DEPTH) ** 0.25
DEEPNORM_BETA = (8.0 * DEPTH) ** -0.25
NEG_INF = -1e30
FORCE_SELECT = 1e9

kernel_name = "hybrid_fox_nsa_deepseekmoe_deepnorm"


def layer_norm(x, g, b):
    xf = x.astype(jnp.float32)
    mu = jnp.mean(xf, axis=-1, keepdims=True)
    var = jnp.mean(jnp.square(xf - mu), axis=-1, keepdims=True)
    return ((xf - mu) * lax.rsqrt(var + LN_EPS) * g + b).astype(x.dtype)


def masked_softmax(s, mask):
    s = jnp.where(mask, s, NEG_INF)
    return jnp.where(mask, jax.nn.softmax(s, axis=-1), 0.0)


def alibi_slopes():
    return jnp.asarray(2.0 ** (-8.0 * np.arange(1, N_HEADS + 1) / N_HEADS), jnp.float32)


def compressed_to_selection_weights(n_cmp, n_slc):
    cs = np.arange(n_cmp)[:, None] * CMP_STRIDE
    ce = cs + CMP_BLOCK
    ss = np.arange(n_slc)[None, :] * SLC_BLOCK
    se = ss + SLC_BLOCK
    w = np.clip(np.minimum(ce, se) - np.maximum(cs, ss), 0, None) / CMP_STRIDE
    return jnp.asarray(w, jnp.float32)


def forgetting_attention(x, w_in, b_f, w_out):
    B, S, _ = x.shape
    proj = x @ w_in
    q, k, v = [proj[..., i * ATTN_DIM:(i + 1) * ATTN_DIM].reshape(B, S, N_HEADS, HEAD_DIM)
               for i in range(3)]
    log_f = jax.nn.log_sigmoid((proj[..., 3 * ATTN_DIM:] + b_f).astype(jnp.float32))
    c = jnp.cumsum(log_f, axis=1).transpose(0, 2, 1)
    scale = HEAD_DIM ** -0.5
    k_pos = jnp.arange(S)

    def block(i):
        q0 = i * FOX_Q_BLOCK
        qb = lax.dynamic_slice_in_dim(q, q0, FOX_Q_BLOCK, axis=1)
        cb = lax.dynamic_slice_in_dim(c, q0, FOX_Q_BLOCK, axis=2)
        t = q0 + jnp.arange(FOX_Q_BLOCK)
        s = jnp.einsum('bqhd,bshd->bhqs', qb, k, preferred_element_type=jnp.float32) * scale
        s = s + cb[..., :, None] - c[..., None, :]
        p = masked_softmax(s, k_pos[None, :] <= t[:, None])
        return jnp.einsum('bhqs,bshd->bqhd', p.astype(v.dtype), v)

    o = lax.map(block, jnp.arange(S // FOX_Q_BLOCK))
    o = o.transpose(1, 0, 2, 3, 4).reshape(B, S, ATTN_DIM)
    return o @ w_out


def nsa_shared_kv(x, w_kv, cmp_pe, cmp_w1, cmp_b1, cmp_w2):
    B, S, _ = x.shape
    kv = (x @ w_kv).reshape(B, S, NSA_N_KV, NSA_KV_GROUPS, HEAD_DIM)
    k_cr, v_cr, k_sl, v_sl, k_wn, v_wn = [kv[:, :, i] for i in range(NSA_N_KV)]
    n_cmp = (S - CMP_BLOCK) // CMP_STRIDE + 1
    idx = np.arange(n_cmp)[:, None] * CMP_STRIDE + np.arange(CMP_BLOCK)[None, :]

    def compress(raw, j):
        blk = raw[:, idx] + cmp_pe[j][None, None, :, None, :]
        blk = blk.transpose(0, 1, 3, 2, 4).reshape(B, n_cmp, NSA_KV_GROUPS, CMP_BLOCK * HEAD_DIM)
        h = jax.nn.gelu(blk @ cmp_w1[j] + cmp_b1[j])
        return h @ cmp_w2[j]

    k_cmp = compress(k_cr, 0)
    v_cmp = compress(v_cr, 1)
    n_slc = S // SLC_BLOCK
    k_slc = k_sl.reshape(B, n_slc, SLC_BLOCK, NSA_KV_GROUPS, HEAD_DIM).transpose(0, 3, 1, 2, 4)
    v_slc = v_sl.reshape(B, n_slc, SLC_BLOCK, NSA_KV_GROUPS, HEAD_DIM).transpose(0, 3, 1, 2, 4)
    pad = ((0, 0), (WINDOW, 0), (0, 0), (0, 0))
    k_win = jnp.pad(k_wn, pad)
    v_win = jnp.pad(v_wn, pad)
    return (k_cmp, v_cmp, k_slc, v_slc, k_win, v_win)


def native_sparse_attention(x, shared_kv, w_q, b_g, w_out):
    k_cmp, v_cmp, k_slc, v_slc, k_win, v_win = shared_kv
    B, S, _ = x.shape
    G, R, dh = NSA_KV_GROUPS, NSA_HEADS_PER_GROUP, HEAD_DIM
    dt = x.dtype
    proj = x @ w_q
    q = proj[..., :ATTN_DIM].reshape(B, S, G, R, dh)
    gates = jax.nn.sigmoid((proj[..., ATTN_DIM:] + b_g).astype(jnp.float32)).reshape(B, S, G, R, 3)
    slopes = alibi_slopes().reshape(G, R)
    n_cmp = k_cmp.shape[1]
    n_slc = k_slc.shape[2]
    topk = min(SLC_TOPK, n_slc)
    cmp_end = jnp.arange(n_cmp) * CMP_STRIDE + CMP_BLOCK - 1
    cmp_to_slc = compressed_to_selection_weights(n_cmp, n_slc)
    blk_ids = jnp.arange(n_slc)
    b_ix = jnp.arange(B)[:, None, None, None]
    g_ix = jnp.arange(G)[None, :, None, None]
    scale = dh ** -0.5

    def block(i):
        q0 = i * NSA_Q_BLOCK
        qb = lax.dynamic_slice_in_dim(q, q0, NSA_Q_BLOCK, axis=1)
        gb = lax.dynamic_slice_in_dim(gates, q0, NSA_Q_BLOCK, axis=1)
        t = q0 + jnp.arange(NSA_Q_BLOCK)
        dist_c = (t[:, None] - cmp_end[None, :]).astype(jnp.float32)
        s_c = jnp.einsum('bqgrd,bcgd->bgrqc', qb, k_cmp, preferred_element_type=jnp.float32) * scale
        s_c = s_c - slopes[:, :, None, None] * dist_c
        p_c = masked_softmax(s_c, dist_c >= 0)
        o_c = jnp.einsum('bgrqc,bcgd->bqgrd', p_c.astype(dt), v_cmp)
        imp = jnp.einsum('bgrqc,cj->bgqj', p_c, cmp_to_slc)
        cur = t // SLC_BLOCK
        visible = blk_ids[None, :] * SLC_BLOCK <= t[:, None]
        forced = ((blk_ids[None, :] == 0) | (blk_ids[None, :] == cur[:, None])
                  | (blk_ids[None, :] == cur[:, None] - 1))
        sel_score = jnp.where(forced, FORCE_SELECT, jnp.where(visible, imp, NEG_INF))
        sel_val, sel_idx = lax.top_k(sel_score, topk)
        valid = sel_val > 0.5 * NEG_INF
        k_sel = k_slc[b_ix, g_ix, sel_idx].reshape(B, G, NSA_Q_BLOCK, topk * SLC_BLOCK, dh)
        v_sel = v_slc[b_ix, g_ix, sel_idx].reshape(B, G, NSA_Q_BLOCK, topk * SLC_BLOCK, dh)
        s_pos = (sel_idx[..., None] * SLC_BLOCK + jnp.arange(SLC_BLOCK)).reshape(
            B, G, NSA_Q_BLOCK, topk * SLC_BLOCK)
        dist_s = (t[None, None, :, None] - s_pos).astype(jnp.float32)
        mask_s = jnp.repeat(valid, SLC_BLOCK, axis=-1) & (dist_s >= 0)
        s_s = jnp.einsum('bqgrd,bgqnd->bgrqn', qb, k_sel, preferred_element_type=jnp.float32) * scale
        s_s = s_s - slopes[None, :, :, None, None] * dist_s[:, :, None]
        p_s = masked_softmax(s_s, mask_s[:, :, None])
        o_s = jnp.einsum('bgrqn,bgqnd->bqgrd', p_s.astype(dt), v_sel)
        kw = lax.dynamic_slice_in_dim(k_win, q0, NSA_Q_BLOCK + WINDOW, axis=1)
        vw = lax.dynamic_slice_in_dim(v_win, q0, NSA_Q_BLOCK + WINDOW, axis=1)
        w_pos = q0 - WINDOW + jnp.arange(NSA_Q_BLOCK + WINDOW)
        dist_w = t[:, None] - w_pos[None, :]
        mask_w = (w_pos[None, :] >= 0) & (dist_w >= 0) & (dist_w < WINDOW)
        s_w = jnp.einsum('bqgrd,bsgd->bgrqs', qb, kw, preferred_element_type=jnp.float32) * scale
        s_w = s_w - slopes[:, :, None, None] * dist_w.astype(jnp.float32)
        p_w = masked_softmax(s_w, mask_w)
        o_w = jnp.einsum('bgrqs,bsgd->bqgrd', p_w.astype(dt), vw)
        o = gb[..., 0:1] * o_c + gb[..., 1:2] * o_s + gb[..., 2:3] * o_w
        return o.reshape(B, NSA_Q_BLOCK, ATTN_DIM).astype(dt)

    o = lax.map(block, jnp.arange(S // NSA_Q_BLOCK))
    o = o.transpose(1, 0, 2, 3).reshape(B, S, ATTN_DIM)
    return o @ w_out


def swiglu(x, w_gate, w_up, w_down):
    return (jax.nn.silu(x @ w_gate) * (x @ w_up)) @ w_down


def routed_experts(xt, idx, w, w_gate, w_up, w_down):
    N, D = xt.shape
    A = N * EXPERT_TOPK
    e_flat = idx.reshape(A)
    tok_flat = jnp.repeat(jnp.arange(N, dtype=jnp.int32), EXPERT_TOPK)
    w_flat = w.reshape(A)
    order = jnp.argsort(e_flat)
    e_sorted = e_flat[order]
    counts = jnp.bincount(e_flat, length=N_EXPERTS)
    padded = (counts + MOE_ROW_BLOCK - 1) // MOE_ROW_BLOCK * MOE_ROW_BLOCK
    start = jnp.cumsum(counts) - counts
    pad_end = jnp.cumsum(padded)
    pad_start = pad_end - padded
    dest = pad_start[e_sorted] + jnp.arange(A) - start[e_sorted]
    n_blocks = -(-(A + N_EXPERTS * (MOE_ROW_BLOCK - 1)) // MOE_ROW_BLOCK)
    n_rows = n_blocks * MOE_ROW_BLOCK
    tok_pad = jnp.zeros((n_rows,), jnp.int32).at[dest].set(tok_flat[order])
    w_pad = jnp.zeros((n_rows,), jnp.float32).at[dest].set(w_flat[order])
    blk_expert = jnp.minimum(
        jnp.searchsorted(pad_end, jnp.arange(n_blocks) * MOE_ROW_BLOCK, side='right'), N_EXPERTS - 1)

    def step(acc, b):
        toks = lax.dynamic_slice_in_dim(tok_pad, b * MOE_ROW_BLOCK, MOE_ROW_BLOCK)
        wb = lax.dynamic_slice_in_dim(w_pad, b * MOE_ROW_BLOCK, MOE_ROW_BLOCK)
        e = blk_expert[b]
        y = swiglu(xt[toks], w_gate[e], w_up[e], w_down[e]).astype(jnp.float32) * wb[:, None]
        return acc.at[toks].add(y), None

    acc, _ = lax.scan(step, jnp.zeros((N, D), jnp.float32), jnp.arange(n_blocks))
    return acc.astype(xt.dtype)


def moe_ffn(x, w_router, b_router, w_gate, w_up, w_down, sh_gate, sh_up, sh_down):
    B, S, D = x.shape
    xt = x.reshape(B * S, D)
    N = xt.shape[0]
    scores = jax.nn.sigmoid((xt @ w_router).astype(jnp.float32))
    biased = scores + b_router.astype(jnp.float32)
    grp = biased.reshape(N, EXPERT_GROUPS, N_EXPERTS // EXPERT_GROUPS)
    grp_score = jnp.sum(lax.top_k(grp, 2)[0], axis=-1)
    _, grp_idx = lax.top_k(grp_score, EXPERT_TOPK_GROUPS)
    grp_mask = jnp.sum(jax.nn.one_hot(grp_idx, EXPERT_GROUPS), axis=-2) > 0
    cand = jnp.where(grp_mask[:, :, None], grp, NEG_INF).reshape(N, N_EXPERTS)
    _, idx = lax.top_k(cand, EXPERT_TOPK)
    s_sel = jnp.take_along_axis(scores, idx, axis=-1)
    w = s_sel / jnp.sum(s_sel, axis=-1, keepdims=True) * ROUTED_SCALE
    routed = routed_experts(xt, idx, w, w_gate, w_up, w_down)
    shared = swiglu(xt, sh_gate, sh_up, sh_down)
    return (routed + shared).reshape(B, S, D)


def per_layer_embedding(x, p_i, w_proj, w_gate, b_gate):
    return jax.nn.sigmoid(x @ w_gate + b_gate) * (p_i @ w_proj)


def setup_inputs(seed: int = 0) -> dict:
    key = jax.random.key(seed)
    ks = jax.random.split(key, 26)

    def nrm(k, shape, scale):
        return jax.random.normal(k, shape, jnp.float32) * scale

    fox_cols = jnp.concatenate([jnp.ones((2 * ATTN_DIM,), jnp.float32),
                                jnp.full((ATTN_DIM,), DEEPNORM_BETA, jnp.float32),
                                jnp.ones((N_HEADS,), jnp.float32)])
    kv_cols = jnp.repeat(jnp.asarray([1.0, DEEPNORM_BETA] * 3, jnp.float32), NSA_KV_DIM)
    return {
        'x': nrm(ks[0], (BATCH, SEQ, D_MODEL), 1.0),
        'p': nrm(ks[1], (DEPTH, BATCH, SEQ, PLE_DIM), 1.0),
        'fox_w_in': nrm(ks[2], (N_A_LAYERS, D_MODEL, 3 * ATTN_DIM + N_HEADS), D_MODEL ** -0.5) * fox_cols,
        'fox_b_f': 2.0 + nrm(ks[3], (N_A_LAYERS, N_HEADS), 0.5),
        'fox_w_out': nrm(ks[4], (N_A_LAYERS, ATTN_DIM, D_MODEL), ATTN_DIM ** -0.5 * DEEPNORM_BETA),
        'nsa_w_kv': nrm(ks[5], (D_MODEL, NSA_N_KV * NSA_KV_DIM), D_MODEL ** -0.5) * kv_cols,
        'cmp_pe': nrm(ks[6], (2, CMP_BLOCK, HEAD_DIM), 0.5),
        'cmp_w1': nrm(ks[7], (2, CMP_BLOCK * HEAD_DIM, CMP_HIDDEN), (CMP_BLOCK * HEAD_DIM) ** -0.5),
        'cmp_b1': nrm(ks[8], (2, CMP_HIDDEN), 0.01),
        'cmp_w2': nrm(ks[9], (2, CMP_HIDDEN, HEAD_DIM), CMP_HIDDEN ** -0.5),
        'nsa_w_q': nrm(ks[10], (N_B_LAYERS, D_MODEL, ATTN_DIM + 3 * N_HEADS), D_MODEL ** -0.5),
        'nsa_b_g': nrm(ks[11], (N_B_LAYERS, 3 * N_HEADS), 0.1),
        'nsa_w_out': nrm(ks[12], (N_B_LAYERS, ATTN_DIM, D_MODEL), ATTN_DIM ** -0.5 * DEEPNORM_BETA),
        'ln_g': 1.0 + nrm(ks[13], (DEPTH, 3, D_MODEL), 0.05),
        'ln_b': nrm(ks[14], (DEPTH, 3, D_MODEL), 0.02),
        'moe_w_router': nrm(ks[15], (DEPTH, D_MODEL, N_EXPERTS), D_MODEL ** -0.5),
        'moe_b_router': nrm(ks[16], (DEPTH, N_EXPERTS), 0.01),
        'moe_w_gate': nrm(ks[17], (DEPTH, N_EXPERTS, D_MODEL, EXPERT_HIDDEN), D_MODEL ** -0.5),
        'moe_w_up': nrm(ks[18], (DEPTH, N_EXPERTS, D_MODEL, EXPERT_HIDDEN), D_MODEL ** -0.5),
        'moe_w_down': nrm(ks[19], (DEPTH, N_EXPERTS, EXPERT_HIDDEN, D_MODEL), EXPERT_HIDDEN ** -0.5 * DEEPNORM_BETA),
        'shared_w_gate': nrm(ks[20], (DEPTH, D_MODEL, EXPERT_HIDDEN), D_MODEL ** -0.5),
        'shared_w_up': nrm(ks[21], (DEPTH, D_MODEL, EXPERT_HIDDEN), D_MODEL ** -0.5),
        'shared_w_down': nrm(ks[22], (DEPTH, EXPERT_HIDDEN, D_MODEL), EXPERT_HIDDEN ** -0.5 * DEEPNORM_BETA),
        'ple_w_proj': nrm(ks[23], (DEPTH, PLE_DIM, D_MODEL), PLE_DIM ** -0.5 * DEEPNORM_BETA),
        'ple_w_gate': nrm(ks[24], (DEPTH, D_MODEL, D_MODEL), D_MODEL ** -0.5),
        'ple_b_gate': nrm(ks[25], (DEPTH, D_MODEL), 0.1),
    }


def reference(x, p, fox_w_in, fox_b_f, fox_w_out, nsa_w_kv, cmp_pe, cmp_w1, cmp_b1, cmp_w2,
              nsa_w_q, nsa_b_g, nsa_w_out, ln_g, ln_b, moe_w_router, moe_b_router,
              moe_w_gate, moe_w_up, moe_w_down, shared_w_gate, shared_w_up, shared_w_down,
              ple_w_proj, ple_w_gate, ple_b_gate):
    shared_kv = None
    for i in range(DEPTH):
        if i < N_A_LAYERS:
            mix = forgetting_attention(x, fox_w_in[i], fox_b_f[i], fox_w_out[i])
        else:
            if i == N_A_LAYERS:
                shared_kv = nsa_shared_kv(x, nsa_w_kv, cmp_pe, cmp_w1, cmp_b1, cmp_w2)
            j = i - N_A_LAYERS
            mix = native_sparse_attention(x, shared_kv, nsa_w_q[j], nsa_b_g[j], nsa_w_out[j])
        x = layer_norm(DEEPNORM_ALPHA * x + mix, ln_g[i, 0], ln_b[i, 0])
        ffn = moe_ffn(x, moe_w_router[i], moe_b_router[i], moe_w_gate[i], moe_w_up[i], moe_w_down[i],
                      shared_w_gate[i], shared_w_up[i], shared_w_down[i])
        x = layer_norm(DEEPNORM_ALPHA * x + ffn, ln_g[i, 1], ln_b[i, 1])
        ple = per_layer_embedding(x, p[i], ple_w_proj[i], ple_w_gate[i], ple_b_gate[i])
        x = layer_norm(DEEPNORM_ALPHA * x + ple, ln_g[i, 2], ln_b[i, 2])
    return x
```

```python
import functools

import numpy as np
import jax
import jax.numpy as jnp
from jax import lax
from jax.experimental import pallas as pl
from jax.experimental.pallas import tpu as pltpu

F32 = jnp.float32
BF16 = jnp.bfloat16
NEG = -1e30

HEAD_DIM = 64
LANES = 128
NSA_KV_GROUPS = 4
NSA_HEADS_PER_GROUP = 4
GROUP_LANES = NSA_KV_GROUPS * HEAD_DIM
CMP_BLOCK = 32
CMP_STRIDE = 16
SLC_BLOCK = 64
SLC_TOPK = 16
WINDOW = 512
EXPERT_TOPK = 8
EXPERT_GROUPS = 8
EXPERT_TOPK_GROUPS = 4
ROUTED_SCALE = 2.5
LN_EPS = 1e-5
DEPTH = 2
DEEPNORM_ALPHA = (2.0 * DEPTH) ** 0.25
VMEM_LIMIT = 52 * 1024 * 1024

ROW_TILE = 512
FOX_TILE = 512
NSA_Q_TILE = 256
NSA_K_TILE = 512
ROUTER_TILE = 512
EXPERT_ROW_BLOCK = 256


def _cparams(sem):
    return pltpu.CompilerParams(dimension_semantics=sem, vmem_limit_bytes=VMEM_LIMIT)


def _nt(a, b):
    return lax.dot_general(a, b, (((1,), (1,)), ((), ())), preferred_element_type=F32)


def _dot(a, b):
    return jnp.dot(a, b, preferred_element_type=F32)


def _split2(a):
    hi = a.astype(BF16)
    lo = (a - hi.astype(F32)).astype(BF16)
    return hi, lo


def _split3(a):
    hi = a.astype(BF16)
    r = a - hi.astype(F32)
    mid = r.astype(BF16)
    lo = (r - mid.astype(F32)).astype(BF16)
    return hi, mid, lo


def _sigmoid(x):
    return 1.0 / (1.0 + jnp.exp(-x))


def _layer_norm(z, g, b):
    mu = jnp.mean(z, axis=-1, keepdims=True)
    zc = z - mu
    var = jnp.mean(zc * zc, axis=-1, keepdims=True)
    return zc * lax.rsqrt(var + LN_EPS) * g + b


def _proj_kernel(x_ref, w_ref, cs_ref, wg_ref, bg_ref, o_ref, g_ref, *, tn, gate_sigmoid):
    x = x_ref[...]
    xh = x.astype(BF16)
    m_out = w_ref.shape[1]
    for j in range(m_out // tn):
        sl = slice(j * tn, (j + 1) * tn)
        y = _dot(xh, w_ref[:, sl])
        o_ref[:, sl] = (y * cs_ref[:, sl]).astype(o_ref.dtype)
    xl = (x - xh.astype(F32)).astype(BF16)
    wh, wl = _split2(wg_ref[...])
    g = _dot(xh, wh) + _dot(xh, wl) + _dot(xl, wh) + bg_ref[...]
    if gate_sigmoid:
        g = _sigmoid(g)
    g_ref[...] = g


def _project(x2d, w_bf16, col_scale, w_gate, b_gate, gate_sigmoid):
    n, k = x2d.shape
    m_out = w_bf16.shape[1]
    tm = min(ROW_TILE, n)
    return pl.pallas_call(
        functools.partial(_proj_kernel, tn=512, gate_sigmoid=gate_sigmoid),
        grid=(n // tm,),
        in_specs=[
            pl.BlockSpec((tm, k), lambda i: (i, 0)),
            pl.BlockSpec((k, m_out), lambda i: (0, 0)),
            pl.BlockSpec((1, m_out), lambda i: (0, 0)),
            pl.BlockSpec((k, LANES), lambda i: (0, 0)),
            pl.BlockSpec((1, LANES), lambda i: (0, 0)),
        ],
        out_specs=[
            pl.BlockSpec((tm, m_out), lambda i: (i, 0)),
            pl.BlockSpec((tm, LANES), lambda i: (i, 0)),
        ],
        out_shape=[
            jax.ShapeDtypeStruct((n, m_out), BF16),
            jax.ShapeDtypeStruct((n, LANES), F32),
        ],
        compiler_params=_cparams(("arbitrary",)),
        name="project",
    )(x2d, w_bf16, col_scale, w_gate, b_gate)


def _gate_cumsum_kernel(z_ref, o_ref, carry_ref, *, ts):
    @pl.when(pl.program_id(1) == 0)
    def _():
        carry_ref[...] = jnp.zeros_like(carry_ref)

    z = z_ref[0]
    log_f = jnp.minimum(z, 0.0) - jnp.log(1.0 + jnp.exp(-jnp.abs(z)))
    row = lax.broadcasted_iota(jnp.int32, (ts, ts), 0)
    col = lax.broadcasted_iota(jnp.int32, (ts, ts), 1)
    tri = jnp.where(col <= row, 1.0, 0.0).astype(BF16)
    hi, mid, lo = _split3(log_f)
    cs = _dot(tri, hi) + _dot(tri, mid) + _dot(tri, lo) + carry_ref[...]
    o_ref[0] = -cs
    carry_ref[...] = cs[ts - 1:ts, :]


def _gate_cumsum(z):
    b, s, _ = z.shape
    ts = min(256, s)
    return pl.pallas_call(
        functools.partial(_gate_cumsum_kernel, ts=ts),
        grid=(b, s // ts),
        in_specs=[pl.BlockSpec((1, ts, LANES), lambda i, j: (i, j, 0))],
        out_specs=pl.BlockSpec((1, ts, LANES), lambda i, j: (i, j, 0)),
        out_shape=jax.ShapeDtypeStruct((b, s, LANES), F32),
        scratch_shapes=[pltpu.VMEM((1, LANES), F32)],
        compiler_params=_cparams(("arbitrary", "arbitrary")),
        name="gate_cumsum",
    )(z)


def _flash_update(s, v, m_ref, l_ref, acc_ref, h):
    m_prev = m_ref[h]
    m_new = jnp.maximum(m_prev, jnp.max(s, axis=1, keepdims=True))
    alpha = jnp.exp(m_prev - m_new)
    p = jnp.exp(s - m_new)
    l_ref[h] = alpha * l_ref[h] + jnp.sum(p, axis=1, keepdims=True)
    m_ref[h] = m_new
    acc_ref[h] = alpha * acc_ref[h] + _dot(p.astype(BF16), v)


def _fox_kernel(qt_ref, kt_ref, q_ref, k_ref, v_ref, nc_ref, o_ref, m_ref, l_ref, acc_ref, *, t):
    step = pl.program_id(2)
    qi = qt_ref[step]
    ki = kt_ref[step]

    @pl.when(ki == 0)
    def _():
        m_ref[...] = jnp.full_like(m_ref, NEG)
        l_ref[...] = jnp.zeros_like(l_ref)
        acc_ref[...] = jnp.zeros_like(acc_ref)

    lane = lax.broadcasted_iota(jnp.int32, (1, LANES), 1)

    def body(masked):
        q = q_ref[0]
        k = k_ref[0]
        v = v_ref[0]
        for h in range(2):
            in_head = (lane >= h * HEAD_DIM) & (lane < (h + 1) * HEAD_DIM)
            qm = jnp.where(in_head, q, jnp.zeros_like(q))
            s = _nt(qm, k) + nc_ref[0, 0, h:h + 1, :]
            if masked:
                row = lax.broadcasted_iota(jnp.int32, (t, t), 0)
                col = lax.broadcasted_iota(jnp.int32, (t, t), 1)
                s = jnp.where(col <= row, s, NEG)
            _flash_update(s, v, m_ref, l_ref, acc_ref, h)

    @pl.when(ki < qi)
    def _():
        body(False)

    @pl.when(ki == qi)
    def _():
        body(True)
        o0 = acc_ref[0] / l_ref[0]
        o1 = acc_ref[1] / l_ref[1]
        o_ref[0] = jnp.where(lane < HEAD_DIM, o0, o1).astype(o_ref.dtype)


def _fox_attention(qkv, negc_t, n_heads):
    b, s, _ = qkv.shape
    t = min(FOX_TILE, s)
    nq = s // t
    pairs = n_heads // 2
    qt = np.concatenate([np.full((i + 1,), i, np.int32) for i in range(nq)])
    kt = np.concatenate([np.arange(i + 1, dtype=np.int32) for i in range(nq)])
    grid_spec = pltpu.PrefetchScalarGridSpec(
        num_scalar_prefetch=2,
        grid=(b, pairs, len(qt)),
        in_specs=[
            pl.BlockSpec((1, t, LANES), lambda i, j, u, qt, kt: (i, qt[u], j)),
            pl.BlockSpec((1, t, LANES), lambda i, j, u, qt, kt: (i, kt[u], pairs + j)),
            pl.BlockSpec((1, t, LANES), lambda i, j, u, qt, kt: (i, kt[u], 2 * pairs + j)),
            pl.BlockSpec((1, 1, 2, t), lambda i, j, u, qt, kt: (i, j, 0, kt[u])),
        ],
        out_specs=pl.BlockSpec((1, t, LANES), lambda i, j, u, qt, kt: (i, qt[u], j)),
        scratch_shapes=[
            pltpu.VMEM((2, t, 1), F32),
            pltpu.VMEM((2, t, 1), F32),
            pltpu.VMEM((2, t, LANES), F32),
        ],
    )
    return pl.pallas_call(
        functools.partial(_fox_kernel, t=t),
        grid_spec=grid_spec,
        out_shape=jax.ShapeDtypeStruct((b, s, n_heads * HEAD_DIM), BF16),
        compiler_params=_cparams(("arbitrary", "arbitrary", "arbitrary")),
        name="fox_attention",
    )(jnp.asarray(qt), jnp.asarray(kt), qkv, qkv, qkv, negc_t)


def _out_ln_kernel(*refs, n_in):
    o_refs = refs[:n_in]
    w_ref, x_ref, g_ref, b_ref, y_ref, yb_ref = refs[n_in:]
    if n_in == 1:
        o = o_refs[0][...]
    else:
        o = o_refs[0][...].astype(F32)
        for r in o_refs[1:]:
            o = o + r[...].astype(F32)
        o = o.astype(BF16)
    z = DEEPNORM_ALPHA * x_ref[...] + _dot(o, w_ref[...])
    y = _layer_norm(z, g_ref[...], b_ref[...])
    y_ref[...] = y
    yb_ref[...] = y.astype(BF16)


def _out_ln(o_list, w_bf16, x2d, g, b):
    n, d = x2d.shape
    k = w_bf16.shape[0]
    tm = min(ROW_TILE, n)
    n_in = len(o_list)
    row = lambda i: (i, 0)
    fixed = lambda i: (0, 0)
    return pl.pallas_call(
        functools.partial(_out_ln_kernel, n_in=n_in),
        grid=(n // tm,),
        in_specs=[pl.BlockSpec((tm, k), row)] * n_in + [
            pl.BlockSpec((k, d), fixed),
            pl.BlockSpec((tm, d), row),
            pl.BlockSpec((1, d), fixed),
            pl.BlockSpec((1, d), fixed),
        ],
        out_specs=[pl.BlockSpec((tm, d), row), pl.BlockSpec((tm, d), row)],
        out_shape=[jax.ShapeDtypeStruct((n, d), F32), jax.ShapeDtypeStruct((n, d), BF16)],
        compiler_params=_cparams(("arbitrary",)),
        name="out_ln",
    )(*o_list, w_bf16, x2d, g, b)


def _router_kernel(x_ref, wt_ref, b_ref, idx_ref, w_ref, rank_ref, cnt_ref, carry_ref, *, tn, n_exp):
    @pl.when(pl.program_id(0) == 0)
    def _():
        carry_ref[...] = jnp.zeros_like(carry_ref)

    xh, xl = _split2(x_ref[...])
    wh, wl = _split2(wt_ref[...])
    logits = _nt(wh, xh) + _nt(wh, xl) + _nt(wl, xh)
    scores = _sigmoid(logits)
    biased = scores + b_ref[...]

    per_grp = n_exp // EXPERT_GROUPS
    blocks, gscore = [], []
    for g in range(EXPERT_GROUPS):
        blk = biased[g * per_grp:(g + 1) * per_grp, :]
        m1 = jnp.max(blk, axis=0, keepdims=True)
        eq = blk == m1
        n_eq = jnp.sum(jnp.where(eq, 1.0, 0.0), axis=0, keepdims=True)
        m2 = jnp.max(jnp.where(eq, -3e38, blk), axis=0, keepdims=True)
        blocks.append(blk)
        gscore.append(m1 + jnp.where(n_eq >= 2.0, m1, m2))
    cand = []
    for g in range(EXPERT_GROUPS):
        beaten = jnp.zeros_like(gscore[g])
        for o in range(EXPERT_GROUPS):
            if o == g:
                continue
            wins = (gscore[o] >= gscore[g]) if o < g else (gscore[o] > gscore[g])
            beaten = beaten + jnp.where(wins, 1.0, 0.0)
        cand.append(jnp.where(beaten < float(EXPERT_TOPK_GROUPS), blocks[g], NEG))
    cand = jnp.concatenate(cand, axis=0)

    erow = lax.broadcasted_iota(jnp.int32, (n_exp, tn), 0).astype(F32)
    hot, sel_idx, sel_score = [], [], []
    member = jnp.zeros((n_exp, tn), F32)
    for _ in range(EXPERT_TOPK):
        m = jnp.max(cand, axis=0, keepdims=True)
        first = jnp.min(jnp.where(cand == m, erow, float(n_exp)), axis=0, keepdims=True)
        onehot = erow == first
        sel_idx.append(first)
        sel_score.append(jnp.sum(jnp.where(onehot, scores, 0.0), axis=0, keepdims=True))
        cand = jnp.where(onehot, -3e38, cand)
        member = member + jnp.where(onehot, 1.0, 0.0)
        hot.append(onehot)
    total = sel_score[0]
    for sc in sel_score[1:]:
        total = total + sc

    trow = lax.broadcasted_iota(jnp.int32, (tn, tn), 0)
    tcol = lax.broadcasted_iota(jnp.int32, (tn, tn), 1)
    before = jnp.where(trow < tcol, 1.0, 0.0).astype(BF16)
    prior = _dot(member.astype(BF16), before) + carry_ref[...]
    ranks = [jnp.sum(jnp.where(hot[k], prior, 0.0), axis=0, keepdims=True) for k in range(EXPERT_TOPK)]

    idx_ref[...] = jnp.concatenate(sel_idx, axis=0).astype(jnp.int32)
    w_ref[...] = jnp.concatenate([sc / total * ROUTED_SCALE for sc in sel_score], axis=0)
    rank_ref[...] = jnp.concatenate(ranks, axis=0).astype(jnp.int32)
    carry_ref[...] = carry_ref[...] + jnp.sum(member, axis=1, keepdims=True)
    cnt_ref[...] = carry_ref[...]


def _route(x2d, w_router_t, b_router):
    n, d = x2d.shape
    n_exp = w_router_t.shape[0]
    tn = min(ROUTER_TILE, n)
    col = lambda i: (0, i)
    return pl.pallas_call(
        functools.partial(_router_kernel, tn=tn, n_exp=n_exp),
        grid=(n // tn,),
        in_specs=[
            pl.BlockSpec((tn, d), lambda i: (i, 0)),
            pl.BlockSpec((n_exp, d), lambda i: (0, 0)),
            pl.BlockSpec((n_exp, 1), lambda i: (0, 0)),
        ],
        out_specs=[
            pl.BlockSpec((EXPERT_TOPK, tn), col),
            pl.BlockSpec((EXPERT_TOPK, tn), col),
            pl.BlockSpec((EXPERT_TOPK, tn), col),
            pl.BlockSpec((n_exp, 1), lambda i: (0, 0)),
        ],
        out_shape=[
            jax.ShapeDtypeStruct((EXPERT_TOPK, n), jnp.int32),
            jax.ShapeDtypeStruct((EXPERT_TOPK, n), F32),
            jax.ShapeDtypeStruct((EXPERT_TOPK, n), jnp.int32),
            jax.ShapeDtypeStruct((n_exp, 1), F32),
        ],
        scratch_shapes=[pltpu.VMEM((n_exp, 1), F32)],
        compiler_params=_cparams(("arbitrary",)),
        name="router",
    )(x2d, w_router_t, b_router)


def _experts_kernel(be_ref, first_ref, nused_ref, x_ref, wg_ref, wu_ref, wd_ref, y_ref,
                    wg_s, wu_s, wd_s):
    blk = pl.program_id(0)

    @pl.when(blk < nused_ref[0])
    def _():
        @pl.when(first_ref[blk] == 1)
        def _():
            wg_s[...] = wg_ref[0].astype(BF16)
            wu_s[...] = wu_ref[0].astype(BF16)
            wd_s[...] = wd_ref[0].astype(BF16)

        x = x_ref[...]
        gate = _dot(x, wg_s[...])
        up = _dot(x, wu_s[...])
        h = gate * _sigmoid(gate) * up
        y_ref[...] = _dot(h.astype(BF16), wd_s[...]).astype(y_ref.dtype)

    @pl.when(blk >= nused_ref[0])
    def _():
        y_ref[...] = jnp.zeros_like(y_ref)


def _experts(xg, blk_expert, blk_first, n_used, w_gate, w_up, w_down):
    n_rows, d = xg.shape
    hdim = w_gate.shape[2]
    tm = EXPERT_ROW_BLOCK
    n_blocks = n_rows // tm
    grid_spec = pltpu.PrefetchScalarGridSpec(
        num_scalar_prefetch=3,
        grid=(n_blocks,),
        in_specs=[
            pl.BlockSpec((tm, d), lambda i, be, fi, nu: (i, 0)),
            pl.BlockSpec((1, d, hdim), lambda i, be, fi, nu: (be[i], 0, 0)),
            pl.BlockSpec((1, d, hdim), lambda i, be, fi, nu: (be[i], 0, 0)),
            pl.BlockSpec((1, hdim, d), lambda i, be, fi, nu: (be[i], 0, 0)),
        ],
        out_specs=pl.BlockSpec((tm, d), lambda i, be, fi, nu: (i, 0)),
        scratch_shapes=[
            pltpu.VMEM((d, hdim), BF16),
            pltpu.VMEM((d, hdim), BF16),
            pltpu.VMEM((hdim, d), BF16),
        ],
    )
    return pl.pallas_call(
        _experts_kernel,
        grid_spec=grid_spec,
        out_shape=jax.ShapeDtypeStruct((n_rows, d), BF16),
        compiler_params=_cparams(("arbitrary",)),
        name="experts",
    )(blk_expert, blk_first, n_used, xg, w_gate, w_up, w_down)


def _moe_tail_kernel(x_ref, r_ref, p_ref, sg_ref, su_ref, sd_ref, pg_ref, pb_ref, pp_ref,
                     g1_ref, b1_ref, g2_ref, b2_ref, y_ref):
    x = x_ref[...]
    xb = x.astype(BF16)
    gate = _dot(xb, sg_ref[...])
    up = _dot(xb, su_ref[...])
    shared = _dot((gate * _sigmoid(gate) * up).astype(BF16), sd_ref[...])
    x2 = _layer_norm(DEEPNORM_ALPHA * x + (r_ref[...] + shared), g1_ref[...], b1_ref[...])
    ple_gate = _sigmoid(_dot(x2.astype(BF16), pg_ref[...]) + pb_ref[...])
    ple = ple_gate * _dot(p_ref[...].astype(BF16), pp_ref[...])
    y_ref[...] = _layer_norm(DEEPNORM_ALPHA * x2 + ple, g2_ref[...], b2_ref[...])


def _moe_tail(x2d, routed, p2d, sg, su, sd, pg, pb, pp, g1, b1, g2, b2):
    n, d = x2d.shape
    tm = min(ROW_TILE, n)
    row = lambda i: (i, 0)
    fixed = lambda i: (0, 0)
    full = lambda a: pl.BlockSpec(a.shape, fixed)
    return pl.pallas_call(
        _moe_tail_kernel,
        grid=(n // tm,),
        in_specs=[
            pl.BlockSpec((tm, d), row),
            pl.BlockSpec((tm, d), row),
            pl.BlockSpec((tm, p2d.shape[1]), row),
            full(sg), full(su), full(sd), full(pg), full(pb), full(pp),
            full(g1), full(b1), full(g2), full(b2),
        ],
        out_specs=pl.BlockSpec((tm, d), row),
        out_shape=jax.ShapeDtypeStruct((n, d), F32),
        compiler_params=_cparams(("arbitrary",)),
        name="moe_tail",
    )(x2d, routed, p2d, sg, su, sd, pg, pb, pp, g1, b1, g2, b2)


def _moe_layer(x1, x1b, p2d, w_router, b_router, w_gate, w_up, w_down, sg, su, sd,
               ple_proj, ple_gate, ple_bias, g1, b1, g2, b2):
    n, d = x1.shape
    n_exp = w_router.shape[1]
    idx, wts, rank, counts = _route(x1, w_router.T, b_router.reshape(n_exp, 1))
    tm = EXPERT_ROW_BLOCK
    counts = counts[:, 0].astype(jnp.int32)
    padded = (counts + tm - 1) // tm * tm
    pad_end = jnp.cumsum(padded)
    pad_start = pad_end - padded
    n_blocks = (n * EXPERT_TOPK + n_exp * (tm - 1)) // tm + 1
    n_rows = n_blocks * tm
    pos = pad_start[idx] + rank
    tok = jnp.broadcast_to(jnp.arange(n, dtype=jnp.int32)[None, :], pos.shape)
    tok_pad = jnp.zeros((n_rows,), jnp.int32).at[pos.reshape(-1)].set(
        tok.reshape(-1), unique_indices=True)
    blk_start = jnp.arange(n_blocks, dtype=jnp.int32) * tm
    blk_expert = jnp.minimum(jnp.searchsorted(pad_end, blk_start, side="right"),
                             n_exp - 1).astype(jnp.int32)
    blk_first = jnp.concatenate([jnp.ones((1,), jnp.int32),
                                 (blk_expert[1:] != blk_expert[:-1]).astype(jnp.int32)])
    n_used = (pad_end[-1:] // tm).astype(jnp.int32)
    xg = jnp.take(x1b, tok_pad, axis=0)
    y = _experts(xg, blk_expert, blk_first, n_used, w_gate, w_up, w_down)
    routed = jnp.sum(jnp.take(y, pos, axis=0).astype(F32) * wts[:, :, None], axis=0)
    return _moe_tail(x1, routed, p2d, sg.astype(BF16), su.astype(BF16), sd.astype(BF16),
                     ple_gate.astype(BF16), ple_bias.reshape(1, d), ple_proj.astype(BF16),
                     g1.reshape(1, d), b1.reshape(1, d), g2.reshape(1, d), b2.reshape(1, d))


def _gelu_tanh(x):
    return 0.5 * x * (1.0 + jnp.tanh(0.7978845608028654 * (x + 0.044715 * (x * x * x))))


def _compress_kernel(c_ref, pe_ref, w1_ref, b1_ref, w2_ref, o_ref, *, nc, half):
    out = jnp.zeros((nc, GROUP_LANES), F32)
    for g in range(NSA_KV_GROUPS):
        c = c_ref[0, 0, g].astype(F32)
        a = _dot((c + pe_ref[0, 0:1, :]).astype(BF16), w1_ref[0, :half, :])
        bm = _dot((c + pe_ref[0, 1:2, :]).astype(BF16), w1_ref[0, half:, :])
        h = _gelu_tanh(a + pltpu.roll(bm, nc - 1, 0) + b1_ref[0])
        out = out + _dot(h.astype(BF16), w2_ref[0, g])
    o_ref[0, 0] = out.astype(o_ref.dtype)


def _compress(chunks, pe_flat, w1, b1, w2_placed):
    b, _, _, nc, half = chunks.shape
    hid = w1.shape[2]
    return pl.pallas_call(
        functools.partial(_compress_kernel, nc=nc, half=half),
        grid=(b, 2),
        in_specs=[
            pl.BlockSpec((1, 1, NSA_KV_GROUPS, nc, half), lambda i, j: (i, j, 0, 0, 0)),
            pl.BlockSpec((1, 2, half), lambda i, j: (j, 0, 0)),
            pl.BlockSpec((1, 2 * half, hid), lambda i, j: (j, 0, 0)),
            pl.BlockSpec((1, 1, hid), lambda i, j: (j, 0, 0)),
            pl.BlockSpec((1, NSA_KV_GROUPS, hid, GROUP_LANES), lambda i, j: (j, 0, 0, 0)),
        ],
        out_specs=pl.BlockSpec((1, 1, nc, GROUP_LANES), lambda i, j: (i, j, 0, 0)),
        out_shape=jax.ShapeDtypeStruct((b, 2, nc, GROUP_LANES), BF16),
        compiler_params=_cparams(("arbitrary", "arbitrary")),
        name="compress",
    )(chunks, pe_flat, w1, b1, w2_placed)


def _group_mask(g):
    lane = lax.broadcasted_iota(jnp.int32, (1, GROUP_LANES), 1)
    return (lane >= g * HEAD_DIM) & (lane < (g + 1) * HEAD_DIM)


def _nsa_cmp_kernel(q_ref, kc_ref, vc_ref, gate_ref, slope_ref, wt_ref, o_ref, sel_ref, psum_ref, score_ref,
                    *, tq, nc, n_slc):
    qi = pl.program_id(1)
    r = pl.program_id(2)
    q0 = qi * tq

    @pl.when(r == 0)
    def _():
        psum_ref[...] = jnp.zeros_like(psum_ref)

    q = q_ref[0]
    kc = kc_ref[0, 0]
    vc = vc_ref[0, 0]
    t_pos = q0 + lax.broadcasted_iota(jnp.int32, (tq, nc), 0)
    cmp_end = lax.broadcasted_iota(jnp.int32, (tq, nc), 1) * CMP_STRIDE + (CMP_BLOCK - 1)
    visible = cmp_end <= t_pos
    end_rel = (lax.broadcasted_iota(jnp.int32, (1, nc), 1) * CMP_STRIDE + (CMP_BLOCK - 1) - q0).astype(F32)
    gates = gate_ref[0, 0]
    out = jnp.zeros((tq, GROUP_LANES), F32)
    for g in range(NSA_KV_GROUPS):
        in_grp = _group_mask(g)
        qm = jnp.where(in_grp, q, jnp.zeros_like(q))
        s = _nt(qm, kc) + slope_ref[0, g, :, :nc] * end_rel
        s = jnp.where(visible, s, NEG)
        m = jnp.max(s, axis=1, keepdims=True)
        p = jnp.where(visible, jnp.exp(s - m), 0.0)
        l = jnp.sum(p, axis=1, keepdims=True)
        p = p * jnp.where(l > 0.0, 1.0 / l, 0.0)
        psum_ref[g] = psum_ref[g] + p
        o_g = _dot(p.astype(BF16), vc) * gates[:, g:g + 1]
        out = out + jnp.where(in_grp, o_g, 0.0)
    o_ref[0] = out.astype(o_ref.dtype)

    @pl.when(r == NSA_HEADS_PER_GROUP - 1)
    def _():
        blk = lax.broadcasted_iota(jnp.int32, (n_slc, tq), 0)
        t_row = q0 + lax.broadcasted_iota(jnp.int32, (n_slc, tq), 1)
        cur = t_row // SLC_BLOCK
        forced = (blk == 0) | (blk == cur) | (blk == cur - 1)
        vis = blk * SLC_BLOCK <= t_row
        wt = wt_ref[...]
        for g in range(NSA_KV_GROUPS):
            hi, mid, lo = _split3(psum_ref[g])
            imp = _nt(wt, hi) + _nt(wt, mid) + _nt(wt, lo)
            score = jnp.where(forced, 1e9, jnp.where(vis, imp, NEG))
            score_ref[...] = score

            def count(j, beaten):
                vj = score_ref[pl.ds(j, 1), :]
                ge = jnp.where(vj >= score, 1.0, 0.0)
                gt = jnp.where(vj > score, 1.0, 0.0)
                return beaten + jnp.where(blk > j, ge, gt)

            beaten = lax.fori_loop(0, n_slc, count, jnp.zeros((n_slc, tq), F32))
            keep = (beaten < float(min(SLC_TOPK, n_slc))) & (score > 0.5 * NEG)
            sel_ref[0, g] = jnp.where(keep, 0.0, NEG)


def _nsa_compressed(q, cmp_kv, gates_c, slopes_rep, w_sel_t):
    b, s, _ = q.shape
    nc = cmp_kv.shape[2]
    n_slc = s // SLC_BLOCK
    tq = min(NSA_Q_TILE, s)
    rr = NSA_HEADS_PER_GROUP
    return pl.pallas_call(
        functools.partial(_nsa_cmp_kernel, tq=tq, nc=nc, n_slc=n_slc),
        grid=(b, s // tq, rr),
        in_specs=[
            pl.BlockSpec((1, tq, GROUP_LANES), lambda i, j, r: (i, j, r)),
            pl.BlockSpec((1, 1, nc, GROUP_LANES), lambda i, j, r: (i, 0, 0, 0)),
            pl.BlockSpec((1, 1, nc, GROUP_LANES), lambda i, j, r: (i, 1, 0, 0)),
            pl.BlockSpec((1, 1, tq, NSA_KV_GROUPS), lambda i, j, r: (i, r, j, 0)),
            pl.BlockSpec((1, NSA_KV_GROUPS, 1, slopes_rep.shape[3]), lambda i, j, r: (r, 0, 0, 0)),
            pl.BlockSpec((n_slc, nc), lambda i, j, r: (0, 0)),
        ],
        out_specs=[
            pl.BlockSpec((1, tq, GROUP_LANES), lambda i, j, r: (i, j, r)),
            pl.BlockSpec((1, NSA_KV_GROUPS, n_slc, tq), lambda i, j, r: (i, 0, 0, j)),
        ],
        out_shape=[
            jax.ShapeDtypeStruct((b, s, rr * GROUP_LANES), BF16),
            jax.ShapeDtypeStruct((b, NSA_KV_GROUPS, n_slc, s), F32),
        ],
        scratch_shapes=[pltpu.VMEM((NSA_KV_GROUPS, tq, nc), F32), pltpu.VMEM((n_slc, tq), F32)],
        compiler_params=_cparams(("arbitrary", "arbitrary", "arbitrary")),
        name="nsa_compressed",
    )(q, cmp_kv, cmp_kv, gates_c, slopes_rep, w_sel_t)


def _nsa_attn_kernel(qt_ref, kt_ref, fl_ref, *refs, tq, tk, use_sel):
    if use_sel:
        q_ref, k_ref, v_ref, gate_ref, slope_ref, sel_ref, et_ref, o_ref, m_ref, l_ref, acc_ref = refs
    else:
        q_ref, k_ref, v_ref, gate_ref, slope_ref, o_ref, m_ref, l_ref, acc_ref = refs
    step = pl.program_id(2)
    qi = qt_ref[step]
    ki = kt_ref[step]
    flags = fl_ref[step]

    @pl.when((flags & 1) != 0)
    def _():
        m_ref[...] = jnp.full_like(m_ref, NEG)
        l_ref[...] = jnp.zeros_like(l_ref)
        acc_ref[...] = jnp.zeros_like(acc_ref)

    def body(masked):
        q = q_ref[0]
        k = k_ref[0]
        v = v_ref[0]
        key_rel = (ki * tk - qi * tq + lax.broadcasted_iota(jnp.int32, (1, tk), 1)).astype(F32)
        if masked:
            dist = (qi * tq + lax.broadcasted_iota(jnp.int32, (tq, tk), 0)) - (
                ki * tk + lax.broadcasted_iota(jnp.int32, (tq, tk), 1))
            ok = dist >= 0
            if not use_sel:
                ok = ok & (dist < WINDOW)
        for g in range(NSA_KV_GROUPS):
            qm = jnp.where(_group_mask(g), q, jnp.zeros_like(q))
            s = _nt(qm, k) + slope_ref[0, g, :, :tk] * key_rel
            if use_sel:
                s = s + _dot(sel_ref[0, g], et_ref[...])
            if masked:
                s = jnp.where(ok, s, NEG)
            _flash_update(s, v, m_ref, l_ref, acc_ref, g)

    @pl.when((flags & 4) == 0)
    def _():
        body(False)

    @pl.when((flags & 4) != 0)
    def _():
        body(True)

    @pl.when((flags & 2) != 0)
    def _():
        gates = gate_ref[0, 0]
        out = jnp.zeros((tq, GROUP_LANES), F32)
        for g in range(NSA_KV_GROUPS):
            scale = gates[:, g:g + 1] / l_ref[g]
            out = out + jnp.where(_group_mask(g), acc_ref[g] * scale, 0.0)
        o_ref[0] = out.astype(o_ref.dtype)


def _nsa_tables(s, tq, tk, window):
    qt, kt, fl = [], [], []
    for qi in range(s // tq):
        lo_t, hi_t = qi * tq, qi * tq + tq - 1
        first_key = 0 if window is None else max(0, lo_t - window + 1)
        tiles = list(range(first_key // tk, hi_t // tk + 1))
        for n, ki in enumerate(tiles):
            k_lo, k_hi = ki * tk, ki * tk + tk - 1
            masked = k_hi > lo_t or (window is not None and hi_t - k_lo >= window)
            qt.append(qi)
            kt.append(ki)
            fl.append((1 if n == 0 else 0) | (2 if n == len(tiles) - 1 else 0) | (4 if masked else 0))
    return (jnp.asarray(np.asarray(qt, np.int32)), jnp.asarray(np.asarray(kt, np.int32)),
            jnp.asarray(np.asarray(fl, np.int32)))


def _nsa_attention(q, kv, k_col, v_col, gates, slopes_rep, sel_bias=None, e_t=None):
    b, s, _ = q.shape
    use_sel = sel_bias is not None
    tq = min(NSA_Q_TILE, s)
    tk = min(NSA_K_TILE if use_sel else NSA_Q_TILE, s)
    qt, kt, fl = _nsa_tables(s, tq, tk, None if use_sel else WINDOW)
    rr = NSA_HEADS_PER_GROUP
    in_specs = [
        pl.BlockSpec((1, tq, GROUP_LANES), lambda i, r, u, qt, kt, fl: (i, qt[u], r)),
        pl.BlockSpec((1, tk, GROUP_LANES), lambda i, r, u, qt, kt, fl: (i, kt[u], k_col)),
        pl.BlockSpec((1, tk, GROUP_LANES), lambda i, r, u, qt, kt, fl: (i, kt[u], v_col)),
        pl.BlockSpec((1, 1, tq, NSA_KV_GROUPS), lambda i, r, u, qt, kt, fl: (i, r, qt[u], 0)),
        pl.BlockSpec((1, NSA_KV_GROUPS, 1, slopes_rep.shape[3]), lambda i, r, u, qt, kt, fl: (r, 0, 0, 0)),
    ]
    args = [q, kv, kv, gates, slopes_rep]
    if use_sel:
        n_slc = s // SLC_BLOCK
        in_specs += [
            pl.BlockSpec((1, NSA_KV_GROUPS, tq, n_slc), lambda i, r, u, qt, kt, fl: (i, 0, qt[u], 0)),
            pl.BlockSpec((n_slc, tk), lambda i, r, u, qt, kt, fl: (0, kt[u])),
        ]
        args += [sel_bias, e_t]
    grid_spec = pltpu.PrefetchScalarGridSpec(
        num_scalar_prefetch=3,
        grid=(b, rr, int(qt.shape[0])),
        in_specs=in_specs,
        out_specs=pl.BlockSpec((1, tq, GROUP_LANES), lambda i, r, u, qt, kt, fl: (i, qt[u], r)),
        scratch_shapes=[
            pltpu.VMEM((NSA_KV_GROUPS, tq, 1), F32),
            pltpu.VMEM((NSA_KV_GROUPS, tq, 1), F32),
            pltpu.VMEM((NSA_KV_GROUPS, tq, GROUP_LANES), F32),
        ],
    )
    return pl.pallas_call(
        functools.partial(_nsa_attn_kernel, tq=tq, tk=tk, use_sel=use_sel),
        grid_spec=grid_spec,
        out_shape=jax.ShapeDtypeStruct((b, s, rr * GROUP_LANES), BF16),
        compiler_params=_cparams(("arbitrary", "arbitrary", "arbitrary")),
        name="nsa_selected" if use_sel else "nsa_window",
    )(qt, kt, fl, *args)


def _pad_lanes(a, width=LANES):
    return jnp.pad(a, ((0, 0), (0, width - a.shape[1])))


def _fox_layer(x2d, b, s, w_in, b_f, w_out, ln_g, ln_b):
    d = x2d.shape[1]
    attn = w_out.shape[0]
    n_heads = attn // HEAD_DIM
    scale = HEAD_DIM ** -0.5
    col_scale = jnp.concatenate([jnp.full((attn,), scale, F32), jnp.ones((2 * attn,), F32)])[None, :]
    qkv, z = _project(x2d, w_in[:, :3 * attn].astype(BF16), col_scale,
                      _pad_lanes(w_in[:, 3 * attn:]), _pad_lanes(b_f[None, :]), False)
    negc = _gate_cumsum(z.reshape(b, s, LANES))
    negc_t = negc[:, :, :n_heads].transpose(0, 2, 1).reshape(b, n_heads // 2, 2, s)
    o = _fox_attention(qkv.reshape(b, s, 3 * attn), negc_t, n_heads)
    return _out_ln([o.reshape(b * s, attn)], w_out.astype(BF16), x2d,
                   ln_g.reshape(1, d), ln_b.reshape(1, d))


def _selection_weights_t(nc, n_slc):
    cs = np.arange(nc)[:, None] * CMP_STRIDE
    ce = cs + CMP_BLOCK
    ss = np.arange(n_slc)[None, :] * SLC_BLOCK
    se = ss + SLC_BLOCK
    w = np.clip(np.minimum(ce, se) - np.maximum(cs, ss), 0, None) / CMP_STRIDE
    w[nc - 1, :] = 0.0
    return jnp.asarray(w.T, BF16)


def _nsa_layer(x2d, b, s, w_kv, cmp_pe, cmp_w1, cmp_b1, cmp_w2, w_q, b_g, w_out, ln_g, ln_b):
    d = x2d.shape[1]
    gg, rr = NSA_KV_GROUPS, NSA_HEADS_PER_GROUP
    attn = gg * rr * HEAD_DIM
    n_heads = gg * rr
    wq = w_q[:, :attn].reshape(d, gg, rr, HEAD_DIM).transpose(0, 2, 1, 3).reshape(d, attn)
    wgate = w_q[:, attn:].reshape(d, gg, rr, 3).transpose(0, 3, 2, 1).reshape(d, 3 * n_heads)
    bgate = b_g.reshape(gg, rr, 3).transpose(2, 1, 0).reshape(1, 3 * n_heads)
    wo = w_out.reshape(gg, rr, HEAD_DIM, d).transpose(1, 0, 2, 3).reshape(attn, d)
    w_all = jnp.concatenate([wq, w_kv], axis=1).astype(BF16)
    col_scale = jnp.concatenate([jnp.full((attn,), HEAD_DIM ** -0.5, F32),
                                 jnp.ones((w_kv.shape[1],), F32)])[None, :]
    qkv, gates = _project(x2d, w_all, col_scale, _pad_lanes(wgate), _pad_lanes(bgate), True)
    width = qkv.shape[1]
    qkv = qkv.reshape(b, s, width)
    gates = gates[:, :3 * n_heads].reshape(b, s, 3, rr, gg).transpose(2, 0, 3, 1, 4)

    nc = s // CMP_STRIDE
    half = CMP_STRIDE * HEAD_DIM
    raw = qkv[:, :, attn:attn + 2 * GROUP_LANES]
    chunks = raw.reshape(b, nc, CMP_STRIDE, 2, gg, HEAD_DIM).transpose(0, 3, 4, 1, 2, 5)
    chunks = chunks.reshape(b, 2, gg, nc, half)
    pe_flat = cmp_pe.reshape(2, 2, half)
    hid = cmp_w1.shape[2]
    w2_placed = jnp.zeros((2, gg, hid, GROUP_LANES), F32)
    for g in range(gg):
        w2_placed = w2_placed.at[:, g, :, g * HEAD_DIM:(g + 1) * HEAD_DIM].set(cmp_w2)
    cmp_kv = _compress(chunks, pe_flat, cmp_w1.astype(BF16), cmp_b1.reshape(2, 1, hid),
                       w2_placed.astype(BF16))

    slopes = 2.0 ** (-8.0 * np.arange(1, n_heads + 1) / n_heads)
    rep = max(nc, NSA_K_TILE)
    slopes_rep = jnp.asarray(np.broadcast_to(
        slopes.reshape(gg, rr).T[:, :, None, None], (rr, gg, 1, rep)).astype(np.float32))
    n_slc = s // SLC_BLOCK
    o_c, sel_t = _nsa_compressed(qkv, cmp_kv, gates[0], slopes_rep, _selection_weights_t(nc, n_slc))
    sel_bias = sel_t.transpose(0, 1, 3, 2).astype(BF16)
    e_t = jnp.asarray((np.arange(s)[None, :] // SLC_BLOCK == np.arange(n_slc)[:, None]), BF16)
    base = attn // GROUP_LANES
    o_s = _nsa_attention(qkv, qkv, base + 2, base + 3, gates[1], slopes_rep, sel_bias, e_t)
    o_w = _nsa_attention(qkv, qkv, base + 4, base + 5, gates[2], slopes_rep)
    n = b * s
    return _out_ln([o_c.reshape(n, attn), o_s.reshape(n, attn), o_w.reshape(n, attn)],
                   wo.astype(BF16), x2d, ln_g.reshape(1, d), ln_b.reshape(1, d))


def kernel(x, p, fox_w_in, fox_b_f, fox_w_out, nsa_w_kv, cmp_pe, cmp_w1, cmp_b1, cmp_w2, nsa_w_q, nsa_b_g, nsa_w_out, ln_g, ln_b, moe_w_router, moe_b_router, moe_w_gate, moe_w_up, moe_w_down, shared_w_gate, shared_w_up, shared_w_down, ple_w_proj, ple_w_gate, ple_b_gate):
    b, s, d = x.shape
    n = b * s
    depth = p.shape[0]
    n_a = depth // 2
    h = x.reshape(n, d)
    for i in range(depth):
        if i < n_a:
            h1, h1b = _fox_layer(h, b, s, fox_w_in[i], fox_b_f[i], fox_w_out[i], ln_g[i, 0], ln_b[i, 0])
        else:
            j = i - n_a
            h1, h1b = _nsa_layer(h, b, s, nsa_w_kv, cmp_pe, cmp_w1, cmp_b1, cmp_w2,
                                 nsa_w_q[j], nsa_b_g[j], nsa_w_out[j], ln_g[i, 0], ln_b[i, 0])
        h = _moe_layer(h1, h1b, p[i].reshape(n, -1), moe_w_router[i], moe_b_router[i],
                       moe_w_gate[i], moe_w_up[i], moe_w_down[i],
                       shared_w_gate[i], shared_w_up[i], shared_w_down[i],
                       ple_w_proj[i], ple_w_gate[i], ple_b_gate[i],
                       ln_g[i, 1], ln_b[i, 1], ln_g[i, 2], ln_b[i, 2])
    return h.reshape(b, s, d)
```

```python
import functools

import numpy as np
import jax
import jax.numpy as jnp
from jax import lax
from jax.experimental import pallas as pl
from jax.experimental.pallas import tpu as pltpu

F32 = jnp.float32
BF16 = jnp.bfloat16
NEG = -1e30

HEAD_DIM = 64
LANES = 128
NSA_KV_GROUPS = 4
NSA_HEADS_PER_GROUP = 4
GROUP_LANES = NSA_KV_GROUPS * HEAD_DIM
CMP_BLOCK = 32
CMP_STRIDE = 16
SLC_BLOCK = 64
SLC_TOPK = 16
WINDOW = 512
EXPERT_TOPK = 8
EXPERT_GROUPS = 8
EXPERT_TOPK_GROUPS = 4
ROUTED_SCALE = 2.5
LN_EPS = 1e-5
DEPTH = 2
DEEPNORM_ALPHA = (2.0 * DEPTH) ** 0.25
VMEM_LIMIT = 52 * 1024 * 1024

ROW_TILE = 512
FOX_TILE = 512
FOX_SUB = 128
NSA_Q_TILE = 256
NSA_K_TILE = 512
NSA_TILE = 512
NSA_SUB = 128
ROUTER_TILE = 512
EXPERT_ROW_BLOCK = 256


def _cparams(sem):
    return pltpu.CompilerParams(dimension_semantics=sem, vmem_limit_bytes=VMEM_LIMIT)


def _nt(a, b):
    return lax.dot_general(a, b, (((1,), (1,)), ((), ())), preferred_element_type=F32)


def _dot(a, b):
    return jnp.dot(a, b, preferred_element_type=F32)


def _split2(a):
    hi = a.astype(BF16)
    lo = (a - hi.astype(F32)).astype(BF16)
    return hi, lo


def _split3(a):
    hi = a.astype(BF16)
    r = a - hi.astype(F32)
    mid = r.astype(BF16)
    lo = (r - mid.astype(F32)).astype(BF16)
    return hi, mid, lo


def _sigmoid(x):
    return 1.0 / (1.0 + jnp.exp(-x))


def _layer_norm(z, g, b):
    mu = jnp.mean(z, axis=-1, keepdims=True)
    zc = z - mu
    var = jnp.mean(zc * zc, axis=-1, keepdims=True)
    return zc * lax.rsqrt(var + LN_EPS) * g + b


def _proj_kernel(x_ref, w_ref, cs_ref, wg_ref, bg_ref, o_ref, g_ref, *, tn, gate_sigmoid):
    x = x_ref[...]
    xh = x.astype(BF16)
    m_out = w_ref.shape[1]
    for j in range(m_out // tn):
        sl = slice(j * tn, (j + 1) * tn)
        y = _dot(xh, w_ref[:, sl])
        o_ref[:, sl] = (y * cs_ref[:, sl]).astype(o_ref.dtype)
    xl = (x - xh.astype(F32)).astype(BF16)
    wh, wl = _split2(wg_ref[...])
    g = _dot(xh, wh) + _dot(xh, wl) + _dot(xl, wh) + bg_ref[...]
    if gate_sigmoid:
        g = _sigmoid(g)
    g_ref[...] = g


def _project(x2d, w_bf16, col_scale, w_gate, b_gate, gate_sigmoid):
    n, k = x2d.shape
    m_out = w_bf16.shape[1]
    tm = min(ROW_TILE, n)
    return pl.pallas_call(
        functools.partial(_proj_kernel, tn=512, gate_sigmoid=gate_sigmoid),
        grid=(n // tm,),
        in_specs=[
            pl.BlockSpec((tm, k), lambda i: (i, 0)),
            pl.BlockSpec((k, m_out), lambda i: (0, 0)),
            pl.BlockSpec((1, m_out), lambda i: (0, 0)),
            pl.BlockSpec((k, LANES), lambda i: (0, 0)),
            pl.BlockSpec((1, LANES), lambda i: (0, 0)),
        ],
        out_specs=[
            pl.BlockSpec((tm, m_out), lambda i: (i, 0)),
            pl.BlockSpec((tm, LANES), lambda i: (i, 0)),
        ],
        out_shape=[
            jax.ShapeDtypeStruct((n, m_out), BF16),
            jax.ShapeDtypeStruct((n, LANES), F32),
        ],
        compiler_params=_cparams(("arbitrary",)),
        name="project",
    )(x2d, w_bf16, col_scale, w_gate, b_gate)


LOG2E = 1.4426950408889634
GATE_PIECE_STRIDE = 16


def _gate_cumsum_kernel(z_ref, o_ref, carry_ref, *, ts, n_heads):
    @pl.when(pl.program_id(1) == 0)
    def _():
        carry_ref[...] = jnp.zeros_like(carry_ref)

    z = z_ref[0]
    log_f = jnp.minimum(z, 0.0) - jnp.log(1.0 + jnp.exp(-jnp.abs(z)))
    row = lax.broadcasted_iota(jnp.int32, (ts, ts), 0)
    col = lax.broadcasted_iota(jnp.int32, (ts, ts), 1)
    tri = jnp.where(col <= row, 1.0, 0.0).astype(BF16)
    hi, mid, lo = _split3(log_f)
    cs = _dot(tri, hi) + _dot(tri, mid) + _dot(tri, lo) + carry_ref[...]
    carry_ref[...] = cs[ts - 1:ts, :]
    lane = lax.broadcasted_iota(jnp.int32, (1, LANES), 1)
    bias = jnp.where(lane < n_heads, cs * (-LOG2E), 0.0)
    hi, mid, lo = _split3(bias)
    pieces = (hi.astype(F32) + pltpu.roll(mid.astype(F32), GATE_PIECE_STRIDE, 1)
              + pltpu.roll(lo.astype(F32), 2 * GATE_PIECE_STRIDE, 1))
    o_ref[0] = (pieces + pltpu.roll(pieces, HEAD_DIM, 1)).astype(o_ref.dtype)


def _gate_cumsum(z, n_heads):
    b, s, _ = z.shape
    ts = min(256, s)
    return pl.pallas_call(
        functools.partial(_gate_cumsum_kernel, ts=ts, n_heads=n_heads),
        grid=(b, s // ts),
        in_specs=[pl.BlockSpec((1, ts, LANES), lambda i, j: (i, j, 0))],
        out_specs=pl.BlockSpec((1, ts, LANES), lambda i, j: (i, j, 0)),
        out_shape=jax.ShapeDtypeStruct((b, s, LANES), BF16),
        scratch_shapes=[pltpu.VMEM((1, LANES), F32)],
        compiler_params=_cparams(("arbitrary", "arbitrary")),
        name="gate_cumsum",
    )(z)


def _flash_items(items):
    outs = []
    for key, s, v, m_prev, acc_prev in items:
        m_new = jnp.maximum(m_prev, jnp.max(s, axis=1, keepdims=True))
        alpha = jnp.exp2(m_prev - m_new)
        p = jnp.concatenate(
            [jnp.exp2(s[:, j * LANES:(j + 1) * LANES] - m_new[:, :LANES])
             for j in range(s.shape[1] // LANES)], axis=1)
        reps = acc_prev.shape[1] // LANES
        alpha = alpha if reps == 1 else jnp.concatenate([alpha] * reps, axis=1)
        outs.append((key, m_new, alpha * acc_prev + _dot(p.astype(BF16), v)))
    return outs


def _fox_kernel(qt_ref, kt_ref, q_ref, k_ref, v_ref, c_ref, o_ref, m_ref, acc_ref, *, t, sub):
    step = pl.program_id(2)
    pair = pl.program_id(1)
    qi = qt_ref[step]
    ki = kt_ref[step]

    @pl.when(ki == 0)
    def _():
        m_ref[...] = jnp.full_like(m_ref, NEG)
        acc_ref[...] = jnp.zeros_like(acc_ref)

    lane = lax.broadcasted_iota(jnp.int32, (1, LANES), 1)
    low = lane < HEAD_DIM

    def body(diag):
        q = q_ref[0]
        k = k_ref[0]
        v = v_ref[0]
        c = c_ref[0]
        items = []
        for h in range(2):
            own = low if h == 0 else jnp.logical_not(low)
            base = (1 - h) * HEAD_DIM + 2 * pair + h
            ones_at = ((lane == base) | (lane == base + GATE_PIECE_STRIDE)
                       | (lane == base + 2 * GATE_PIECE_STRIDE))
            qh = jnp.where(own, q, jnp.where(ones_at, 1.0, 0.0).astype(BF16))
            kh = jnp.where(own, k, c)
            vh = jnp.where(own, v, jnp.ones_like(v))
            for qs in range(t // sub):
                rows = slice(qs * sub, (qs + 1) * sub)
                nk = (qs + 1) * sub if diag else t
                s = _nt(qh[rows], kh[:nk])
                if diag:
                    r_i = lax.broadcasted_iota(jnp.int32, (sub, sub), 0)
                    c_i = lax.broadcasted_iota(jnp.int32, (sub, sub), 1)
                    last = jnp.where(c_i <= r_i, s[:, nk - sub:], NEG)
                    s = last if nk == sub else jnp.concatenate([s[:, :nk - sub], last], axis=1)
                items.append(((h, rows), s, vh[:nk], m_ref[h, rows, :], acc_ref[h, rows, :]))
        for (h, rows), m_new, acc_new in _flash_items(items):
            acc_ref[h, rows, :] = acc_new
            m_ref[h, rows, :] = m_new

    @pl.when(ki < qi)
    def _():
        body(False)

    @pl.when(ki == qi)
    def _():
        body(True)
        a0 = acc_ref[0]
        a1 = acc_ref[1]
        o0 = a0 / pltpu.roll(a0, HEAD_DIM, 1)
        o1 = a1 / pltpu.roll(a1, HEAD_DIM, 1)
        o_ref[0] = jnp.where(low, o0, o1).astype(o_ref.dtype)


def _fox_attention(qkv, gate_bias, n_heads):
    b, s, _ = qkv.shape
    t = min(FOX_TILE, s)
    nq = s // t
    pairs = n_heads // 2
    qt = np.concatenate([np.full((i + 1,), i, np.int32) for i in range(nq)])
    kt = np.concatenate([np.arange(i + 1, dtype=np.int32) for i in range(nq)])
    grid_spec = pltpu.PrefetchScalarGridSpec(
        num_scalar_prefetch=2,
        grid=(b, pairs, len(qt)),
        in_specs=[
            pl.BlockSpec((1, t, LANES), lambda i, j, u, qt, kt: (i, qt[u], j)),
            pl.BlockSpec((1, t, LANES), lambda i, j, u, qt, kt: (i, kt[u], pairs + j)),
            pl.BlockSpec((1, t, LANES), lambda i, j, u, qt, kt: (i, kt[u], 2 * pairs + j)),
            pl.BlockSpec((1, t, LANES), lambda i, j, u, qt, kt: (i, kt[u], 0)),
        ],
        out_specs=pl.BlockSpec((1, t, LANES), lambda i, j, u, qt, kt: (i, qt[u], j)),
        scratch_shapes=[
            pltpu.VMEM((2, t, LANES), F32),
            pltpu.VMEM((2, t, LANES), F32),
        ],
    )
    return pl.pallas_call(
        functools.partial(_fox_kernel, t=t, sub=min(FOX_SUB, t)),
        grid_spec=grid_spec,
        out_shape=jax.ShapeDtypeStruct((b, s, n_heads * HEAD_DIM), BF16),
        compiler_params=_cparams(("arbitrary", "arbitrary", "arbitrary")),
        name="fox_attention",
    )(jnp.asarray(qt), jnp.asarray(kt), qkv, qkv, qkv, gate_bias)


def _out_ln_kernel(*refs, n_in):
    o_refs = refs[:n_in]
    w_ref, x_ref, g_ref, b_ref, y_ref, yb_ref = refs[n_in:]
    if n_in == 1:
        o = o_refs[0][...]
    else:
        o = o_refs[0][...].astype(F32)
        for r in o_refs[1:]:
            o = o + r[...].astype(F32)
        o = o.astype(BF16)
    z = DEEPNORM_ALPHA * x_ref[...] + _dot(o, w_ref[...])
    y = _layer_norm(z, g_ref[...], b_ref[...])
    y_ref[...] = y
    yb_ref[...] = y.astype(BF16)


def _out_ln(o_list, w_bf16, x2d, g, b):
    n, d = x2d.shape
    k = w_bf16.shape[0]
    tm = min(ROW_TILE, n)
    n_in = len(o_list)
    row = lambda i: (i, 0)
    fixed = lambda i: (0, 0)
    return pl.pallas_call(
        functools.partial(_out_ln_kernel, n_in=n_in),
        grid=(n // tm,),
        in_specs=[pl.BlockSpec((tm, k), row)] * n_in + [
            pl.BlockSpec((k, d), fixed),
            pl.BlockSpec((tm, d), row),
            pl.BlockSpec((1, d), fixed),
            pl.BlockSpec((1, d), fixed),
        ],
        out_specs=[pl.BlockSpec((tm, d), row), pl.BlockSpec((tm, d), row)],
        out_shape=[jax.ShapeDtypeStruct((n, d), F32), jax.ShapeDtypeStruct((n, d), BF16)],
        compiler_params=_cparams(("arbitrary",)),
        name="out_ln",
    )(*o_list, w_bf16, x2d, g, b)


def _router_kernel(x_ref, wt_ref, b_ref, idx_ref, w_ref, rank_ref, cnt_ref, carry_ref, *, tn, n_exp):
    @pl.when(pl.program_id(0) == 0)
    def _():
        carry_ref[...] = jnp.zeros_like(carry_ref)

    xh, xl = _split2(x_ref[...])
    wh, wl = _split2(wt_ref[...])
    logits = _nt(wh, xh) + _nt(wh, xl) + _nt(wl, xh)
    scores = _sigmoid(logits)
    biased = scores + b_ref[...]

    per_grp = n_exp // EXPERT_GROUPS
    blocks, gscore = [], []
    for g in range(EXPERT_GROUPS):
        blk = biased[g * per_grp:(g + 1) * per_grp, :]
        m1 = jnp.max(blk, axis=0, keepdims=True)
        eq = blk == m1
        n_eq = jnp.sum(jnp.where(eq, 1.0, 0.0), axis=0, keepdims=True)
        m2 = jnp.max(jnp.where(eq, -3e38, blk), axis=0, keepdims=True)
        blocks.append(blk)
        gscore.append(m1 + jnp.where(n_eq >= 2.0, m1, m2))
    cand = []
    for g in range(EXPERT_GROUPS):
        beaten = jnp.zeros_like(gscore[g])
        for o in range(EXPERT_GROUPS):
            if o == g:
                continue
            wins = (gscore[o] >= gscore[g]) if o < g else (gscore[o] > gscore[g])
            beaten = beaten + jnp.where(wins, 1.0, 0.0)
        cand.append(jnp.where(beaten < float(EXPERT_TOPK_GROUPS), blocks[g], NEG))
    cand = jnp.concatenate(cand, axis=0)

    erow = lax.broadcasted_iota(jnp.int32, (n_exp, tn), 0).astype(F32)
    hot, sel_idx, sel_score = [], [], []
    member = jnp.zeros((n_exp, tn), F32)
    for _ in range(EXPERT_TOPK):
        m = jnp.max(cand, axis=0, keepdims=True)
        first = jnp.min(jnp.where(cand == m, erow, float(n_exp)), axis=0, keepdims=True)
        onehot = erow == first
        sel_idx.append(first)
        sel_score.append(jnp.sum(jnp.where(onehot, scores, 0.0), axis=0, keepdims=True))
        cand = jnp.where(onehot, -3e38, cand)
        member = member + jnp.where(onehot, 1.0, 0.0)
        hot.append(onehot)
    total = sel_score[0]
    for sc in sel_score[1:]:
        total = total + sc

    trow = lax.broadcasted_iota(jnp.int32, (tn, tn), 0)
    tcol = lax.broadcasted_iota(jnp.int32, (tn, tn), 1)
    before = jnp.where(trow < tcol, 1.0, 0.0).astype(BF16)
    prior = _dot(member.astype(BF16), before) + carry_ref[...]
    ranks = [jnp.sum(jnp.where(hot[k], prior, 0.0), axis=0, keepdims=True) for k in range(EXPERT_TOPK)]

    idx_ref[...] = jnp.concatenate(sel_idx, axis=0).astype(jnp.int32)
    w_ref[...] = jnp.concatenate([sc / total * ROUTED_SCALE for sc in sel_score], axis=0)
    rank_ref[...] = jnp.concatenate(ranks, axis=0).astype(jnp.int32)
    carry_ref[...] = carry_ref[...] + jnp.sum(member, axis=1, keepdims=True)
    cnt_ref[...] = carry_ref[...]


def _route(x2d, w_router_t, b_router):
    n, d = x2d.shape
    n_exp = w_router_t.shape[0]
    tn = min(ROUTER_TILE, n)
    col = lambda i: (0, i)
    return pl.pallas_call(
        functools.partial(_router_kernel, tn=tn, n_exp=n_exp),
        grid=(n // tn,),
        in_specs=[
            pl.BlockSpec((tn, d), lambda i: (i, 0)),
            pl.BlockSpec((n_exp, d), lambda i: (0, 0)),
            pl.BlockSpec((n_exp, 1), lambda i: (0, 0)),
        ],
        out_specs=[
            pl.BlockSpec((EXPERT_TOPK, tn), col),
            pl.BlockSpec((EXPERT_TOPK, tn), col),
            pl.BlockSpec((EXPERT_TOPK, tn), col),
            pl.BlockSpec((n_exp, 1), lambda i: (0, 0)),
        ],
        out_shape=[
            jax.ShapeDtypeStruct((EXPERT_TOPK, n), jnp.int32),
            jax.ShapeDtypeStruct((EXPERT_TOPK, n), F32),
            jax.ShapeDtypeStruct((EXPERT_TOPK, n), jnp.int32),
            jax.ShapeDtypeStruct((n_exp, 1), F32),
        ],
        scratch_shapes=[pltpu.VMEM((n_exp, 1), F32)],
        compiler_params=_cparams(("arbitrary",)),
        name="router",
    )(x2d, w_router_t, b_router)


def _experts_kernel(be_ref, first_ref, nused_ref, x_ref, wg_ref, wu_ref, wd_ref, y_ref,
                    wg_s, wu_s, wd_s):
    blk = pl.program_id(0)

    @pl.when(blk < nused_ref[0])
    def _():
        @pl.when(first_ref[blk] == 1)
        def _():
            wg_s[...] = wg_ref[0, 0].astype(BF16)
            wu_s[...] = wu_ref[0, 0].astype(BF16)
            wd_s[...] = wd_ref[0, 0].astype(BF16)

        x = x_ref[...]
        gate = _dot(x, wg_s[...])
        up = _dot(x, wu_s[...])
        h = gate * _sigmoid(gate) * up
        y_ref[...] = _dot(h.astype(BF16), wd_s[...]).astype(y_ref.dtype)

    @pl.when(blk >= nused_ref[0])
    def _():
        y_ref[...] = jnp.zeros_like(y_ref)


def _experts(xg, blk_expert, blk_first, n_used, w_gate, w_up, w_down, layer):
    n_rows, d = xg.shape
    hdim = w_gate.shape[3]
    tm = EXPERT_ROW_BLOCK
    n_blocks = n_rows // tm
    grid_spec = pltpu.PrefetchScalarGridSpec(
        num_scalar_prefetch=3,
        grid=(n_blocks,),
        in_specs=[
            pl.BlockSpec((tm, d), lambda i, be, fi, nu: (i, 0)),
            pl.BlockSpec((1, 1, d, hdim), lambda i, be, fi, nu: (layer, be[i], 0, 0)),
            pl.BlockSpec((1, 1, d, hdim), lambda i, be, fi, nu: (layer, be[i], 0, 0)),
            pl.BlockSpec((1, 1, hdim, d), lambda i, be, fi, nu: (layer, be[i], 0, 0)),
        ],
        out_specs=pl.BlockSpec((tm, d), lambda i, be, fi, nu: (i, 0)),
        scratch_shapes=[
            pltpu.VMEM((d, hdim), BF16),
            pltpu.VMEM((d, hdim), BF16),
            pltpu.VMEM((hdim, d), BF16),
        ],
    )
    return pl.pallas_call(
        _experts_kernel,
        grid_spec=grid_spec,
        out_shape=jax.ShapeDtypeStruct((n_rows, d), BF16),
        compiler_params=_cparams(("arbitrary",)),
        name="experts",
    )(blk_expert, blk_first, n_used, xg, w_gate, w_up, w_down)


def _moe_tail_kernel(x_ref, r_ref, p_ref, sg_ref, su_ref, sd_ref, pg_ref, pb_ref, pp_ref,
                     g1_ref, b1_ref, g2_ref, b2_ref, y_ref):
    x = x_ref[...]
    xb = x.astype(BF16)
    gate = _dot(xb, sg_ref[...])
    up = _dot(xb, su_ref[...])
    shared = _dot((gate * _sigmoid(gate) * up).astype(BF16), sd_ref[...])
    x2 = _layer_norm(DEEPNORM_ALPHA * x + (r_ref[...] + shared), g1_ref[...], b1_ref[...])
    ple_gate = _sigmoid(_dot(x2.astype(BF16), pg_ref[...]) + pb_ref[...])
    ple = ple_gate * _dot(p_ref[...].astype(BF16), pp_ref[...])
    y_ref[...] = _layer_norm(DEEPNORM_ALPHA * x2 + ple, g2_ref[...], b2_ref[...])


def _moe_tail(x2d, routed, p2d, sg, su, sd, pg, pb, pp, g1, b1, g2, b2):
    n, d = x2d.shape
    tm = min(ROW_TILE, n)
    row = lambda i: (i, 0)
    fixed = lambda i: (0, 0)
    full = lambda a: pl.BlockSpec(a.shape, fixed)
    return pl.pallas_call(
        _moe_tail_kernel,
        grid=(n // tm,),
        in_specs=[
            pl.BlockSpec((tm, d), row),
            pl.BlockSpec((tm, d), row),
            pl.BlockSpec((tm, p2d.shape[1]), row),
            full(sg), full(su), full(sd), full(pg), full(pb), full(pp),
            full(g1), full(b1), full(g2), full(b2),
        ],
        out_specs=pl.BlockSpec((tm, d), row),
        out_shape=jax.ShapeDtypeStruct((n, d), F32),
        compiler_params=_cparams(("arbitrary",)),
        name="moe_tail",
    )(x2d, routed, p2d, sg, su, sd, pg, pb, pp, g1, b1, g2, b2)


def _moe_layer(x1, x1b, p2d, layer, w_router, b_router, w_gate, w_up, w_down, sg, su, sd,
               ple_proj, ple_gate, ple_bias, g1, b1, g2, b2):
    n, d = x1.shape
    n_exp = w_router.shape[1]
    idx, wts, rank, counts = _route(x1, w_router.T, b_router.reshape(n_exp, 1))
    tm = EXPERT_ROW_BLOCK
    counts = counts[:, 0].astype(jnp.int32)
    padded = (counts + tm - 1) // tm * tm
    pad_end = jnp.cumsum(padded)
    pad_start = pad_end - padded
    n_blocks = (n * EXPERT_TOPK + n_exp * (tm - 1)) // tm + 1
    n_rows = n_blocks * tm
    experts = jnp.arange(n_exp, dtype=jnp.int32)
    start_of = jnp.sum(jnp.where(idx[:, :, None] == experts, pad_start, 0), axis=-1)
    pos = start_of + rank
    tok = jnp.broadcast_to(jnp.arange(n, dtype=jnp.int32)[None, :], pos.shape)
    tok_pad = jnp.zeros((n_rows,), jnp.int32).at[pos.reshape(-1)].set(
        tok.reshape(-1), unique_indices=True, mode="promise_in_bounds")
    blk_start = jnp.arange(n_blocks, dtype=jnp.int32) * tm
    blk_expert = jnp.minimum(jnp.sum((pad_end[None, :] <= blk_start[:, None]).astype(jnp.int32), axis=1),
                             n_exp - 1)
    blk_first = jnp.concatenate([jnp.ones((1,), jnp.int32),
                                 (blk_expert[1:] != blk_expert[:-1]).astype(jnp.int32)])
    n_used = (pad_end[-1:] // tm).astype(jnp.int32)
    xg = x1b.at[tok_pad].get(mode="promise_in_bounds")
    y = _experts(xg, blk_expert, blk_first, n_used, w_gate, w_up, w_down, layer)
    routed = jnp.sum(y.at[pos].get(mode="promise_in_bounds").astype(F32) * wts[:, :, None], axis=0)
    return _moe_tail(x1, routed, p2d, sg.astype(BF16), su.astype(BF16), sd.astype(BF16),
                     ple_gate.astype(BF16), ple_bias.reshape(1, d), ple_proj.astype(BF16),
                     g1.reshape(1, d), b1.reshape(1, d), g2.reshape(1, d), b2.reshape(1, d))


def _gelu_tanh(x):
    return 0.5 * x * (1.0 + jnp.tanh(0.7978845608028654 * (x + 0.044715 * (x * x * x))))


def _compress_kernel(c_ref, pe_ref, w1_ref, b1_ref, w2_ref, o_ref, *, nc, half):
    out = jnp.zeros((nc, GROUP_LANES), F32)
    for g in range(NSA_KV_GROUPS):
        c = c_ref[0, 0, g].astype(F32)
        a = _dot((c + pe_ref[0, 0:1, :]).astype(BF16), w1_ref[0, :half, :])
        bm = _dot((c + pe_ref[0, 1:2, :]).astype(BF16), w1_ref[0, half:, :])
        h = _gelu_tanh(a + pltpu.roll(bm, nc - 1, 0) + b1_ref[0])
        out = out + _dot(h.astype(BF16), w2_ref[0, g])
    o_ref[0, 0] = out.astype(o_ref.dtype)


def _compress(chunks, pe_flat, w1, b1, w2_placed):
    b, _, _, nc, half = chunks.shape
    hid = w1.shape[2]
    return pl.pallas_call(
        functools.partial(_compress_kernel, nc=nc, half=half),
        grid=(b, 2),
        in_specs=[
            pl.BlockSpec((1, 1, NSA_KV_GROUPS, nc, half), lambda i, j: (i, j, 0, 0, 0)),
            pl.BlockSpec((1, 2, half), lambda i, j: (j, 0, 0)),
            pl.BlockSpec((1, 2 * half, hid), lambda i, j: (j, 0, 0)),
            pl.BlockSpec((1, 1, hid), lambda i, j: (j, 0, 0)),
            pl.BlockSpec((1, NSA_KV_GROUPS, hid, GROUP_LANES), lambda i, j: (j, 0, 0, 0)),
        ],
        out_specs=pl.BlockSpec((1, 1, nc, GROUP_LANES), lambda i, j: (i, j, 0, 0)),
        out_shape=jax.ShapeDtypeStruct((b, 2, nc, GROUP_LANES), BF16),
        compiler_params=_cparams(("arbitrary", "arbitrary")),
        name="compress",
    )(chunks, pe_flat, w1, b1, w2_placed)


def _group_mask(g):
    lane = lax.broadcasted_iota(jnp.int32, (1, GROUP_LANES), 1)
    return (lane >= g * HEAD_DIM) & (lane < (g + 1) * HEAD_DIM)


def _nsa_cmp_kernel(q_ref, kc_ref, vc_ref, gate_ref, slope_ref, wt_ref, o_ref, sel_ref, psum_ref, score_ref,
                    *, tq, nc, n_slc):
    qi = pl.program_id(1)
    r = pl.program_id(2)
    q0 = qi * tq

    @pl.when(r == 0)
    def _():
        psum_ref[...] = jnp.zeros_like(psum_ref)

    q = q_ref[0]
    kc = kc_ref[0, 0]
    vc = vc_ref[0, 0]
    t_pos = q0 + lax.broadcasted_iota(jnp.int32, (tq, nc), 0)
    cmp_end = lax.broadcasted_iota(jnp.int32, (tq, nc), 1) * CMP_STRIDE + (CMP_BLOCK - 1)
    visible = cmp_end <= t_pos
    end_rel = (lax.broadcasted_iota(jnp.int32, (1, nc), 1) * CMP_STRIDE + (CMP_BLOCK - 1) - q0).astype(F32)
    gates = gate_ref[0, 0]
    out = jnp.zeros((tq, GROUP_LANES), F32)
    psum_prev = [psum_ref[g] for g in range(NSA_KV_GROUPS)]
    psum_new = []
    for g in range(NSA_KV_GROUPS):
        in_grp = _group_mask(g)
        qm = jnp.where(in_grp, q, jnp.zeros_like(q))
        s = _nt(qm, kc) + slope_ref[0, g, :, :nc] * end_rel
        s = jnp.where(visible, s, NEG)
        m = jnp.max(s, axis=1, keepdims=True)
        p = jnp.where(visible, jnp.exp2(s - m), 0.0)
        o_g = _dot(p.astype(BF16), jnp.where(in_grp, vc, jnp.ones_like(vc)))
        other = (1 - g // 2) * LANES
        l = o_g[:, other:other + LANES]
        inv = jnp.where(l > 0.0, 1.0 / l, 0.0)
        inv2 = jnp.concatenate([inv, inv], axis=1)
        psum_new.append(psum_prev[g] + p * inv2[:, :nc] if nc == GROUP_LANES else
                        psum_prev[g] + p * inv[:, :1])
        out = out + jnp.where(in_grp, o_g * (inv2 * gates[:, g:g + 1]), 0.0)
    for g in range(NSA_KV_GROUPS):
        psum_ref[g] = psum_new[g]
    o_ref[0] = out.astype(o_ref.dtype)

    @pl.when(r == NSA_HEADS_PER_GROUP - 1)
    def _():
        blk = lax.broadcasted_iota(jnp.int32, (n_slc, tq), 0)
        t_row = q0 + lax.broadcasted_iota(jnp.int32, (n_slc, tq), 1)
        cur = t_row // SLC_BLOCK
        forced = (blk == 0) | (blk == cur) | (blk == cur - 1)
        vis = blk * SLC_BLOCK <= t_row
        wt = wt_ref[...]
        for g in range(NSA_KV_GROUPS):
            hi, mid, lo = _split3(psum_ref[g])
            imp = _nt(wt, hi) + _nt(wt, mid) + _nt(wt, lo)
            score = jnp.where(forced, 1e9, jnp.where(vis, imp, NEG))
            score_ref[...] = score

            def count(j, beaten):
                vj = score_ref[pl.ds(j, 1), :]
                ge = jnp.where(vj >= score, 1.0, 0.0)
                gt = jnp.where(vj > score, 1.0, 0.0)
                return beaten + jnp.where(blk > j, ge, gt)

            beaten = lax.fori_loop(0, n_slc, count, jnp.zeros((n_slc, tq), F32))
            keep = (beaten < float(min(SLC_TOPK, n_slc))) & (score > 0.5 * NEG)
            sel_ref[0, g] = jnp.where(keep, 0.0, NEG)


def _nsa_compressed(q, cmp_kv, gates_c, slopes_rep, w_sel_t):
    b, s, _ = q.shape
    nc = cmp_kv.shape[2]
    n_slc = s // SLC_BLOCK
    tq = min(NSA_Q_TILE, s)
    rr = NSA_HEADS_PER_GROUP
    return pl.pallas_call(
        functools.partial(_nsa_cmp_kernel, tq=tq, nc=nc, n_slc=n_slc),
        grid=(b, s // tq, rr),
        in_specs=[
            pl.BlockSpec((1, tq, GROUP_LANES), lambda i, j, r: (i, j, r)),
            pl.BlockSpec((1, 1, nc, GROUP_LANES), lambda i, j, r: (i, 0, 0, 0)),
            pl.BlockSpec((1, 1, nc, GROUP_LANES), lambda i, j, r: (i, 1, 0, 0)),
            pl.BlockSpec((1, 1, tq, NSA_KV_GROUPS), lambda i, j, r: (i, r, j, 0)),
            pl.BlockSpec((1, NSA_KV_GROUPS, 1, slopes_rep.shape[3]), lambda i, j, r: (r, 0, 0, 0)),
            pl.BlockSpec((n_slc, nc), lambda i, j, r: (0, 0)),
        ],
        out_specs=[
            pl.BlockSpec((1, tq, GROUP_LANES), lambda i, j, r: (i, j, r)),
            pl.BlockSpec((1, NSA_KV_GROUPS, n_slc, tq), lambda i, j, r: (i, 0, 0, j)),
        ],
        out_shape=[
            jax.ShapeDtypeStruct((b, s, rr * GROUP_LANES), BF16),
            jax.ShapeDtypeStruct((b, NSA_KV_GROUPS, n_slc, s), F32),
        ],
        scratch_shapes=[pltpu.VMEM((NSA_KV_GROUPS, tq, nc), F32), pltpu.VMEM((n_slc, tq), F32)],
        compiler_params=_cparams(("arbitrary", "arbitrary", "arbitrary")),
        name="nsa_compressed",
    )(q, cmp_kv, cmp_kv, gates_c, slopes_rep, w_sel_t)


def _nsa_attn_kernel(qt_ref, kt_ref, fl_ref, *refs, tq, tk, use_sel):
    if use_sel:
        q_ref, k_ref, v_ref, gate_ref, slope_ref, sel_ref, et_ref, o_ref, m_ref, l_ref, acc_ref = refs
    else:
        q_ref, k_ref, v_ref, gate_ref, slope_ref, o_ref, m_ref, l_ref, acc_ref = refs
    step = pl.program_id(2)
    qi = qt_ref[step]
    ki = kt_ref[step]
    flags = fl_ref[step]

    @pl.when((flags & 1) != 0)
    def _():
        m_ref[...] = jnp.full_like(m_ref, NEG)
        l_ref[...] = jnp.zeros_like(l_ref)
        acc_ref[...] = jnp.zeros_like(acc_ref)

    def body(masked):
        q = q_ref[0]
        k = k_ref[0]
        v = v_ref[0]
        key_rel = (ki * tk - qi * tq + lax.broadcasted_iota(jnp.int32, (1, tk), 1)).astype(F32)
        if masked:
            dist = (qi * tq + lax.broadcasted_iota(jnp.int32, (tq, tk), 0)) - (
                ki * tk + lax.broadcasted_iota(jnp.int32, (tq, tk), 1))
            ok = dist >= 0
            if not use_sel:
                ok = ok & (dist < WINDOW)
        for g in range(NSA_KV_GROUPS):
            qm = jnp.where(_group_mask(g), q, jnp.zeros_like(q))
            s = _nt(qm, k) + slope_ref[0, g, :, :tk] * key_rel
            if use_sel:
                s = s + _dot(sel_ref[0, g], et_ref[...])
            if masked:
                s = jnp.where(ok, s, NEG)
            _flash_update(s, v, m_ref, l_ref, acc_ref, g)

    @pl.when((flags & 4) == 0)
    def _():
        body(False)

    @pl.when((flags & 4) != 0)
    def _():
        body(True)

    @pl.when((flags & 2) != 0)
    def _():
        gates = gate_ref[0, 0]
        out = jnp.zeros((tq, GROUP_LANES), F32)
        for g in range(NSA_KV_GROUPS):
            scale = gates[:, g:g + 1] / l_ref[g]
            out = out + jnp.where(_group_mask(g), acc_ref[g] * scale, 0.0)
        o_ref[0] = out.astype(o_ref.dtype)


def _nsa_tables(s, tq, tk, window):
    qt, kt, fl = [], [], []
    for qi in range(s // tq):
        lo_t, hi_t = qi * tq, qi * tq + tq - 1
        first_key = 0 if window is None else max(0, lo_t - window + 1)
        tiles = list(range(first_key // tk, hi_t // tk + 1))
        for n, ki in enumerate(tiles):
            k_lo, k_hi = ki * tk, ki * tk + tk - 1
            masked = k_hi > lo_t or (window is not None and hi_t - k_lo >= window)
            qt.append(qi)
            kt.append(ki)
            fl.append((1 if n == 0 else 0) | (2 if n == len(tiles) - 1 else 0) | (4 if masked else 0))
    return (jnp.asarray(np.asarray(qt, np.int32)), jnp.asarray(np.asarray(kt, np.int32)),
            jnp.asarray(np.asarray(fl, np.int32)))


def _nsa_attention(q, kv, k_col, v_col, gates, slopes_rep, sel_bias=None, e_t=None):
    b, s, _ = q.shape
    use_sel = sel_bias is not None
    tq = min(NSA_Q_TILE, s)
    tk = min(NSA_K_TILE if use_sel else NSA_Q_TILE, s)
    qt, kt, fl = _nsa_tables(s, tq, tk, None if use_sel else WINDOW)
    rr = NSA_HEADS_PER_GROUP
    in_specs = [
        pl.BlockSpec((1, tq, GROUP_LANES), lambda i, r, u, qt, kt, fl: (i, qt[u], r)),
        pl.BlockSpec((1, tk, GROUP_LANES), lambda i, r, u, qt, kt, fl: (i, kt[u], k_col)),
        pl.BlockSpec((1, tk, GROUP_LANES), lambda i, r, u, qt, kt, fl: (i, kt[u], v_col)),
        pl.BlockSpec((1, 1, tq, NSA_KV_GROUPS), lambda i, r, u, qt, kt, fl: (i, r, qt[u], 0)),
        pl.BlockSpec((1, NSA_KV_GROUPS, 1, slopes_rep.shape[3]), lambda i, r, u, qt, kt, fl: (r, 0, 0, 0)),
    ]
    args = [q, kv, kv, gates, slopes_rep]
    if use_sel:
        n_slc = s // SLC_BLOCK
        in_specs += [
            pl.BlockSpec((1, NSA_KV_GROUPS, tq, n_slc), lambda i, r, u, qt, kt, fl: (i, 0, qt[u], 0)),
            pl.BlockSpec((n_slc, tk), lambda i, r, u, qt, kt, fl: (0, kt[u])),
        ]
        args += [sel_bias, e_t]
    grid_spec = pltpu.PrefetchScalarGridSpec(
        num_scalar_prefetch=3,
        grid=(b, rr, int(qt.shape[0])),
        in_specs=in_specs,
        out_specs=pl.BlockSpec((1, tq, GROUP_LANES), lambda i, r, u, qt, kt, fl: (i, qt[u], r)),
        scratch_shapes=[
            pltpu.VMEM((NSA_KV_GROUPS, tq, 1), F32),
            pltpu.VMEM((NSA_KV_GROUPS, tq, 1), F32),
            pltpu.VMEM((NSA_KV_GROUPS, tq, GROUP_LANES), F32),
        ],
    )
    return pl.pallas_call(
        functools.partial(_nsa_attn_kernel, tq=tq, tk=tk, use_sel=use_sel),
        grid_spec=grid_spec,
        out_shape=jax.ShapeDtypeStruct((b, s, rr * GROUP_LANES), BF16),
        compiler_params=_cparams(("arbitrary", "arbitrary", "arbitrary")),
        name="nsa_selected" if use_sel else "nsa_window",
    )(qt, kt, fl, *args)


AUG_POS_LANE = 64
MODE_FULL, MODE_LOWER, MODE_UPPER = 0, 1, 2


def _nsa_flash_kernel(qt_ref, kt_ref, fl_ref, *refs, t, sub, use_sel):
    if use_sel:
        q_ref, k_ref, v_ref, ka_ref, gate_ref, slope_ref, sel_ref, o_ref, m_ref, acc_ref = refs
    else:
        q_ref, k_ref, v_ref, ka_ref, gate_ref, slope_ref, o_ref, m_ref, acc_ref = refs
    step = pl.program_id(2)
    flags = fl_ref[step]
    mode = flags >> 2

    @pl.when((flags & 1) != 0)
    def _():
        m_ref[...] = jnp.full_like(m_ref, NEG)
        acc_ref[...] = jnp.zeros_like(acc_ref)

    lane = lax.broadcasted_iota(jnp.int32, (1, LANES), 1)

    def body(mode_static):
        q = q_ref[0]
        k = k_ref[0]
        v = v_ref[0]
        ka = ka_ref[...]
        k_aug = (jnp.concatenate([k[:, :LANES], ka], axis=1), jnp.concatenate([ka, k[:, LANES:]], axis=1))
        items = []
        for g in range(NSA_KV_GROUPS):
            half = g // 2
            in_grp = (lane >= (g % 2) * HEAD_DIM) & (lane < (g % 2 + 1) * HEAD_DIM)
            q_own = jnp.where(in_grp, q[:, half * LANES:(half + 1) * LANES], jnp.zeros((t, LANES), BF16))
            q_extra = jnp.broadcast_to(slope_ref[0, g], (t, LANES))
            if use_sel:
                q_extra = jnp.where(lane < AUG_POS_LANE, sel_ref[0, g], q_extra)
            qg = jnp.concatenate([q_own, q_extra] if half == 0 else [q_extra, q_own], axis=1)
            vg = jnp.where(_group_mask(g), v, jnp.ones_like(v))
            for qs in range(t // sub):
                rows = slice(qs * sub, (qs + 1) * sub)
                if mode_static == MODE_FULL:
                    keys = slice(0, t)
                elif mode_static == MODE_LOWER:
                    keys = slice(0, (qs + 1) * sub)
                else:
                    keys = slice(qs * sub, t)
                s = _nt(qg[rows], k_aug[half][keys])
                if mode_static != MODE_FULL:
                    r_i = lax.broadcasted_iota(jnp.int32, (sub, sub), 0)
                    c_i = lax.broadcasted_iota(jnp.int32, (sub, sub), 1)
                    if mode_static == MODE_LOWER:
                        edge = jnp.where(c_i <= r_i, s[:, -sub:], NEG)
                        s = edge if s.shape[1] == sub else jnp.concatenate([s[:, :-sub], edge], axis=1)
                    else:
                        edge = jnp.where(c_i > r_i, s[:, :sub], NEG)
                        s = edge if s.shape[1] == sub else jnp.concatenate([edge, s[:, sub:]], axis=1)
                items.append(((g, rows), s, vg[keys], m_ref[g, rows, :], acc_ref[g, rows, :]))
        for (g, rows), m_new, acc_new in _flash_items(items):
            acc_ref[g, rows, :] = acc_new
            m_ref[g, rows, :] = m_new

    for mode_static in ((MODE_FULL, MODE_LOWER) if use_sel else (MODE_LOWER, MODE_UPPER)):
        @pl.when(mode == mode_static)
        def _(mode_static=mode_static):
            body(mode_static)

    @pl.when((flags & 2) != 0)
    def _():
        gates = gate_ref[0, 0]
        out = jnp.zeros((t, GROUP_LANES), F32)
        for g in range(NSA_KV_GROUPS):
            acc = acc_ref[g]
            denom = pltpu.roll(acc, LANES, 1)
            out = out + jnp.where(_group_mask(g), acc * (gates[:, g:g + 1] / denom), 0.0)
        o_ref[0] = out.astype(o_ref.dtype)


def _nsa_flash_tables(s, t, windowed):
    qt, kt, fl = [], [], []
    for qi in range(s // t):
        if windowed:
            tiles = [(qi - 1, MODE_UPPER)] if qi > 0 else []
            tiles.append((qi, MODE_LOWER))
        else:
            tiles = [(ki, MODE_FULL) for ki in range(qi)] + [(qi, MODE_LOWER)]
        for n, (ki, mode) in enumerate(tiles):
            qt.append(qi)
            kt.append(ki)
            fl.append((1 if n == 0 else 0) | (2 if n == len(tiles) - 1 else 0) | (mode << 2))
    return tuple(jnp.asarray(np.asarray(a, np.int32)) for a in (qt, kt, fl))


def _nsa_flash(q, kv, k_col, v_col, key_aug, gates, slope_rows, sel_q=None):
    b, s, _ = q.shape
    use_sel = sel_q is not None
    t = min(NSA_TILE, s)
    assert use_sel or t == WINDOW, "the window branch assumes tile == WINDOW"
    qt, kt, fl = _nsa_flash_tables(s, t, not use_sel)
    rr = NSA_HEADS_PER_GROUP
    in_specs = [
        pl.BlockSpec((1, t, GROUP_LANES), lambda i, r, u, qt, kt, fl: (i, qt[u], r)),
        pl.BlockSpec((1, t, GROUP_LANES), lambda i, r, u, qt, kt, fl: (i, kt[u], k_col)),
        pl.BlockSpec((1, t, GROUP_LANES), lambda i, r, u, qt, kt, fl: (i, kt[u], v_col)),
        pl.BlockSpec((t, LANES), lambda i, r, u, qt, kt, fl: (kt[u], 0)),
        pl.BlockSpec((1, 1, t, NSA_KV_GROUPS), lambda i, r, u, qt, kt, fl: (i, r, qt[u], 0)),
        pl.BlockSpec((1, NSA_KV_GROUPS, 1, LANES), lambda i, r, u, qt, kt, fl: (r, 0, 0, 0)),
    ]
    args = [q, kv, kv, key_aug, gates, slope_rows]
    if use_sel:
        in_specs.append(pl.BlockSpec((1, NSA_KV_GROUPS, t, LANES), lambda i, r, u, qt, kt, fl: (i, 0, qt[u], 0)))
        args.append(sel_q)
    grid_spec = pltpu.PrefetchScalarGridSpec(
        num_scalar_prefetch=3,
        grid=(b, rr, int(qt.shape[0])),
        in_specs=in_specs,
        out_specs=pl.BlockSpec((1, t, GROUP_LANES), lambda i, r, u, qt, kt, fl: (i, qt[u], r)),
        scratch_shapes=[
            pltpu.VMEM((NSA_KV_GROUPS, t, LANES), F32),
            pltpu.VMEM((NSA_KV_GROUPS, t, GROUP_LANES), F32),
        ],
    )
    return pl.pallas_call(
        functools.partial(_nsa_flash_kernel, t=t, sub=min(NSA_SUB, t), use_sel=use_sel),
        grid_spec=grid_spec,
        out_shape=jax.ShapeDtypeStruct((b, s, rr * GROUP_LANES), BF16),
        compiler_params=_cparams(("arbitrary", "arbitrary", "arbitrary")),
        name="nsa_selected" if use_sel else "nsa_window",
    )(qt, kt, fl, *args)


def _nsa_key_aug(s):
    pos = np.arange(s)
    aug = np.zeros((s, LANES), np.float32)
    aug[pos, pos // SLC_BLOCK] = 1.0
    aug[:, AUG_POS_LANE:AUG_POS_LANE + 3] = (pos // 64)[:, None]
    aug[:, AUG_POS_LANE + 3:AUG_POS_LANE + 6] = (pos % 64)[:, None]
    return jnp.asarray(aug, BF16)


def _nsa_slope_rows(slopes_l2):
    rr, gg = slopes_l2.shape
    hi, mid, lo = _split3(jnp.asarray(slopes_l2, F32))
    pieces = jnp.stack([hi, mid, lo], axis=-1).astype(F32)
    rows = jnp.zeros((rr, gg, 1, LANES), F32)
    rows = rows.at[:, :, 0, AUG_POS_LANE:AUG_POS_LANE + 3].set(pieces * 64.0)
    rows = rows.at[:, :, 0, AUG_POS_LANE + 3:AUG_POS_LANE + 6].set(pieces)
    return rows.astype(BF16)


def _pad_lanes(a, width=LANES):
    return jnp.pad(a, ((0, 0), (0, width - a.shape[1])))


def _fox_layer(x2d, b, s, w_in, b_f, w_out, ln_g, ln_b):
    d = x2d.shape[1]
    attn = w_out.shape[0]
    n_heads = attn // HEAD_DIM
    scale = HEAD_DIM ** -0.5 * LOG2E
    col_scale = jnp.concatenate([jnp.full((attn,), scale, F32), jnp.ones((2 * attn,), F32)])[None, :]
    qkv, z = _project(x2d, w_in[:, :3 * attn].astype(BF16), col_scale,
                      _pad_lanes(w_in[:, 3 * attn:]), _pad_lanes(b_f[None, :]), False)
    gate_bias = _gate_cumsum(z.reshape(b, s, LANES), n_heads)
    o = _fox_attention(qkv.reshape(b, s, 3 * attn), gate_bias, n_heads)
    return _out_ln([o.reshape(b * s, attn)], w_out.astype(BF16), x2d,
                   ln_g.reshape(1, d), ln_b.reshape(1, d))


def _selection_weights_t(nc, n_slc):
    cs = np.arange(nc)[:, None] * CMP_STRIDE
    ce = cs + CMP_BLOCK
    ss = np.arange(n_slc)[None, :] * SLC_BLOCK
    se = ss + SLC_BLOCK
    w = np.clip(np.minimum(ce, se) - np.maximum(cs, ss), 0, None) / CMP_STRIDE
    w[nc - 1, :] = 0.0
    return jnp.asarray(w.T, BF16)


def _nsa_layer(x2d, b, s, w_kv, cmp_pe, cmp_w1, cmp_b1, cmp_w2, w_q, b_g, w_out, ln_g, ln_b):
    d = x2d.shape[1]
    gg, rr = NSA_KV_GROUPS, NSA_HEADS_PER_GROUP
    attn = gg * rr * HEAD_DIM
    n_heads = gg * rr
    wq = w_q[:, :attn].reshape(d, gg, rr, HEAD_DIM).transpose(0, 2, 1, 3).reshape(d, attn)
    wgate = w_q[:, attn:].reshape(d, gg, rr, 3).transpose(0, 3, 2, 1).reshape(d, 3 * n_heads)
    bgate = b_g.reshape(gg, rr, 3).transpose(2, 1, 0).reshape(1, 3 * n_heads)
    wo = w_out.reshape(gg, rr, HEAD_DIM, d).transpose(1, 0, 2, 3).reshape(attn, d)
    w_all = jnp.concatenate([wq, w_kv], axis=1).astype(BF16)
    col_scale = jnp.concatenate([jnp.full((attn,), HEAD_DIM ** -0.5 * LOG2E, F32),
                                 jnp.ones((w_kv.shape[1],), F32)])[None, :]
    qkv, gates = _project(x2d, w_all, col_scale, _pad_lanes(wgate), _pad_lanes(bgate), True)
    width = qkv.shape[1]
    qkv = qkv.reshape(b, s, width)
    gates = gates[:, :3 * n_heads].reshape(b, s, 3, rr, gg).transpose(2, 0, 3, 1, 4)

    nc = s // CMP_STRIDE
    half = CMP_STRIDE * HEAD_DIM
    raw = qkv[:, :, attn:attn + 2 * GROUP_LANES]
    chunks = raw.reshape(b, nc, CMP_STRIDE, 2, gg, HEAD_DIM).transpose(0, 3, 4, 1, 2, 5)
    chunks = chunks.reshape(b, 2, gg, nc, half)
    pe_flat = cmp_pe.reshape(2, 2, half)
    hid = cmp_w1.shape[2]
    w2_placed = jnp.zeros((2, gg, hid, GROUP_LANES), F32)
    for g in range(gg):
        w2_placed = w2_placed.at[:, g, :, g * HEAD_DIM:(g + 1) * HEAD_DIM].set(cmp_w2)
    cmp_kv = _compress(chunks, pe_flat, cmp_w1.astype(BF16), cmp_b1.reshape(2, 1, hid),
                       w2_placed.astype(BF16))

    slopes_l2 = (2.0 ** (-8.0 * np.arange(1, n_heads + 1) / n_heads) * LOG2E).reshape(gg, rr).T
    slopes_rep = jnp.asarray(np.broadcast_to(
        slopes_l2[:, :, None, None], (rr, gg, 1, nc)).astype(np.float32))
    n_slc = s // SLC_BLOCK
    assert n_slc <= AUG_POS_LANE, "selection flags must fit below the position lanes"
    o_c, sel_t = _nsa_compressed(qkv, cmp_kv, gates[0], slopes_rep, _selection_weights_t(nc, n_slc))
    sel_q = jnp.pad(sel_t.transpose(0, 1, 3, 2).astype(BF16),
                    ((0, 0), (0, 0), (0, 0), (0, LANES - n_slc)))
    key_aug = _nsa_key_aug(s)
    slope_rows = _nsa_slope_rows(slopes_l2)
    base = attn // GROUP_LANES
    o_s = _nsa_flash(qkv, qkv, base + 2, base + 3, key_aug, gates[1], slope_rows, sel_q)
    o_w = _nsa_flash(qkv, qkv, base + 4, base + 5, key_aug, gates[2], slope_rows)
    n = b * s
    return _out_ln([o_c.reshape(n, attn), o_s.reshape(n, attn), o_w.reshape(n, attn)],
                   wo.astype(BF16), x2d, ln_g.reshape(1, d), ln_b.reshape(1, d))


def kernel(x, p, fox_w_in, fox_b_f, fox_w_out, nsa_w_kv, cmp_pe, cmp_w1, cmp_b1, cmp_w2, nsa_w_q, nsa_b_g, nsa_w_out, ln_g, ln_b, moe_w_router, moe_b_router, moe_w_gate, moe_w_up, moe_w_down, shared_w_gate, shared_w_up, shared_w_down, ple_w_proj, ple_w_gate, ple_b_gate):
    b, s, d = x.shape
    n = b * s
    depth = p.shape[0]
    n_a = depth // 2
    h = x.reshape(n, d)
    for i in range(depth):
        if i < n_a:
            h1, h1b = _fox_layer(h, b, s, fox_w_in[i], fox_b_f[i], fox_w_out[i], ln_g[i, 0], ln_b[i, 0])
        else:
            j = i - n_a
            h1, h1b = _nsa_layer(h, b, s, nsa_w_kv, cmp_pe, cmp_w1, cmp_b1, cmp_w2,
                                 nsa_w_q[j], nsa_b_g[j], nsa_w_out[j], ln_g[i, 0], ln_b[i, 0])
        h = _moe_layer(h1, h1b, p[i].reshape(n, -1), i, moe_w_router[i], moe_b_router[i],
                       moe_w_gate, moe_w_up, moe_w_down,
                       shared_w_gate[i], shared_w_up[i], shared_w_down[i],
                       ple_w_proj[i], ple_w_gate[i], ple_b_gate[i],
                       ln_g[i, 1], ln_b[i, 1], ln_g[i, 2], ln_b[i, 2])
    return h.reshape(b, s, d)
```

```python
import functools

import numpy as np
import jax
import jax.numpy as jnp
from jax import lax
from jax.experimental import pallas as pl
from jax.experimental.pallas import tpu as pltpu

F32 = jnp.float32
BF16 = jnp.bfloat16
NEG = -1e30

HEAD_DIM = 64
LANES = 128
NSA_KV_GROUPS = 4
NSA_HEADS_PER_GROUP = 4
GROUP_LANES = NSA_KV_GROUPS * HEAD_DIM
CMP_BLOCK = 32
CMP_STRIDE = 16
SLC_BLOCK = 64
SLC_TOPK = 16
WINDOW = 512
EXPERT_TOPK = 8
EXPERT_GROUPS = 8
EXPERT_TOPK_GROUPS = 4
ROUTED_SCALE = 2.5
LN_EPS = 1e-5
DEPTH = 2
DEEPNORM_ALPHA = (2.0 * DEPTH) ** 0.25
VMEM_LIMIT = 52 * 1024 * 1024

ROW_TILE = 512
FOX_TILE = 512
FOX_SUB = 128
NSA_Q_TILE = 256
NSA_TILE = 512
NSA_SUB = 256
ROUTER_TILE = 512
EXPERT_ROW_BLOCK = 512


def _cparams(sem):
    return pltpu.CompilerParams(dimension_semantics=sem, vmem_limit_bytes=VMEM_LIMIT)


def _nt(a, b):
    return lax.dot_general(a, b, (((1,), (1,)), ((), ())), preferred_element_type=F32)


def _dot(a, b):
    return jnp.dot(a, b, preferred_element_type=F32)


def _split2(a):
    hi = a.astype(BF16)
    lo = (a - hi.astype(F32)).astype(BF16)
    return hi, lo


def _split3(a):
    hi = a.astype(BF16)
    r = a - hi.astype(F32)
    mid = r.astype(BF16)
    lo = (r - mid.astype(F32)).astype(BF16)
    return hi, mid, lo


def _sigmoid(x):
    return 1.0 / (1.0 + jnp.exp(-x))


def _layer_norm(z, g, b):
    mu = jnp.mean(z, axis=-1, keepdims=True)
    zc = z - mu
    var = jnp.mean(zc * zc, axis=-1, keepdims=True)
    return zc * lax.rsqrt(var + LN_EPS) * g + b


def _proj_kernel(x_ref, w_ref, cs_ref, wg_ref, bg_ref, o_ref, g_ref, *, tn, gate_sigmoid):
    x = x_ref[...]
    xh = x.astype(BF16)
    m_out = w_ref.shape[1]
    for j in range(m_out // tn):
        sl = slice(j * tn, (j + 1) * tn)
        y = _dot(xh, w_ref[:, sl])
        o_ref[:, sl] = (y * cs_ref[:, sl]).astype(o_ref.dtype)
    xl = (x - xh.astype(F32)).astype(BF16)
    wh, wl = _split2(wg_ref[...])
    g = _dot(xh, wh) + _dot(xh, wl) + _dot(xl, wh) + bg_ref[...]
    if gate_sigmoid:
        g = _sigmoid(g)
    g_ref[...] = g


def _project(x2d, w_bf16, col_scale, w_gate, b_gate, gate_sigmoid):
    n, k = x2d.shape
    m_out = w_bf16.shape[1]
    tm = min(ROW_TILE, n)
    return pl.pallas_call(
        functools.partial(_proj_kernel, tn=512, gate_sigmoid=gate_sigmoid),
        grid=(n // tm,),
        in_specs=[
            pl.BlockSpec((tm, k), lambda i: (i, 0)),
            pl.BlockSpec((k, m_out), lambda i: (0, 0)),
            pl.BlockSpec((1, m_out), lambda i: (0, 0)),
            pl.BlockSpec((k, LANES), lambda i: (0, 0)),
            pl.BlockSpec((1, LANES), lambda i: (0, 0)),
        ],
        out_specs=[
            pl.BlockSpec((tm, m_out), lambda i: (i, 0)),
            pl.BlockSpec((tm, LANES), lambda i: (i, 0)),
        ],
        out_shape=[
            jax.ShapeDtypeStruct((n, m_out), BF16),
            jax.ShapeDtypeStruct((n, LANES), F32),
        ],
        compiler_params=_cparams(("arbitrary",)),
        name="project",
    )(x2d, w_bf16, col_scale, w_gate, b_gate)


LOG2E = 1.4426950408889634
GATE_PIECE_STRIDE = 16


def _gate_cumsum_kernel(z_ref, o_ref, carry_ref, *, ts, n_heads):
    @pl.when(pl.program_id(1) == 0)
    def _():
        carry_ref[...] = jnp.zeros_like(carry_ref)

    z = z_ref[0]
    log_f = jnp.minimum(z, 0.0) - jnp.log(1.0 + jnp.exp(-jnp.abs(z)))
    row = lax.broadcasted_iota(jnp.int32, (ts, ts), 0)
    col = lax.broadcasted_iota(jnp.int32, (ts, ts), 1)
    tri = jnp.where(col <= row, 1.0, 0.0).astype(BF16)
    hi, mid, lo = _split3(log_f)
    cs = _dot(tri, hi) + _dot(tri, mid) + _dot(tri, lo) + carry_ref[...]
    carry_ref[...] = cs[ts - 1:ts, :]
    lane = lax.broadcasted_iota(jnp.int32, (1, LANES), 1)
    bias = jnp.where(lane < n_heads, cs * (-LOG2E), 0.0)
    hi, mid, lo = _split3(bias)
    pieces = (hi.astype(F32) + pltpu.roll(mid.astype(F32), GATE_PIECE_STRIDE, 1)
              + pltpu.roll(lo.astype(F32), 2 * GATE_PIECE_STRIDE, 1))
    o_ref[0] = (pieces + pltpu.roll(pieces, HEAD_DIM, 1)).astype(o_ref.dtype)


def _gate_cumsum(z, n_heads):
    b, s, _ = z.shape
    ts = min(256, s)
    return pl.pallas_call(
        functools.partial(_gate_cumsum_kernel, ts=ts, n_heads=n_heads),
        grid=(b, s // ts),
        in_specs=[pl.BlockSpec((1, ts, LANES), lambda i, j: (i, j, 0))],
        out_specs=pl.BlockSpec((1, ts, LANES), lambda i, j: (i, j, 0)),
        out_shape=jax.ShapeDtypeStruct((b, s, LANES), BF16),
        scratch_shapes=[pltpu.VMEM((1, LANES), F32)],
        compiler_params=_cparams(("arbitrary", "arbitrary")),
        name="gate_cumsum",
    )(z)


def _flash_items(items):
    outs = []
    for key, s, v, m_prev, acc_prev in items:
        m_new = jnp.maximum(m_prev, jnp.max(s, axis=1, keepdims=True))
        alpha = jnp.exp2(m_prev - m_new)
        p = jnp.concatenate(
            [jnp.exp2(s[:, j * LANES:(j + 1) * LANES] - m_new[:, :LANES])
             for j in range(s.shape[1] // LANES)], axis=1)
        reps = acc_prev.shape[1] // LANES
        alpha = alpha if reps == 1 else jnp.concatenate([alpha] * reps, axis=1)
        outs.append((key, m_new, alpha * acc_prev + _dot(p.astype(BF16), v)))
    return outs


def _fox_kernel(qt_ref, kt_ref, q_ref, k_ref, v_ref, c_ref, o_ref, m_ref, acc_ref, *, t, sub):
    step = pl.program_id(2)
    pair = pl.program_id(1)
    qi = qt_ref[step]
    ki = kt_ref[step]

    @pl.when(ki == 0)
    def _():
        m_ref[...] = jnp.full_like(m_ref, NEG)
        acc_ref[...] = jnp.zeros_like(acc_ref)

    lane = lax.broadcasted_iota(jnp.int32, (1, LANES), 1)
    low = lane < HEAD_DIM

    def body(diag):
        q = q_ref[0]
        k = k_ref[0]
        v = v_ref[0]
        c = c_ref[0]
        items = []
        for h in range(2):
            own = low if h == 0 else jnp.logical_not(low)
            base = (1 - h) * HEAD_DIM + 2 * pair + h
            ones_at = ((lane == base) | (lane == base + GATE_PIECE_STRIDE)
                       | (lane == base + 2 * GATE_PIECE_STRIDE))
            qh = jnp.where(own, q, jnp.where(ones_at, 1.0, 0.0).astype(BF16))
            kh = jnp.where(own, k, c)
            vh = jnp.where(own, v, jnp.ones_like(v))
            for qs in range(t // sub):
                rows = slice(qs * sub, (qs + 1) * sub)
                nk = (qs + 1) * sub if diag else t
                s = _nt(qh[rows], kh[:nk])
                if diag:
                    r_i = lax.broadcasted_iota(jnp.int32, (sub, sub), 0)
                    c_i = lax.broadcasted_iota(jnp.int32, (sub, sub), 1)
                    last = jnp.where(c_i <= r_i, s[:, nk - sub:], NEG)
                    s = last if nk == sub else jnp.concatenate([s[:, :nk - sub], last], axis=1)
                items.append(((h, rows), s, vh[:nk], m_ref[h, rows, :], acc_ref[h, rows, :]))
        for (h, rows), m_new, acc_new in _flash_items(items):
            acc_ref[h, rows, :] = acc_new
            m_ref[h, rows, :] = m_new

    @pl.when(ki < qi)
    def _():
        body(False)

    @pl.when(ki == qi)
    def _():
        body(True)
        a0 = acc_ref[0]
        a1 = acc_ref[1]
        o0 = a0 / pltpu.roll(a0, HEAD_DIM, 1)
        o1 = a1 / pltpu.roll(a1, HEAD_DIM, 1)
        o_ref[0] = jnp.where(low, o0, o1).astype(o_ref.dtype)


def _fox_attention(qkv, gate_bias, n_heads):
    b, s, _ = qkv.shape
    t = min(FOX_TILE, s)
    nq = s // t
    pairs = n_heads // 2
    qt = np.concatenate([np.full((i + 1,), i, np.int32) for i in range(nq)])
    kt = np.concatenate([np.arange(i + 1, dtype=np.int32) for i in range(nq)])
    grid_spec = pltpu.PrefetchScalarGridSpec(
        num_scalar_prefetch=2,
        grid=(b, pairs, len(qt)),
        in_specs=[
            pl.BlockSpec((1, t, LANES), lambda i, j, u, qt, kt: (i, qt[u], j)),
            pl.BlockSpec((1, t, LANES), lambda i, j, u, qt, kt: (i, kt[u], pairs + j)),
            pl.BlockSpec((1, t, LANES), lambda i, j, u, qt, kt: (i, kt[u], 2 * pairs + j)),
            pl.BlockSpec((1, t, LANES), lambda i, j, u, qt, kt: (i, kt[u], 0)),
        ],
        out_specs=pl.BlockSpec((1, t, LANES), lambda i, j, u, qt, kt: (i, qt[u], j)),
        scratch_shapes=[
            pltpu.VMEM((2, t, LANES), F32),
            pltpu.VMEM((2, t, LANES), F32),
        ],
    )
    return pl.pallas_call(
        functools.partial(_fox_kernel, t=t, sub=min(FOX_SUB, t)),
        grid_spec=grid_spec,
        out_shape=jax.ShapeDtypeStruct((b, s, n_heads * HEAD_DIM), BF16),
        compiler_params=_cparams(("arbitrary", "arbitrary", "arbitrary")),
        name="fox_attention",
    )(jnp.asarray(qt), jnp.asarray(kt), qkv, qkv, qkv, gate_bias)


def _out_ln_kernel(*refs, n_in):
    o_refs = refs[:n_in]
    w_ref, x_ref, g_ref, b_ref, y_ref, yb_ref = refs[n_in:]
    if n_in == 1:
        o = o_refs[0][...]
    else:
        o = o_refs[0][...].astype(F32)
        for r in o_refs[1:]:
            o = o + r[...].astype(F32)
        o = o.astype(BF16)
    z = DEEPNORM_ALPHA * x_ref[...] + _dot(o, w_ref[...])
    y = _layer_norm(z, g_ref[...], b_ref[...])
    y_ref[...] = y
    yb_ref[...] = y.astype(BF16)


def _out_ln(o_list, w_bf16, x2d, g, b):
    n, d = x2d.shape
    k = w_bf16.shape[0]
    tm = min(ROW_TILE, n)
    n_in = len(o_list)
    row = lambda i: (i, 0)
    fixed = lambda i: (0, 0)
    return pl.pallas_call(
        functools.partial(_out_ln_kernel, n_in=n_in),
        grid=(n // tm,),
        in_specs=[pl.BlockSpec((tm, k), row)] * n_in + [
            pl.BlockSpec((k, d), fixed),
            pl.BlockSpec((tm, d), row),
            pl.BlockSpec((1, d), fixed),
            pl.BlockSpec((1, d), fixed),
        ],
        out_specs=[pl.BlockSpec((tm, d), row), pl.BlockSpec((tm, d), row)],
        out_shape=[jax.ShapeDtypeStruct((n, d), F32), jax.ShapeDtypeStruct((n, d), BF16)],
        compiler_params=_cparams(("arbitrary",)),
        name="out_ln",
    )(*o_list, w_bf16, x2d, g, b)


def _router_kernel(x_ref, wt_ref, b_ref, idx_ref, w_ref, rank_ref, cnt_ref, carry_ref, *, tn, n_exp):
    @pl.when(pl.program_id(0) == 0)
    def _():
        carry_ref[...] = jnp.zeros_like(carry_ref)

    xh, xl = _split2(x_ref[...])
    wh, wl = _split2(wt_ref[...])
    logits = _nt(wh, xh) + _nt(wh, xl) + _nt(wl, xh)
    scores = _sigmoid(logits)
    biased = scores + b_ref[...]

    per_grp = n_exp // EXPERT_GROUPS
    blocks, gscore = [], []
    for g in range(EXPERT_GROUPS):
        blk = biased[g * per_grp:(g + 1) * per_grp, :]
        m1 = jnp.max(blk, axis=0, keepdims=True)
        eq = blk == m1
        n_eq = jnp.sum(jnp.where(eq, 1.0, 0.0), axis=0, keepdims=True)
        m2 = jnp.max(jnp.where(eq, -3e38, blk), axis=0, keepdims=True)
        blocks.append(blk)
        gscore.append(m1 + jnp.where(n_eq >= 2.0, m1, m2))
    cand = []
    for g in range(EXPERT_GROUPS):
        beaten = jnp.zeros_like(gscore[g])
        for o in range(EXPERT_GROUPS):
            if o == g:
                continue
            wins = (gscore[o] >= gscore[g]) if o < g else (gscore[o] > gscore[g])
            beaten = beaten + jnp.where(wins, 1.0, 0.0)
        cand.append(jnp.where(beaten < float(EXPERT_TOPK_GROUPS), blocks[g], NEG))
    cand = jnp.concatenate(cand, axis=0)

    erow = lax.broadcasted_iota(jnp.int32, (n_exp, tn), 0).astype(F32)
    hot, sel_idx, sel_score = [], [], []
    member = jnp.zeros((n_exp, tn), F32)
    for _ in range(EXPERT_TOPK):
        m = jnp.max(cand, axis=0, keepdims=True)
        first = jnp.min(jnp.where(cand == m, erow, float(n_exp)), axis=0, keepdims=True)
        onehot = erow == first
        sel_idx.append(first)
        sel_score.append(jnp.sum(jnp.where(onehot, scores, 0.0), axis=0, keepdims=True))
        cand = jnp.where(onehot, -3e38, cand)
        member = member + jnp.where(onehot, 1.0, 0.0)
        hot.append(onehot)
    total = sel_score[0]
    for sc in sel_score[1:]:
        total = total + sc

    trow = lax.broadcasted_iota(jnp.int32, (tn, tn), 0)
    tcol = lax.broadcasted_iota(jnp.int32, (tn, tn), 1)
    before = jnp.where(trow < tcol, 1.0, 0.0).astype(BF16)
    prior = _dot(member.astype(BF16), before) + carry_ref[...]
    ranks = [jnp.sum(jnp.where(hot[k], prior, 0.0), axis=0, keepdims=True) for k in range(EXPERT_TOPK)]

    idx_ref[...] = jnp.concatenate(sel_idx, axis=0).astype(jnp.int32)
    w_ref[...] = jnp.concatenate([sc / total * ROUTED_SCALE for sc in sel_score], axis=0)
    rank_ref[...] = jnp.concatenate(ranks, axis=0).astype(jnp.int32)
    carry_ref[...] = carry_ref[...] + jnp.sum(member, axis=1, keepdims=True)
    cnt_ref[...] = carry_ref[...]


def _route(x2d, w_router_t, b_router):
    n, d = x2d.shape
    n_exp = w_router_t.shape[0]
    tn = min(ROUTER_TILE, n)
    col = lambda i: (0, i)
    return pl.pallas_call(
        functools.partial(_router_kernel, tn=tn, n_exp=n_exp),
        grid=(n // tn,),
        in_specs=[
            pl.BlockSpec((tn, d), lambda i: (i, 0)),
            pl.BlockSpec((n_exp, d), lambda i: (0, 0)),
            pl.BlockSpec((n_exp, 1), lambda i: (0, 0)),
        ],
        out_specs=[
            pl.BlockSpec((EXPERT_TOPK, tn), col),
            pl.BlockSpec((EXPERT_TOPK, tn), col),
            pl.BlockSpec((EXPERT_TOPK, tn), col),
            pl.BlockSpec((n_exp, 1), lambda i: (0, 0)),
        ],
        out_shape=[
            jax.ShapeDtypeStruct((EXPERT_TOPK, n), jnp.int32),
            jax.ShapeDtypeStruct((EXPERT_TOPK, n), F32),
            jax.ShapeDtypeStruct((EXPERT_TOPK, n), jnp.int32),
            jax.ShapeDtypeStruct((n_exp, 1), F32),
        ],
        scratch_shapes=[pltpu.VMEM((n_exp, 1), F32)],
        compiler_params=_cparams(("arbitrary",)),
        name="router",
    )(x2d, w_router_t, b_router)


def _experts_kernel(be_ref, first_ref, nused_ref, x_ref, wg_ref, wu_ref, wd_ref, y_ref,
                    wg_s, wu_s, wd_s):
    blk = pl.program_id(0)

    @pl.when(blk < nused_ref[0])
    def _():
        @pl.when(first_ref[blk] == 1)
        def _():
            wg_s[...] = wg_ref[0, 0].astype(BF16)
            wu_s[...] = wu_ref[0, 0].astype(BF16)
            wd_s[...] = wd_ref[0, 0].astype(BF16)

        x = x_ref[...]
        gate = _dot(x, wg_s[...])
        up = _dot(x, wu_s[...])
        h = gate * _sigmoid(gate) * up
        y_ref[...] = _dot(h.astype(BF16), wd_s[...]).astype(y_ref.dtype)

    @pl.when(blk >= nused_ref[0])
    def _():
        y_ref[...] = jnp.zeros_like(y_ref)


def _experts(xg, blk_expert, blk_first, n_used, w_gate, w_up, w_down, layer):
    n_rows, d = xg.shape
    hdim = w_gate.shape[3]
    tm = EXPERT_ROW_BLOCK
    n_blocks = n_rows // tm
    grid_spec = pltpu.PrefetchScalarGridSpec(
        num_scalar_prefetch=3,
        grid=(n_blocks,),
        in_specs=[
            pl.BlockSpec((tm, d), lambda i, be, fi, nu: (i, 0)),
            pl.BlockSpec((1, 1, d, hdim), lambda i, be, fi, nu: (layer, be[i], 0, 0)),
            pl.BlockSpec((1, 1, d, hdim), lambda i, be, fi, nu: (layer, be[i], 0, 0)),
            pl.BlockSpec((1, 1, hdim, d), lambda i, be, fi, nu: (layer, be[i], 0, 0)),
        ],
        out_specs=pl.BlockSpec((tm, d), lambda i, be, fi, nu: (i, 0)),
        scratch_shapes=[
            pltpu.VMEM((d, hdim), BF16),
            pltpu.VMEM((d, hdim), BF16),
            pltpu.VMEM((hdim, d), BF16),
        ],
    )
    return pl.pallas_call(
        _experts_kernel,
        grid_spec=grid_spec,
        out_shape=jax.ShapeDtypeStruct((n_rows, d), BF16),
        compiler_params=_cparams(("arbitrary",)),
        name="experts",
    )(blk_expert, blk_first, n_used, xg, w_gate, w_up, w_down)


def _moe_tail_kernel(x_ref, r_ref, rw_ref, p_ref, sg_ref, su_ref, sd_ref, pg_ref, pb_ref, pp_ref,
                     g1_ref, b1_ref, g2_ref, b2_ref, y_ref):
    x = x_ref[...]
    xb = x.astype(BF16)
    gate = _dot(xb, sg_ref[...])
    up = _dot(xb, su_ref[...])
    shared = _dot((gate * _sigmoid(gate) * up).astype(BF16), sd_ref[...])
    rw = rw_ref[...]
    routed = r_ref[0].astype(F32) * rw[:, 0:1]
    for k in range(1, r_ref.shape[0]):
        routed = routed + r_ref[k].astype(F32) * rw[:, k:k + 1]
    x2 = _layer_norm(DEEPNORM_ALPHA * x + (routed + shared), g1_ref[...], b1_ref[...])
    ple_gate = _sigmoid(_dot(x2.astype(BF16), pg_ref[...]) + pb_ref[...])
    ple = ple_gate * _dot(p_ref[...].astype(BF16), pp_ref[...])
    y_ref[...] = _layer_norm(DEEPNORM_ALPHA * x2 + ple, g2_ref[...], b2_ref[...])


def _moe_tail(x2d, routed, routed_w, p2d, sg, su, sd, pg, pb, pp, g1, b1, g2, b2):
    n, d = x2d.shape
    tm = min(ROW_TILE, n)
    topk = routed.shape[0]
    row = lambda i: (i, 0)
    fixed = lambda i: (0, 0)
    full = lambda a: pl.BlockSpec(a.shape, fixed)
    return pl.pallas_call(
        _moe_tail_kernel,
        grid=(n // tm,),
        in_specs=[
            pl.BlockSpec((tm, d), row),
            pl.BlockSpec((topk, tm, d), lambda i: (0, i, 0)),
            pl.BlockSpec((tm, topk), row),
            pl.BlockSpec((tm, p2d.shape[1]), row),
            full(sg), full(su), full(sd), full(pg), full(pb), full(pp),
            full(g1), full(b1), full(g2), full(b2),
        ],
        out_specs=pl.BlockSpec((tm, d), row),
        out_shape=jax.ShapeDtypeStruct((n, d), F32),
        compiler_params=_cparams(("arbitrary",)),
        name="moe_tail",
    )(x2d, routed, routed_w, p2d, sg, su, sd, pg, pb, pp, g1, b1, g2, b2)


def _moe_layer(x1, x1b, p2d, layer, w_router, b_router, w_gate, w_up, w_down, sg, su, sd,
               ple_proj, ple_gate, ple_bias, g1, b1, g2, b2):
    n, d = x1.shape
    n_exp = w_router.shape[1]
    idx, wts, rank, counts = _route(x1, w_router.T, b_router.reshape(n_exp, 1))
    tm = EXPERT_ROW_BLOCK
    counts = counts[:, 0].astype(jnp.int32)
    padded = (counts + tm - 1) // tm * tm
    pad_end = jnp.cumsum(padded)
    pad_start = pad_end - padded
    n_blocks = (n * EXPERT_TOPK + n_exp * (tm - 1)) // tm + 1
    n_rows = n_blocks * tm
    experts = jnp.arange(n_exp, dtype=jnp.int32)
    start_of = jnp.sum(jnp.where(idx[:, :, None] == experts, pad_start, 0), axis=-1)
    pos = start_of + rank
    tok = jnp.broadcast_to(jnp.arange(n, dtype=jnp.int32)[None, :], pos.shape)
    tok_pad = (jnp.arange(n_rows, dtype=jnp.int32) % n).at[pos.reshape(-1)].set(
        tok.reshape(-1), unique_indices=True, mode="promise_in_bounds")
    blk_start = jnp.arange(n_blocks, dtype=jnp.int32) * tm
    blk_expert = jnp.minimum(jnp.sum((pad_end[None, :] <= blk_start[:, None]).astype(jnp.int32), axis=1),
                             n_exp - 1)
    blk_first = jnp.concatenate([jnp.ones((1,), jnp.int32),
                                 (blk_expert[1:] != blk_expert[:-1]).astype(jnp.int32)])
    n_used = (pad_end[-1:] // tm).astype(jnp.int32)
    xg = x1b.at[tok_pad].get(mode="promise_in_bounds")
    y = _experts(xg, blk_expert, blk_first, n_used, w_gate, w_up, w_down, layer)
    routed = y.at[pos].get(mode="promise_in_bounds")
    return _moe_tail(x1, routed, wts.T, p2d, sg.astype(BF16), su.astype(BF16), sd.astype(BF16),
                     ple_gate.astype(BF16), ple_bias.reshape(1, d), ple_proj.astype(BF16),
                     g1.reshape(1, d), b1.reshape(1, d), g2.reshape(1, d), b2.reshape(1, d))


def _gelu_tanh(x):
    return 0.5 * x * (1.0 + jnp.tanh(0.7978845608028654 * (x + 0.044715 * (x * x * x))))


def _compress_kernel(c_ref, pe_ref, w1_ref, b1_ref, w2_ref, o_ref, *, nc, half):
    out = jnp.zeros((nc, GROUP_LANES), F32)
    for g in range(NSA_KV_GROUPS):
        c = c_ref[0, 0, g].astype(F32)
        a = _dot((c + pe_ref[0, 0:1, :]).astype(BF16), w1_ref[0, :half, :])
        bm = _dot((c + pe_ref[0, 1:2, :]).astype(BF16), w1_ref[0, half:, :])
        h = _gelu_tanh(a + pltpu.roll(bm, nc - 1, 0) + b1_ref[0])
        out = out + _dot(h.astype(BF16), w2_ref[0, g])
    o_ref[0, 0] = out.astype(o_ref.dtype)


def _compress(chunks, pe_flat, w1, b1, w2_placed):
    b, _, _, nc, half = chunks.shape
    hid = w1.shape[2]
    return pl.pallas_call(
        functools.partial(_compress_kernel, nc=nc, half=half),
        grid=(b, 2),
        in_specs=[
            pl.BlockSpec((1, 1, NSA_KV_GROUPS, nc, half), lambda i, j: (i, j, 0, 0, 0)),
            pl.BlockSpec((1, 2, half), lambda i, j: (j, 0, 0)),
            pl.BlockSpec((1, 2 * half, hid), lambda i, j: (j, 0, 0)),
            pl.BlockSpec((1, 1, hid), lambda i, j: (j, 0, 0)),
            pl.BlockSpec((1, NSA_KV_GROUPS, hid, GROUP_LANES), lambda i, j: (j, 0, 0, 0)),
        ],
        out_specs=pl.BlockSpec((1, 1, nc, GROUP_LANES), lambda i, j: (i, j, 0, 0)),
        out_shape=jax.ShapeDtypeStruct((b, 2, nc, GROUP_LANES), BF16),
        compiler_params=_cparams(("arbitrary", "arbitrary")),
        name="compress",
    )(chunks, pe_flat, w1, b1, w2_placed)


def _group_mask(g):
    lane = lax.broadcasted_iota(jnp.int32, (1, GROUP_LANES), 1)
    return (lane >= g * HEAD_DIM) & (lane < (g + 1) * HEAD_DIM)


def _nsa_cmp_kernel(q_ref, kc_ref, vc_ref, gate_ref, slope_ref, wt_ref, o_ref, sel_ref, psum_ref, score_ref,
                    *, tq, nc, n_slc):
    qi = pl.program_id(1)
    r = pl.program_id(2)
    q0 = qi * tq

    @pl.when(r == 0)
    def _():
        psum_ref[...] = jnp.zeros_like(psum_ref)

    q = q_ref[0]
    kc = kc_ref[0, 0]
    vc = vc_ref[0, 0]
    t_pos = q0 + lax.broadcasted_iota(jnp.int32, (tq, nc), 0)
    cmp_end = lax.broadcasted_iota(jnp.int32, (tq, nc), 1) * CMP_STRIDE + (CMP_BLOCK - 1)
    visible = cmp_end <= t_pos
    end_rel = (lax.broadcasted_iota(jnp.int32, (1, nc), 1) * CMP_STRIDE + (CMP_BLOCK - 1) - q0).astype(F32)
    gates = gate_ref[0, 0]
    out = jnp.zeros((tq, GROUP_LANES), F32)
    psum_prev = [psum_ref[g] for g in range(NSA_KV_GROUPS)]
    psum_new = []
    for g in range(NSA_KV_GROUPS):
        in_grp = _group_mask(g)
        qm = jnp.where(in_grp, q, jnp.zeros_like(q))
        s = _nt(qm, kc) + slope_ref[0, g, :, :nc] * end_rel
        s = jnp.where(visible, s, NEG)
        m = jnp.max(s, axis=1, keepdims=True)
        p = jnp.where(visible, jnp.exp2(s - m), 0.0)
        o_g = _dot(p.astype(BF16), jnp.where(in_grp, vc, jnp.ones_like(vc)))
        other = (1 - g // 2) * LANES
        l = o_g[:, other:other + LANES]
        inv = jnp.where(l > 0.0, 1.0 / l, 0.0)
        inv2 = jnp.concatenate([inv, inv], axis=1)
        psum_new.append(psum_prev[g] + p * inv2[:, :nc] if nc == GROUP_LANES else
                        psum_prev[g] + p * inv[:, :1])
        out = out + jnp.where(in_grp, o_g * (inv2 * gates[:, g:g + 1]), 0.0)
    for g in range(NSA_KV_GROUPS):
        psum_ref[g] = psum_new[g]
    o_ref[0] = out.astype(o_ref.dtype)

    @pl.when(r == NSA_HEADS_PER_GROUP - 1)
    def _():
        blk = lax.broadcasted_iota(jnp.int32, (n_slc, tq), 0)
        t_row = q0 + lax.broadcasted_iota(jnp.int32, (n_slc, tq), 1)
        cur = t_row // SLC_BLOCK
        forced = (blk == 0) | (blk == cur) | (blk == cur - 1)
        vis = blk * SLC_BLOCK <= t_row
        wt = wt_ref[...]
        for g in range(NSA_KV_GROUPS):
            hi, mid, lo = _split3(psum_ref[g])
            imp = _nt(wt, hi) + _nt(wt, mid) + _nt(wt, lo)
            score = jnp.where(forced, 1e9, jnp.where(vis, imp, NEG))
            score_ref[...] = score

            def count(j, beaten):
                vj = score_ref[pl.ds(j, 1), :]
                ge = jnp.where(vj >= score, 1.0, 0.0)
                gt = jnp.where(vj > score, 1.0, 0.0)
                return beaten + jnp.where(blk > j, ge, gt)

            beaten = lax.fori_loop(0, n_slc, count, jnp.zeros((n_slc, tq), F32))
            keep = (beaten < float(min(SLC_TOPK, n_slc))) & (score > 0.5 * NEG)
            sel_ref[0, g] = jnp.where(keep, 0.0, NEG)


def _nsa_compressed(q, cmp_kv, gates_c, slopes_rep, w_sel_t):
    b, s, _ = q.shape
    nc = cmp_kv.shape[2]
    n_slc = s // SLC_BLOCK
    tq = min(NSA_Q_TILE, s)
    rr = NSA_HEADS_PER_GROUP
    return pl.pallas_call(
        functools.partial(_nsa_cmp_kernel, tq=tq, nc=nc, n_slc=n_slc),
        grid=(b, s // tq, rr),
        in_specs=[
            pl.BlockSpec((1, tq, GROUP_LANES), lambda i, j, r: (i, j, r)),
            pl.BlockSpec((1, 1, nc, GROUP_LANES), lambda i, j, r: (i, 0, 0, 0)),
            pl.BlockSpec((1, 1, nc, GROUP_LANES), lambda i, j, r: (i, 1, 0, 0)),
            pl.BlockSpec((1, 1, tq, NSA_KV_GROUPS), lambda i, j, r: (i, r, j, 0)),
            pl.BlockSpec((1, NSA_KV_GROUPS, 1, slopes_rep.shape[3]), lambda i, j, r: (r, 0, 0, 0)),
            pl.BlockSpec((n_slc, nc), lambda i, j, r: (0, 0)),
        ],
        out_specs=[
            pl.BlockSpec((1, tq, GROUP_LANES), lambda i, j, r: (i, j, r)),
            pl.BlockSpec((1, NSA_KV_GROUPS, n_slc, tq), lambda i, j, r: (i, 0, 0, j)),
        ],
        out_shape=[
            jax.ShapeDtypeStruct((b, s, rr * GROUP_LANES), BF16),
            jax.ShapeDtypeStruct((b, NSA_KV_GROUPS, n_slc, s), F32),
        ],
        scratch_shapes=[pltpu.VMEM((NSA_KV_GROUPS, tq, nc), F32), pltpu.VMEM((n_slc, tq), F32)],
        compiler_params=_cparams(("arbitrary", "arbitrary", "arbitrary")),
        name="nsa_compressed",
    )(q, cmp_kv, cmp_kv, gates_c, slopes_rep, w_sel_t)


AUG_POS_LANE = 64
MODE_FULL, MODE_LOWER, MODE_UPPER = 0, 1, 2


def _nsa_flash_kernel(qt_ref, kt_ref, fl_ref, *refs, t, sub, use_sel):
    if use_sel:
        q_ref, k_ref, v_ref, ka_ref, gate_ref, slope_ref, sel_ref, o_ref, m_ref, acc_ref = refs
    else:
        q_ref, k_ref, v_ref, ka_ref, gate_ref, slope_ref, o_ref, m_ref, acc_ref = refs
    step = pl.program_id(2)
    flags = fl_ref[step]
    mode = flags >> 2

    @pl.when((flags & 1) != 0)
    def _():
        m_ref[...] = jnp.full_like(m_ref, NEG)
        acc_ref[...] = jnp.zeros_like(acc_ref)

    lane = lax.broadcasted_iota(jnp.int32, (1, LANES), 1)

    def body(mode_static):
        q = q_ref[0]
        k = k_ref[0]
        v = v_ref[0]
        ka = ka_ref[...]
        k_aug = (jnp.concatenate([k[:, :LANES], ka], axis=1), jnp.concatenate([ka, k[:, LANES:]], axis=1))
        items = []
        for g in range(NSA_KV_GROUPS):
            half = g // 2
            in_grp = (lane >= (g % 2) * HEAD_DIM) & (lane < (g % 2 + 1) * HEAD_DIM)
            q_own = jnp.where(in_grp, q[:, half * LANES:(half + 1) * LANES], jnp.zeros((t, LANES), BF16))
            q_extra = jnp.broadcast_to(slope_ref[0, g], (t, LANES))
            if use_sel:
                q_extra = jnp.where(lane < AUG_POS_LANE, sel_ref[0, g], q_extra)
            qg = jnp.concatenate([q_own, q_extra] if half == 0 else [q_extra, q_own], axis=1)
            vg = jnp.where(_group_mask(g), v, jnp.ones_like(v))
            for qs in range(t // sub):
                rows = slice(qs * sub, (qs + 1) * sub)
                if mode_static == MODE_FULL:
                    keys = slice(0, t)
                elif mode_static == MODE_LOWER:
                    keys = slice(0, (qs + 1) * sub)
                else:
                    keys = slice(qs * sub, t)
                s = _nt(qg[rows], k_aug[half][keys])
                if mode_static != MODE_FULL:
                    r_i = lax.broadcasted_iota(jnp.int32, (sub, sub), 0)
                    c_i = lax.broadcasted_iota(jnp.int32, (sub, sub), 1)
                    if mode_static == MODE_LOWER:
                        edge = jnp.where(c_i <= r_i, s[:, -sub:], NEG)
                        s = edge if s.shape[1] == sub else jnp.concatenate([s[:, :-sub], edge], axis=1)
                    else:
                        edge = jnp.where(c_i > r_i, s[:, :sub], NEG)
                        s = edge if s.shape[1] == sub else jnp.concatenate([edge, s[:, sub:]], axis=1)
                items.append(((g, rows), s, vg[keys], m_ref[g, rows, :], acc_ref[g, rows, :]))
        for (g, rows), m_new, acc_new in _flash_items(items):
            acc_ref[g, rows, :] = acc_new
            m_ref[g, rows, :] = m_new

    for mode_static in ((MODE_FULL, MODE_LOWER) if use_sel else (MODE_LOWER, MODE_UPPER)):
        @pl.when(mode == mode_static)
        def _(mode_static=mode_static):
            body(mode_static)

    @pl.when((flags & 2) != 0)
    def _():
        gates = gate_ref[0, 0]
        out = jnp.zeros((t, GROUP_LANES), F32)
        for g in range(NSA_KV_GROUPS):
            acc = acc_ref[g]
            denom = pltpu.roll(acc, LANES, 1)
            out = out + jnp.where(_group_mask(g), acc * (gates[:, g:g + 1] / denom), 0.0)
        o_ref[0] = out.astype(o_ref.dtype)


def _nsa_flash_tables(s, t, windowed):
    qt, kt, fl = [], [], []
    for qi in range(s // t):
        if windowed:
            tiles = [(qi - 1, MODE_UPPER)] if qi > 0 else []
            tiles.append((qi, MODE_LOWER))
        else:
            tiles = [(ki, MODE_FULL) for ki in range(qi)] + [(qi, MODE_LOWER)]
        for n, (ki, mode) in enumerate(tiles):
            qt.append(qi)
            kt.append(ki)
            fl.append((1 if n == 0 else 0) | (2 if n == len(tiles) - 1 else 0) | (mode << 2))
    return tuple(jnp.asarray(np.asarray(a, np.int32)) for a in (qt, kt, fl))


def _nsa_flash(q, kv, k_col, v_col, key_aug, gates, slope_rows, sel_q=None):
    b, s, _ = q.shape
    use_sel = sel_q is not None
    t = min(NSA_TILE, s)
    assert use_sel or t == WINDOW, "the window branch assumes tile == WINDOW"
    qt, kt, fl = _nsa_flash_tables(s, t, not use_sel)
    rr = NSA_HEADS_PER_GROUP
    in_specs = [
        pl.BlockSpec((1, t, GROUP_LANES), lambda i, r, u, qt, kt, fl: (i, qt[u], r)),
        pl.BlockSpec((1, t, GROUP_LANES), lambda i, r, u, qt, kt, fl: (i, kt[u], k_col)),
        pl.BlockSpec((1, t, GROUP_LANES), lambda i, r, u, qt, kt, fl: (i, kt[u], v_col)),
        pl.BlockSpec((t, LANES), lambda i, r, u, qt, kt, fl: (kt[u], 0)),
        pl.BlockSpec((1, 1, t, NSA_KV_GROUPS), lambda i, r, u, qt, kt, fl: (i, r, qt[u], 0)),
        pl.BlockSpec((1, NSA_KV_GROUPS, 1, LANES), lambda i, r, u, qt, kt, fl: (r, 0, 0, 0)),
    ]
    args = [q, kv, kv, key_aug, gates, slope_rows]
    if use_sel:
        in_specs.append(pl.BlockSpec((1, NSA_KV_GROUPS, t, LANES), lambda i, r, u, qt, kt, fl: (i, 0, qt[u], 0)))
        args.append(sel_q)
    grid_spec = pltpu.PrefetchScalarGridSpec(
        num_scalar_prefetch=3,
        grid=(b, rr, int(qt.shape[0])),
        in_specs=in_specs,
        out_specs=pl.BlockSpec((1, t, GROUP_LANES), lambda i, r, u, qt, kt, fl: (i, qt[u], r)),
        scratch_shapes=[
            pltpu.VMEM((NSA_KV_GROUPS, t, LANES), F32),
            pltpu.VMEM((NSA_KV_GROUPS, t, GROUP_LANES), F32),
        ],
    )
    return pl.pallas_call(
        functools.partial(_nsa_flash_kernel, t=t, sub=min(NSA_SUB, t), use_sel=use_sel),
        grid_spec=grid_spec,
        out_shape=jax.ShapeDtypeStruct((b, s, rr * GROUP_LANES), BF16),
        compiler_params=_cparams(("arbitrary", "arbitrary", "arbitrary")),
        name="nsa_selected" if use_sel else "nsa_window",
    )(qt, kt, fl, *args)


def _nsa_key_aug(s):
    pos = np.arange(s)
    aug = np.zeros((s, LANES), np.float32)
    aug[pos, pos // SLC_BLOCK] = 1.0
    aug[:, AUG_POS_LANE:AUG_POS_LANE + 3] = (pos // 64)[:, None]
    aug[:, AUG_POS_LANE + 3:AUG_POS_LANE + 6] = (pos % 64)[:, None]
    return jnp.asarray(aug, BF16)


def _nsa_slope_rows(slopes_l2):
    rr, gg = slopes_l2.shape
    hi, mid, lo = _split3(jnp.asarray(slopes_l2, F32))
    pieces = jnp.stack([hi, mid, lo], axis=-1).astype(F32)
    rows = jnp.zeros((rr, gg, 1, LANES), F32)
    rows = rows.at[:, :, 0, AUG_POS_LANE:AUG_POS_LANE + 3].set(pieces * 64.0)
    rows = rows.at[:, :, 0, AUG_POS_LANE + 3:AUG_POS_LANE + 6].set(pieces)
    return rows.astype(BF16)


def _pad_lanes(a, width=LANES):
    return jnp.pad(a, ((0, 0), (0, width - a.shape[1])))


def _fox_layer(x2d, b, s, w_in, b_f, w_out, ln_g, ln_b):
    d = x2d.shape[1]
    attn = w_out.shape[0]
    n_heads = attn // HEAD_DIM
    scale = HEAD_DIM ** -0.5 * LOG2E
    col_scale = jnp.concatenate([jnp.full((attn,), scale, F32), jnp.ones((2 * attn,), F32)])[None, :]
    qkv, z = _project(x2d, w_in[:, :3 * attn].astype(BF16), col_scale,
                      _pad_lanes(w_in[:, 3 * attn:]), _pad_lanes(b_f[None, :]), False)
    gate_bias = _gate_cumsum(z.reshape(b, s, LANES), n_heads)
    o = _fox_attention(qkv.reshape(b, s, 3 * attn), gate_bias, n_heads)
    return _out_ln([o.reshape(b * s, attn)], w_out.astype(BF16), x2d,
                   ln_g.reshape(1, d), ln_b.reshape(1, d))


def _selection_weights_t(nc, n_slc):
    cs = np.arange(nc)[:, None] * CMP_STRIDE
    ce = cs + CMP_BLOCK
    ss = np.arange(n_slc)[None, :] * SLC_BLOCK
    se = ss + SLC_BLOCK
    w = np.clip(np.minimum(ce, se) - np.maximum(cs, ss), 0, None) / CMP_STRIDE
    w[nc - 1, :] = 0.0
    return jnp.asarray(w.T, BF16)


def _nsa_layer(x2d, b, s, w_kv, cmp_pe, cmp_w1, cmp_b1, cmp_w2, w_q, b_g, w_out, ln_g, ln_b):
    d = x2d.shape[1]
    gg, rr = NSA_KV_GROUPS, NSA_HEADS_PER_GROUP
    attn = gg * rr * HEAD_DIM
    n_heads = gg * rr
    wq = w_q[:, :attn].reshape(d, gg, rr, HEAD_DIM).transpose(0, 2, 1, 3).reshape(d, attn)
    wgate = w_q[:, attn:].reshape(d, gg, rr, 3).transpose(0, 3, 2, 1).reshape(d, 3 * n_heads)
    bgate = b_g.reshape(gg, rr, 3).transpose(2, 1, 0).reshape(1, 3 * n_heads)
    wo = w_out.reshape(gg, rr, HEAD_DIM, d).transpose(1, 0, 2, 3).reshape(attn, d)
    w_all = jnp.concatenate([wq, w_kv], axis=1).astype(BF16)
    col_scale = jnp.concatenate([jnp.full((attn,), HEAD_DIM ** -0.5 * LOG2E, F32),
                                 jnp.ones((w_kv.shape[1],), F32)])[None, :]
    qkv, gates = _project(x2d, w_all, col_scale, _pad_lanes(wgate), _pad_lanes(bgate), True)
    width = qkv.shape[1]
    qkv = qkv.reshape(b, s, width)
    gates = gates[:, :3 * n_heads].reshape(b, s, 3, rr, gg).transpose(2, 0, 3, 1, 4)

    nc = s // CMP_STRIDE
    half = CMP_STRIDE * HEAD_DIM
    raw = qkv[:, :, attn:attn + 2 * GROUP_LANES]
    chunks = raw.reshape(b, nc, CMP_STRIDE, 2, gg, HEAD_DIM).transpose(0, 3, 4, 1, 2, 5)
    chunks = chunks.reshape(b, 2, gg, nc, half)
    pe_flat = cmp_pe.reshape(2, 2, half)
    hid = cmp_w1.shape[2]
    w2_placed = jnp.zeros((2, gg, hid, GROUP_LANES), F32)
    for g in range(gg):
        w2_placed = w2_placed.at[:, g, :, g * HEAD_DIM:(g + 1) * HEAD_DIM].set(cmp_w2)
    cmp_kv = _compress(chunks, pe_flat, cmp_w1.astype(BF16), cmp_b1.reshape(2, 1, hid),
                       w2_placed.astype(BF16))

    slopes_l2 = (2.0 ** (-8.0 * np.arange(1, n_heads + 1) / n_heads) * LOG2E).reshape(gg, rr).T
    slopes_rep = jnp.asarray(np.broadcast_to(
        slopes_l2[:, :, None, None], (rr, gg, 1, nc)).astype(np.float32))
    n_slc = s // SLC_BLOCK
    assert n_slc <= AUG_POS_LANE, "selection flags must fit below the position lanes"
    o_c, sel_t = _nsa_compressed(qkv, cmp_kv, gates[0], slopes_rep, _selection_weights_t(nc, n_slc))
    sel_q = jnp.pad(sel_t.transpose(0, 1, 3, 2).astype(BF16),
                    ((0, 0), (0, 0), (0, 0), (0, LANES - n_slc)))
    key_aug = _nsa_key_aug(s)
    slope_rows = _nsa_slope_rows(slopes_l2)
    base = attn // GROUP_LANES
    o_s = _nsa_flash(qkv, qkv, base + 2, base + 3, key_aug, gates[1], slope_rows, sel_q)
    o_w = _nsa_flash(qkv, qkv, base + 4, base + 5, key_aug, gates[2], slope_rows)
    n = b * s
    return _out_ln([o_c.reshape(n, attn), o_s.reshape(n, attn), o_w.reshape(n, attn)],
                   wo.astype(BF16), x2d, ln_g.reshape(1, d), ln_b.reshape(1, d))


def kernel(x, p, fox_w_in, fox_b_f, fox_w_out, nsa_w_kv, cmp_pe, cmp_w1, cmp_b1, cmp_w2, nsa_w_q, nsa_b_g, nsa_w_out, ln_g, ln_b, moe_w_router, moe_b_router, moe_w_gate, moe_w_up, moe_w_down, shared_w_gate, shared_w_up, shared_w_down, ple_w_proj, ple_w_gate, ple_b_gate):
    b, s, d = x.shape
    n = b * s
    depth = p.shape[0]
    n_a = depth // 2
    h = x.reshape(n, d)
    for i in range(depth):
        if i < n_a:
            h1, h1b = _fox_layer(h, b, s, fox_w_in[i], fox_b_f[i], fox_w_out[i], ln_g[i, 0], ln_b[i, 0])
        else:
            j = i - n_a
            h1, h1b = _nsa_layer(h, b, s, nsa_w_kv, cmp_pe, cmp_w1, cmp_b1, cmp_w2,
                                 nsa_w_q[j], nsa_b_g[j], nsa_w_out[j], ln_g[i, 0], ln_b[i, 0])
        h = _moe_layer(h1, h1b, p[i].reshape(n, -1), i, moe_w_router[i], moe_b_router[i],
                       moe_w_gate, moe_w_up, moe_w_down,
                       shared_w_gate[i], shared_w_up[i], shared_w_down[i],
                       ple_w_proj[i], ple_w_gate[i], ple_b_gate[i],
                       ln_g[i, 1], ln_b[i, 1], ln_g[i, 2], ln_b[i, 2])
    return h.reshape(b, s, d)
```

```python
import functools

import numpy as np
import jax
import jax.numpy as jnp
from jax import lax
from jax.experimental import pallas as pl
from jax.experimental.pallas import tpu as pltpu

F32 = jnp.float32
BF16 = jnp.bfloat16
NEG = -1e30

HEAD_DIM = 64
LANES = 128
NSA_KV_GROUPS = 4
NSA_HEADS_PER_GROUP = 4
GROUP_LANES = NSA_KV_GROUPS * HEAD_DIM
CMP_BLOCK = 32
CMP_STRIDE = 16
SLC_BLOCK = 64
SLC_TOPK = 16
WINDOW = 512
EXPERT_TOPK = 8
EXPERT_GROUPS = 8
EXPERT_TOPK_GROUPS = 4
ROUTED_SCALE = 2.5
LN_EPS = 1e-5
DEPTH = 2
DEEPNORM_ALPHA = (2.0 * DEPTH) ** 0.25
VMEM_LIMIT = 52 * 1024 * 1024

ROW_TILE = 512
FOX_TILE = 1024
FOX_SUB = 128
NSA_Q_TILE = 256
NSA_SEL_TILE = 1024
NSA_WIN_TILE = 512
NSA_SUB = 256
ROUTER_TILE = 512
EXPERT_ROW_BLOCK = 512


def _cparams(sem):
    return pltpu.CompilerParams(dimension_semantics=sem, vmem_limit_bytes=VMEM_LIMIT)


def _nt(a, b):
    return lax.dot_general(a, b, (((1,), (1,)), ((), ())), preferred_element_type=F32)


def _dot(a, b):
    return jnp.dot(a, b, preferred_element_type=F32)


def _split2(a):
    hi = a.astype(BF16)
    lo = (a - hi.astype(F32)).astype(BF16)
    return hi, lo


def _split3(a):
    hi = a.astype(BF16)
    r = a - hi.astype(F32)
    mid = r.astype(BF16)
    lo = (r - mid.astype(F32)).astype(BF16)
    return hi, mid, lo


def _sigmoid(x):
    return 1.0 / (1.0 + jnp.exp(-x))


def _layer_norm(z, g, b):
    mu = jnp.mean(z, axis=-1, keepdims=True)
    zc = z - mu
    var = jnp.mean(zc * zc, axis=-1, keepdims=True)
    return zc * lax.rsqrt(var + LN_EPS) * g + b


def _proj_kernel(x_ref, w_ref, cs_ref, wg_ref, bg_ref, o_ref, g_ref, *, tn, gate_sigmoid):
    x = x_ref[...]
    xh = x.astype(BF16)
    m_out = w_ref.shape[1]
    for j in range(m_out // tn):
        sl = slice(j * tn, (j + 1) * tn)
        y = _dot(xh, w_ref[:, sl])
        o_ref[:, sl] = (y * cs_ref[:, sl]).astype(o_ref.dtype)
    xl = (x - xh.astype(F32)).astype(BF16)
    wh, wl = _split2(wg_ref[...])
    g = _dot(xh, wh) + _dot(xh, wl) + _dot(xl, wh) + bg_ref[...]
    if gate_sigmoid:
        g = _sigmoid(g)
    g_ref[...] = g


def _project(x2d, w_bf16, col_scale, w_gate, b_gate, gate_sigmoid):
    n, k = x2d.shape
    m_out = w_bf16.shape[1]
    tm = min(ROW_TILE, n)
    return pl.pallas_call(
        functools.partial(_proj_kernel, tn=512, gate_sigmoid=gate_sigmoid),
        grid=(n // tm,),
        in_specs=[
            pl.BlockSpec((tm, k), lambda i: (i, 0)),
            pl.BlockSpec((k, m_out), lambda i: (0, 0)),
            pl.BlockSpec((1, m_out), lambda i: (0, 0)),
            pl.BlockSpec((k, LANES), lambda i: (0, 0)),
            pl.BlockSpec((1, LANES), lambda i: (0, 0)),
        ],
        out_specs=[
            pl.BlockSpec((tm, m_out), lambda i: (i, 0)),
            pl.BlockSpec((tm, LANES), lambda i: (i, 0)),
        ],
        out_shape=[
            jax.ShapeDtypeStruct((n, m_out), BF16),
            jax.ShapeDtypeStruct((n, LANES), F32),
        ],
        compiler_params=_cparams(("arbitrary",)),
        name="project",
    )(x2d, w_bf16, col_scale, w_gate, b_gate)


LOG2E = 1.4426950408889634
GATE_PIECE_STRIDE = 16


def _gate_cumsum_kernel(z_ref, o_ref, carry_ref, *, ts, n_heads):
    @pl.when(pl.program_id(1) == 0)
    def _():
        carry_ref[...] = jnp.zeros_like(carry_ref)

    z = z_ref[0]
    log_f = jnp.minimum(z, 0.0) - jnp.log(1.0 + jnp.exp(-jnp.abs(z)))
    row = lax.broadcasted_iota(jnp.int32, (ts, ts), 0)
    col = lax.broadcasted_iota(jnp.int32, (ts, ts), 1)
    tri = jnp.where(col <= row, 1.0, 0.0).astype(BF16)
    hi, mid, lo = _split3(log_f)
    cs = _dot(tri, hi) + _dot(tri, mid) + _dot(tri, lo) + carry_ref[...]
    carry_ref[...] = cs[ts - 1:ts, :]
    lane = lax.broadcasted_iota(jnp.int32, (1, LANES), 1)
    bias = jnp.where(lane < n_heads, cs * (-LOG2E), 0.0)
    hi, mid, lo = _split3(bias)
    pieces = (hi.astype(F32) + pltpu.roll(mid.astype(F32), GATE_PIECE_STRIDE, 1)
              + pltpu.roll(lo.astype(F32), 2 * GATE_PIECE_STRIDE, 1))
    o_ref[0] = (pieces + pltpu.roll(pieces, HEAD_DIM, 1)).astype(o_ref.dtype)


def _gate_cumsum(z, n_heads):
    b, s, _ = z.shape
    ts = min(256, s)
    return pl.pallas_call(
        functools.partial(_gate_cumsum_kernel, ts=ts, n_heads=n_heads),
        grid=(b, s // ts),
        in_specs=[pl.BlockSpec((1, ts, LANES), lambda i, j: (i, j, 0))],
        out_specs=pl.BlockSpec((1, ts, LANES), lambda i, j: (i, j, 0)),
        out_shape=jax.ShapeDtypeStruct((b, s, LANES), BF16),
        scratch_shapes=[pltpu.VMEM((1, LANES), F32)],
        compiler_params=_cparams(("arbitrary", "arbitrary")),
        name="gate_cumsum",
    )(z)


def _flash_items(items):
    outs = []
    for key, s, v, m_prev, acc_prev in items:
        m_new = jnp.maximum(m_prev, jnp.max(s, axis=1, keepdims=True))
        alpha = jnp.exp2(m_prev - m_new)
        p = jnp.concatenate(
            [jnp.exp2(s[:, j * LANES:(j + 1) * LANES] - m_new[:, :LANES])
             for j in range(s.shape[1] // LANES)], axis=1)
        reps = acc_prev.shape[1] // LANES
        alpha = alpha if reps == 1 else jnp.concatenate([alpha] * reps, axis=1)
        outs.append((key, m_new, alpha * acc_prev + _dot(p.astype(BF16), v)))
    return outs


def _fox_kernel(qt_ref, kt_ref, q_ref, k_ref, v_ref, c_ref, o_ref, m_ref, acc_ref, *, t, sub):
    step = pl.program_id(2)
    pair = pl.program_id(1)
    qi = qt_ref[step]
    ki = kt_ref[step]

    @pl.when(ki == 0)
    def _():
        m_ref[...] = jnp.full_like(m_ref, NEG)
        acc_ref[...] = jnp.zeros_like(acc_ref)

    lane = lax.broadcasted_iota(jnp.int32, (1, LANES), 1)
    low = lane < HEAD_DIM

    def body(diag):
        q = q_ref[0]
        k = k_ref[0]
        v = v_ref[0]
        c = c_ref[0]
        items = []
        for h in range(2):
            own = low if h == 0 else jnp.logical_not(low)
            base = (1 - h) * HEAD_DIM + 2 * pair + h
            ones_at = ((lane == base) | (lane == base + GATE_PIECE_STRIDE)
                       | (lane == base + 2 * GATE_PIECE_STRIDE))
            qh = jnp.where(own, q, jnp.where(ones_at, 1.0, 0.0).astype(BF16))
            kh = jnp.where(own, k, c)
            vh = jnp.where(own, v, jnp.ones_like(v))
            for qs in range(t // sub):
                rows = slice(qs * sub, (qs + 1) * sub)
                nk = (qs + 1) * sub if diag else t
                s = _nt(qh[rows], kh[:nk])
                if diag:
                    r_i = lax.broadcasted_iota(jnp.int32, (sub, sub), 0)
                    c_i = lax.broadcasted_iota(jnp.int32, (sub, sub), 1)
                    last = jnp.where(c_i <= r_i, s[:, nk - sub:], NEG)
                    s = last if nk == sub else jnp.concatenate([s[:, :nk - sub], last], axis=1)
                items.append(((h, rows), s, vh[:nk], m_ref[h, rows, :], acc_ref[h, rows, :]))
        for (h, rows), m_new, acc_new in _flash_items(items):
            acc_ref[h, rows, :] = acc_new
            m_ref[h, rows, :] = m_new

    @pl.when(ki < qi)
    def _():
        body(False)

    @pl.when(ki == qi)
    def _():
        body(True)
        a0 = acc_ref[0]
        a1 = acc_ref[1]
        o0 = a0 / pltpu.roll(a0, HEAD_DIM, 1)
        o1 = a1 / pltpu.roll(a1, HEAD_DIM, 1)
        o_ref[0] = jnp.where(low, o0, o1).astype(o_ref.dtype)


def _fox_attention(qkv, gate_bias, n_heads):
    b, s, _ = qkv.shape
    t = min(FOX_TILE, s)
    nq = s // t
    pairs = n_heads // 2
    qt = np.concatenate([np.full((i + 1,), i, np.int32) for i in range(nq)])
    kt = np.concatenate([np.arange(i + 1, dtype=np.int32) for i in range(nq)])
    grid_spec = pltpu.PrefetchScalarGridSpec(
        num_scalar_prefetch=2,
        grid=(b, pairs, len(qt)),
        in_specs=[
            pl.BlockSpec((1, t, LANES), lambda i, j, u, qt, kt: (i, qt[u], j)),
            pl.BlockSpec((1, t, LANES), lambda i, j, u, qt, kt: (i, kt[u], pairs + j)),
            pl.BlockSpec((1, t, LANES), lambda i, j, u, qt, kt: (i, kt[u], 2 * pairs + j)),
            pl.BlockSpec((1, t, LANES), lambda i, j, u, qt, kt: (i, kt[u], 0)),
        ],
        out_specs=pl.BlockSpec((1, t, LANES), lambda i, j, u, qt, kt: (i, qt[u], j)),
        scratch_shapes=[
            pltpu.VMEM((2, t, LANES), F32),
            pltpu.VMEM((2, t, LANES), F32),
        ],
    )
    return pl.pallas_call(
        functools.partial(_fox_kernel, t=t, sub=min(FOX_SUB, t)),
        grid_spec=grid_spec,
        out_shape=jax.ShapeDtypeStruct((b, s, n_heads * HEAD_DIM), BF16),
        compiler_params=_cparams(("arbitrary", "arbitrary", "arbitrary")),
        name="fox_attention",
    )(jnp.asarray(qt), jnp.asarray(kt), qkv, qkv, qkv, gate_bias)


def _out_ln_kernel(*refs, n_in):
    o_refs = refs[:n_in]
    w_ref, x_ref, g_ref, b_ref, y_ref, yb_ref = refs[n_in:]
    if n_in == 1:
        o = o_refs[0][...]
    else:
        o = o_refs[0][...].astype(F32)
        for r in o_refs[1:]:
            o = o + r[...].astype(F32)
        o = o.astype(BF16)
    z = DEEPNORM_ALPHA * x_ref[...] + _dot(o, w_ref[...])
    y = _layer_norm(z, g_ref[...], b_ref[...])
    y_ref[...] = y
    yb_ref[...] = y.astype(BF16)


def _out_ln(o_list, w_bf16, x2d, g, b):
    n, d = x2d.shape
    k = w_bf16.shape[0]
    tm = min(ROW_TILE, n)
    n_in = len(o_list)
    row = lambda i: (i, 0)
    fixed = lambda i: (0, 0)
    return pl.pallas_call(
        functools.partial(_out_ln_kernel, n_in=n_in),
        grid=(n // tm,),
        in_specs=[pl.BlockSpec((tm, k), row)] * n_in + [
            pl.BlockSpec((k, d), fixed),
            pl.BlockSpec((tm, d), row),
            pl.BlockSpec((1, d), fixed),
            pl.BlockSpec((1, d), fixed),
        ],
        out_specs=[pl.BlockSpec((tm, d), row), pl.BlockSpec((tm, d), row)],
        out_shape=[jax.ShapeDtypeStruct((n, d), F32), jax.ShapeDtypeStruct((n, d), BF16)],
        compiler_params=_cparams(("arbitrary",)),
        name="out_ln",
    )(*o_list, w_bf16, x2d, g, b)


def _router_kernel(x_ref, wt_ref, b_ref, idx_ref, w_ref, rank_ref, cnt_ref, carry_ref, *, tn, n_exp):
    @pl.when(pl.program_id(0) == 0)
    def _():
        carry_ref[...] = jnp.zeros_like(carry_ref)

    xh, xl = _split2(x_ref[...])
    wh, wl = _split2(wt_ref[...])
    logits = _nt(wh, xh) + _nt(wh, xl) + _nt(wl, xh)
    scores = _sigmoid(logits)
    biased = scores + b_ref[...]

    per_grp = n_exp // EXPERT_GROUPS
    blocks, gscore = [], []
    for g in range(EXPERT_GROUPS):
        blk = biased[g * per_grp:(g + 1) * per_grp, :]
        m1 = jnp.max(blk, axis=0, keepdims=True)
        eq = blk == m1
        n_eq = jnp.sum(jnp.where(eq, 1.0, 0.0), axis=0, keepdims=True)
        m2 = jnp.max(jnp.where(eq, -3e38, blk), axis=0, keepdims=True)
        blocks.append(blk)
        gscore.append(m1 + jnp.where(n_eq >= 2.0, m1, m2))
    cand = []
    for g in range(EXPERT_GROUPS):
        beaten = jnp.zeros_like(gscore[g])
        for o in range(EXPERT_GROUPS):
            if o == g:
                continue
            wins = (gscore[o] >= gscore[g]) if o < g else (gscore[o] > gscore[g])
            beaten = beaten + jnp.where(wins, 1.0, 0.0)
        cand.append(jnp.where(beaten < float(EXPERT_TOPK_GROUPS), blocks[g], NEG))
    cand = jnp.concatenate(cand, axis=0)

    erow = lax.broadcasted_iota(jnp.int32, (n_exp, tn), 0).astype(F32)
    hot, sel_idx, sel_score = [], [], []
    member = jnp.zeros((n_exp, tn), F32)
    for _ in range(EXPERT_TOPK):
        m = jnp.max(cand, axis=0, keepdims=True)
        first = jnp.min(jnp.where(cand == m, erow, float(n_exp)), axis=0, keepdims=True)
        onehot = erow == first
        sel_idx.append(first)
        sel_score.append(jnp.sum(jnp.where(onehot, scores, 0.0), axis=0, keepdims=True))
        cand = jnp.where(onehot, -3e38, cand)
        member = member + jnp.where(onehot, 1.0, 0.0)
        hot.append(onehot)
    total = sel_score[0]
    for sc in sel_score[1:]:
        total = total + sc

    trow = lax.broadcasted_iota(jnp.int32, (tn, tn), 0)
    tcol = lax.broadcasted_iota(jnp.int32, (tn, tn), 1)
    before = jnp.where(trow < tcol, 1.0, 0.0).astype(BF16)
    prior = _dot(member.astype(BF16), before) + carry_ref[...]
    ranks = [jnp.sum(jnp.where(hot[k], prior, 0.0), axis=0, keepdims=True) for k in range(EXPERT_TOPK)]

    idx_ref[...] = jnp.concatenate(sel_idx, axis=0).astype(jnp.int32)
    w_ref[...] = jnp.concatenate([sc / total * ROUTED_SCALE for sc in sel_score], axis=0)
    rank_ref[...] = jnp.concatenate(ranks, axis=0).astype(jnp.int32)
    carry_ref[...] = carry_ref[...] + jnp.sum(member, axis=1, keepdims=True)
    cnt_ref[...] = carry_ref[...]


def _route(x2d, w_router_t, b_router):
    n, d = x2d.shape
    n_exp = w_router_t.shape[0]
    tn = min(ROUTER_TILE, n)
    col = lambda i: (0, i)
    return pl.pallas_call(
        functools.partial(_router_kernel, tn=tn, n_exp=n_exp),
        grid=(n // tn,),
        in_specs=[
            pl.BlockSpec((tn, d), lambda i: (i, 0)),
            pl.BlockSpec((n_exp, d), lambda i: (0, 0)),
            pl.BlockSpec((n_exp, 1), lambda i: (0, 0)),
        ],
        out_specs=[
            pl.BlockSpec((EXPERT_TOPK, tn), col),
            pl.BlockSpec((EXPERT_TOPK, tn), col),
            pl.BlockSpec((EXPERT_TOPK, tn), col),
            pl.BlockSpec((n_exp, 1), lambda i: (0, 0)),
        ],
        out_shape=[
            jax.ShapeDtypeStruct((EXPERT_TOPK, n), jnp.int32),
            jax.ShapeDtypeStruct((EXPERT_TOPK, n), F32),
            jax.ShapeDtypeStruct((EXPERT_TOPK, n), jnp.int32),
            jax.ShapeDtypeStruct((n_exp, 1), F32),
        ],
        scratch_shapes=[pltpu.VMEM((n_exp, 1), F32)],
        compiler_params=_cparams(("arbitrary",)),
        name="router",
    )(x2d, w_router_t, b_router)


def _experts_kernel(be_ref, first_ref, nused_ref, x_ref, wg_ref, wu_ref, wd_ref, y_ref,
                    wg_s, wu_s, wd_s):
    blk = pl.program_id(0)

    @pl.when(blk < nused_ref[0])
    def _():
        @pl.when(first_ref[blk] == 1)
        def _():
            wg_s[...] = wg_ref[0, 0].astype(BF16)
            wu_s[...] = wu_ref[0, 0].astype(BF16)
            wd_s[...] = wd_ref[0, 0].astype(BF16)

        x = x_ref[...]
        gate = _dot(x, wg_s[...])
        up = _dot(x, wu_s[...])
        h = gate * _sigmoid(gate) * up
        y_ref[...] = _dot(h.astype(BF16), wd_s[...]).astype(y_ref.dtype)

    @pl.when(blk >= nused_ref[0])
    def _():
        y_ref[...] = jnp.zeros_like(y_ref)


def _experts(xg, blk_expert, blk_first, n_used, w_gate, w_up, w_down, layer):
    n_rows, d = xg.shape
    hdim = w_gate.shape[3]
    tm = EXPERT_ROW_BLOCK
    n_blocks = n_rows // tm
    grid_spec = pltpu.PrefetchScalarGridSpec(
        num_scalar_prefetch=3,
        grid=(n_blocks,),
        in_specs=[
            pl.BlockSpec((tm, d), lambda i, be, fi, nu: (i, 0)),
            pl.BlockSpec((1, 1, d, hdim), lambda i, be, fi, nu: (layer, be[i], 0, 0)),
            pl.BlockSpec((1, 1, d, hdim), lambda i, be, fi, nu: (layer, be[i], 0, 0)),
            pl.BlockSpec((1, 1, hdim, d), lambda i, be, fi, nu: (layer, be[i], 0, 0)),
        ],
        out_specs=pl.BlockSpec((tm, d), lambda i, be, fi, nu: (i, 0)),
        scratch_shapes=[
            pltpu.VMEM((d, hdim), BF16),
            pltpu.VMEM((d, hdim), BF16),
            pltpu.VMEM((hdim, d), BF16),
        ],
    )
    return pl.pallas_call(
        _experts_kernel,
        grid_spec=grid_spec,
        out_shape=jax.ShapeDtypeStruct((n_rows, d), BF16),
        compiler_params=_cparams(("arbitrary",)),
        name="experts",
    )(blk_expert, blk_first, n_used, xg, w_gate, w_up, w_down)


def _moe_tail_kernel(x_ref, r_ref, rw_ref, p_ref, sg_ref, su_ref, sd_ref, pg_ref, pb_ref, pp_ref,
                     g1_ref, b1_ref, g2_ref, b2_ref, y_ref):
    x = x_ref[...]
    xb = x.astype(BF16)
    gate = _dot(xb, sg_ref[...])
    up = _dot(xb, su_ref[...])
    shared = _dot((gate * _sigmoid(gate) * up).astype(BF16), sd_ref[...])
    rw = rw_ref[...]
    routed = r_ref[0].astype(F32) * rw[:, 0:1]
    for k in range(1, r_ref.shape[0]):
        routed = routed + r_ref[k].astype(F32) * rw[:, k:k + 1]
    x2 = _layer_norm(DEEPNORM_ALPHA * x + (routed + shared), g1_ref[...], b1_ref[...])
    ple_gate = _sigmoid(_dot(x2.astype(BF16), pg_ref[...]) + pb_ref[...])
    ple = ple_gate * _dot(p_ref[...].astype(BF16), pp_ref[...])
    y_ref[...] = _layer_norm(DEEPNORM_ALPHA * x2 + ple, g2_ref[...], b2_ref[...])


def _moe_tail(x2d, routed, routed_w, p2d, sg, su, sd, pg, pb, pp, g1, b1, g2, b2):
    n, d = x2d.shape
    tm = min(ROW_TILE, n)
    topk = routed.shape[0]
    row = lambda i: (i, 0)
    fixed = lambda i: (0, 0)
    full = lambda a: pl.BlockSpec(a.shape, fixed)
    return pl.pallas_call(
        _moe_tail_kernel,
        grid=(n // tm,),
        in_specs=[
            pl.BlockSpec((tm, d), row),
            pl.BlockSpec((topk, tm, d), lambda i: (0, i, 0)),
            pl.BlockSpec((tm, topk), row),
            pl.BlockSpec((tm, p2d.shape[1]), row),
            full(sg), full(su), full(sd), full(pg), full(pb), full(pp),
            full(g1), full(b1), full(g2), full(b2),
        ],
        out_specs=pl.BlockSpec((tm, d), row),
        out_shape=jax.ShapeDtypeStruct((n, d), F32),
        compiler_params=_cparams(("arbitrary",)),
        name="moe_tail",
    )(x2d, routed, routed_w, p2d, sg, su, sd, pg, pb, pp, g1, b1, g2, b2)


def _moe_layer(x1, x1b, p2d, layer, w_router, b_router, w_gate, w_up, w_down, sg, su, sd,
               ple_proj, ple_gate, ple_bias, g1, b1, g2, b2):
    n, d = x1.shape
    n_exp = w_router.shape[1]
    idx, wts, rank, counts = _route(x1, w_router.T, b_router.reshape(n_exp, 1))
    tm = EXPERT_ROW_BLOCK
    counts = counts[:, 0].astype(jnp.int32)
    padded = (counts + tm - 1) // tm * tm
    pad_end = jnp.cumsum(padded)
    pad_start = pad_end - padded
    n_blocks = (n * EXPERT_TOPK + n_exp * (tm - 1)) // tm + 1
    n_rows = n_blocks * tm
    experts = jnp.arange(n_exp, dtype=jnp.int32)
    start_of = jnp.sum(jnp.where(idx[:, :, None] == experts, pad_start, 0), axis=-1)
    pos = start_of + rank
    tok = jnp.broadcast_to(jnp.arange(n, dtype=jnp.int32)[None, :], pos.shape)
    tok_pad = (jnp.arange(n_rows, dtype=jnp.int32) % n).at[pos.reshape(-1)].set(
        tok.reshape(-1), unique_indices=True, mode="promise_in_bounds")
    blk_start = jnp.arange(n_blocks, dtype=jnp.int32) * tm
    blk_expert = jnp.minimum(jnp.sum((pad_end[None, :] <= blk_start[:, None]).astype(jnp.int32), axis=1),
                             n_exp - 1)
    blk_first = jnp.concatenate([jnp.ones((1,), jnp.int32),
                                 (blk_expert[1:] != blk_expert[:-1]).astype(jnp.int32)])
    n_used = (pad_end[-1:] // tm).astype(jnp.int32)
    xg = x1b.at[tok_pad].get(mode="promise_in_bounds")
    y = _experts(xg, blk_expert, blk_first, n_used, w_gate, w_up, w_down, layer)
    routed = y.at[pos].get(mode="promise_in_bounds")
    return _moe_tail(x1, routed, wts.T, p2d, sg.astype(BF16), su.astype(BF16), sd.astype(BF16),
                     ple_gate.astype(BF16), ple_bias.reshape(1, d), ple_proj.astype(BF16),
                     g1.reshape(1, d), b1.reshape(1, d), g2.reshape(1, d), b2.reshape(1, d))


def _gelu_tanh(x):
    return 0.5 * x * (1.0 + jnp.tanh(0.7978845608028654 * (x + 0.044715 * (x * x * x))))


def _compress_kernel(c_ref, pe_ref, w1_ref, b1_ref, w2_ref, o_ref, *, nc, half):
    out = jnp.zeros((nc, GROUP_LANES), F32)
    for g in range(NSA_KV_GROUPS):
        c = c_ref[0, 0, g].astype(F32)
        a = _dot((c + pe_ref[0, 0:1, :]).astype(BF16), w1_ref[0, :half, :])
        bm = _dot((c + pe_ref[0, 1:2, :]).astype(BF16), w1_ref[0, half:, :])
        h = _gelu_tanh(a + pltpu.roll(bm, nc - 1, 0) + b1_ref[0])
        out = out + _dot(h.astype(BF16), w2_ref[0, g])
    o_ref[0, 0] = out.astype(o_ref.dtype)


def _compress(chunks, pe_flat, w1, b1, w2_placed):
    b, _, _, nc, half = chunks.shape
    hid = w1.shape[2]
    return pl.pallas_call(
        functools.partial(_compress_kernel, nc=nc, half=half),
        grid=(b, 2),
        in_specs=[
            pl.BlockSpec((1, 1, NSA_KV_GROUPS, nc, half), lambda i, j: (i, j, 0, 0, 0)),
            pl.BlockSpec((1, 2, half), lambda i, j: (j, 0, 0)),
            pl.BlockSpec((1, 2 * half, hid), lambda i, j: (j, 0, 0)),
            pl.BlockSpec((1, 1, hid), lambda i, j: (j, 0, 0)),
            pl.BlockSpec((1, NSA_KV_GROUPS, hid, GROUP_LANES), lambda i, j: (j, 0, 0, 0)),
        ],
        out_specs=pl.BlockSpec((1, 1, nc, GROUP_LANES), lambda i, j: (i, j, 0, 0)),
        out_shape=jax.ShapeDtypeStruct((b, 2, nc, GROUP_LANES), BF16),
        compiler_params=_cparams(("arbitrary", "arbitrary")),
        name="compress",
    )(chunks, pe_flat, w1, b1, w2_placed)


def _group_mask(g):
    lane = lax.broadcasted_iota(jnp.int32, (1, GROUP_LANES), 1)
    return (lane >= g * HEAD_DIM) & (lane < (g + 1) * HEAD_DIM)


def _nsa_cmp_kernel(q_ref, kc_ref, vc_ref, gate_ref, slope_ref, wt_ref, o_ref, sel_ref, psum_ref,
                    *, tq, nc, n_slc):
    qi = pl.program_id(1)
    r = pl.program_id(2)
    q0 = qi * tq

    @pl.when(r == 0)
    def _():
        psum_ref[...] = jnp.zeros_like(psum_ref)

    q = q_ref[0]
    kc = kc_ref[0, 0]
    vc = vc_ref[0, 0]
    t_pos = q0 + lax.broadcasted_iota(jnp.int32, (tq, nc), 0)
    cmp_end = lax.broadcasted_iota(jnp.int32, (tq, nc), 1) * CMP_STRIDE + (CMP_BLOCK - 1)
    visible = cmp_end <= t_pos
    end_rel = (lax.broadcasted_iota(jnp.int32, (1, nc), 1) * CMP_STRIDE + (CMP_BLOCK - 1) - q0).astype(F32)
    gates = gate_ref[0, 0]
    out = jnp.zeros((tq, GROUP_LANES), F32)
    psum_prev = [psum_ref[g] for g in range(NSA_KV_GROUPS)]
    psum_new = []
    for g in range(NSA_KV_GROUPS):
        in_grp = _group_mask(g)
        qm = jnp.where(in_grp, q, jnp.zeros_like(q))
        s = _nt(qm, kc) + slope_ref[0, g, :, :nc] * end_rel
        s = jnp.where(visible, s, NEG)
        m = jnp.max(s, axis=1, keepdims=True)
        p = jnp.where(visible, jnp.exp2(s - m), 0.0)
        o_g = _dot(p.astype(BF16), jnp.where(in_grp, vc, jnp.ones_like(vc)))
        other = (1 - g // 2) * LANES
        l = o_g[:, other:other + LANES]
        inv = jnp.where(l > 0.0, 1.0 / l, 0.0)
        inv2 = jnp.concatenate([inv, inv], axis=1)
        psum_new.append(psum_prev[g] + p * inv2[:, :nc] if nc == GROUP_LANES else
                        psum_prev[g] + p * inv[:, :1])
        out = out + jnp.where(in_grp, o_g * (inv2 * gates[:, g:g + 1]), 0.0)
    for g in range(NSA_KV_GROUPS):
        psum_ref[g] = psum_new[g]
    o_ref[0] = out.astype(o_ref.dtype)

    @pl.when(r == NSA_HEADS_PER_GROUP - 1)
    def _():
        blk = lax.broadcasted_iota(jnp.int32, (n_slc, tq), 0)
        t_row = q0 + lax.broadcasted_iota(jnp.int32, (n_slc, tq), 1)
        cur = t_row // SLC_BLOCK
        forced = (blk == 0) | (blk == cur) | (blk == cur - 1)
        vis = blk * SLC_BLOCK <= t_row
        wt = wt_ref[...]
        sub8 = lax.broadcasted_iota(jnp.int32, (8, tq), 0)
        pad_rows = jnp.zeros((LANES - n_slc, tq), F32)
        for g in range(NSA_KV_GROUPS):
            hi, mid, lo = _split3(psum_ref[g])
            imp = _nt(wt, hi) + _nt(wt, mid) + _nt(wt, lo)
            score = jnp.where(forced, 1e9, jnp.where(vis, imp, NEG))
            beaten = jnp.zeros((n_slc, tq), F32)
            for j in range(n_slc):
                vj = score[j:j + 1, :]
                lo_r, hi_r = (j // 8) * 8, (j // 8) * 8 + 8
                parts = []
                if lo_r > 0:
                    parts.append(jnp.where(vj > score[:lo_r], 1.0, 0.0))
                own = score[lo_r:hi_r]
                parts.append(jnp.where(sub8 > (j % 8), jnp.where(vj >= own, 1.0, 0.0),
                                       jnp.where(vj > own, 1.0, 0.0)))
                if hi_r < n_slc:
                    parts.append(jnp.where(vj >= score[hi_r:], 1.0, 0.0))
                beaten = beaten + jnp.concatenate(parts, axis=0)
            keep = (beaten < float(min(SLC_TOPK, n_slc))) & (score > 0.5 * NEG)
            flags = jnp.concatenate([jnp.where(keep, 0.0, NEG), pad_rows], axis=0)
            sel_ref[0, g] = flags.T.astype(sel_ref.dtype)


def _nsa_compressed(q, cmp_kv, gates_c, slopes_rep, w_sel_t):
    b, s, _ = q.shape
    nc = cmp_kv.shape[2]
    n_slc = s // SLC_BLOCK
    tq = min(NSA_Q_TILE, s)
    rr = NSA_HEADS_PER_GROUP
    return pl.pallas_call(
        functools.partial(_nsa_cmp_kernel, tq=tq, nc=nc, n_slc=n_slc),
        grid=(b, s // tq, rr),
        in_specs=[
            pl.BlockSpec((1, tq, GROUP_LANES), lambda i, j, r: (i, j, r)),
            pl.BlockSpec((1, 1, nc, GROUP_LANES), lambda i, j, r: (i, 0, 0, 0)),
            pl.BlockSpec((1, 1, nc, GROUP_LANES), lambda i, j, r: (i, 1, 0, 0)),
            pl.BlockSpec((1, 1, tq, NSA_KV_GROUPS), lambda i, j, r: (i, r, j, 0)),
            pl.BlockSpec((1, NSA_KV_GROUPS, 1, slopes_rep.shape[3]), lambda i, j, r: (r, 0, 0, 0)),
            pl.BlockSpec((n_slc, nc), lambda i, j, r: (0, 0)),
        ],
        out_specs=[
            pl.BlockSpec((1, tq, GROUP_LANES), lambda i, j, r: (i, j, r)),
            pl.BlockSpec((1, NSA_KV_GROUPS, tq, LANES), lambda i, j, r: (i, 0, j, 0)),
        ],
        out_shape=[
            jax.ShapeDtypeStruct((b, s, rr * GROUP_LANES), BF16),
            jax.ShapeDtypeStruct((b, NSA_KV_GROUPS, s, LANES), BF16),
        ],
        scratch_shapes=[pltpu.VMEM((NSA_KV_GROUPS, tq, nc), F32)],
        compiler_params=_cparams(("arbitrary", "arbitrary", "arbitrary")),
        name="nsa_compressed",
    )(q, cmp_kv, cmp_kv, gates_c, slopes_rep, w_sel_t)


AUG_POS_LANE = 64
MODE_FULL, MODE_LOWER, MODE_UPPER = 0, 1, 2


def _nsa_flash_kernel(qt_ref, kt_ref, fl_ref, *refs, t, sub, use_sel):
    if use_sel:
        q_ref, k_ref, v_ref, ka_ref, gate_ref, slope_ref, sel_ref, o_ref, m_ref, acc_ref = refs
    else:
        q_ref, k_ref, v_ref, ka_ref, gate_ref, slope_ref, o_ref, m_ref, acc_ref = refs
    step = pl.program_id(2)
    flags = fl_ref[step]
    mode = flags >> 2

    @pl.when((flags & 1) != 0)
    def _():
        m_ref[...] = jnp.full_like(m_ref, NEG)
        acc_ref[...] = jnp.zeros_like(acc_ref)

    lane = lax.broadcasted_iota(jnp.int32, (1, LANES), 1)

    def body(mode_static):
        q = q_ref[0]
        k = k_ref[0]
        v = v_ref[0]
        ka = ka_ref[...]
        k_aug = (jnp.concatenate([k[:, :LANES], ka], axis=1), jnp.concatenate([ka, k[:, LANES:]], axis=1))
        items = []
        for g in range(NSA_KV_GROUPS):
            half = g // 2
            in_grp = (lane >= (g % 2) * HEAD_DIM) & (lane < (g % 2 + 1) * HEAD_DIM)
            q_own = jnp.where(in_grp, q[:, half * LANES:(half + 1) * LANES], jnp.zeros((t, LANES), BF16))
            q_extra = jnp.broadcast_to(slope_ref[0, g], (t, LANES))
            if use_sel:
                q_extra = jnp.where(lane < AUG_POS_LANE, sel_ref[0, g], q_extra)
            qg = jnp.concatenate([q_own, q_extra] if half == 0 else [q_extra, q_own], axis=1)
            vg = jnp.where(_group_mask(g), v, jnp.ones_like(v))
            for qs in range(t // sub):
                rows = slice(qs * sub, (qs + 1) * sub)
                if mode_static == MODE_FULL:
                    keys = slice(0, t)
                elif mode_static == MODE_LOWER:
                    keys = slice(0, (qs + 1) * sub)
                else:
                    keys = slice(qs * sub, t)
                s = _nt(qg[rows], k_aug[half][keys])
                if mode_static != MODE_FULL:
                    r_i = lax.broadcasted_iota(jnp.int32, (sub, sub), 0)
                    c_i = lax.broadcasted_iota(jnp.int32, (sub, sub), 1)
                    if mode_static == MODE_LOWER:
                        edge = jnp.where(c_i <= r_i, s[:, -sub:], NEG)
                        s = edge if s.shape[1] == sub else jnp.concatenate([s[:, :-sub], edge], axis=1)
                    else:
                        edge = jnp.where(c_i > r_i, s[:, :sub], NEG)
                        s = edge if s.shape[1] == sub else jnp.concatenate([edge, s[:, sub:]], axis=1)
                items.append(((g, rows), s, vg[keys], m_ref[g, rows, :], acc_ref[g, rows, :]))
        for (g, rows), m_new, acc_new in _flash_items(items):
            acc_ref[g, rows, :] = acc_new
            m_ref[g, rows, :] = m_new

    for mode_static in ((MODE_FULL, MODE_LOWER) if use_sel else (MODE_LOWER, MODE_UPPER)):
        @pl.when(mode == mode_static)
        def _(mode_static=mode_static):
            body(mode_static)

    @pl.when((flags & 2) != 0)
    def _():
        gates = gate_ref[0, 0]
        out = jnp.zeros((t, GROUP_LANES), F32)
        for g in range(NSA_KV_GROUPS):
            acc = acc_ref[g]
            denom = pltpu.roll(acc, LANES, 1)
            out = out + jnp.where(_group_mask(g), acc * (gates[:, g:g + 1] / denom), 0.0)
        o_ref[0] = out.astype(o_ref.dtype)


def _nsa_flash_tables(s, t, windowed):
    qt, kt, fl = [], [], []
    for qi in range(s // t):
        if windowed:
            tiles = [(qi - 1, MODE_UPPER)] if qi > 0 else []
            tiles.append((qi, MODE_LOWER))
        else:
            tiles = [(ki, MODE_FULL) for ki in range(qi)] + [(qi, MODE_LOWER)]
        for n, (ki, mode) in enumerate(tiles):
            qt.append(qi)
            kt.append(ki)
            fl.append((1 if n == 0 else 0) | (2 if n == len(tiles) - 1 else 0) | (mode << 2))
    return tuple(jnp.asarray(np.asarray(a, np.int32)) for a in (qt, kt, fl))


def _nsa_flash(q, kv, k_col, v_col, key_aug, gates, slope_rows, sel_q=None):
    b, s, _ = q.shape
    use_sel = sel_q is not None
    t = min(NSA_SEL_TILE if use_sel else NSA_WIN_TILE, s)
    assert use_sel or t == WINDOW, "the window branch assumes tile == WINDOW"
    qt, kt, fl = _nsa_flash_tables(s, t, not use_sel)
    rr = NSA_HEADS_PER_GROUP
    in_specs = [
        pl.BlockSpec((1, t, GROUP_LANES), lambda i, r, u, qt, kt, fl: (i, qt[u], r)),
        pl.BlockSpec((1, t, GROUP_LANES), lambda i, r, u, qt, kt, fl: (i, kt[u], k_col)),
        pl.BlockSpec((1, t, GROUP_LANES), lambda i, r, u, qt, kt, fl: (i, kt[u], v_col)),
        pl.BlockSpec((t, LANES), lambda i, r, u, qt, kt, fl: (kt[u], 0)),
        pl.BlockSpec((1, 1, t, NSA_KV_GROUPS), lambda i, r, u, qt, kt, fl: (i, r, qt[u], 0)),
        pl.BlockSpec((1, NSA_KV_GROUPS, 1, LANES), lambda i, r, u, qt, kt, fl: (r, 0, 0, 0)),
    ]
    args = [q, kv, kv, key_aug, gates, slope_rows]
    if use_sel:
        in_specs.append(pl.BlockSpec((1, NSA_KV_GROUPS, t, LANES), lambda i, r, u, qt, kt, fl: (i, 0, qt[u], 0)))
        args.append(sel_q)
    grid_spec = pltpu.PrefetchScalarGridSpec(
        num_scalar_prefetch=3,
        grid=(b, rr, int(qt.shape[0])),
        in_specs=in_specs,
        out_specs=pl.BlockSpec((1, t, GROUP_LANES), lambda i, r, u, qt, kt, fl: (i, qt[u], r)),
        scratch_shapes=[
            pltpu.VMEM((NSA_KV_GROUPS, t, LANES), F32),
            pltpu.VMEM((NSA_KV_GROUPS, t, GROUP_LANES), F32),
        ],
    )
    return pl.pallas_call(
        functools.partial(_nsa_flash_kernel, t=t, sub=min(NSA_SUB, t), use_sel=use_sel),
        grid_spec=grid_spec,
        out_shape=jax.ShapeDtypeStruct((b, s, rr * GROUP_LANES), BF16),
        compiler_params=_cparams(("arbitrary", "arbitrary", "arbitrary")),
        name="nsa_selected" if use_sel else "nsa_window",
    )(qt, kt, fl, *args)


def _nsa_key_aug(s):
    pos = np.arange(s)
    aug = np.zeros((s, LANES), np.float32)
    aug[pos, pos // SLC_BLOCK] = 1.0
    aug[:, AUG_POS_LANE:AUG_POS_LANE + 3] = (pos // 64)[:, None]
    aug[:, AUG_POS_LANE + 3:AUG_POS_LANE + 6] = (pos % 64)[:, None]
    return jnp.asarray(aug, BF16)


def _nsa_slope_rows(slopes_l2):
    rr, gg = slopes_l2.shape
    hi, mid, lo = _split3(jnp.asarray(slopes_l2, F32))
    pieces = jnp.stack([hi, mid, lo], axis=-1).astype(F32)
    rows = jnp.zeros((rr, gg, 1, LANES), F32)
    rows = rows.at[:, :, 0, AUG_POS_LANE:AUG_POS_LANE + 3].set(pieces * 64.0)
    rows = rows.at[:, :, 0, AUG_POS_LANE + 3:AUG_POS_LANE + 6].set(pieces)
    return rows.astype(BF16)


def _pad_lanes(a, width=LANES):
    return jnp.pad(a, ((0, 0), (0, width - a.shape[1])))


def _fox_layer(x2d, b, s, w_in, b_f, w_out, ln_g, ln_b):
    d = x2d.shape[1]
    attn = w_out.shape[0]
    n_heads = attn // HEAD_DIM
    scale = HEAD_DIM ** -0.5 * LOG2E
    col_scale = jnp.concatenate([jnp.full((attn,), scale, F32), jnp.ones((2 * attn,), F32)])[None, :]
    qkv, z = _project(x2d, w_in[:, :3 * attn].astype(BF16), col_scale,
                      _pad_lanes(w_in[:, 3 * attn:]), _pad_lanes(b_f[None, :]), False)
    gate_bias = _gate_cumsum(z.reshape(b, s, LANES), n_heads)
    o = _fox_attention(qkv.reshape(b, s, 3 * attn), gate_bias, n_heads)
    return _out_ln([o.reshape(b * s, attn)], w_out.astype(BF16), x2d,
                   ln_g.reshape(1, d), ln_b.reshape(1, d))


def _selection_weights_t(nc, n_slc):
    cs = np.arange(nc)[:, None] * CMP_STRIDE
    ce = cs + CMP_BLOCK
    ss = np.arange(n_slc)[None, :] * SLC_BLOCK
    se = ss + SLC_BLOCK
    w = np.clip(np.minimum(ce, se) - np.maximum(cs, ss), 0, None) / CMP_STRIDE
    w[nc - 1, :] = 0.0
    return jnp.asarray(w.T, BF16)


def _nsa_layer(x2d, b, s, w_kv, cmp_pe, cmp_w1, cmp_b1, cmp_w2, w_q, b_g, w_out, ln_g, ln_b):
    d = x2d.shape[1]
    gg, rr = NSA_KV_GROUPS, NSA_HEADS_PER_GROUP
    attn = gg * rr * HEAD_DIM
    n_heads = gg * rr
    wq = w_q[:, :attn].reshape(d, gg, rr, HEAD_DIM).transpose(0, 2, 1, 3).reshape(d, attn)
    wgate = w_q[:, attn:].reshape(d, gg, rr, 3).transpose(0, 3, 2, 1).reshape(d, 3 * n_heads)
    bgate = b_g.reshape(gg, rr, 3).transpose(2, 1, 0).reshape(1, 3 * n_heads)
    wo = w_out.reshape(gg, rr, HEAD_DIM, d).transpose(1, 0, 2, 3).reshape(attn, d)
    w_all = jnp.concatenate([wq, w_kv], axis=1).astype(BF16)
    col_scale = jnp.concatenate([jnp.full((attn,), HEAD_DIM ** -0.5 * LOG2E, F32),
                                 jnp.ones((w_kv.shape[1],), F32)])[None, :]
    qkv, gates = _project(x2d, w_all, col_scale, _pad_lanes(wgate), _pad_lanes(bgate), True)
    width = qkv.shape[1]
    qkv = qkv.reshape(b, s, width)
    gates = gates[:, :3 * n_heads].reshape(b, s, 3, rr, gg).transpose(2, 0, 3, 1, 4)

    nc = s // CMP_STRIDE
    half = CMP_STRIDE * HEAD_DIM
    raw = qkv[:, :, attn:attn + 2 * GROUP_LANES]
    chunks = raw.reshape(b, nc, CMP_STRIDE, 2, gg, HEAD_DIM).transpose(0, 3, 4, 1, 2, 5)
    chunks = chunks.reshape(b, 2, gg, nc, half)
    pe_flat = cmp_pe.reshape(2, 2, half)
    hid = cmp_w1.shape[2]
    w2_placed = jnp.zeros((2, gg, hid, GROUP_LANES), F32)
    for g in range(gg):
        w2_placed = w2_placed.at[:, g, :, g * HEAD_DIM:(g + 1) * HEAD_DIM].set(cmp_w2)
    cmp_kv = _compress(chunks, pe_flat, cmp_w1.astype(BF16), cmp_b1.reshape(2, 1, hid),
                       w2_placed.astype(BF16))

    slopes_l2 = (2.0 ** (-8.0 * np.arange(1, n_heads + 1) / n_heads) * LOG2E).reshape(gg, rr).T
    slopes_rep = jnp.asarray(np.broadcast_to(
        slopes_l2[:, :, None, None], (rr, gg, 1, nc)).astype(np.float32))
    n_slc = s // SLC_BLOCK
    assert n_slc <= AUG_POS_LANE, "selection flags must fit below the position lanes"
    o_c, sel_q = _nsa_compressed(qkv, cmp_kv, gates[0], slopes_rep, _selection_weights_t(nc, n_slc))
    key_aug = _nsa_key_aug(s)
    slope_rows = _nsa_slope_rows(slopes_l2)
    base = attn // GROUP_LANES
    o_s = _nsa_flash(qkv, qkv, base + 2, base + 3, key_aug, gates[1], slope_rows, sel_q)
    o_w = _nsa_flash(qkv, qkv, base + 4, base + 5, key_aug, gates[2], slope_rows)
    n = b * s
    return _out_ln([o_c.reshape(n, attn), o_s.reshape(n, attn), o_w.reshape(n, attn)],
                   wo.astype(BF16), x2d, ln_g.reshape(1, d), ln_b.reshape(1, d))


def kernel(x, p, fox_w_in, fox_b_f, fox_w_out, nsa_w_kv, cmp_pe, cmp_w1, cmp_b1, cmp_w2, nsa_w_q, nsa_b_g, nsa_w_out, ln_g, ln_b, moe_w_router, moe_b_router, moe_w_gate, moe_w_up, moe_w_down, shared_w_gate, shared_w_up, shared_w_down, ple_w_proj, ple_w_gate, ple_b_gate):
    b, s, d = x.shape
    n = b * s
    depth = p.shape[0]
    n_a = depth // 2
    h = x.reshape(n, d)
    for i in range(depth):
        if i < n_a:
            h1, h1b = _fox_layer(h, b, s, fox_w_in[i], fox_b_f[i], fox_w_out[i], ln_g[i, 0], ln_b[i, 0])
        else:
            j = i - n_a
            h1, h1b = _nsa_layer(h, b, s, nsa_w_kv, cmp_pe, cmp_w1, cmp_b1, cmp_w2,
                                 nsa_w_q[j], nsa_b_g[j], nsa_w_out[j], ln_g[i, 0], ln_b[i, 0])
        h = _moe_layer(h1, h1b, p[i].reshape(n, -1), i, moe_w_router[i], moe_b_router[i],
                       moe_w_gate, moe_w_up, moe_w_down,
                       shared_w_gate[i], shared_w_up[i], shared_w_down[i],
                       ple_w_proj[i], ple_w_gate[i], ple_b_gate[i],
                       ln_g[i, 1], ln_b[i, 1], ln_g[i, 2], ln_b[i, 2])
    return h.reshape(b, s, d)
```

```python
import functools

import numpy as np
import jax
import jax.numpy as jnp
from jax import lax
from jax.experimental import pallas as pl
from jax.experimental.pallas import tpu as pltpu

F32 = jnp.float32
BF16 = jnp.bfloat16
NEG = -1e30

HEAD_DIM = 64
LANES = 128
NSA_KV_GROUPS = 4
NSA_HEADS_PER_GROUP = 4
GROUP_LANES = NSA_KV_GROUPS * HEAD_DIM
CMP_BLOCK = 32
CMP_STRIDE = 16
SLC_BLOCK = 64
SLC_TOPK = 16
WINDOW = 512
EXPERT_TOPK = 8
EXPERT_GROUPS = 8
EXPERT_TOPK_GROUPS = 4
ROUTED_SCALE = 2.5
LN_EPS = 1e-5
DEPTH = 2
DEEPNORM_ALPHA = (2.0 * DEPTH) ** 0.25
VMEM_LIMIT = 52 * 1024 * 1024

ROW_TILE = 512
FOX_TILE = 1024
FOX_SUB = 128
NSA_Q_TILE = 256
NSA_SEL_TILE = 1024
NSA_WIN_TILE = 512
NSA_SUB = 256
ROUTER_TILE = 512
EXPERT_ROW_BLOCK = 512


def _cparams(sem):
    return pltpu.CompilerParams(dimension_semantics=sem, vmem_limit_bytes=VMEM_LIMIT)


def _nt(a, b):
    return lax.dot_general(a, b, (((1,), (1,)), ((), ())), preferred_element_type=F32)


def _dot(a, b):
    return jnp.dot(a, b, preferred_element_type=F32)


def _split2(a):
    hi = a.astype(BF16)
    lo = (a - hi.astype(F32)).astype(BF16)
    return hi, lo


def _split3(a):
    hi = a.astype(BF16)
    r = a - hi.astype(F32)
    mid = r.astype(BF16)
    lo = (r - mid.astype(F32)).astype(BF16)
    return hi, mid, lo


def _sigmoid(x):
    return 1.0 / (1.0 + jnp.exp(-x))


def _layer_norm(z, g, b):
    mu = jnp.mean(z, axis=-1, keepdims=True)
    zc = z - mu
    var = jnp.mean(zc * zc, axis=-1, keepdims=True)
    return zc * lax.rsqrt(var + LN_EPS) * g + b


def _proj_kernel(x_ref, w_ref, cs_ref, wg_ref, bg_ref, o_ref, g_ref, *, tn, gate_sigmoid):
    x = x_ref[...]
    xh = x.astype(BF16)
    m_out = w_ref.shape[1]
    for j in range(m_out // tn):
        sl = slice(j * tn, (j + 1) * tn)
        y = _dot(xh, w_ref[:, sl])
        o_ref[:, sl] = (y * cs_ref[:, sl]).astype(o_ref.dtype)
    xl = (x - xh.astype(F32)).astype(BF16)
    for i in range(wg_ref.shape[0]):
        wh, wl = _split2(wg_ref[i])
        g = _dot(xh, wh) + _dot(xh, wl) + _dot(xl, wh) + bg_ref[i]
        if gate_sigmoid:
            g = _sigmoid(g)
        g_ref[i] = g


def _project(x2d, w_bf16, col_scale, w_gate, b_gate, gate_sigmoid):
    n, k = x2d.shape
    m_out = w_bf16.shape[1]
    n_g = w_gate.shape[0]
    tm = min(ROW_TILE, n)
    return pl.pallas_call(
        functools.partial(_proj_kernel, tn=512, gate_sigmoid=gate_sigmoid),
        grid=(n // tm,),
        in_specs=[
            pl.BlockSpec((tm, k), lambda i: (i, 0)),
            pl.BlockSpec((k, m_out), lambda i: (0, 0)),
            pl.BlockSpec((1, m_out), lambda i: (0, 0)),
            pl.BlockSpec((n_g, k, LANES), lambda i: (0, 0, 0)),
            pl.BlockSpec((n_g, 1, LANES), lambda i: (0, 0, 0)),
        ],
        out_specs=[
            pl.BlockSpec((tm, m_out), lambda i: (i, 0)),
            pl.BlockSpec((n_g, tm, LANES), lambda i: (0, i, 0)),
        ],
        out_shape=[
            jax.ShapeDtypeStruct((n, m_out), BF16),
            jax.ShapeDtypeStruct((n_g, n, LANES), F32),
        ],
        compiler_params=_cparams(("arbitrary",)),
        name="project",
    )(x2d, w_bf16, col_scale, w_gate, b_gate)


LOG2E = 1.4426950408889634
GATE_PIECE_STRIDE = 16


def _gate_cumsum_kernel(z_ref, o_ref, carry_ref, *, ts, n_heads):
    @pl.when(pl.program_id(1) == 0)
    def _():
        carry_ref[...] = jnp.zeros_like(carry_ref)

    z = z_ref[0]
    log_f = jnp.minimum(z, 0.0) - jnp.log(1.0 + jnp.exp(-jnp.abs(z)))
    row = lax.broadcasted_iota(jnp.int32, (ts, ts), 0)
    col = lax.broadcasted_iota(jnp.int32, (ts, ts), 1)
    tri = jnp.where(col <= row, 1.0, 0.0).astype(BF16)
    hi, mid, lo = _split3(log_f)
    cs = _dot(tri, hi) + _dot(tri, mid) + _dot(tri, lo) + carry_ref[...]
    carry_ref[...] = cs[ts - 1:ts, :]
    lane = lax.broadcasted_iota(jnp.int32, (1, LANES), 1)
    bias = jnp.where(lane < n_heads, cs * (-LOG2E), 0.0)
    hi, mid, lo = _split3(bias)
    pieces = (hi.astype(F32) + pltpu.roll(mid.astype(F32), GATE_PIECE_STRIDE, 1)
              + pltpu.roll(lo.astype(F32), 2 * GATE_PIECE_STRIDE, 1))
    o_ref[0] = (pieces + pltpu.roll(pieces, HEAD_DIM, 1)).astype(o_ref.dtype)


def _gate_cumsum(z, n_heads):
    b, s, _ = z.shape
    ts = min(256, s)
    return pl.pallas_call(
        functools.partial(_gate_cumsum_kernel, ts=ts, n_heads=n_heads),
        grid=(b, s // ts),
        in_specs=[pl.BlockSpec((1, ts, LANES), lambda i, j: (i, j, 0))],
        out_specs=pl.BlockSpec((1, ts, LANES), lambda i, j: (i, j, 0)),
        out_shape=jax.ShapeDtypeStruct((b, s, LANES), BF16),
        scratch_shapes=[pltpu.VMEM((1, LANES), F32)],
        compiler_params=_cparams(("arbitrary", "arbitrary")),
        name="gate_cumsum",
    )(z)


def _flash_items(items):
    outs = []
    for key, s, v, m_prev, acc_prev in items:
        m_new = jnp.maximum(m_prev, jnp.max(s, axis=1, keepdims=True))
        alpha = jnp.exp2(m_prev - m_new)
        p = jnp.concatenate(
            [jnp.exp2(s[:, j * LANES:(j + 1) * LANES] - m_new[:, :LANES])
             for j in range(s.shape[1] // LANES)], axis=1)
        reps = acc_prev.shape[1] // LANES
        alpha = alpha if reps == 1 else jnp.concatenate([alpha] * reps, axis=1)
        outs.append((key, m_new, alpha * acc_prev + _dot(p.astype(BF16), v)))
    return outs


def _fox_kernel(qt_ref, kt_ref, q_ref, k_ref, v_ref, c_ref, o_ref, m_ref, acc_ref, *, t, sub):
    step = pl.program_id(2)
    pair = pl.program_id(1)
    qi = qt_ref[step]
    ki = kt_ref[step]

    @pl.when(ki == 0)
    def _():
        m_ref[...] = jnp.full_like(m_ref, NEG)
        acc_ref[...] = jnp.zeros_like(acc_ref)

    lane = lax.broadcasted_iota(jnp.int32, (1, LANES), 1)
    low = lane < HEAD_DIM

    def body(diag):
        q = q_ref[0]
        k = k_ref[0]
        v = v_ref[0]
        c = c_ref[0]
        items = []
        for h in range(2):
            own = low if h == 0 else jnp.logical_not(low)
            base = (1 - h) * HEAD_DIM + 2 * pair + h
            ones_at = ((lane == base) | (lane == base + GATE_PIECE_STRIDE)
                       | (lane == base + 2 * GATE_PIECE_STRIDE))
            qh = jnp.where(own, q, jnp.where(ones_at, 1.0, 0.0).astype(BF16))
            kh = jnp.where(own, k, c)
            vh = jnp.where(own, v, jnp.ones_like(v))
            for qs in range(t // sub):
                rows = slice(qs * sub, (qs + 1) * sub)
                nk = (qs + 1) * sub if diag else t
                s = _nt(qh[rows], kh[:nk])
                if diag:
                    r_i = lax.broadcasted_iota(jnp.int32, (sub, sub), 0)
                    c_i = lax.broadcasted_iota(jnp.int32, (sub, sub), 1)
                    last = jnp.where(c_i <= r_i, s[:, nk - sub:], NEG)
                    s = last if nk == sub else jnp.concatenate([s[:, :nk - sub], last], axis=1)
                items.append(((h, rows), s, vh[:nk], m_ref[h, rows, :], acc_ref[h, rows, :]))
        for (h, rows), m_new, acc_new in _flash_items(items):
            acc_ref[h, rows, :] = acc_new
            m_ref[h, rows, :] = m_new

    @pl.when(ki < qi)
    def _():
        body(False)

    @pl.when(ki == qi)
    def _():
        body(True)
        a0 = acc_ref[0]
        a1 = acc_ref[1]
        o0 = a0 / pltpu.roll(a0, HEAD_DIM, 1)
        o1 = a1 / pltpu.roll(a1, HEAD_DIM, 1)
        o_ref[0] = jnp.where(low, o0, o1).astype(o_ref.dtype)


def _fox_attention(qkv, gate_bias, n_heads):
    b, s, _ = qkv.shape
    t = min(FOX_TILE, s)
    nq = s // t
    pairs = n_heads // 2
    qt = np.concatenate([np.full((i + 1,), i, np.int32) for i in range(nq)])
    kt = np.concatenate([np.arange(i + 1, dtype=np.int32) for i in range(nq)])
    grid_spec = pltpu.PrefetchScalarGridSpec(
        num_scalar_prefetch=2,
        grid=(b, pairs, len(qt)),
        in_specs=[
            pl.BlockSpec((1, t, LANES), lambda i, j, u, qt, kt: (i, qt[u], j)),
            pl.BlockSpec((1, t, LANES), lambda i, j, u, qt, kt: (i, kt[u], pairs + j)),
            pl.BlockSpec((1, t, LANES), lambda i, j, u, qt, kt: (i, kt[u], 2 * pairs + j)),
            pl.BlockSpec((1, t, LANES), lambda i, j, u, qt, kt: (i, kt[u], 0)),
        ],
        out_specs=pl.BlockSpec((1, t, LANES), lambda i, j, u, qt, kt: (i, qt[u], j)),
        scratch_shapes=[
            pltpu.VMEM((2, t, LANES), F32),
            pltpu.VMEM((2, t, LANES), F32),
        ],
    )
    return pl.pallas_call(
        functools.partial(_fox_kernel, t=t, sub=min(FOX_SUB, t)),
        grid_spec=grid_spec,
        out_shape=jax.ShapeDtypeStruct((b, s, n_heads * HEAD_DIM), BF16),
        compiler_params=_cparams(("arbitrary", "arbitrary", "arbitrary")),
        name="fox_attention",
    )(jnp.asarray(qt), jnp.asarray(kt), qkv, qkv, qkv, gate_bias)


def _out_ln_kernel(*refs, n_in):
    o_refs = refs[:n_in]
    w_ref, x_ref, g_ref, b_ref, y_ref, yb_ref = refs[n_in:]
    if n_in == 1:
        o = o_refs[0][...]
    else:
        o = o_refs[0][...].astype(F32)
        for r in o_refs[1:]:
            o = o + r[...].astype(F32)
        o = o.astype(BF16)
    z = DEEPNORM_ALPHA * x_ref[...] + _dot(o, w_ref[...])
    y = _layer_norm(z, g_ref[...], b_ref[...])
    y_ref[...] = y
    yb_ref[...] = y.astype(BF16)


def _out_ln(o_list, w_bf16, x2d, g, b):
    n, d = x2d.shape
    k = w_bf16.shape[0]
    tm = min(ROW_TILE, n)
    n_in = len(o_list)
    row = lambda i: (i, 0)
    fixed = lambda i: (0, 0)
    return pl.pallas_call(
        functools.partial(_out_ln_kernel, n_in=n_in),
        grid=(n // tm,),
        in_specs=[pl.BlockSpec((tm, k), row)] * n_in + [
            pl.BlockSpec((k, d), fixed),
            pl.BlockSpec((tm, d), row),
            pl.BlockSpec((1, d), fixed),
            pl.BlockSpec((1, d), fixed),
        ],
        out_specs=[pl.BlockSpec((tm, d), row), pl.BlockSpec((tm, d), row)],
        out_shape=[jax.ShapeDtypeStruct((n, d), F32), jax.ShapeDtypeStruct((n, d), BF16)],
        compiler_params=_cparams(("arbitrary",)),
        name="out_ln",
    )(*o_list, w_bf16, x2d, g, b)


def _router_kernel(x_ref, wt_ref, b_ref, idx_ref, w_ref, rank_ref, cnt_ref, carry_ref, *, tn, n_exp):
    @pl.when(pl.program_id(0) == 0)
    def _():
        carry_ref[...] = jnp.zeros_like(carry_ref)

    xh, xl = _split2(x_ref[...])
    wh, wl = _split2(wt_ref[...])
    logits = _nt(wh, xh) + _nt(wh, xl) + _nt(wl, xh)
    scores = _sigmoid(logits)
    biased = scores + b_ref[...]

    per_grp = n_exp // EXPERT_GROUPS
    blocks, gscore = [], []
    for g in range(EXPERT_GROUPS):
        blk = biased[g * per_grp:(g + 1) * per_grp, :]
        m1 = jnp.max(blk, axis=0, keepdims=True)
        eq = blk == m1
        n_eq = jnp.sum(jnp.where(eq, 1.0, 0.0), axis=0, keepdims=True)
        m2 = jnp.max(jnp.where(eq, -3e38, blk), axis=0, keepdims=True)
        blocks.append(blk)
        gscore.append(m1 + jnp.where(n_eq >= 2.0, m1, m2))
    cand = []
    for g in range(EXPERT_GROUPS):
        beaten = jnp.zeros_like(gscore[g])
        for o in range(EXPERT_GROUPS):
            if o == g:
                continue
            wins = (gscore[o] >= gscore[g]) if o < g else (gscore[o] > gscore[g])
            beaten = beaten + jnp.where(wins, 1.0, 0.0)
        cand.append(jnp.where(beaten < float(EXPERT_TOPK_GROUPS), blocks[g], NEG))
    cand = jnp.concatenate(cand, axis=0)

    erow = lax.broadcasted_iota(jnp.int32, (n_exp, tn), 0).astype(F32)
    hot, sel_idx, sel_score = [], [], []
    member = jnp.zeros((n_exp, tn), F32)
    for _ in range(EXPERT_TOPK):
        m = jnp.max(cand, axis=0, keepdims=True)
        first = jnp.min(jnp.where(cand == m, erow, float(n_exp)), axis=0, keepdims=True)
        onehot = erow == first
        sel_idx.append(first)
        sel_score.append(jnp.sum(jnp.where(onehot, scores, 0.0), axis=0, keepdims=True))
        cand = jnp.where(onehot, -3e38, cand)
        member = member + jnp.where(onehot, 1.0, 0.0)
        hot.append(onehot)
    total = sel_score[0]
    for sc in sel_score[1:]:
        total = total + sc

    trow = lax.broadcasted_iota(jnp.int32, (tn, tn), 0)
    tcol = lax.broadcasted_iota(jnp.int32, (tn, tn), 1)
    before = jnp.where(trow < tcol, 1.0, 0.0).astype(BF16)
    prior = _dot(member.astype(BF16), before) + carry_ref[...]
    ranks = [jnp.sum(jnp.where(hot[k], prior, 0.0), axis=0, keepdims=True) for k in range(EXPERT_TOPK)]

    idx_ref[...] = jnp.concatenate(sel_idx, axis=0).astype(jnp.int32)
    w_ref[...] = jnp.concatenate([sc / total * ROUTED_SCALE for sc in sel_score], axis=0)
    rank_ref[...] = jnp.concatenate(ranks, axis=0).astype(jnp.int32)
    carry_ref[...] = carry_ref[...] + jnp.sum(member, axis=1, keepdims=True)
    cnt_ref[...] = carry_ref[...]


def _route(x2d, w_router_t, b_router):
    n, d = x2d.shape
    n_exp = w_router_t.shape[0]
    tn = min(ROUTER_TILE, n)
    col = lambda i: (0, i)
    return pl.pallas_call(
        functools.partial(_router_kernel, tn=tn, n_exp=n_exp),
        grid=(n // tn,),
        in_specs=[
            pl.BlockSpec((tn, d), lambda i: (i, 0)),
            pl.BlockSpec((n_exp, d), lambda i: (0, 0)),
            pl.BlockSpec((n_exp, 1), lambda i: (0, 0)),
        ],
        out_specs=[
            pl.BlockSpec((EXPERT_TOPK, tn), col),
            pl.BlockSpec((EXPERT_TOPK, tn), col),
            pl.BlockSpec((EXPERT_TOPK, tn), col),
            pl.BlockSpec((n_exp, 1), lambda i: (0, 0)),
        ],
        out_shape=[
            jax.ShapeDtypeStruct((EXPERT_TOPK, n), jnp.int32),
            jax.ShapeDtypeStruct((EXPERT_TOPK, n), F32),
            jax.ShapeDtypeStruct((EXPERT_TOPK, n), jnp.int32),
            jax.ShapeDtypeStruct((n_exp, 1), F32),
        ],
        scratch_shapes=[pltpu.VMEM((n_exp, 1), F32)],
        compiler_params=_cparams(("arbitrary",)),
        name="router",
    )(x2d, w_router_t, b_router)


def _experts_kernel(be_ref, first_ref, nused_ref, x_ref, wg_ref, wu_ref, wd_ref, y_ref,
                    wg_s, wu_s, wd_s):
    blk = pl.program_id(0)

    @pl.when(blk < nused_ref[0])
    def _():
        @pl.when(first_ref[blk] == 1)
        def _():
            wg_s[...] = wg_ref[0, 0].astype(BF16)
            wu_s[...] = wu_ref[0, 0].astype(BF16)
            wd_s[...] = wd_ref[0, 0].astype(BF16)

        x = x_ref[...]
        gate = _dot(x, wg_s[...])
        up = _dot(x, wu_s[...])
        h = gate * _sigmoid(gate) * up
        y_ref[...] = _dot(h.astype(BF16), wd_s[...]).astype(y_ref.dtype)

    @pl.when(blk >= nused_ref[0])
    def _():
        y_ref[...] = jnp.zeros_like(y_ref)


def _experts(xg, blk_expert, blk_first, n_used, w_gate, w_up, w_down, layer):
    n_rows, d = xg.shape
    hdim = w_gate.shape[3]
    tm = EXPERT_ROW_BLOCK
    n_blocks = n_rows // tm
    grid_spec = pltpu.PrefetchScalarGridSpec(
        num_scalar_prefetch=3,
        grid=(n_blocks,),
        in_specs=[
            pl.BlockSpec((tm, d), lambda i, be, fi, nu: (i, 0)),
            pl.BlockSpec((1, 1, d, hdim), lambda i, be, fi, nu: (layer, be[i], 0, 0)),
            pl.BlockSpec((1, 1, d, hdim), lambda i, be, fi, nu: (layer, be[i], 0, 0)),
            pl.BlockSpec((1, 1, hdim, d), lambda i, be, fi, nu: (layer, be[i], 0, 0)),
        ],
        out_specs=pl.BlockSpec((tm, d), lambda i, be, fi, nu: (i, 0)),
        scratch_shapes=[
            pltpu.VMEM((d, hdim), BF16),
            pltpu.VMEM((d, hdim), BF16),
            pltpu.VMEM((hdim, d), BF16),
        ],
    )
    return pl.pallas_call(
        _experts_kernel,
        grid_spec=grid_spec,
        out_shape=jax.ShapeDtypeStruct((n_rows, d), BF16),
        compiler_params=_cparams(("arbitrary",)),
        name="experts",
    )(blk_expert, blk_first, n_used, xg, w_gate, w_up, w_down)


def _moe_tail_kernel(x_ref, r_ref, rw_ref, p_ref, sg_ref, su_ref, sd_ref, pg_ref, pb_ref, pp_ref,
                     g1_ref, b1_ref, g2_ref, b2_ref, y_ref):
    x = x_ref[...]
    xb = x.astype(BF16)
    gate = _dot(xb, sg_ref[...])
    up = _dot(xb, su_ref[...])
    shared = _dot((gate * _sigmoid(gate) * up).astype(BF16), sd_ref[...])
    rw = rw_ref[...]
    routed = r_ref[0].astype(F32) * rw[:, 0:1]
    for k in range(1, r_ref.shape[0]):
        routed = routed + r_ref[k].astype(F32) * rw[:, k:k + 1]
    x2 = _layer_norm(DEEPNORM_ALPHA * x + (routed + shared), g1_ref[...], b1_ref[...])
    ple_gate = _sigmoid(_dot(x2.astype(BF16), pg_ref[...]) + pb_ref[...])
    ple = ple_gate * _dot(p_ref[...].astype(BF16), pp_ref[...])
    y_ref[...] = _layer_norm(DEEPNORM_ALPHA * x2 + ple, g2_ref[...], b2_ref[...])


def _moe_tail(x2d, routed, routed_w, p2d, sg, su, sd, pg, pb, pp, g1, b1, g2, b2):
    n, d = x2d.shape
    tm = min(ROW_TILE, n)
    topk = routed.shape[0]
    row = lambda i: (i, 0)
    fixed = lambda i: (0, 0)
    full = lambda a: pl.BlockSpec(a.shape, fixed)
    return pl.pallas_call(
        _moe_tail_kernel,
        grid=(n // tm,),
        in_specs=[
            pl.BlockSpec((tm, d), row),
            pl.BlockSpec((topk, tm, d), lambda i: (0, i, 0)),
            pl.BlockSpec((tm, topk), row),
            pl.BlockSpec((tm, p2d.shape[1]), row),
            full(sg), full(su), full(sd), full(pg), full(pb), full(pp),
            full(g1), full(b1), full(g2), full(b2),
        ],
        out_specs=pl.BlockSpec((tm, d), row),
        out_shape=jax.ShapeDtypeStruct((n, d), F32),
        compiler_params=_cparams(("arbitrary",)),
        name="moe_tail",
    )(x2d, routed, routed_w, p2d, sg, su, sd, pg, pb, pp, g1, b1, g2, b2)


def _moe_layer(x1, x1b, p2d, layer, w_router, b_router, w_gate, w_up, w_down, sg, su, sd,
               ple_proj, ple_gate, ple_bias, g1, b1, g2, b2):
    n, d = x1.shape
    n_exp = w_router.shape[1]
    idx, wts, rank, counts = _route(x1, w_router.T, b_router.reshape(n_exp, 1))
    tm = EXPERT_ROW_BLOCK
    counts = counts[:, 0].astype(jnp.int32)
    padded = (counts + tm - 1) // tm * tm
    pad_end = jnp.cumsum(padded)
    pad_start = pad_end - padded
    n_blocks = (n * EXPERT_TOPK + n_exp * (tm - 1)) // tm + 1
    n_rows = n_blocks * tm
    experts = jnp.arange(n_exp, dtype=jnp.int32)
    start_of = jnp.sum(jnp.where(idx[:, :, None] == experts, pad_start, 0), axis=-1)
    pos = start_of + rank
    tok = jnp.broadcast_to(jnp.arange(n, dtype=jnp.int32)[None, :], pos.shape)
    filler = jnp.arange(tm, dtype=jnp.int32)[None, :]
    filler_key = jnp.where(filler < (padded - counts)[:, None],
                           (pad_start + counts)[:, None] + filler, jnp.int32(2 ** 30))
    n_tail = max(n_rows - (n * EXPERT_TOPK + n_exp * tm), 0)
    keys = jnp.concatenate([pos.reshape(-1), filler_key.reshape(-1),
                            jnp.full((n_tail,), 2 ** 30, jnp.int32)])
    vals = jnp.concatenate([tok.reshape(-1), jnp.arange(n_exp * tm + n_tail, dtype=jnp.int32) % n])
    tok_pad = lax.sort((keys, vals), num_keys=1)[1][:n_rows]
    blk_start = jnp.arange(n_blocks, dtype=jnp.int32) * tm
    blk_expert = jnp.minimum(jnp.sum((pad_end[None, :] <= blk_start[:, None]).astype(jnp.int32), axis=1),
                             n_exp - 1)
    blk_first = jnp.concatenate([jnp.ones((1,), jnp.int32),
                                 (blk_expert[1:] != blk_expert[:-1]).astype(jnp.int32)])
    n_used = (pad_end[-1:] // tm).astype(jnp.int32)
    xg = x1b.at[tok_pad].get(mode="promise_in_bounds")
    y = _experts(xg, blk_expert, blk_first, n_used, w_gate, w_up, w_down, layer)
    routed = y.at[pos].get(mode="promise_in_bounds")
    return _moe_tail(x1, routed, wts.T, p2d, sg.astype(BF16), su.astype(BF16), sd.astype(BF16),
                     ple_gate.astype(BF16), ple_bias.reshape(1, d), ple_proj.astype(BF16),
                     g1.reshape(1, d), b1.reshape(1, d), g2.reshape(1, d), b2.reshape(1, d))


def _gelu_tanh(x):
    return 0.5 * x * (1.0 + jnp.tanh(0.7978845608028654 * (x + 0.044715 * (x * x * x))))


def _compress_kernel(c_ref, pe_ref, w1_ref, b1_ref, w2_ref, o_ref, *, nc, half):
    out = jnp.zeros((nc, GROUP_LANES), F32)
    for g in range(NSA_KV_GROUPS):
        c = c_ref[0, 0, g].astype(F32)
        a = _dot((c + pe_ref[0, 0:1, :]).astype(BF16), w1_ref[0, :half, :])
        bm = _dot((c + pe_ref[0, 1:2, :]).astype(BF16), w1_ref[0, half:, :])
        h = _gelu_tanh(a + pltpu.roll(bm, nc - 1, 0) + b1_ref[0])
        out = out + _dot(h.astype(BF16), w2_ref[0, g])
    o_ref[0, 0] = out.astype(o_ref.dtype)


def _compress(chunks, pe_flat, w1, b1, w2_placed):
    b, _, _, nc, half = chunks.shape
    hid = w1.shape[2]
    return pl.pallas_call(
        functools.partial(_compress_kernel, nc=nc, half=half),
        grid=(b, 2),
        in_specs=[
            pl.BlockSpec((1, 1, NSA_KV_GROUPS, nc, half), lambda i, j: (i, j, 0, 0, 0)),
            pl.BlockSpec((1, 2, half), lambda i, j: (j, 0, 0)),
            pl.BlockSpec((1, 2 * half, hid), lambda i, j: (j, 0, 0)),
            pl.BlockSpec((1, 1, hid), lambda i, j: (j, 0, 0)),
            pl.BlockSpec((1, NSA_KV_GROUPS, hid, GROUP_LANES), lambda i, j: (j, 0, 0, 0)),
        ],
        out_specs=pl.BlockSpec((1, 1, nc, GROUP_LANES), lambda i, j: (i, j, 0, 0)),
        out_shape=jax.ShapeDtypeStruct((b, 2, nc, GROUP_LANES), BF16),
        compiler_params=_cparams(("arbitrary", "arbitrary")),
        name="compress",
    )(chunks, pe_flat, w1, b1, w2_placed)


def _group_mask(g):
    lane = lax.broadcasted_iota(jnp.int32, (1, GROUP_LANES), 1)
    return (lane >= g * HEAD_DIM) & (lane < (g + 1) * HEAD_DIM)


def _nsa_cmp_kernel(q_ref, kc_ref, vc_ref, gate_ref, slope_ref, wt_ref, o_ref, sel_ref, psum_ref,
                    *, tq, nc, n_slc):
    qi = pl.program_id(1)
    r = pl.program_id(2)
    q0 = qi * tq

    @pl.when(r == 0)
    def _():
        psum_ref[...] = jnp.zeros_like(psum_ref)

    q = q_ref[0]
    kc = kc_ref[0, 0]
    vc = vc_ref[0, 0]
    t_pos = q0 + lax.broadcasted_iota(jnp.int32, (tq, nc), 0)
    cmp_end = lax.broadcasted_iota(jnp.int32, (tq, nc), 1) * CMP_STRIDE + (CMP_BLOCK - 1)
    visible = cmp_end <= t_pos
    end_rel = (lax.broadcasted_iota(jnp.int32, (1, nc), 1) * CMP_STRIDE + (CMP_BLOCK - 1) - q0).astype(F32)
    gates = gate_ref[0, 0]
    out = jnp.zeros((tq, GROUP_LANES), F32)
    psum_prev = [psum_ref[g] for g in range(NSA_KV_GROUPS)]
    psum_new = []
    for g in range(NSA_KV_GROUPS):
        in_grp = _group_mask(g)
        qm = jnp.where(in_grp, q, jnp.zeros_like(q))
        s = _nt(qm, kc) + slope_ref[0, g, :, :nc] * end_rel
        s = jnp.where(visible, s, NEG)
        m = jnp.max(s, axis=1, keepdims=True)
        p = jnp.where(visible, jnp.exp2(s - m), 0.0)
        o_g = _dot(p.astype(BF16), jnp.where(in_grp, vc, jnp.ones_like(vc)))
        other = (1 - g // 2) * LANES
        l = o_g[:, other:other + LANES]
        inv = jnp.where(l > 0.0, 1.0 / l, 0.0)
        inv2 = jnp.concatenate([inv, inv], axis=1)
        psum_new.append(psum_prev[g] + p * inv2[:, :nc] if nc == GROUP_LANES else
                        psum_prev[g] + p * inv[:, :1])
        out = out + jnp.where(in_grp, o_g * (inv2 * gates[:, g:g + 1]), 0.0)
    for g in range(NSA_KV_GROUPS):
        psum_ref[g] = psum_new[g]
    o_ref[0] = out.astype(o_ref.dtype)

    @pl.when(r == NSA_HEADS_PER_GROUP - 1)
    def _():
        blk = lax.broadcasted_iota(jnp.int32, (n_slc, tq), 0)
        t_row = q0 + lax.broadcasted_iota(jnp.int32, (n_slc, tq), 1)
        cur = t_row // SLC_BLOCK
        forced = (blk == 0) | (blk == cur) | (blk == cur - 1)
        vis = blk * SLC_BLOCK <= t_row
        wt = wt_ref[...]
        sub8 = lax.broadcasted_iota(jnp.int32, (8, tq), 0)
        pad_rows = jnp.zeros((LANES - n_slc, tq), F32)
        for g in range(NSA_KV_GROUPS):
            hi, mid, lo = _split3(psum_ref[g])
            imp = _nt(wt, hi) + _nt(wt, mid) + _nt(wt, lo)
            score = jnp.where(forced, 1e9, jnp.where(vis, imp, NEG))
            beaten = jnp.zeros((n_slc, tq), F32)
            for j in range(n_slc):
                vj = score[j:j + 1, :]
                lo_r, hi_r = (j // 8) * 8, (j // 8) * 8 + 8
                parts = []
                if lo_r > 0:
                    parts.append(jnp.where(vj > score[:lo_r], 1.0, 0.0))
                own = score[lo_r:hi_r]
                parts.append(jnp.where(sub8 > (j % 8), jnp.where(vj >= own, 1.0, 0.0),
                                       jnp.where(vj > own, 1.0, 0.0)))
                if hi_r < n_slc:
                    parts.append(jnp.where(vj >= score[hi_r:], 1.0, 0.0))
                beaten = beaten + jnp.concatenate(parts, axis=0)
            keep = (beaten < float(min(SLC_TOPK, n_slc))) & (score > 0.5 * NEG)
            flags = jnp.concatenate([jnp.where(keep, 0.0, NEG), pad_rows], axis=0)
            sel_ref[0, g] = flags.T.astype(sel_ref.dtype)


def _nsa_compressed(q, cmp_kv, gates_c, slopes_rep, w_sel_t):
    b, s, _ = q.shape
    nc = cmp_kv.shape[2]
    n_slc = s // SLC_BLOCK
    tq = min(NSA_Q_TILE, s)
    rr = NSA_HEADS_PER_GROUP
    return pl.pallas_call(
        functools.partial(_nsa_cmp_kernel, tq=tq, nc=nc, n_slc=n_slc),
        grid=(b, s // tq, rr),
        in_specs=[
            pl.BlockSpec((1, tq, GROUP_LANES), lambda i, j, r: (i, j, r)),
            pl.BlockSpec((1, 1, nc, GROUP_LANES), lambda i, j, r: (i, 0, 0, 0)),
            pl.BlockSpec((1, 1, nc, GROUP_LANES), lambda i, j, r: (i, 1, 0, 0)),
            pl.BlockSpec((1, 1, tq, LANES), lambda i, j, r: (r, i, j, 0)),
            pl.BlockSpec((1, NSA_KV_GROUPS, 1, slopes_rep.shape[3]), lambda i, j, r: (r, 0, 0, 0)),
            pl.BlockSpec((n_slc, nc), lambda i, j, r: (0, 0)),
        ],
        out_specs=[
            pl.BlockSpec((1, tq, GROUP_LANES), lambda i, j, r: (i, j, r)),
            pl.BlockSpec((1, NSA_KV_GROUPS, tq, LANES), lambda i, j, r: (i, 0, j, 0)),
        ],
        out_shape=[
            jax.ShapeDtypeStruct((b, s, rr * GROUP_LANES), BF16),
            jax.ShapeDtypeStruct((b, NSA_KV_GROUPS, s, LANES), BF16),
        ],
        scratch_shapes=[pltpu.VMEM((NSA_KV_GROUPS, tq, nc), F32)],
        compiler_params=_cparams(("arbitrary", "arbitrary", "arbitrary")),
        name="nsa_compressed",
    )(q, cmp_kv, cmp_kv, gates_c, slopes_rep, w_sel_t)


AUG_POS_LANE = 64
MODE_FULL, MODE_LOWER, MODE_UPPER = 0, 1, 2


def _nsa_flash_kernel(qt_ref, kt_ref, fl_ref, *refs, t, sub, use_sel):
    if use_sel:
        q_ref, k_ref, v_ref, ka_ref, gate_ref, slope_ref, sel_ref, o_ref, m_ref, acc_ref = refs
    else:
        q_ref, k_ref, v_ref, ka_ref, gate_ref, slope_ref, o_ref, m_ref, acc_ref = refs
    step = pl.program_id(2)
    flags = fl_ref[step]
    mode = flags >> 2

    @pl.when((flags & 1) != 0)
    def _():
        m_ref[...] = jnp.full_like(m_ref, NEG)
        acc_ref[...] = jnp.zeros_like(acc_ref)

    lane = lax.broadcasted_iota(jnp.int32, (1, LANES), 1)

    def body(mode_static):
        q = q_ref[0]
        k = k_ref[0]
        v = v_ref[0]
        ka = ka_ref[...]
        k_aug = (jnp.concatenate([k[:, :LANES], ka], axis=1), jnp.concatenate([ka, k[:, LANES:]], axis=1))
        items = []
        for g in range(NSA_KV_GROUPS):
            half = g // 2
            in_grp = (lane >= (g % 2) * HEAD_DIM) & (lane < (g % 2 + 1) * HEAD_DIM)
            q_own = jnp.where(in_grp, q[:, half * LANES:(half + 1) * LANES], jnp.zeros((t, LANES), BF16))
            q_extra = jnp.broadcast_to(slope_ref[0, g], (t, LANES))
            if use_sel:
                q_extra = jnp.where(lane < AUG_POS_LANE, sel_ref[0, g], q_extra)
            qg = jnp.concatenate([q_own, q_extra] if half == 0 else [q_extra, q_own], axis=1)
            vg = jnp.where(_group_mask(g), v, jnp.ones_like(v))
            for qs in range(t // sub):
                rows = slice(qs * sub, (qs + 1) * sub)
                if mode_static == MODE_FULL:
                    keys = slice(0, t)
                elif mode_static == MODE_LOWER:
                    keys = slice(0, (qs + 1) * sub)
                else:
                    keys = slice(qs * sub, t)
                s = _nt(qg[rows], k_aug[half][keys])
                if mode_static != MODE_FULL:
                    r_i = lax.broadcasted_iota(jnp.int32, (sub, sub), 0)
                    c_i = lax.broadcasted_iota(jnp.int32, (sub, sub), 1)
                    if mode_static == MODE_LOWER:
                        edge = jnp.where(c_i <= r_i, s[:, -sub:], NEG)
                        s = edge if s.shape[1] == sub else jnp.concatenate([s[:, :-sub], edge], axis=1)
                    else:
                        edge = jnp.where(c_i > r_i, s[:, :sub], NEG)
                        s = edge if s.shape[1] == sub else jnp.concatenate([edge, s[:, sub:]], axis=1)
                items.append(((g, rows), s, vg[keys], m_ref[g, rows, :], acc_ref[g, rows, :]))
        for (g, rows), m_new, acc_new in _flash_items(items):
            acc_ref[g, rows, :] = acc_new
            m_ref[g, rows, :] = m_new

    for mode_static in ((MODE_FULL, MODE_LOWER) if use_sel else (MODE_LOWER, MODE_UPPER)):
        @pl.when(mode == mode_static)
        def _(mode_static=mode_static):
            body(mode_static)

    @pl.when((flags & 2) != 0)
    def _():
        gates = gate_ref[0, 0]
        out = jnp.zeros((t, GROUP_LANES), F32)
        for g in range(NSA_KV_GROUPS):
            acc = acc_ref[g]
            denom = pltpu.roll(acc, LANES, 1)
            col = (1 if use_sel else 2) * NSA_KV_GROUPS + g
            out = out + jnp.where(_group_mask(g), acc * (gates[:, col:col + 1] / denom), 0.0)
        o_ref[0] = out.astype(o_ref.dtype)


def _nsa_flash_tables(s, t, windowed):
    qt, kt, fl = [], [], []
    for qi in range(s // t):
        if windowed:
            tiles = [(qi - 1, MODE_UPPER)] if qi > 0 else []
            tiles.append((qi, MODE_LOWER))
        else:
            tiles = [(ki, MODE_FULL) for ki in range(qi)] + [(qi, MODE_LOWER)]
        for n, (ki, mode) in enumerate(tiles):
            qt.append(qi)
            kt.append(ki)
            fl.append((1 if n == 0 else 0) | (2 if n == len(tiles) - 1 else 0) | (mode << 2))
    return tuple(jnp.asarray(np.asarray(a, np.int32)) for a in (qt, kt, fl))


def _nsa_flash(q, kv, k_col, v_col, key_aug, gates, slope_rows, sel_q=None):
    b, s, _ = q.shape
    use_sel = sel_q is not None
    t = min(NSA_SEL_TILE if use_sel else NSA_WIN_TILE, s)
    assert use_sel or t == WINDOW, "the window branch assumes tile == WINDOW"
    qt, kt, fl = _nsa_flash_tables(s, t, not use_sel)
    rr = NSA_HEADS_PER_GROUP
    in_specs = [
        pl.BlockSpec((1, t, GROUP_LANES), lambda i, r, u, qt, kt, fl: (i, qt[u], r)),
        pl.BlockSpec((1, t, GROUP_LANES), lambda i, r, u, qt, kt, fl: (i, kt[u], k_col)),
        pl.BlockSpec((1, t, GROUP_LANES), lambda i, r, u, qt, kt, fl: (i, kt[u], v_col)),
        pl.BlockSpec((t, LANES), lambda i, r, u, qt, kt, fl: (kt[u], 0)),
        pl.BlockSpec((1, 1, t, LANES), lambda i, r, u, qt, kt, fl: (r, i, qt[u], 0)),
        pl.BlockSpec((1, NSA_KV_GROUPS, 1, LANES), lambda i, r, u, qt, kt, fl: (r, 0, 0, 0)),
    ]
    args = [q, kv, kv, key_aug, gates, slope_rows]
    if use_sel:
        in_specs.append(pl.BlockSpec((1, NSA_KV_GROUPS, t, LANES), lambda i, r, u, qt, kt, fl: (i, 0, qt[u], 0)))
        args.append(sel_q)
    grid_spec = pltpu.PrefetchScalarGridSpec(
        num_scalar_prefetch=3,
        grid=(b, rr, int(qt.shape[0])),
        in_specs=in_specs,
        out_specs=pl.BlockSpec((1, t, GROUP_LANES), lambda i, r, u, qt, kt, fl: (i, qt[u], r)),
        scratch_shapes=[
            pltpu.VMEM((NSA_KV_GROUPS, t, LANES), F32),
            pltpu.VMEM((NSA_KV_GROUPS, t, GROUP_LANES), F32),
        ],
    )
    return pl.pallas_call(
        functools.partial(_nsa_flash_kernel, t=t, sub=min(NSA_SUB, t), use_sel=use_sel),
        grid_spec=grid_spec,
        out_shape=jax.ShapeDtypeStruct((b, s, rr * GROUP_LANES), BF16),
        compiler_params=_cparams(("arbitrary", "arbitrary", "arbitrary")),
        name="nsa_selected" if use_sel else "nsa_window",
    )(qt, kt, fl, *args)


def _nsa_key_aug(s):
    pos = np.arange(s)
    aug = np.zeros((s, LANES), np.float32)
    aug[pos, pos // SLC_BLOCK] = 1.0
    aug[:, AUG_POS_LANE:AUG_POS_LANE + 3] = (pos // 64)[:, None]
    aug[:, AUG_POS_LANE + 3:AUG_POS_LANE + 6] = (pos % 64)[:, None]
    return jnp.asarray(aug, BF16)


def _nsa_slope_rows(slopes_l2):
    rr, gg = slopes_l2.shape
    hi, mid, lo = _split3(jnp.asarray(slopes_l2, F32))
    pieces = jnp.stack([hi, mid, lo], axis=-1).astype(F32)
    rows = jnp.zeros((rr, gg, 1, LANES), F32)
    rows = rows.at[:, :, 0, AUG_POS_LANE:AUG_POS_LANE + 3].set(pieces * 64.0)
    rows = rows.at[:, :, 0, AUG_POS_LANE + 3:AUG_POS_LANE + 6].set(pieces)
    return rows.astype(BF16)


def _pad_lanes(a, width=LANES):
    return jnp.pad(a, ((0, 0), (0, width - a.shape[1])))


def _fox_layer(x2d, b, s, w_in, b_f, w_out, ln_g, ln_b):
    d = x2d.shape[1]
    attn = w_out.shape[0]
    n_heads = attn // HEAD_DIM
    scale = HEAD_DIM ** -0.5 * LOG2E
    col_scale = jnp.concatenate([jnp.full((attn,), scale, F32), jnp.ones((2 * attn,), F32)])[None, :]
    qkv, z = _project(x2d, w_in[:, :3 * attn].astype(BF16), col_scale,
                      _pad_lanes(w_in[:, 3 * attn:])[None], _pad_lanes(b_f[None, :])[None], False)
    gate_bias = _gate_cumsum(z.reshape(b, s, LANES), n_heads)
    o = _fox_attention(qkv.reshape(b, s, 3 * attn), gate_bias, n_heads)
    return _out_ln([o.reshape(b * s, attn)], w_out.astype(BF16), x2d,
                   ln_g.reshape(1, d), ln_b.reshape(1, d))


def _selection_weights_t(nc, n_slc):
    cs = np.arange(nc)[:, None] * CMP_STRIDE
    ce = cs + CMP_BLOCK
    ss = np.arange(n_slc)[None, :] * SLC_BLOCK
    se = ss + SLC_BLOCK
    w = np.clip(np.minimum(ce, se) - np.maximum(cs, ss), 0, None) / CMP_STRIDE
    w[nc - 1, :] = 0.0
    return jnp.asarray(w.T, BF16)


def _nsa_layer(x2d, b, s, w_kv, cmp_pe, cmp_w1, cmp_b1, cmp_w2, w_q, b_g, w_out, ln_g, ln_b):
    d = x2d.shape[1]
    gg, rr = NSA_KV_GROUPS, NSA_HEADS_PER_GROUP
    attn = gg * rr * HEAD_DIM
    n_heads = gg * rr
    wq = w_q[:, :attn].reshape(d, gg, rr, HEAD_DIM).transpose(0, 2, 1, 3).reshape(d, attn)
    wgate = w_q[:, attn:].reshape(d, gg, rr, 3).transpose(2, 0, 3, 1).reshape(rr, d, 3 * gg)
    bgate = b_g.reshape(gg, rr, 3).transpose(1, 2, 0).reshape(rr, 1, 3 * gg)
    wgate = jnp.pad(wgate, ((0, 0), (0, 0), (0, LANES - 3 * gg)))
    bgate = jnp.pad(bgate, ((0, 0), (0, 0), (0, LANES - 3 * gg)))
    wo = w_out.reshape(gg, rr, HEAD_DIM, d).transpose(1, 0, 2, 3).reshape(attn, d)
    w_all = jnp.concatenate([wq, w_kv], axis=1).astype(BF16)
    col_scale = jnp.concatenate([jnp.full((attn,), HEAD_DIM ** -0.5 * LOG2E, F32),
                                 jnp.ones((w_kv.shape[1],), F32)])[None, :]
    qkv, gates = _project(x2d, w_all, col_scale, wgate, bgate, True)
    width = qkv.shape[1]
    qkv = qkv.reshape(b, s, width)
    gates = gates.reshape(rr, b, s, LANES)

    nc = s // CMP_STRIDE
    half = CMP_STRIDE * HEAD_DIM
    raw = qkv[:, :, attn:attn + 2 * GROUP_LANES]
    chunks = raw.reshape(b, nc, CMP_STRIDE, 2, gg, HEAD_DIM).transpose(0, 3, 4, 1, 2, 5)
    chunks = chunks.reshape(b, 2, gg, nc, half)
    pe_flat = cmp_pe.reshape(2, 2, half)
    hid = cmp_w1.shape[2]
    w2_placed = jnp.zeros((2, gg, hid, GROUP_LANES), F32)
    for g in range(gg):
        w2_placed = w2_placed.at[:, g, :, g * HEAD_DIM:(g + 1) * HEAD_DIM].set(cmp_w2)
    cmp_kv = _compress(chunks, pe_flat, cmp_w1.astype(BF16), cmp_b1.reshape(2, 1, hid),
                       w2_placed.astype(BF16))

    slopes_l2 = (2.0 ** (-8.0 * np.arange(1, n_heads + 1) / n_heads) * LOG2E).reshape(gg, rr).T
    slopes_rep = jnp.asarray(np.broadcast_to(
        slopes_l2[:, :, None, None], (rr, gg, 1, nc)).astype(np.float32))
    n_slc = s // SLC_BLOCK
    assert n_slc <= AUG_POS_LANE, "selection flags must fit below the position lanes"
    o_c, sel_q = _nsa_compressed(qkv, cmp_kv, gates, slopes_rep, _selection_weights_t(nc, n_slc))
    key_aug = _nsa_key_aug(s)
    slope_rows = _nsa_slope_rows(slopes_l2)
    base = attn // GROUP_LANES
    o_s = _nsa_flash(qkv, qkv, base + 2, base + 3, key_aug, gates, slope_rows, sel_q)
    o_w = _nsa_flash(qkv, qkv, base + 4, base + 5, key_aug, gates, slope_rows)
    n = b * s
    return _out_ln([o_c.reshape(n, attn), o_s.reshape(n, attn), o_w.reshape(n, attn)],
                   wo.astype(BF16), x2d, ln_g.reshape(1, d), ln_b.reshape(1, d))


def kernel(x, p, fox_w_in, fox_b_f, fox_w_out, nsa_w_kv, cmp_pe, cmp_w1, cmp_b1, cmp_w2, nsa_w_q, nsa_b_g, nsa_w_out, ln_g, ln_b, moe_w_router, moe_b_router, moe_w_gate, moe_w_up, moe_w_down, shared_w_gate, shared_w_up, shared_w_down, ple_w_proj, ple_w_gate, ple_b_gate):
    b, s, d = x.shape
    n = b * s
    depth = p.shape[0]
    n_a = depth // 2
    h = x.reshape(n, d)
    for i in range(depth):
        if i < n_a:
            h1, h1b = _fox_layer(h, b, s, fox_w_in[i], fox_b_f[i], fox_w_out[i], ln_g[i, 0], ln_b[i, 0])
        else:
            j = i - n_a
            h1, h1b = _nsa_layer(h, b, s, nsa_w_kv, cmp_pe, cmp_w1, cmp_b1, cmp_w2,
                                 nsa_w_q[j], nsa_b_g[j], nsa_w_out[j], ln_g[i, 0], ln_b[i, 0])
        h = _moe_layer(h1, h1b, p[i].reshape(n, -1), i, moe_w_router[i], moe_b_router[i],
                       moe_w_gate, moe_w_up, moe_w_down,
                       shared_w_gate[i], shared_w_up[i], shared_w_down[i],
                       ple_w_proj[i], ple_w_gate[i], ple_b_gate[i],
                       ln_g[i, 1], ln_b[i, 1], ln_g[i, 2], ln_b[i, 2])
    return h.reshape(b, s, d)
```

```python
import functools

import numpy as np
import jax
import jax.numpy as jnp
from jax import lax
from jax.experimental import pallas as pl
from jax.experimental.pallas import tpu as pltpu

F32 = jnp.float32
BF16 = jnp.bfloat16
NEG = -1e30

HEAD_DIM = 64
LANES = 128
NSA_KV_GROUPS = 4
NSA_HEADS_PER_GROUP = 4
GROUP_LANES = NSA_KV_GROUPS * HEAD_DIM
CMP_BLOCK = 32
CMP_STRIDE = 16
SLC_BLOCK = 64
SLC_TOPK = 16
WINDOW = 512
EXPERT_TOPK = 8
EXPERT_GROUPS = 8
EXPERT_TOPK_GROUPS = 4
ROUTED_SCALE = 2.5
LN_EPS = 1e-5
DEPTH = 2
DEEPNORM_ALPHA = (2.0 * DEPTH) ** 0.25
VMEM_LIMIT = 52 * 1024 * 1024

ROW_TILE = 512
FOX_TILE = 1024
FOX_SUB = 128
NSA_Q_TILE = 256
NSA_SEL_TILE = 1024
NSA_WIN_TILE = 1024
NSA_SUB = 256
ROUTER_TILE = 512
EXPERT_ROW_BLOCK = 512


def _cparams(sem):
    return pltpu.CompilerParams(dimension_semantics=sem, vmem_limit_bytes=VMEM_LIMIT)


def _nt(a, b):
    return lax.dot_general(a, b, (((1,), (1,)), ((), ())), preferred_element_type=F32)


def _dot(a, b):
    return jnp.dot(a, b, preferred_element_type=F32)


def _split2(a):
    hi = a.astype(BF16)
    lo = (a - hi.astype(F32)).astype(BF16)
    return hi, lo


def _split3(a):
    hi = a.astype(BF16)
    r = a - hi.astype(F32)
    mid = r.astype(BF16)
    lo = (r - mid.astype(F32)).astype(BF16)
    return hi, mid, lo


def _sigmoid(x):
    return 1.0 / (1.0 + jnp.exp(-x))


def _layer_norm(z, g, b):
    mu = jnp.mean(z, axis=-1, keepdims=True)
    zc = z - mu
    var = jnp.mean(zc * zc, axis=-1, keepdims=True)
    return zc * lax.rsqrt(var + LN_EPS) * g + b


def _proj_kernel(x_ref, w_ref, cs_ref, wg_ref, bg_ref, o_ref, g_ref, *, tn, gate_sigmoid, gate_split):
    x = x_ref[...]
    xh = x.astype(BF16)
    m_out = w_ref.shape[1]
    for j in range(m_out // tn):
        sl = slice(j * tn, (j + 1) * tn)
        y = _dot(xh, w_ref[:, sl])
        o_ref[:, sl] = (y * cs_ref[:, sl]).astype(o_ref.dtype)
    xl = (x - xh.astype(F32)).astype(BF16)
    for i in range(wg_ref.shape[0]):
        wh, wl = _split2(wg_ref[i])
        g = _dot(xh, wh) + bg_ref[i]
        if gate_split:
            g = g + _dot(xh, wl) + _dot(xl, wh)
        if gate_sigmoid:
            g = _sigmoid(g)
        g_ref[i] = g


def _project(x2d, w_bf16, col_scale, w_gate, b_gate, gate_sigmoid, gate_split):
    n, k = x2d.shape
    m_out = w_bf16.shape[1]
    n_g = w_gate.shape[0]
    tm = min(ROW_TILE, n)
    return pl.pallas_call(
        functools.partial(_proj_kernel, tn=512, gate_sigmoid=gate_sigmoid, gate_split=gate_split),
        grid=(n // tm,),
        in_specs=[
            pl.BlockSpec((tm, k), lambda i: (i, 0)),
            pl.BlockSpec((k, m_out), lambda i: (0, 0)),
            pl.BlockSpec((1, m_out), lambda i: (0, 0)),
            pl.BlockSpec((n_g, k, LANES), lambda i: (0, 0, 0)),
            pl.BlockSpec((n_g, 1, LANES), lambda i: (0, 0, 0)),
        ],
        out_specs=[
            pl.BlockSpec((tm, m_out), lambda i: (i, 0)),
            pl.BlockSpec((n_g, tm, LANES), lambda i: (0, i, 0)),
        ],
        out_shape=[
            jax.ShapeDtypeStruct((n, m_out), BF16),
            jax.ShapeDtypeStruct((n_g, n, LANES), F32),
        ],
        compiler_params=_cparams(("arbitrary",)),
        name="project",
    )(x2d, w_bf16, col_scale, w_gate, b_gate)


LOG2E = 1.4426950408889634
GATE_PIECE_STRIDE = 16


def _gate_cumsum_kernel(z_ref, o_ref, carry_ref, *, ts, n_heads):
    @pl.when(pl.program_id(1) == 0)
    def _():
        carry_ref[...] = jnp.zeros_like(carry_ref)

    z = z_ref[0]
    log_f = jnp.minimum(z, 0.0) - jnp.log(1.0 + jnp.exp(-jnp.abs(z)))
    row = lax.broadcasted_iota(jnp.int32, (ts, ts), 0)
    col = lax.broadcasted_iota(jnp.int32, (ts, ts), 1)
    tri = jnp.where(col <= row, 1.0, 0.0).astype(BF16)
    hi, mid, lo = _split3(log_f)
    cs = _dot(tri, hi) + _dot(tri, mid) + _dot(tri, lo) + carry_ref[...]
    carry_ref[...] = cs[ts - 1:ts, :]
    lane = lax.broadcasted_iota(jnp.int32, (1, LANES), 1)
    bias = jnp.where(lane < n_heads, cs * (-LOG2E), 0.0)
    hi, mid, lo = _split3(bias)
    pieces = (hi.astype(F32) + pltpu.roll(mid.astype(F32), GATE_PIECE_STRIDE, 1)
              + pltpu.roll(lo.astype(F32), 2 * GATE_PIECE_STRIDE, 1))
    o_ref[0] = (pieces + pltpu.roll(pieces, HEAD_DIM, 1)).astype(o_ref.dtype)


def _gate_cumsum(z, n_heads):
    b, s, _ = z.shape
    ts = min(256, s)
    return pl.pallas_call(
        functools.partial(_gate_cumsum_kernel, ts=ts, n_heads=n_heads),
        grid=(b, s // ts),
        in_specs=[pl.BlockSpec((1, ts, LANES), lambda i, j: (i, j, 0))],
        out_specs=pl.BlockSpec((1, ts, LANES), lambda i, j: (i, j, 0)),
        out_shape=jax.ShapeDtypeStruct((b, s, LANES), BF16),
        scratch_shapes=[pltpu.VMEM((1, LANES), F32)],
        compiler_params=_cparams(("arbitrary", "arbitrary")),
        name="gate_cumsum",
    )(z)


def _flash_items(items):
    outs = []
    for key, s, v, m_prev, acc_prev in items:
        m_new = jnp.maximum(m_prev, jnp.max(s, axis=1, keepdims=True))
        alpha = jnp.exp2(m_prev - m_new)
        p = jnp.concatenate(
            [jnp.exp2(s[:, j * LANES:(j + 1) * LANES] - m_new[:, :LANES])
             for j in range(s.shape[1] // LANES)], axis=1)
        reps = acc_prev.shape[1] // LANES
        alpha = alpha if reps == 1 else jnp.concatenate([alpha] * reps, axis=1)
        outs.append((key, m_new, alpha * acc_prev + _dot(p.astype(BF16), v)))
    return outs


def _fox_kernel(qt_ref, kt_ref, q_ref, k_ref, v_ref, c_ref, o_ref, m_ref, acc_ref, *, t, sub):
    step = pl.program_id(2)
    pair = pl.program_id(1)
    qi = qt_ref[step]
    ki = kt_ref[step]

    @pl.when(ki == 0)
    def _():
        m_ref[...] = jnp.full_like(m_ref, NEG)
        acc_ref[...] = jnp.zeros_like(acc_ref)

    lane = lax.broadcasted_iota(jnp.int32, (1, LANES), 1)
    low = lane < HEAD_DIM

    def body(diag):
        q = q_ref[0]
        k = k_ref[0]
        v = v_ref[0]
        c = c_ref[0]
        items = []
        for h in range(2):
            own = low if h == 0 else jnp.logical_not(low)
            base = (1 - h) * HEAD_DIM + 2 * pair + h
            ones_at = ((lane == base) | (lane == base + GATE_PIECE_STRIDE)
                       | (lane == base + 2 * GATE_PIECE_STRIDE))
            qh = jnp.where(own, q, jnp.where(ones_at, 1.0, 0.0).astype(BF16))
            kh = jnp.where(own, k, c)
            vh = jnp.where(own, v, jnp.ones_like(v))
            for qs in range(t // sub):
                rows = slice(qs * sub, (qs + 1) * sub)
                nk = (qs + 1) * sub if diag else t
                s = _nt(qh[rows], kh[:nk])
                if diag:
                    r_i = lax.broadcasted_iota(jnp.int32, (sub, sub), 0)
                    c_i = lax.broadcasted_iota(jnp.int32, (sub, sub), 1)
                    last = jnp.where(c_i <= r_i, s[:, nk - sub:], NEG)
                    s = last if nk == sub else jnp.concatenate([s[:, :nk - sub], last], axis=1)
                items.append(((h, rows), s, vh[:nk], m_ref[h, rows, :], acc_ref[h, rows, :]))
        for (h, rows), m_new, acc_new in _flash_items(items):
            acc_ref[h, rows, :] = acc_new
            m_ref[h, rows, :] = m_new

    @pl.when(ki < qi)
    def _():
        body(False)

    @pl.when(ki == qi)
    def _():
        body(True)
        a0 = acc_ref[0]
        a1 = acc_ref[1]
        o0 = a0 / pltpu.roll(a0, HEAD_DIM, 1)
        o1 = a1 / pltpu.roll(a1, HEAD_DIM, 1)
        o_ref[0] = jnp.where(low, o0, o1).astype(o_ref.dtype)


def _fox_attention(qkv, gate_bias, n_heads):
    b, s, _ = qkv.shape
    t = min(FOX_TILE, s)
    nq = s // t
    pairs = n_heads // 2
    qt = np.concatenate([np.full((i + 1,), i, np.int32) for i in range(nq)])
    kt = np.concatenate([np.arange(i + 1, dtype=np.int32) for i in range(nq)])
    grid_spec = pltpu.PrefetchScalarGridSpec(
        num_scalar_prefetch=2,
        grid=(b, pairs, len(qt)),
        in_specs=[
            pl.BlockSpec((1, t, LANES), lambda i, j, u, qt, kt: (i, qt[u], j)),
            pl.BlockSpec((1, t, LANES), lambda i, j, u, qt, kt: (i, kt[u], pairs + j)),
            pl.BlockSpec((1, t, LANES), lambda i, j, u, qt, kt: (i, kt[u], 2 * pairs + j)),
            pl.BlockSpec((1, t, LANES), lambda i, j, u, qt, kt: (i, kt[u], 0)),
        ],
        out_specs=pl.BlockSpec((1, t, LANES), lambda i, j, u, qt, kt: (i, qt[u], j)),
        scratch_shapes=[
            pltpu.VMEM((2, t, LANES), F32),
            pltpu.VMEM((2, t, LANES), F32),
        ],
    )
    return pl.pallas_call(
        functools.partial(_fox_kernel, t=t, sub=min(FOX_SUB, t)),
        grid_spec=grid_spec,
        out_shape=jax.ShapeDtypeStruct((b, s, n_heads * HEAD_DIM), BF16),
        compiler_params=_cparams(("arbitrary", "arbitrary", "arbitrary")),
        name="fox_attention",
    )(jnp.asarray(qt), jnp.asarray(kt), qkv, qkv, qkv, gate_bias)


def _out_ln_kernel(*refs, n_in):
    o_refs = refs[:n_in]
    w_ref, x_ref, g_ref, b_ref, y_ref, yb_ref = refs[n_in:]
    if n_in == 1:
        o = o_refs[0][...]
    else:
        o = o_refs[0][...].astype(F32)
        for r in o_refs[1:]:
            o = o + r[...].astype(F32)
        o = o.astype(BF16)
    z = DEEPNORM_ALPHA * x_ref[...] + _dot(o, w_ref[...])
    y = _layer_norm(z, g_ref[...], b_ref[...])
    y_ref[...] = y
    yb_ref[...] = y.astype(BF16)


def _out_ln(o_list, w_bf16, x2d, g, b):
    n, d = x2d.shape
    k = w_bf16.shape[0]
    tm = min(ROW_TILE, n)
    n_in = len(o_list)
    row = lambda i: (i, 0)
    fixed = lambda i: (0, 0)
    return pl.pallas_call(
        functools.partial(_out_ln_kernel, n_in=n_in),
        grid=(n // tm,),
        in_specs=[pl.BlockSpec((tm, k), row)] * n_in + [
            pl.BlockSpec((k, d), fixed),
            pl.BlockSpec((tm, d), row),
            pl.BlockSpec((1, d), fixed),
            pl.BlockSpec((1, d), fixed),
        ],
        out_specs=[pl.BlockSpec((tm, d), row), pl.BlockSpec((tm, d), row)],
        out_shape=[jax.ShapeDtypeStruct((n, d), F32), jax.ShapeDtypeStruct((n, d), BF16)],
        compiler_params=_cparams(("arbitrary",)),
        name="out_ln",
    )(*o_list, w_bf16, x2d, g, b)


def _router_kernel(x_ref, wt_ref, b_ref, idx_ref, w_ref, rank_ref, cnt_ref, carry_ref, *, tn, n_exp):
    @pl.when(pl.program_id(0) == 0)
    def _():
        carry_ref[...] = jnp.zeros_like(carry_ref)

    xh, xl = _split2(x_ref[...])
    wh, wl = _split2(wt_ref[...])
    logits = _nt(wh, xh) + _nt(wh, xl) + _nt(wl, xh)
    scores = _sigmoid(logits)
    biased = scores + b_ref[...]

    per_grp = n_exp // EXPERT_GROUPS
    blocks, gscore = [], []
    for g in range(EXPERT_GROUPS):
        blk = biased[g * per_grp:(g + 1) * per_grp, :]
        m1 = jnp.max(blk, axis=0, keepdims=True)
        eq = blk == m1
        n_eq = jnp.sum(jnp.where(eq, 1.0, 0.0), axis=0, keepdims=True)
        m2 = jnp.max(jnp.where(eq, -3e38, blk), axis=0, keepdims=True)
        blocks.append(blk)
        gscore.append(m1 + jnp.where(n_eq >= 2.0, m1, m2))
    cand = []
    for g in range(EXPERT_GROUPS):
        beaten = jnp.zeros_like(gscore[g])
        for o in range(EXPERT_GROUPS):
            if o == g:
                continue
            wins = (gscore[o] >= gscore[g]) if o < g else (gscore[o] > gscore[g])
            beaten = beaten + jnp.where(wins, 1.0, 0.0)
        cand.append(jnp.where(beaten < float(EXPERT_TOPK_GROUPS), blocks[g], NEG))
    cand = jnp.concatenate(cand, axis=0)

    erow = lax.broadcasted_iota(jnp.int32, (n_exp, tn), 0).astype(F32)
    hot, sel_idx, sel_score = [], [], []
    member = jnp.zeros((n_exp, tn), F32)
    for _ in range(EXPERT_TOPK):
        m = jnp.max(cand, axis=0, keepdims=True)
        first = jnp.min(jnp.where(cand == m, erow, float(n_exp)), axis=0, keepdims=True)
        onehot = erow == first
        sel_idx.append(first)
        sel_score.append(jnp.sum(jnp.where(onehot, scores, 0.0), axis=0, keepdims=True))
        cand = jnp.where(onehot, -3e38, cand)
        member = member + jnp.where(onehot, 1.0, 0.0)
        hot.append(onehot)
    total = sel_score[0]
    for sc in sel_score[1:]:
        total = total + sc

    trow = lax.broadcasted_iota(jnp.int32, (tn, tn), 0)
    tcol = lax.broadcasted_iota(jnp.int32, (tn, tn), 1)
    before = jnp.where(trow < tcol, 1.0, 0.0).astype(BF16)
    prior = _dot(member.astype(BF16), before) + carry_ref[...]
    ranks = [jnp.sum(jnp.where(hot[k], prior, 0.0), axis=0, keepdims=True) for k in range(EXPERT_TOPK)]

    idx_ref[...] = jnp.concatenate(sel_idx, axis=0).astype(jnp.int32)
    w_ref[...] = jnp.concatenate([sc / total * ROUTED_SCALE for sc in sel_score], axis=0)
    rank_ref[...] = jnp.concatenate(ranks, axis=0).astype(jnp.int32)
    carry_ref[...] = carry_ref[...] + jnp.sum(member, axis=1, keepdims=True)
    cnt_ref[...] = carry_ref[...]


def _route(x2d, w_router_t, b_router):
    n, d = x2d.shape
    n_exp = w_router_t.shape[0]
    tn = min(ROUTER_TILE, n)
    col = lambda i: (0, i)
    return pl.pallas_call(
        functools.partial(_router_kernel, tn=tn, n_exp=n_exp),
        grid=(n // tn,),
        in_specs=[
            pl.BlockSpec((tn, d), lambda i: (i, 0)),
            pl.BlockSpec((n_exp, d), lambda i: (0, 0)),
            pl.BlockSpec((n_exp, 1), lambda i: (0, 0)),
        ],
        out_specs=[
            pl.BlockSpec((EXPERT_TOPK, tn), col),
            pl.BlockSpec((EXPERT_TOPK, tn), col),
            pl.BlockSpec((EXPERT_TOPK, tn), col),
            pl.BlockSpec((n_exp, 1), lambda i: (0, 0)),
        ],
        out_shape=[
            jax.ShapeDtypeStruct((EXPERT_TOPK, n), jnp.int32),
            jax.ShapeDtypeStruct((EXPERT_TOPK, n), F32),
            jax.ShapeDtypeStruct((EXPERT_TOPK, n), jnp.int32),
            jax.ShapeDtypeStruct((n_exp, 1), F32),
        ],
        scratch_shapes=[pltpu.VMEM((n_exp, 1), F32)],
        compiler_params=_cparams(("arbitrary",)),
        name="router",
    )(x2d, w_router_t, b_router)


def _experts_kernel(be_ref, first_ref, nused_ref, x_ref, wg_ref, wu_ref, wd_ref, y_ref,
                    wg_s, wu_s, wd_s):
    blk = pl.program_id(0)

    @pl.when(blk < nused_ref[0])
    def _():
        @pl.when(first_ref[blk] == 1)
        def _():
            wg_s[...] = wg_ref[0, 0].astype(BF16)
            wu_s[...] = wu_ref[0, 0].astype(BF16)
            wd_s[...] = wd_ref[0, 0].astype(BF16)

        x = x_ref[...]
        gate = _dot(x, wg_s[...])
        up = _dot(x, wu_s[...])
        h = gate * _sigmoid(gate) * up
        y_ref[...] = _dot(h.astype(BF16), wd_s[...]).astype(y_ref.dtype)

    @pl.when(blk >= nused_ref[0])
    def _():
        y_ref[...] = jnp.zeros_like(y_ref)


def _experts(xg, blk_expert, blk_first, n_used, w_gate, w_up, w_down, layer):
    n_rows, d = xg.shape
    hdim = w_gate.shape[3]
    tm = EXPERT_ROW_BLOCK
    n_blocks = n_rows // tm
    grid_spec = pltpu.PrefetchScalarGridSpec(
        num_scalar_prefetch=3,
        grid=(n_blocks,),
        in_specs=[
            pl.BlockSpec((tm, d), lambda i, be, fi, nu: (i, 0)),
            pl.BlockSpec((1, 1, d, hdim), lambda i, be, fi, nu: (layer, be[i], 0, 0)),
            pl.BlockSpec((1, 1, d, hdim), lambda i, be, fi, nu: (layer, be[i], 0, 0)),
            pl.BlockSpec((1, 1, hdim, d), lambda i, be, fi, nu: (layer, be[i], 0, 0)),
        ],
        out_specs=pl.BlockSpec((tm, d), lambda i, be, fi, nu: (i, 0)),
        scratch_shapes=[
            pltpu.VMEM((d, hdim), BF16),
            pltpu.VMEM((d, hdim), BF16),
            pltpu.VMEM((hdim, d), BF16),
        ],
    )
    return pl.pallas_call(
        _experts_kernel,
        grid_spec=grid_spec,
        out_shape=jax.ShapeDtypeStruct((n_rows, d), BF16),
        compiler_params=_cparams(("arbitrary",)),
        name="experts",
    )(blk_expert, blk_first, n_used, xg, w_gate, w_up, w_down)


def _moe_tail_kernel(x_ref, r_ref, rw_ref, p_ref, sg_ref, su_ref, sd_ref, pg_ref, pb_ref, pp_ref,
                     g1_ref, b1_ref, g2_ref, b2_ref, y_ref):
    x = x_ref[...]
    xb = x.astype(BF16)
    gate = _dot(xb, sg_ref[...])
    up = _dot(xb, su_ref[...])
    shared = _dot((gate * _sigmoid(gate) * up).astype(BF16), sd_ref[...])
    rw = rw_ref[...]
    routed = r_ref[0].astype(F32) * rw[:, 0:1]
    for k in range(1, r_ref.shape[0]):
        routed = routed + r_ref[k].astype(F32) * rw[:, k:k + 1]
    x2 = _layer_norm(DEEPNORM_ALPHA * x + (routed + shared), g1_ref[...], b1_ref[...])
    ple_gate = _sigmoid(_dot(x2.astype(BF16), pg_ref[...]) + pb_ref[...])
    ple = ple_gate * _dot(p_ref[...].astype(BF16), pp_ref[...])
    y_ref[...] = _layer_norm(DEEPNORM_ALPHA * x2 + ple, g2_ref[...], b2_ref[...])


def _moe_tail(x2d, routed, routed_w, p2d, sg, su, sd, pg, pb, pp, g1, b1, g2, b2):
    n, d = x2d.shape
    tm = min(ROW_TILE, n)
    topk = routed.shape[0]
    row = lambda i: (i, 0)
    fixed = lambda i: (0, 0)
    full = lambda a: pl.BlockSpec(a.shape, fixed)
    return pl.pallas_call(
        _moe_tail_kernel,
        grid=(n // tm,),
        in_specs=[
            pl.BlockSpec((tm, d), row),
            pl.BlockSpec((topk, tm, d), lambda i: (0, i, 0)),
            pl.BlockSpec((tm, topk), row),
            pl.BlockSpec((tm, p2d.shape[1]), row),
            full(sg), full(su), full(sd), full(pg), full(pb), full(pp),
            full(g1), full(b1), full(g2), full(b2),
        ],
        out_specs=pl.BlockSpec((tm, d), row),
        out_shape=jax.ShapeDtypeStruct((n, d), F32),
        compiler_params=_cparams(("arbitrary",)),
        name="moe_tail",
    )(x2d, routed, routed_w, p2d, sg, su, sd, pg, pb, pp, g1, b1, g2, b2)


def _moe_layer(x1, x1b, p2d, layer, w_router, b_router, w_gate, w_up, w_down, sg, su, sd,
               ple_proj, ple_gate, ple_bias, g1, b1, g2, b2):
    n, d = x1.shape
    n_exp = w_router.shape[1]
    idx, wts, rank, counts = _route(x1, w_router.T, b_router.reshape(n_exp, 1))
    tm = EXPERT_ROW_BLOCK
    counts = counts[:, 0].astype(jnp.int32)
    padded = (counts + tm - 1) // tm * tm
    pad_end = jnp.cumsum(padded)
    pad_start = pad_end - padded
    n_blocks = (n * EXPERT_TOPK + n_exp * (tm - 1)) // tm + 1
    n_rows = n_blocks * tm
    experts = jnp.arange(n_exp, dtype=jnp.int32)
    start_of = jnp.sum(jnp.where(idx[:, :, None] == experts, pad_start, 0), axis=-1)
    pos = start_of + rank
    tok = jnp.broadcast_to(jnp.arange(n, dtype=jnp.int32)[None, :], pos.shape)
    filler = jnp.arange(tm, dtype=jnp.int32)[None, :]
    filler_key = jnp.where(filler < (padded - counts)[:, None],
                           (pad_start + counts)[:, None] + filler, jnp.int32(2 ** 30))
    n_tail = max(n_rows - (n * EXPERT_TOPK + n_exp * tm), 0)
    keys = jnp.concatenate([pos.reshape(-1), filler_key.reshape(-1),
                            jnp.full((n_tail,), 2 ** 30, jnp.int32)])
    vals = jnp.concatenate([tok.reshape(-1), jnp.arange(n_exp * tm + n_tail, dtype=jnp.int32) % n])
    tok_pad = lax.sort((keys, vals), num_keys=1)[1][:n_rows]
    blk_start = jnp.arange(n_blocks, dtype=jnp.int32) * tm
    blk_expert = jnp.minimum(jnp.sum((pad_end[None, :] <= blk_start[:, None]).astype(jnp.int32), axis=1),
                             n_exp - 1)
    blk_first = jnp.concatenate([jnp.ones((1,), jnp.int32),
                                 (blk_expert[1:] != blk_expert[:-1]).astype(jnp.int32)])
    n_used = (pad_end[-1:] // tm).astype(jnp.int32)
    xg = x1b.at[tok_pad].get(mode="promise_in_bounds")
    y = _experts(xg, blk_expert, blk_first, n_used, w_gate, w_up, w_down, layer)
    routed = y.at[pos].get(mode="promise_in_bounds")
    return _moe_tail(x1, routed, wts.T, p2d, sg.astype(BF16), su.astype(BF16), sd.astype(BF16),
                     ple_gate.astype(BF16), ple_bias.reshape(1, d), ple_proj.astype(BF16),
                     g1.reshape(1, d), b1.reshape(1, d), g2.reshape(1, d), b2.reshape(1, d))


def _gelu_tanh(x):
    return 0.5 * x * (1.0 + jnp.tanh(0.7978845608028654 * (x + 0.044715 * (x * x * x))))


def _compress_kernel(c_ref, pe_ref, w1_ref, b1_ref, w2_ref, o_ref, *, nc, half):
    out = jnp.zeros((nc, GROUP_LANES), F32)
    for g in range(NSA_KV_GROUPS):
        c = c_ref[0, 0, g].astype(F32)
        a = _dot((c + pe_ref[0, 0:1, :]).astype(BF16), w1_ref[0, :half, :])
        bm = _dot((c + pe_ref[0, 1:2, :]).astype(BF16), w1_ref[0, half:, :])
        h = _gelu_tanh(a + pltpu.roll(bm, nc - 1, 0) + b1_ref[0])
        out = out + _dot(h.astype(BF16), w2_ref[0, g])
    o_ref[0, 0] = out.astype(o_ref.dtype)


def _compress(chunks, pe_flat, w1, b1, w2_placed):
    b, _, _, nc, half = chunks.shape
    hid = w1.shape[2]
    return pl.pallas_call(
        functools.partial(_compress_kernel, nc=nc, half=half),
        grid=(b, 2),
        in_specs=[
            pl.BlockSpec((1, 1, NSA_KV_GROUPS, nc, half), lambda i, j: (i, j, 0, 0, 0)),
            pl.BlockSpec((1, 2, half), lambda i, j: (j, 0, 0)),
            pl.BlockSpec((1, 2 * half, hid), lambda i, j: (j, 0, 0)),
            pl.BlockSpec((1, 1, hid), lambda i, j: (j, 0, 0)),
            pl.BlockSpec((1, NSA_KV_GROUPS, hid, GROUP_LANES), lambda i, j: (j, 0, 0, 0)),
        ],
        out_specs=pl.BlockSpec((1, 1, nc, GROUP_LANES), lambda i, j: (i, j, 0, 0)),
        out_shape=jax.ShapeDtypeStruct((b, 2, nc, GROUP_LANES), BF16),
        compiler_params=_cparams(("arbitrary", "arbitrary")),
        name="compress",
    )(chunks, pe_flat, w1, b1, w2_placed)


def _group_mask(g):
    lane = lax.broadcasted_iota(jnp.int32, (1, GROUP_LANES), 1)
    return (lane >= g * HEAD_DIM) & (lane < (g + 1) * HEAD_DIM)


def _nsa_cmp_kernel(q_ref, kc_ref, vc_ref, gate_ref, slope_ref, wt_ref, o_ref, sel_ref, psum_ref,
                    *, tq, nc, n_slc):
    qi = pl.program_id(1)
    r = pl.program_id(2)
    q0 = qi * tq

    @pl.when(r == 0)
    def _():
        psum_ref[...] = jnp.zeros_like(psum_ref)

    q = q_ref[0]
    kc = kc_ref[0, 0]
    vc = vc_ref[0, 0]
    t_pos = q0 + lax.broadcasted_iota(jnp.int32, (tq, nc), 0)
    cmp_end = lax.broadcasted_iota(jnp.int32, (tq, nc), 1) * CMP_STRIDE + (CMP_BLOCK - 1)
    visible = cmp_end <= t_pos
    end_rel = (lax.broadcasted_iota(jnp.int32, (1, nc), 1) * CMP_STRIDE + (CMP_BLOCK - 1) - q0).astype(F32)
    gates = gate_ref[0, 0]
    out = jnp.zeros((tq, GROUP_LANES), F32)
    psum_prev = [psum_ref[g] for g in range(NSA_KV_GROUPS)]
    psum_new = []
    for g in range(NSA_KV_GROUPS):
        in_grp = _group_mask(g)
        qm = jnp.where(in_grp, q, jnp.zeros_like(q))
        s = _nt(qm, kc) + slope_ref[0, g, :, :nc] * end_rel
        s = jnp.where(visible, s, NEG)
        m = jnp.max(s, axis=1, keepdims=True)
        p = jnp.where(visible, jnp.exp2(s - m), 0.0)
        o_g = _dot(p.astype(BF16), jnp.where(in_grp, vc, jnp.ones_like(vc)))
        other = (1 - g // 2) * LANES
        l = o_g[:, other:other + LANES]
        inv = jnp.where(l > 0.0, 1.0 / l, 0.0)
        inv2 = jnp.concatenate([inv, inv], axis=1)
        psum_new.append(psum_prev[g] + p * inv2[:, :nc] if nc == GROUP_LANES else
                        psum_prev[g] + p * inv[:, :1])
        out = out + jnp.where(in_grp, o_g * (inv2 * gates[:, g:g + 1]), 0.0)
    for g in range(NSA_KV_GROUPS):
        psum_ref[g] = psum_new[g]
    o_ref[0] = out.astype(o_ref.dtype)

    @pl.when(r == NSA_HEADS_PER_GROUP - 1)
    def _():
        blk = lax.broadcasted_iota(jnp.int32, (n_slc, tq), 0)
        t_row = q0 + lax.broadcasted_iota(jnp.int32, (n_slc, tq), 1)
        cur = t_row // SLC_BLOCK
        forced = (blk == 0) | (blk == cur) | (blk == cur - 1)
        vis = blk * SLC_BLOCK <= t_row
        wt = wt_ref[...]
        sub8 = lax.broadcasted_iota(jnp.int32, (8, tq), 0)
        pad_rows = jnp.zeros((LANES - n_slc, tq), F32)
        for g in range(NSA_KV_GROUPS):
            hi, mid, lo = _split3(psum_ref[g])
            imp = _nt(wt, hi) + _nt(wt, mid) + _nt(wt, lo)
            score = jnp.where(forced, 1e9, jnp.where(vis, imp, NEG))
            beaten = jnp.zeros((n_slc, tq), F32)
            for j in range(n_slc):
                vj = score[j:j + 1, :]
                lo_r, hi_r = (j // 8) * 8, (j // 8) * 8 + 8
                parts = []
                if lo_r > 0:
                    parts.append(jnp.where(vj > score[:lo_r], 1.0, 0.0))
                own = score[lo_r:hi_r]
                parts.append(jnp.where(sub8 > (j % 8), jnp.where(vj >= own, 1.0, 0.0),
                                       jnp.where(vj > own, 1.0, 0.0)))
                if hi_r < n_slc:
                    parts.append(jnp.where(vj >= score[hi_r:], 1.0, 0.0))
                beaten = beaten + jnp.concatenate(parts, axis=0)
            keep = (beaten < float(min(SLC_TOPK, n_slc))) & (score > 0.5 * NEG)
            flags = jnp.concatenate([jnp.where(keep, 0.0, NEG), pad_rows], axis=0)
            sel_ref[0, g] = flags.T.astype(sel_ref.dtype)


def _nsa_compressed(q, cmp_kv, gates_c, slopes_rep, w_sel_t):
    b, s, _ = q.shape
    nc = cmp_kv.shape[2]
    n_slc = s // SLC_BLOCK
    tq = min(NSA_Q_TILE, s)
    rr = NSA_HEADS_PER_GROUP
    return pl.pallas_call(
        functools.partial(_nsa_cmp_kernel, tq=tq, nc=nc, n_slc=n_slc),
        grid=(b, s // tq, rr),
        in_specs=[
            pl.BlockSpec((1, tq, GROUP_LANES), lambda i, j, r: (i, j, r)),
            pl.BlockSpec((1, 1, nc, GROUP_LANES), lambda i, j, r: (i, 0, 0, 0)),
            pl.BlockSpec((1, 1, nc, GROUP_LANES), lambda i, j, r: (i, 1, 0, 0)),
            pl.BlockSpec((1, 1, tq, LANES), lambda i, j, r: (r, i, j, 0)),
            pl.BlockSpec((1, NSA_KV_GROUPS, 1, slopes_rep.shape[3]), lambda i, j, r: (r, 0, 0, 0)),
            pl.BlockSpec((n_slc, nc), lambda i, j, r: (0, 0)),
        ],
        out_specs=[
            pl.BlockSpec((1, tq, GROUP_LANES), lambda i, j, r: (i, j, r)),
            pl.BlockSpec((1, NSA_KV_GROUPS, tq, LANES), lambda i, j, r: (i, 0, j, 0)),
        ],
        out_shape=[
            jax.ShapeDtypeStruct((b, s, rr * GROUP_LANES), BF16),
            jax.ShapeDtypeStruct((b, NSA_KV_GROUPS, s, LANES), BF16),
        ],
        scratch_shapes=[pltpu.VMEM((NSA_KV_GROUPS, tq, nc), F32)],
        compiler_params=_cparams(("arbitrary", "arbitrary", "arbitrary")),
        name="nsa_compressed",
    )(q, cmp_kv, cmp_kv, gates_c, slopes_rep, w_sel_t)


AUG_POS_LANE = 64
MODE_FULL, MODE_LOWER, MODE_UPPER = 0, 1, 2


def _nsa_flash_kernel(qt_ref, kt_ref, fl_ref, *refs, t, sub, use_sel):
    if use_sel:
        q_ref, k_ref, v_ref, ka_ref, gate_ref, slope_ref, sel_ref, o_ref, m_ref, acc_ref = refs
    else:
        q_ref, k_ref, v_ref, ka_ref, gate_ref, slope_ref, o_ref, m_ref, acc_ref = refs
    step = pl.program_id(2)
    flags = fl_ref[step]
    mode = flags >> 2

    @pl.when((flags & 1) != 0)
    def _():
        m_ref[...] = jnp.full_like(m_ref, NEG)
        acc_ref[...] = jnp.zeros_like(acc_ref)

    lane = lax.broadcasted_iota(jnp.int32, (1, LANES), 1)

    def body(mode_static):
        q = q_ref[0]
        k = k_ref[0]
        v = v_ref[0]
        ka = ka_ref[...]
        k_aug = (jnp.concatenate([k[:, :LANES], ka], axis=1), jnp.concatenate([ka, k[:, LANES:]], axis=1))
        items = []
        for g in range(NSA_KV_GROUPS):
            half = g // 2
            in_grp = (lane >= (g % 2) * HEAD_DIM) & (lane < (g % 2 + 1) * HEAD_DIM)
            q_own = jnp.where(in_grp, q[:, half * LANES:(half + 1) * LANES], jnp.zeros((t, LANES), BF16))
            q_extra = jnp.broadcast_to(slope_ref[0, g], (t, LANES))
            if use_sel:
                q_extra = jnp.where(lane < AUG_POS_LANE, sel_ref[0, g], q_extra)
            qg = jnp.concatenate([q_own, q_extra] if half == 0 else [q_extra, q_own], axis=1)
            vg = jnp.where(_group_mask(g), v, jnp.ones_like(v))
            for qs in range(t // sub):
                rows = slice(qs * sub, (qs + 1) * sub)
                if mode_static == MODE_FULL:
                    keys = slice(0, t)
                elif mode_static == MODE_LOWER:
                    keys = slice(0, (qs + 1) * sub)
                else:
                    keys = slice(qs * sub, t)
                s = _nt(qg[rows], k_aug[half][keys])
                if mode_static != MODE_FULL:
                    r_i = lax.broadcasted_iota(jnp.int32, (sub, sub), 0)
                    c_i = lax.broadcasted_iota(jnp.int32, (sub, sub), 1)
                    if mode_static == MODE_LOWER:
                        edge = jnp.where(c_i <= r_i, s[:, -sub:], NEG)
                        s = edge if s.shape[1] == sub else jnp.concatenate([s[:, :-sub], edge], axis=1)
                    else:
                        edge = jnp.where(c_i > r_i, s[:, :sub], NEG)
                        s = edge if s.shape[1] == sub else jnp.concatenate([edge, s[:, sub:]], axis=1)
                items.append(((g, rows), s, vg[keys], m_ref[g, rows, :], acc_ref[g, rows, :]))
        for (g, rows), m_new, acc_new in _flash_items(items):
            acc_ref[g, rows, :] = acc_new
            m_ref[g, rows, :] = m_new

    for mode_static in ((MODE_FULL, MODE_LOWER) if use_sel else (MODE_LOWER, MODE_UPPER)):
        @pl.when(mode == mode_static)
        def _(mode_static=mode_static):
            body(mode_static)

    @pl.when((flags & 2) != 0)
    def _():
        gates = gate_ref[0, 0]
        out = jnp.zeros((t, GROUP_LANES), F32)
        for g in range(NSA_KV_GROUPS):
            acc = acc_ref[g]
            denom = pltpu.roll(acc, LANES, 1)
            col = (1 if use_sel else 2) * NSA_KV_GROUPS + g
            out = out + jnp.where(_group_mask(g), acc * (gates[:, col:col + 1] / denom), 0.0)
        o_ref[0] = out.astype(o_ref.dtype)


def _nsa_flash_tables(s, t, windowed):
    qt, kt, fl = [], [], []
    for qi in range(s // t):
        if windowed:
            tiles = [(qi - 1, MODE_UPPER)] if qi > 0 else []
            tiles.append((qi, MODE_LOWER))
        else:
            tiles = [(ki, MODE_FULL) for ki in range(qi)] + [(qi, MODE_LOWER)]
        for n, (ki, mode) in enumerate(tiles):
            qt.append(qi)
            kt.append(ki)
            fl.append((1 if n == 0 else 0) | (2 if n == len(tiles) - 1 else 0) | (mode << 2))
    return tuple(jnp.asarray(np.asarray(a, np.int32)) for a in (qt, kt, fl))


def _nsa_flash(q, kv, k_col, v_col, key_aug, gates, slope_rows, sel_q=None):
    b, s, _ = q.shape
    use_sel = sel_q is not None
    t = min(NSA_SEL_TILE if use_sel else NSA_WIN_TILE, s)
    assert use_sel or t == WINDOW, "the window branch assumes tile == WINDOW"
    qt, kt, fl = _nsa_flash_tables(s, t, not use_sel)
    rr = NSA_HEADS_PER_GROUP
    in_specs = [
        pl.BlockSpec((1, t, GROUP_LANES), lambda i, r, u, qt, kt, fl: (i, qt[u], r)),
        pl.BlockSpec((1, t, GROUP_LANES), lambda i, r, u, qt, kt, fl: (i, kt[u], k_col)),
        pl.BlockSpec((1, t, GROUP_LANES), lambda i, r, u, qt, kt, fl: (i, kt[u], v_col)),
        pl.BlockSpec((t, LANES), lambda i, r, u, qt, kt, fl: (kt[u], 0)),
        pl.BlockSpec((1, 1, t, LANES), lambda i, r, u, qt, kt, fl: (r, i, qt[u], 0)),
        pl.BlockSpec((1, NSA_KV_GROUPS, 1, LANES), lambda i, r, u, qt, kt, fl: (r, 0, 0, 0)),
    ]
    args = [q, kv, kv, key_aug, gates, slope_rows]
    if use_sel:
        in_specs.append(pl.BlockSpec((1, NSA_KV_GROUPS, t, LANES), lambda i, r, u, qt, kt, fl: (i, 0, qt[u], 0)))
        args.append(sel_q)
    grid_spec = pltpu.PrefetchScalarGridSpec(
        num_scalar_prefetch=3,
        grid=(b, rr, int(qt.shape[0])),
        in_specs=in_specs,
        out_specs=pl.BlockSpec((1, t, GROUP_LANES), lambda i, r, u, qt, kt, fl: (i, qt[u], r)),
        scratch_shapes=[
            pltpu.VMEM((NSA_KV_GROUPS, t, LANES), F32),
            pltpu.VMEM((NSA_KV_GROUPS, t, GROUP_LANES), F32),
        ],
    )
    return pl.pallas_call(
        functools.partial(_nsa_flash_kernel, t=t, sub=min(NSA_SUB, t), use_sel=use_sel),
        grid_spec=grid_spec,
        out_shape=jax.ShapeDtypeStruct((b, s, rr * GROUP_LANES), BF16),
        compiler_params=_cparams(("arbitrary", "arbitrary", "arbitrary")),
        name="nsa_selected" if use_sel else "nsa_window",
    )(qt, kt, fl, *args)


def _nsa_window_kernel(q_ref, kp_ref, kc_ref, vp_ref, vc_ref, kap_ref, kac_ref, gate_ref, slope_ref, o_ref,
                       *, t, sub):
    has_prev = pl.program_id(2) > 0
    lane = lax.broadcasted_iota(jnp.int32, (1, LANES), 1)
    q = q_ref[0]
    k = jnp.concatenate([kp_ref[0], kc_ref[0]], axis=0)
    v = jnp.concatenate([vp_ref[0], vc_ref[0]], axis=0)
    ka = jnp.concatenate([kap_ref[...], kac_ref[...]], axis=0)
    k_aug = (jnp.concatenate([k[:, :LANES], ka], axis=1), jnp.concatenate([ka, k[:, LANES:]], axis=1))
    gates = gate_ref[0, 0]
    span = WINDOW + sub
    r_i = lax.broadcasted_iota(jnp.int32, (sub, sub), 0)
    c_i = lax.broadcasted_iota(jnp.int32, (sub, sub), 1)
    out_rows = [jnp.zeros((sub, GROUP_LANES), F32) for _ in range(t // sub)]
    for g in range(NSA_KV_GROUPS):
        half = g // 2
        in_grp = (lane >= (g % 2) * HEAD_DIM) & (lane < (g % 2 + 1) * HEAD_DIM)
        q_own = jnp.where(in_grp, q[:, half * LANES:(half + 1) * LANES], jnp.zeros((t, LANES), BF16))
        q_extra = jnp.broadcast_to(slope_ref[0, g], (t, LANES))
        qg = jnp.concatenate([q_own, q_extra] if half == 0 else [q_extra, q_own], axis=1)
        vg = jnp.where(_group_mask(g), v, jnp.ones_like(v))
        col = 2 * NSA_KV_GROUPS + g
        for qs in range(t // sub):
            rows = slice(qs * sub, (qs + 1) * sub)
            keys = slice(qs * sub, qs * sub + span)
            s = _nt(qg[rows], k_aug[half][keys])
            chunks = [s[:, j * sub:(j + 1) * sub] for j in range(span // sub)]
            chunks[0] = jnp.where(c_i > r_i, chunks[0], NEG)
            chunks[-1] = jnp.where(c_i <= r_i, chunks[-1], NEG)
            for j in range(len(chunks) - 1):
                if qs * sub + j * sub < WINDOW:
                    chunks[j] = jnp.where(has_prev, chunks[j], NEG)
            m = chunks[0]
            for c in chunks[1:]:
                m = jnp.maximum(m, c)
            m = jnp.max(m, axis=1, keepdims=True)
            p = jnp.concatenate([jnp.exp2(c - m) for c in chunks], axis=1)
            acc = _dot(p.astype(BF16), vg[keys])
            denom = pltpu.roll(acc, LANES, 1)
            out_rows[qs] = out_rows[qs] + jnp.where(
                _group_mask(g), acc * (gates[rows, col:col + 1] / denom), 0.0)
    o_ref[0] = jnp.concatenate(out_rows, axis=0).astype(o_ref.dtype)


def _nsa_window(q, kv, k_col, v_col, key_aug, gates, slope_rows):
    b, s, _ = q.shape
    t = min(NSA_WIN_TILE, s)
    sub = min(NSA_SUB, t)
    assert WINDOW % sub == 0 and t % WINDOW == 0
    rr = NSA_HEADS_PER_GROUP
    per = t // WINDOW
    prev = lambda j: jnp.maximum(j * per - 1, 0)
    return pl.pallas_call(
        functools.partial(_nsa_window_kernel, t=t, sub=sub),
        grid=(b, rr, s // t),
        in_specs=[
            pl.BlockSpec((1, t, GROUP_LANES), lambda i, r, j: (i, j, r)),
            pl.BlockSpec((1, WINDOW, GROUP_LANES), lambda i, r, j: (i, prev(j), k_col)),
            pl.BlockSpec((1, t, GROUP_LANES), lambda i, r, j: (i, j, k_col)),
            pl.BlockSpec((1, WINDOW, GROUP_LANES), lambda i, r, j: (i, prev(j), v_col)),
            pl.BlockSpec((1, t, GROUP_LANES), lambda i, r, j: (i, j, v_col)),
            pl.BlockSpec((WINDOW, LANES), lambda i, r, j: (prev(j), 0)),
            pl.BlockSpec((t, LANES), lambda i, r, j: (j, 0)),
            pl.BlockSpec((1, 1, t, LANES), lambda i, r, j: (r, i, j, 0)),
            pl.BlockSpec((1, NSA_KV_GROUPS, 1, LANES), lambda i, r, j: (r, 0, 0, 0)),
        ],
        out_specs=pl.BlockSpec((1, t, GROUP_LANES), lambda i, r, j: (i, j, r)),
        out_shape=jax.ShapeDtypeStruct((b, s, rr * GROUP_LANES), BF16),
        compiler_params=_cparams(("arbitrary", "arbitrary", "arbitrary")),
        name="nsa_window",
    )(q, kv, kv, kv, kv, key_aug, key_aug, gates, slope_rows)


def _nsa_key_aug(s):
    pos = np.arange(s)
    aug = np.zeros((s, LANES), np.float32)
    aug[pos, pos // SLC_BLOCK] = 1.0
    aug[:, AUG_POS_LANE:AUG_POS_LANE + 3] = (pos // 64)[:, None]
    aug[:, AUG_POS_LANE + 3:AUG_POS_LANE + 6] = (pos % 64)[:, None]
    return jnp.asarray(aug, BF16)


def _nsa_slope_rows(slopes_l2):
    rr, gg = slopes_l2.shape
    hi, mid, lo = _split3(jnp.asarray(slopes_l2, F32))
    pieces = jnp.stack([hi, mid, lo], axis=-1).astype(F32)
    rows = jnp.zeros((rr, gg, 1, LANES), F32)
    rows = rows.at[:, :, 0, AUG_POS_LANE:AUG_POS_LANE + 3].set(pieces * 64.0)
    rows = rows.at[:, :, 0, AUG_POS_LANE + 3:AUG_POS_LANE + 6].set(pieces)
    return rows.astype(BF16)


def _pad_lanes(a, width=LANES):
    return jnp.pad(a, ((0, 0), (0, width - a.shape[1])))


def _fox_layer(x2d, b, s, w_in, b_f, w_out, ln_g, ln_b):
    d = x2d.shape[1]
    attn = w_out.shape[0]
    n_heads = attn // HEAD_DIM
    scale = HEAD_DIM ** -0.5 * LOG2E
    col_scale = jnp.concatenate([jnp.full((attn,), scale, F32), jnp.ones((2 * attn,), F32)])[None, :]
    qkv, z = _project(x2d, w_in[:, :3 * attn].astype(BF16), col_scale,
                      _pad_lanes(w_in[:, 3 * attn:])[None], _pad_lanes(b_f[None, :])[None], False, True)
    gate_bias = _gate_cumsum(z.reshape(b, s, LANES), n_heads)
    o = _fox_attention(qkv.reshape(b, s, 3 * attn), gate_bias, n_heads)
    return _out_ln([o.reshape(b * s, attn)], w_out.astype(BF16), x2d,
                   ln_g.reshape(1, d), ln_b.reshape(1, d))


def _selection_weights_t(nc, n_slc):
    cs = np.arange(nc)[:, None] * CMP_STRIDE
    ce = cs + CMP_BLOCK
    ss = np.arange(n_slc)[None, :] * SLC_BLOCK
    se = ss + SLC_BLOCK
    w = np.clip(np.minimum(ce, se) - np.maximum(cs, ss), 0, None) / CMP_STRIDE
    w[nc - 1, :] = 0.0
    return jnp.asarray(w.T, BF16)


def _nsa_layer(x2d, b, s, w_kv, cmp_pe, cmp_w1, cmp_b1, cmp_w2, w_q, b_g, w_out, ln_g, ln_b):
    d = x2d.shape[1]
    gg, rr = NSA_KV_GROUPS, NSA_HEADS_PER_GROUP
    attn = gg * rr * HEAD_DIM
    n_heads = gg * rr
    wq = w_q[:, :attn].reshape(d, gg, rr, HEAD_DIM).transpose(0, 2, 1, 3).reshape(d, attn)
    wgate = w_q[:, attn:].reshape(d, gg, rr, 3).transpose(2, 0, 3, 1).reshape(rr, d, 3 * gg)
    bgate = b_g.reshape(gg, rr, 3).transpose(1, 2, 0).reshape(rr, 1, 3 * gg)
    wgate = jnp.pad(wgate, ((0, 0), (0, 0), (0, LANES - 3 * gg)))
    bgate = jnp.pad(bgate, ((0, 0), (0, 0), (0, LANES - 3 * gg)))
    wo = w_out.reshape(gg, rr, HEAD_DIM, d).transpose(1, 0, 2, 3).reshape(attn, d)
    w_all = jnp.concatenate([wq, w_kv], axis=1).astype(BF16)
    col_scale = jnp.concatenate([jnp.full((attn,), HEAD_DIM ** -0.5 * LOG2E, F32),
                                 jnp.ones((w_kv.shape[1],), F32)])[None, :]
    qkv, gates = _project(x2d, w_all, col_scale, wgate, bgate, True, False)
    width = qkv.shape[1]
    qkv = qkv.reshape(b, s, width)
    gates = gates.reshape(rr, b, s, LANES)

    nc = s // CMP_STRIDE
    half = CMP_STRIDE * HEAD_DIM
    raw = qkv[:, :, attn:attn + 2 * GROUP_LANES]
    chunks = raw.reshape(b, nc, CMP_STRIDE, 2, gg, HEAD_DIM).transpose(0, 3, 4, 1, 2, 5)
    chunks = chunks.reshape(b, 2, gg, nc, half)
    pe_flat = cmp_pe.reshape(2, 2, half)
    hid = cmp_w1.shape[2]
    w2_placed = jnp.zeros((2, gg, hid, GROUP_LANES), F32)
    for g in range(gg):
        w2_placed = w2_placed.at[:, g, :, g * HEAD_DIM:(g + 1) * HEAD_DIM].set(cmp_w2)
    cmp_kv = _compress(chunks, pe_flat, cmp_w1.astype(BF16), cmp_b1.reshape(2, 1, hid),
                       w2_placed.astype(BF16))

    slopes_l2 = (2.0 ** (-8.0 * np.arange(1, n_heads + 1) / n_heads) * LOG2E).reshape(gg, rr).T
    slopes_rep = jnp.asarray(np.broadcast_to(
        slopes_l2[:, :, None, None], (rr, gg, 1, nc)).astype(np.float32))
    n_slc = s // SLC_BLOCK
    assert n_slc <= AUG_POS_LANE, "selection flags must fit below the position lanes"
    o_c, sel_q = _nsa_compressed(qkv, cmp_kv, gates, slopes_rep, _selection_weights_t(nc, n_slc))
    key_aug = _nsa_key_aug(s)
    slope_rows = _nsa_slope_rows(slopes_l2)
    base = attn // GROUP_LANES
    o_s = _nsa_flash(qkv, qkv, base + 2, base + 3, key_aug, gates, slope_rows, sel_q)
    o_w = _nsa_window(qkv, qkv, base + 4, base + 5, key_aug, gates, slope_rows)
    n = b * s
    return _out_ln([o_c.reshape(n, attn), o_s.reshape(n, attn), o_w.reshape(n, attn)],
                   wo.astype(BF16), x2d, ln_g.reshape(1, d), ln_b.reshape(1, d))


def kernel(x, p, fox_w_in, fox_b_f, fox_w_out, nsa_w_kv, cmp_pe, cmp_w1, cmp_b1, cmp_w2, nsa_w_q, nsa_b_g, nsa_w_out, ln_g, ln_b, moe_w_router, moe_b_router, moe_w_gate, moe_w_up, moe_w_down, shared_w_gate, shared_w_up, shared_w_down, ple_w_proj, ple_w_gate, ple_b_gate):
    b, s, d = x.shape
    n = b * s
    depth = p.shape[0]
    n_a = depth // 2
    h = x.reshape(n, d)
    for i in range(depth):
        if i < n_a:
            h1, h1b = _fox_layer(h, b, s, fox_w_in[i], fox_b_f[i], fox_w_out[i], ln_g[i, 0], ln_b[i, 0])
        else:
            j = i - n_a
            h1, h1b = _nsa_layer(h, b, s, nsa_w_kv, cmp_pe, cmp_w1, cmp_b1, cmp_w2,
                                 nsa_w_q[j], nsa_b_g[j], nsa_w_out[j], ln_g[i, 0], ln_b[i, 0])
        h = _moe_layer(h1, h1b, p[i].reshape(n, -1), i, moe_w_router[i], moe_b_router[i],
                       moe_w_gate, moe_w_up, moe_w_down,
                       shared_w_gate[i], shared_w_up[i], shared_w_down[i],
                       ple_w_proj[i], ple_w_gate[i], ple_b_gate[i],
                       ln_g[i, 1], ln_b[i, 1], ln_g[i, 2], ln_b[i, 2])
    return h.reshape(b, s, d)
```

```python
import functools

import numpy as np
import jax
import jax.numpy as jnp
from jax import lax
from jax.experimental import pallas as pl
from jax.experimental.pallas import tpu as pltpu

F32 = jnp.float32
BF16 = jnp.bfloat16
NEG = -1e30

HEAD_DIM = 64
LANES = 128
NSA_KV_GROUPS = 4
NSA_HEADS_PER_GROUP = 4
GROUP_LANES = NSA_KV_GROUPS * HEAD_DIM
CMP_BLOCK = 32
CMP_STRIDE = 16
SLC_BLOCK = 64
SLC_TOPK = 16
WINDOW = 512
EXPERT_TOPK = 8
EXPERT_GROUPS = 8
EXPERT_TOPK_GROUPS = 4
ROUTED_SCALE = 2.5
LN_EPS = 1e-5
DEPTH = 2
DEEPNORM_ALPHA = (2.0 * DEPTH) ** 0.25
VMEM_LIMIT = 52 * 1024 * 1024

ROW_TILE = 512
FOX_TILE = 1024
FOX_SUB = 128
NSA_Q_TILE = 256
NSA_SEL_TILE = 1024
NSA_WIN_TILE = 1024
NSA_SUB = 256
ROUTER_TILE = 512
EXPERT_ROW_BLOCK = 512
EXPERT_CHUNKS = 4


def _cparams(sem):
    return pltpu.CompilerParams(dimension_semantics=sem, vmem_limit_bytes=VMEM_LIMIT)


def _nt(a, b):
    return lax.dot_general(a, b, (((1,), (1,)), ((), ())), preferred_element_type=F32)


def _dot(a, b):
    return jnp.dot(a, b, preferred_element_type=F32)


def _split2(a):
    hi = a.astype(BF16)
    lo = (a - hi.astype(F32)).astype(BF16)
    return hi, lo


def _split3(a):
    hi = a.astype(BF16)
    r = a - hi.astype(F32)
    mid = r.astype(BF16)
    lo = (r - mid.astype(F32)).astype(BF16)
    return hi, mid, lo


def _sigmoid(x):
    return 1.0 / (1.0 + jnp.exp(-x))


def _layer_norm(z, g, b):
    mu = jnp.mean(z, axis=-1, keepdims=True)
    zc = z - mu
    var = jnp.mean(zc * zc, axis=-1, keepdims=True)
    return zc * lax.rsqrt(var + LN_EPS) * g + b


def _proj_kernel(x_ref, w_ref, cs_ref, wg_ref, bg_ref, o_ref, g_ref, *, tn, gate_sigmoid, gate_split):
    x = x_ref[...]
    xh = x.astype(BF16)
    m_out = w_ref.shape[1]
    for j in range(m_out // tn):
        sl = slice(j * tn, (j + 1) * tn)
        y = _dot(xh, w_ref[:, sl])
        o_ref[:, sl] = (y * cs_ref[:, sl]).astype(o_ref.dtype)
    xl = (x - xh.astype(F32)).astype(BF16)
    for i in range(wg_ref.shape[0]):
        wh, wl = _split2(wg_ref[i])
        g = _dot(xh, wh) + bg_ref[i]
        if gate_split:
            g = g + _dot(xh, wl) + _dot(xl, wh)
        if gate_sigmoid:
            g = _sigmoid(g)
        g_ref[i] = g


def _project(x2d, w_bf16, col_scale, w_gate, b_gate, gate_sigmoid, gate_split):
    n, k = x2d.shape
    m_out = w_bf16.shape[1]
    n_g = w_gate.shape[0]
    tm = min(ROW_TILE, n)
    return pl.pallas_call(
        functools.partial(_proj_kernel, tn=512, gate_sigmoid=gate_sigmoid, gate_split=gate_split),
        grid=(n // tm,),
        in_specs=[
            pl.BlockSpec((tm, k), lambda i: (i, 0)),
            pl.BlockSpec((k, m_out), lambda i: (0, 0)),
            pl.BlockSpec((1, m_out), lambda i: (0, 0)),
            pl.BlockSpec((n_g, k, LANES), lambda i: (0, 0, 0)),
            pl.BlockSpec((n_g, 1, LANES), lambda i: (0, 0, 0)),
        ],
        out_specs=[
            pl.BlockSpec((tm, m_out), lambda i: (i, 0)),
            pl.BlockSpec((n_g, tm, LANES), lambda i: (0, i, 0)),
        ],
        out_shape=[
            jax.ShapeDtypeStruct((n, m_out), BF16),
            jax.ShapeDtypeStruct((n_g, n, LANES), F32),
        ],
        compiler_params=_cparams(("arbitrary",)),
        name="project",
    )(x2d, w_bf16, col_scale, w_gate, b_gate)


LOG2E = 1.4426950408889634
GATE_PIECE_STRIDE = 16


def _gate_cumsum_kernel(z_ref, o_ref, carry_ref, *, ts, n_heads):
    @pl.when(pl.program_id(1) == 0)
    def _():
        carry_ref[...] = jnp.zeros_like(carry_ref)

    z = z_ref[0]
    log_f = jnp.minimum(z, 0.0) - jnp.log(1.0 + jnp.exp(-jnp.abs(z)))
    row = lax.broadcasted_iota(jnp.int32, (ts, ts), 0)
    col = lax.broadcasted_iota(jnp.int32, (ts, ts), 1)
    tri = jnp.where(col <= row, 1.0, 0.0).astype(BF16)
    hi, mid, lo = _split3(log_f)
    cs = _dot(tri, hi) + _dot(tri, mid) + _dot(tri, lo) + carry_ref[...]
    carry_ref[...] = cs[ts - 1:ts, :]
    lane = lax.broadcasted_iota(jnp.int32, (1, LANES), 1)
    bias = jnp.where(lane < n_heads, cs * (-LOG2E), 0.0)
    hi, mid, lo = _split3(bias)
    pieces = (hi.astype(F32) + pltpu.roll(mid.astype(F32), GATE_PIECE_STRIDE, 1)
              + pltpu.roll(lo.astype(F32), 2 * GATE_PIECE_STRIDE, 1))
    o_ref[0] = (pieces + pltpu.roll(pieces, HEAD_DIM, 1)).astype(o_ref.dtype)


def _gate_cumsum(z, n_heads):
    b, s, _ = z.shape
    ts = min(256, s)
    return pl.pallas_call(
        functools.partial(_gate_cumsum_kernel, ts=ts, n_heads=n_heads),
        grid=(b, s // ts),
        in_specs=[pl.BlockSpec((1, ts, LANES), lambda i, j: (i, j, 0))],
        out_specs=pl.BlockSpec((1, ts, LANES), lambda i, j: (i, j, 0)),
        out_shape=jax.ShapeDtypeStruct((b, s, LANES), BF16),
        scratch_shapes=[pltpu.VMEM((1, LANES), F32)],
        compiler_params=_cparams(("arbitrary", "arbitrary")),
        name="gate_cumsum",
    )(z)


def _flash_items(items):
    outs = []
    for key, s, v, m_prev, acc_prev in items:
        m_new = jnp.maximum(m_prev, jnp.max(s, axis=1, keepdims=True))
        alpha = jnp.exp2(m_prev - m_new)
        p = jnp.concatenate(
            [jnp.exp2(s[:, j * LANES:(j + 1) * LANES] - m_new[:, :LANES])
             for j in range(s.shape[1] // LANES)], axis=1)
        reps = acc_prev.shape[1] // LANES
        alpha = alpha if reps == 1 else jnp.concatenate([alpha] * reps, axis=1)
        outs.append((key, m_new, alpha * acc_prev + _dot(p.astype(BF16), v)))
    return outs


def _fox_kernel(qt_ref, kt_ref, q_ref, k_ref, v_ref, c_ref, o_ref, m_ref, acc_ref, *, t, sub):
    step = pl.program_id(2)
    pair = pl.program_id(1)
    qi = qt_ref[step]
    ki = kt_ref[step]

    @pl.when(ki == 0)
    def _():
        m_ref[...] = jnp.full_like(m_ref, NEG)
        acc_ref[...] = jnp.zeros_like(acc_ref)

    lane = lax.broadcasted_iota(jnp.int32, (1, LANES), 1)
    low = lane < HEAD_DIM

    def body(diag):
        q = q_ref[0]
        k = k_ref[0]
        v = v_ref[0]
        c = c_ref[0]
        items = []
        for h in range(2):
            own = low if h == 0 else jnp.logical_not(low)
            base = (1 - h) * HEAD_DIM + 2 * pair + h
            ones_at = ((lane == base) | (lane == base + GATE_PIECE_STRIDE)
                       | (lane == base + 2 * GATE_PIECE_STRIDE))
            qh = jnp.where(own, q, jnp.where(ones_at, 1.0, 0.0).astype(BF16))
            kh = jnp.where(own, k, c)
            vh = jnp.where(own, v, jnp.ones_like(v))
            for qs in range(t // sub):
                rows = slice(qs * sub, (qs + 1) * sub)
                nk = (qs + 1) * sub if diag else t
                s = _nt(qh[rows], kh[:nk])
                if diag:
                    r_i = lax.broadcasted_iota(jnp.int32, (sub, sub), 0)
                    c_i = lax.broadcasted_iota(jnp.int32, (sub, sub), 1)
                    last = jnp.where(c_i <= r_i, s[:, nk - sub:], NEG)
                    s = last if nk == sub else jnp.concatenate([s[:, :nk - sub], last], axis=1)
                items.append(((h, rows), s, vh[:nk], m_ref[h, rows, :], acc_ref[h, rows, :]))
        for (h, rows), m_new, acc_new in _flash_items(items):
            acc_ref[h, rows, :] = acc_new
            m_ref[h, rows, :] = m_new

    @pl.when(ki < qi)
    def _():
        body(False)

    @pl.when(ki == qi)
    def _():
        body(True)
        a0 = acc_ref[0]
        a1 = acc_ref[1]
        o0 = a0 / pltpu.roll(a0, HEAD_DIM, 1)
        o1 = a1 / pltpu.roll(a1, HEAD_DIM, 1)
        o_ref[0] = jnp.where(low, o0, o1).astype(o_ref.dtype)


def _fox_attention(qkv, gate_bias, n_heads):
    b, s, _ = qkv.shape
    t = min(FOX_TILE, s)
    nq = s // t
    pairs = n_heads // 2
    qt = np.concatenate([np.full((i + 1,), i, np.int32) for i in range(nq)])
    kt = np.concatenate([np.arange(i + 1, dtype=np.int32) for i in range(nq)])
    grid_spec = pltpu.PrefetchScalarGridSpec(
        num_scalar_prefetch=2,
        grid=(b, pairs, len(qt)),
        in_specs=[
            pl.BlockSpec((1, t, LANES), lambda i, j, u, qt, kt: (i, qt[u], j)),
            pl.BlockSpec((1, t, LANES), lambda i, j, u, qt, kt: (i, kt[u], pairs + j)),
            pl.BlockSpec((1, t, LANES), lambda i, j, u, qt, kt: (i, kt[u], 2 * pairs + j)),
            pl.BlockSpec((1, t, LANES), lambda i, j, u, qt, kt: (i, kt[u], 0)),
        ],
        out_specs=pl.BlockSpec((1, t, LANES), lambda i, j, u, qt, kt: (i, qt[u], j)),
        scratch_shapes=[
            pltpu.VMEM((2, t, LANES), F32),
            pltpu.VMEM((2, t, LANES), F32),
        ],
    )
    return pl.pallas_call(
        functools.partial(_fox_kernel, t=t, sub=min(FOX_SUB, t)),
        grid_spec=grid_spec,
        out_shape=jax.ShapeDtypeStruct((b, s, n_heads * HEAD_DIM), BF16),
        compiler_params=_cparams(("arbitrary", "arbitrary", "arbitrary")),
        name="fox_attention",
    )(jnp.asarray(qt), jnp.asarray(kt), qkv, qkv, qkv, gate_bias)


def _out_ln_kernel(*refs, n_in):
    o_refs = refs[:n_in]
    w_ref, x_ref, g_ref, b_ref, y_ref, yb_ref = refs[n_in:]
    if n_in == 1:
        o = o_refs[0][...]
    else:
        o = o_refs[0][...].astype(F32)
        for r in o_refs[1:]:
            o = o + r[...].astype(F32)
        o = o.astype(BF16)
    z = DEEPNORM_ALPHA * x_ref[...] + _dot(o, w_ref[...])
    y = _layer_norm(z, g_ref[...], b_ref[...])
    y_ref[...] = y
    yb_ref[...] = y.astype(BF16)


def _out_ln(o_list, w_bf16, x2d, g, b):
    n, d = x2d.shape
    k = w_bf16.shape[0]
    tm = min(ROW_TILE, n)
    n_in = len(o_list)
    row = lambda i: (i, 0)
    fixed = lambda i: (0, 0)
    return pl.pallas_call(
        functools.partial(_out_ln_kernel, n_in=n_in),
        grid=(n // tm,),
        in_specs=[pl.BlockSpec((tm, k), row)] * n_in + [
            pl.BlockSpec((k, d), fixed),
            pl.BlockSpec((tm, d), row),
            pl.BlockSpec((1, d), fixed),
            pl.BlockSpec((1, d), fixed),
        ],
        out_specs=[pl.BlockSpec((tm, d), row), pl.BlockSpec((tm, d), row)],
        out_shape=[jax.ShapeDtypeStruct((n, d), F32), jax.ShapeDtypeStruct((n, d), BF16)],
        compiler_params=_cparams(("arbitrary",)),
        name="out_ln",
    )(*o_list, w_bf16, x2d, g, b)


def _router_kernel(x_ref, wt_ref, b_ref, idx_ref, w_ref, rank_ref, cnt_ref, carry_ref, *, tn, n_exp):
    @pl.when(pl.program_id(0) == 0)
    def _():
        carry_ref[...] = jnp.zeros_like(carry_ref)

    xh, xl = _split2(x_ref[...])
    wh, wl = _split2(wt_ref[...])
    logits = _nt(wh, xh) + _nt(wh, xl) + _nt(wl, xh)
    scores = _sigmoid(logits)
    biased = scores + b_ref[...]

    per_grp = n_exp // EXPERT_GROUPS
    blocks, gscore = [], []
    for g in range(EXPERT_GROUPS):
        blk = biased[g * per_grp:(g + 1) * per_grp, :]
        m1 = jnp.max(blk, axis=0, keepdims=True)
        eq = blk == m1
        n_eq = jnp.sum(jnp.where(eq, 1.0, 0.0), axis=0, keepdims=True)
        m2 = jnp.max(jnp.where(eq, -3e38, blk), axis=0, keepdims=True)
        blocks.append(blk)
        gscore.append(m1 + jnp.where(n_eq >= 2.0, m1, m2))
    cand = []
    for g in range(EXPERT_GROUPS):
        beaten = jnp.zeros_like(gscore[g])
        for o in range(EXPERT_GROUPS):
            if o == g:
                continue
            wins = (gscore[o] >= gscore[g]) if o < g else (gscore[o] > gscore[g])
            beaten = beaten + jnp.where(wins, 1.0, 0.0)
        cand.append(jnp.where(beaten < float(EXPERT_TOPK_GROUPS), blocks[g], NEG))
    cand = jnp.concatenate(cand, axis=0)

    erow = lax.broadcasted_iota(jnp.int32, (n_exp, tn), 0).astype(F32)
    hot, sel_idx, sel_score = [], [], []
    member = jnp.zeros((n_exp, tn), F32)
    for _ in range(EXPERT_TOPK):
        m = jnp.max(cand, axis=0, keepdims=True)
        first = jnp.min(jnp.where(cand == m, erow, float(n_exp)), axis=0, keepdims=True)
        onehot = erow == first
        sel_idx.append(first)
        sel_score.append(jnp.sum(jnp.where(onehot, scores, 0.0), axis=0, keepdims=True))
        cand = jnp.where(onehot, -3e38, cand)
        member = member + jnp.where(onehot, 1.0, 0.0)
        hot.append(onehot)
    total = sel_score[0]
    for sc in sel_score[1:]:
        total = total + sc

    trow = lax.broadcasted_iota(jnp.int32, (tn, tn), 0)
    tcol = lax.broadcasted_iota(jnp.int32, (tn, tn), 1)
    before = jnp.where(trow < tcol, 1.0, 0.0).astype(BF16)
    prior = _dot(member.astype(BF16), before) + carry_ref[...]
    ranks = [jnp.sum(jnp.where(hot[k], prior, 0.0), axis=0, keepdims=True) for k in range(EXPERT_TOPK)]

    idx_ref[...] = jnp.concatenate(sel_idx, axis=0).astype(jnp.int32)
    w_ref[...] = jnp.concatenate([sc / total * ROUTED_SCALE for sc in sel_score], axis=0)
    rank_ref[...] = jnp.concatenate(ranks, axis=0).astype(jnp.int32)
    carry_ref[...] = carry_ref[...] + jnp.sum(member, axis=1, keepdims=True)
    cnt_ref[...] = carry_ref[...]


def _route(x2d, w_router_t, b_router):
    n, d = x2d.shape
    n_exp = w_router_t.shape[0]
    tn = min(ROUTER_TILE, n)
    col = lambda i: (0, i)
    return pl.pallas_call(
        functools.partial(_router_kernel, tn=tn, n_exp=n_exp),
        grid=(n // tn,),
        in_specs=[
            pl.BlockSpec((tn, d), lambda i: (i, 0)),
            pl.BlockSpec((n_exp, d), lambda i: (0, 0)),
            pl.BlockSpec((n_exp, 1), lambda i: (0, 0)),
        ],
        out_specs=[
            pl.BlockSpec((EXPERT_TOPK, tn), col),
            pl.BlockSpec((EXPERT_TOPK, tn), col),
            pl.BlockSpec((EXPERT_TOPK, tn), col),
            pl.BlockSpec((n_exp, 1), lambda i: (0, 0)),
        ],
        out_shape=[
            jax.ShapeDtypeStruct((EXPERT_TOPK, n), jnp.int32),
            jax.ShapeDtypeStruct((EXPERT_TOPK, n), F32),
            jax.ShapeDtypeStruct((EXPERT_TOPK, n), jnp.int32),
            jax.ShapeDtypeStruct((n_exp, 1), F32),
        ],
        scratch_shapes=[pltpu.VMEM((n_exp, 1), F32)],
        compiler_params=_cparams(("arbitrary",)),
        name="router",
    )(x2d, w_router_t, b_router)


def _experts_kernel(be_ref, first_ref, nused_ref, x_ref, wg_ref, wu_ref, wd_ref, *rest):
    y_ref, wg_s, wu_s, wd_s = rest[-4:]
    blk = pl.program_id(0)

    @pl.when(blk < nused_ref[0])
    def _():
        @pl.when((first_ref[blk] == 1) | (blk == 0))
        def _():
            wg_s[...] = wg_ref[0, 0].astype(BF16)
            wu_s[...] = wu_ref[0, 0].astype(BF16)
            wd_s[...] = wd_ref[0, 0].astype(BF16)

        x = x_ref[...]
        gate = _dot(x, wg_s[...])
        up = _dot(x, wu_s[...])
        h = gate * _sigmoid(gate) * up
        y_ref[...] = _dot(h.astype(BF16), wd_s[...]).astype(y_ref.dtype)

    @pl.when(blk >= nused_ref[0])
    def _():
        y_ref[...] = jnp.zeros_like(y_ref)


def _experts(xg, blk_expert, blk_first, n_used, w_gate, w_up, w_down, layer, y_prev, first_block, n_rows):
    d = xg.shape[1]
    hdim = w_gate.shape[3]
    tm = EXPERT_ROW_BLOCK
    n_blocks = xg.shape[0] // tm
    extra_specs, extra_args, aliases = [], [], {}
    if y_prev is not None:
        extra_specs, extra_args, aliases = [pl.BlockSpec(memory_space=pl.ANY)], [y_prev], {7: 0}
    grid_spec = pltpu.PrefetchScalarGridSpec(
        num_scalar_prefetch=3,
        grid=(n_blocks,),
        in_specs=[
            pl.BlockSpec((tm, d), lambda i, be, fi, nu: (i, 0)),
            pl.BlockSpec((1, 1, d, hdim), lambda i, be, fi, nu: (layer, be[i], 0, 0)),
            pl.BlockSpec((1, 1, d, hdim), lambda i, be, fi, nu: (layer, be[i], 0, 0)),
            pl.BlockSpec((1, 1, hdim, d), lambda i, be, fi, nu: (layer, be[i], 0, 0)),
        ] + extra_specs,
        out_specs=pl.BlockSpec((tm, d), lambda i, be, fi, nu: (i + first_block, 0)),
        scratch_shapes=[
            pltpu.VMEM((d, hdim), BF16),
            pltpu.VMEM((d, hdim), BF16),
            pltpu.VMEM((hdim, d), BF16),
        ],
    )
    return pl.pallas_call(
        _experts_kernel,
        grid_spec=grid_spec,
        out_shape=jax.ShapeDtypeStruct((n_rows, d), BF16),
        input_output_aliases=aliases,
        compiler_params=_cparams(("arbitrary",)),
        name="experts",
    )(blk_expert, blk_first, n_used, xg, w_gate, w_up, w_down, *extra_args)


def _moe_tail_kernel(x_ref, r_ref, rw_ref, p_ref, sg_ref, su_ref, sd_ref, pg_ref, pb_ref, pp_ref,
                     g1_ref, b1_ref, g2_ref, b2_ref, y_ref):
    x = x_ref[...]
    xb = x.astype(BF16)
    gate = _dot(xb, sg_ref[...])
    up = _dot(xb, su_ref[...])
    shared = _dot((gate * _sigmoid(gate) * up).astype(BF16), sd_ref[...])
    rw = rw_ref[...]
    routed = r_ref[0].astype(F32) * rw[:, 0:1]
    for k in range(1, r_ref.shape[0]):
        routed = routed + r_ref[k].astype(F32) * rw[:, k:k + 1]
    x2 = _layer_norm(DEEPNORM_ALPHA * x + (routed + shared), g1_ref[...], b1_ref[...])
    ple_gate = _sigmoid(_dot(x2.astype(BF16), pg_ref[...]) + pb_ref[...])
    ple = ple_gate * _dot(p_ref[0].astype(BF16), pp_ref[...])
    y_ref[...] = _layer_norm(DEEPNORM_ALPHA * x2 + ple, g2_ref[...], b2_ref[...])


def _moe_tail(x2d, routed, routed_w, p_all, layer, sg, su, sd, pg, pb, pp, g1, b1, g2, b2):
    n, d = x2d.shape
    tm = min(ROW_TILE, n)
    topk = routed.shape[0]
    row = lambda i: (i, 0)
    fixed = lambda i: (0, 0)
    full = lambda a: pl.BlockSpec(a.shape, fixed)
    return pl.pallas_call(
        _moe_tail_kernel,
        grid=(n // tm,),
        in_specs=[
            pl.BlockSpec((tm, d), row),
            pl.BlockSpec((topk, tm, d), lambda i: (0, i, 0)),
            pl.BlockSpec((tm, topk), row),
            pl.BlockSpec((1, tm, p_all.shape[2]), lambda i: (layer, i, 0)),
            full(sg), full(su), full(sd), full(pg), full(pb), full(pp),
            full(g1), full(b1), full(g2), full(b2),
        ],
        out_specs=pl.BlockSpec((tm, d), row),
        out_shape=jax.ShapeDtypeStruct((n, d), F32),
        compiler_params=_cparams(("arbitrary",)),
        name="moe_tail",
    )(x2d, routed, routed_w, p_all, sg, su, sd, pg, pb, pp, g1, b1, g2, b2)


def _moe_layer(x1, x1b, p_all, layer, w_router, b_router, w_gate, w_up, w_down, sg, su, sd,
               ple_proj, ple_gate, ple_bias, g1, b1, g2, b2):
    n, d = x1.shape
    n_exp = w_router.shape[1]
    idx, wts, rank, counts = _route(x1, w_router.T, b_router.reshape(n_exp, 1))
    tm = EXPERT_ROW_BLOCK
    counts = counts[:, 0].astype(jnp.int32)
    padded = (counts + tm - 1) // tm * tm
    pad_end = jnp.cumsum(padded)
    pad_start = pad_end - padded
    n_blocks = (n * EXPERT_TOPK + n_exp * (tm - 1)) // tm + 1
    n_blocks = -(-n_blocks // EXPERT_CHUNKS) * EXPERT_CHUNKS
    n_rows = n_blocks * tm
    experts = jnp.arange(n_exp, dtype=jnp.int32)
    start_of = jnp.sum(jnp.where(idx[:, :, None] == experts, pad_start, 0), axis=-1)
    pos = start_of + rank
    tok = jnp.broadcast_to(jnp.arange(n, dtype=jnp.int32)[None, :], pos.shape)
    filler = jnp.arange(tm, dtype=jnp.int32)[None, :]
    filler_key = jnp.where(filler < (padded - counts)[:, None],
                           (pad_start + counts)[:, None] + filler, jnp.int32(2 ** 30))
    n_tail = max(n_rows - (n * EXPERT_TOPK + n_exp * tm), 0)
    keys = jnp.concatenate([pos.reshape(-1), filler_key.reshape(-1),
                            jnp.full((n_tail,), 2 ** 30, jnp.int32)])
    vals = jnp.concatenate([tok.reshape(-1), jnp.arange(n_exp * tm + n_tail, dtype=jnp.int32) % n])
    tok_pad = lax.sort((keys, vals), num_keys=1, is_stable=False)[1][:n_rows]
    blk_start = jnp.arange(n_blocks, dtype=jnp.int32) * tm
    blk_expert = jnp.minimum(jnp.sum((pad_end[None, :] <= blk_start[:, None]).astype(jnp.int32), axis=1),
                             n_exp - 1)
    blk_first = jnp.concatenate([jnp.ones((1,), jnp.int32),
                                 (blk_expert[1:] != blk_expert[:-1]).astype(jnp.int32)])
    n_used = (pad_end[-1:] // tm).astype(jnp.int32)
    per = n_blocks // EXPERT_CHUNKS
    y = None
    for c in range(EXPERT_CHUNKS):
        blocks = slice(c * per, (c + 1) * per)
        xg = x1b.at[tok_pad[c * per * tm:(c + 1) * per * tm]].get(mode="promise_in_bounds")
        y = _experts(xg, blk_expert[blocks], blk_first[blocks], jnp.clip(n_used - c * per, 0, per),
                     w_gate, w_up, w_down, layer, y, c * per, n_rows)
    routed = y.at[pos].get(mode="promise_in_bounds")
    return _moe_tail(x1, routed, wts.T, p_all, layer, sg.astype(BF16), su.astype(BF16), sd.astype(BF16),
                     ple_gate.astype(BF16), ple_bias.reshape(1, d), ple_proj.astype(BF16),
                     g1.reshape(1, d), b1.reshape(1, d), g2.reshape(1, d), b2.reshape(1, d))


def _gelu_tanh(x):
    return 0.5 * x * (1.0 + jnp.tanh(0.7978845608028654 * (x + 0.044715 * (x * x * x))))


def _compress_kernel(c_ref, pe_ref, w1_ref, b1_ref, w2_ref, o_ref, *, nc, half):
    out = jnp.zeros((nc, GROUP_LANES), F32)
    for g in range(NSA_KV_GROUPS):
        c = c_ref[0, 0, g].astype(F32)
        a = _dot((c + pe_ref[0, 0:1, :]).astype(BF16), w1_ref[0, :half, :])
        bm = _dot((c + pe_ref[0, 1:2, :]).astype(BF16), w1_ref[0, half:, :])
        h = _gelu_tanh(a + pltpu.roll(bm, nc - 1, 0) + b1_ref[0])
        out = out + _dot(h.astype(BF16), w2_ref[0, g])
    o_ref[0, 0] = out.astype(o_ref.dtype)


def _compress(chunks, pe_flat, w1, b1, w2_placed):
    b, _, _, nc, half = chunks.shape
    hid = w1.shape[2]
    return pl.pallas_call(
        functools.partial(_compress_kernel, nc=nc, half=half),
        grid=(b, 2),
        in_specs=[
            pl.BlockSpec((1, 1, NSA_KV_GROUPS, nc, half), lambda i, j: (i, j, 0, 0, 0)),
            pl.BlockSpec((1, 2, half), lambda i, j: (j, 0, 0)),
            pl.BlockSpec((1, 2 * half, hid), lambda i, j: (j, 0, 0)),
            pl.BlockSpec((1, 1, hid), lambda i, j: (j, 0, 0)),
            pl.BlockSpec((1, NSA_KV_GROUPS, hid, GROUP_LANES), lambda i, j: (j, 0, 0, 0)),
        ],
        out_specs=pl.BlockSpec((1, 1, nc, GROUP_LANES), lambda i, j: (i, j, 0, 0)),
        out_shape=jax.ShapeDtypeStruct((b, 2, nc, GROUP_LANES), BF16),
        compiler_params=_cparams(("arbitrary", "arbitrary")),
        name="compress",
    )(chunks, pe_flat, w1, b1, w2_placed)


def _group_mask(g):
    lane = lax.broadcasted_iota(jnp.int32, (1, GROUP_LANES), 1)
    return (lane >= g * HEAD_DIM) & (lane < (g + 1) * HEAD_DIM)


def _nsa_cmp_kernel(q_ref, kc_ref, vc_ref, gate_ref, slope_ref, wt_ref, o_ref, sel_ref, psum_ref,
                    *, tq, nc, n_slc):
    qi = pl.program_id(1)
    r = pl.program_id(2)
    q0 = qi * tq

    @pl.when(r == 0)
    def _():
        psum_ref[...] = jnp.zeros_like(psum_ref)

    q = q_ref[0]
    kc = kc_ref[0, 0]
    vc = vc_ref[0, 0]
    t_pos = q0 + lax.broadcasted_iota(jnp.int32, (tq, nc), 0)
    cmp_end = lax.broadcasted_iota(jnp.int32, (tq, nc), 1) * CMP_STRIDE + (CMP_BLOCK - 1)
    visible = cmp_end <= t_pos
    end_rel = (lax.broadcasted_iota(jnp.int32, (1, nc), 1) * CMP_STRIDE + (CMP_BLOCK - 1) - q0).astype(F32)
    gates = gate_ref[0, 0]
    out = jnp.zeros((tq, GROUP_LANES), F32)
    psum_prev = [psum_ref[g] for g in range(NSA_KV_GROUPS)]
    psum_new = []
    for g in range(NSA_KV_GROUPS):
        in_grp = _group_mask(g)
        qm = jnp.where(in_grp, q, jnp.zeros_like(q))
        s = _nt(qm, kc) + slope_ref[0, g, :, :nc] * end_rel
        s = jnp.where(visible, s, NEG)
        m = jnp.max(s, axis=1, keepdims=True)
        p = jnp.where(visible, jnp.exp2(s - m), 0.0)
        o_g = _dot(p.astype(BF16), jnp.where(in_grp, vc, jnp.ones_like(vc)))
        other = (1 - g // 2) * LANES
        l = o_g[:, other:other + LANES]
        inv = jnp.where(l > 0.0, 1.0 / l, 0.0)
        inv2 = jnp.concatenate([inv, inv], axis=1)
        psum_new.append(psum_prev[g] + p * inv2[:, :nc] if nc == GROUP_LANES else
                        psum_prev[g] + p * inv[:, :1])
        out = out + jnp.where(in_grp, o_g * (inv2 * gates[:, g:g + 1]), 0.0)
    for g in range(NSA_KV_GROUPS):
        psum_ref[g] = psum_new[g]
    o_ref[0] = out.astype(o_ref.dtype)

    @pl.when(r == NSA_HEADS_PER_GROUP - 1)
    def _():
        blk = lax.broadcasted_iota(jnp.int32, (n_slc, tq), 0)
        t_row = q0 + lax.broadcasted_iota(jnp.int32, (n_slc, tq), 1)
        cur = t_row // SLC_BLOCK
        forced = (blk == 0) | (blk == cur) | (blk == cur - 1)
        vis = blk * SLC_BLOCK <= t_row
        wt = wt_ref[...]
        sub8 = lax.broadcasted_iota(jnp.int32, (8, tq), 0)
        pad_rows = jnp.zeros((LANES - n_slc, tq), F32)
        for g in range(NSA_KV_GROUPS):
            hi, mid, lo = _split3(psum_ref[g])
            imp = _nt(wt, hi) + _nt(wt, mid) + _nt(wt, lo)
            score = jnp.where(forced, 1e9, jnp.where(vis, imp, NEG))
            beaten = jnp.zeros((n_slc, tq), F32)
            for j in range(n_slc):
                vj = score[j:j + 1, :]
                lo_r, hi_r = (j // 8) * 8, (j // 8) * 8 + 8
                parts = []
                if lo_r > 0:
                    parts.append(jnp.where(vj > score[:lo_r], 1.0, 0.0))
                own = score[lo_r:hi_r]
                parts.append(jnp.where(sub8 > (j % 8), jnp.where(vj >= own, 1.0, 0.0),
                                       jnp.where(vj > own, 1.0, 0.0)))
                if hi_r < n_slc:
                    parts.append(jnp.where(vj >= score[hi_r:], 1.0, 0.0))
                beaten = beaten + jnp.concatenate(parts, axis=0)
            keep = (beaten < float(min(SLC_TOPK, n_slc))) & (score > 0.5 * NEG)
            flags = jnp.concatenate([jnp.where(keep, 0.0, NEG), pad_rows], axis=0)
            sel_ref[0, g] = flags.T.astype(sel_ref.dtype)


def _nsa_compressed(q, cmp_kv, gates_c, slopes_rep, w_sel_t):
    b, s, _ = q.shape
    nc = cmp_kv.shape[2]
    n_slc = s // SLC_BLOCK
    tq = min(NSA_Q_TILE, s)
    rr = NSA_HEADS_PER_GROUP
    return pl.pallas_call(
        functools.partial(_nsa_cmp_kernel, tq=tq, nc=nc, n_slc=n_slc),
        grid=(b, s // tq, rr),
        in_specs=[
            pl.BlockSpec((1, tq, GROUP_LANES), lambda i, j, r: (i, j, r)),
            pl.BlockSpec((1, 1, nc, GROUP_LANES), lambda i, j, r: (i, 0, 0, 0)),
            pl.BlockSpec((1, 1, nc, GROUP_LANES), lambda i, j, r: (i, 1, 0, 0)),
            pl.BlockSpec((1, 1, tq, LANES), lambda i, j, r: (r, i, j, 0)),
            pl.BlockSpec((1, NSA_KV_GROUPS, 1, slopes_rep.shape[3]), lambda i, j, r: (r, 0, 0, 0)),
            pl.BlockSpec((n_slc, nc), lambda i, j, r: (0, 0)),
        ],
        out_specs=[
            pl.BlockSpec((1, tq, GROUP_LANES), lambda i, j, r: (i, j, r)),
            pl.BlockSpec((1, NSA_KV_GROUPS, tq, LANES), lambda i, j, r: (i, 0, j, 0)),
        ],
        out_shape=[
            jax.ShapeDtypeStruct((b, s, rr * GROUP_LANES), BF16),
            jax.ShapeDtypeStruct((b, NSA_KV_GROUPS, s, LANES), BF16),
        ],
        scratch_shapes=[pltpu.VMEM((NSA_KV_GROUPS, tq, nc), F32)],
        compiler_params=_cparams(("arbitrary", "arbitrary", "arbitrary")),
        name="nsa_compressed",
    )(q, cmp_kv, cmp_kv, gates_c, slopes_rep, w_sel_t)


AUG_POS_LANE = 64
MODE_FULL, MODE_LOWER, MODE_UPPER = 0, 1, 2


def _nsa_flash_kernel(qt_ref, kt_ref, fl_ref, *refs, t, sub, use_sel):
    if use_sel:
        q_ref, k_ref, v_ref, ka_ref, gate_ref, slope_ref, sel_ref, o_ref, m_ref, acc_ref = refs
    else:
        q_ref, k_ref, v_ref, ka_ref, gate_ref, slope_ref, o_ref, m_ref, acc_ref = refs
    step = pl.program_id(2)
    flags = fl_ref[step]
    mode = flags >> 2

    @pl.when((flags & 1) != 0)
    def _():
        m_ref[...] = jnp.full_like(m_ref, NEG)
        acc_ref[...] = jnp.zeros_like(acc_ref)

    lane = lax.broadcasted_iota(jnp.int32, (1, LANES), 1)

    def body(mode_static):
        q = q_ref[0]
        k = k_ref[0]
        v = v_ref[0]
        ka = ka_ref[...]
        k_aug = (jnp.concatenate([k[:, :LANES], ka], axis=1), jnp.concatenate([ka, k[:, LANES:]], axis=1))
        items = []
        for g in range(NSA_KV_GROUPS):
            half = g // 2
            in_grp = (lane >= (g % 2) * HEAD_DIM) & (lane < (g % 2 + 1) * HEAD_DIM)
            q_own = jnp.where(in_grp, q[:, half * LANES:(half + 1) * LANES], jnp.zeros((t, LANES), BF16))
            q_extra = jnp.broadcast_to(slope_ref[0, g], (t, LANES))
            if use_sel:
                q_extra = jnp.where(lane < AUG_POS_LANE, sel_ref[0, g], q_extra)
            qg = jnp.concatenate([q_own, q_extra] if half == 0 else [q_extra, q_own], axis=1)
            vg = jnp.where(_group_mask(g), v, jnp.ones_like(v))
            for qs in range(t // sub):
                rows = slice(qs * sub, (qs + 1) * sub)
                if mode_static == MODE_FULL:
                    keys = slice(0, t)
                elif mode_static == MODE_LOWER:
                    keys = slice(0, (qs + 1) * sub)
                else:
                    keys = slice(qs * sub, t)
                s = _nt(qg[rows], k_aug[half][keys])
                if mode_static != MODE_FULL:
                    r_i = lax.broadcasted_iota(jnp.int32, (sub, sub), 0)
                    c_i = lax.broadcasted_iota(jnp.int32, (sub, sub), 1)
                    if mode_static == MODE_LOWER:
                        edge = jnp.where(c_i <= r_i, s[:, -sub:], NEG)
                        s = edge if s.shape[1] == sub else jnp.concatenate([s[:, :-sub], edge], axis=1)
                    else:
                        edge = jnp.where(c_i > r_i, s[:, :sub], NEG)
                        s = edge if s.shape[1] == sub else jnp.concatenate([edge, s[:, sub:]], axis=1)
                items.append(((g, rows), s, vg[keys], m_ref[g, rows, :], acc_ref[g, rows, :]))
        for (g, rows), m_new, acc_new in _flash_items(items):
            acc_ref[g, rows, :] = acc_new
            m_ref[g, rows, :] = m_new

    for mode_static in ((MODE_FULL, MODE_LOWER) if use_sel else (MODE_LOWER, MODE_UPPER)):
        @pl.when(mode == mode_static)
        def _(mode_static=mode_static):
            body(mode_static)

    @pl.when((flags & 2) != 0)
    def _():
        gates = gate_ref[0, 0]
        out = jnp.zeros((t, GROUP_LANES), F32)
        for g in range(NSA_KV_GROUPS):
            acc = acc_ref[g]
            denom = pltpu.roll(acc, LANES, 1)
            col = (1 if use_sel else 2) * NSA_KV_GROUPS + g
            out = out + jnp.where(_group_mask(g), acc * (gates[:, col:col + 1] / denom), 0.0)
        o_ref[0] = out.astype(o_ref.dtype)


def _nsa_flash_tables(s, t, windowed):
    qt, kt, fl = [], [], []
    for qi in range(s // t):
        if windowed:
            tiles = [(qi - 1, MODE_UPPER)] if qi > 0 else []
            tiles.append((qi, MODE_LOWER))
        else:
            tiles = [(ki, MODE_FULL) for ki in range(qi)] + [(qi, MODE_LOWER)]
        for n, (ki, mode) in enumerate(tiles):
            qt.append(qi)
            kt.append(ki)
            fl.append((1 if n == 0 else 0) | (2 if n == len(tiles) - 1 else 0) | (mode << 2))
    return tuple(jnp.asarray(np.asarray(a, np.int32)) for a in (qt, kt, fl))


def _nsa_flash(q, kv, k_col, v_col, key_aug, gates, slope_rows, sel_q=None):
    b, s, _ = q.shape
    use_sel = sel_q is not None
    t = min(NSA_SEL_TILE if use_sel else NSA_WIN_TILE, s)
    assert use_sel or t == WINDOW, "the window branch assumes tile == WINDOW"
    qt, kt, fl = _nsa_flash_tables(s, t, not use_sel)
    rr = NSA_HEADS_PER_GROUP
    in_specs = [
        pl.BlockSpec((1, t, GROUP_LANES), lambda i, r, u, qt, kt, fl: (i, qt[u], r)),
        pl.BlockSpec((1, t, GROUP_LANES), lambda i, r, u, qt, kt, fl: (i, kt[u], k_col)),
        pl.BlockSpec((1, t, GROUP_LANES), lambda i, r, u, qt, kt, fl: (i, kt[u], v_col)),
        pl.BlockSpec((t, LANES), lambda i, r, u, qt, kt, fl: (kt[u], 0)),
        pl.BlockSpec((1, 1, t, LANES), lambda i, r, u, qt, kt, fl: (r, i, qt[u], 0)),
        pl.BlockSpec((1, NSA_KV_GROUPS, 1, LANES), lambda i, r, u, qt, kt, fl: (r, 0, 0, 0)),
    ]
    args = [q, kv, kv, key_aug, gates, slope_rows]
    if use_sel:
        in_specs.append(pl.BlockSpec((1, NSA_KV_GROUPS, t, LANES), lambda i, r, u, qt, kt, fl: (i, 0, qt[u], 0)))
        args.append(sel_q)
    grid_spec = pltpu.PrefetchScalarGridSpec(
        num_scalar_prefetch=3,
        grid=(b, rr, int(qt.shape[0])),
        in_specs=in_specs,
        out_specs=pl.BlockSpec((1, t, GROUP_LANES), lambda i, r, u, qt, kt, fl: (i, qt[u], r)),
        scratch_shapes=[
            pltpu.VMEM((NSA_KV_GROUPS, t, LANES), F32),
            pltpu.VMEM((NSA_KV_GROUPS, t, GROUP_LANES), F32),
        ],
    )
    return pl.pallas_call(
        functools.partial(_nsa_flash_kernel, t=t, sub=min(NSA_SUB, t), use_sel=use_sel),
        grid_spec=grid_spec,
        out_shape=jax.ShapeDtypeStruct((b, s, rr * GROUP_LANES), BF16),
        compiler_params=_cparams(("arbitrary", "arbitrary", "arbitrary")),
        name="nsa_selected" if use_sel else "nsa_window",
    )(qt, kt, fl, *args)


def _nsa_window_kernel(q_ref, kp_ref, kc_ref, vp_ref, vc_ref, kap_ref, kac_ref, gate_ref, slope_ref, o_ref,
                       *, t, sub):
    has_prev = pl.program_id(2) > 0
    lane = lax.broadcasted_iota(jnp.int32, (1, LANES), 1)
    q = q_ref[0]
    k = jnp.concatenate([kp_ref[0], kc_ref[0]], axis=0)
    v = jnp.concatenate([vp_ref[0], vc_ref[0]], axis=0)
    ka = jnp.concatenate([kap_ref[...], kac_ref[...]], axis=0)
    k_aug = (jnp.concatenate([k[:, :LANES], ka], axis=1), jnp.concatenate([ka, k[:, LANES:]], axis=1))
    gates = gate_ref[0, 0]
    span = WINDOW + sub
    r_i = lax.broadcasted_iota(jnp.int32, (sub, sub), 0)
    c_i = lax.broadcasted_iota(jnp.int32, (sub, sub), 1)
    out_rows = [jnp.zeros((sub, GROUP_LANES), F32) for _ in range(t // sub)]
    for g in range(NSA_KV_GROUPS):
        half = g // 2
        in_grp = (lane >= (g % 2) * HEAD_DIM) & (lane < (g % 2 + 1) * HEAD_DIM)
        q_own = jnp.where(in_grp, q[:, half * LANES:(half + 1) * LANES], jnp.zeros((t, LANES), BF16))
        q_extra = jnp.broadcast_to(slope_ref[0, g], (t, LANES))
        qg = jnp.concatenate([q_own, q_extra] if half == 0 else [q_extra, q_own], axis=1)
        vg = jnp.where(_group_mask(g), v, jnp.ones_like(v))
        col = 2 * NSA_KV_GROUPS + g
        for qs in range(t // sub):
            rows = slice(qs * sub, (qs + 1) * sub)
            keys = slice(qs * sub, qs * sub + span)
            s = _nt(qg[rows], k_aug[half][keys])
            chunks = [s[:, j * sub:(j + 1) * sub] for j in range(span // sub)]
            chunks[0] = jnp.where(c_i > r_i, chunks[0], NEG)
            chunks[-1] = jnp.where(c_i <= r_i, chunks[-1], NEG)
            for j in range(len(chunks) - 1):
                if qs * sub + j * sub < WINDOW:
                    chunks[j] = jnp.where(has_prev, chunks[j], NEG)
            m = chunks[0]
            for c in chunks[1:]:
                m = jnp.maximum(m, c)
            m = jnp.max(m, axis=1, keepdims=True)
            p = jnp.concatenate([jnp.exp2(c - m) for c in chunks], axis=1)
            acc = _dot(p.astype(BF16), vg[keys])
            denom = pltpu.roll(acc, LANES, 1)
            out_rows[qs] = out_rows[qs] + jnp.where(
                _group_mask(g), acc * (gates[rows, col:col + 1] / denom), 0.0)
    o_ref[0] = jnp.concatenate(out_rows, axis=0).astype(o_ref.dtype)


def _nsa_window(q, kv, k_col, v_col, key_aug, gates, slope_rows):
    b, s, _ = q.shape
    t = min(NSA_WIN_TILE, s)
    sub = min(NSA_SUB, t)
    assert WINDOW % sub == 0 and t % WINDOW == 0
    rr = NSA_HEADS_PER_GROUP
    per = t // WINDOW
    prev = lambda j: jnp.maximum(j * per - 1, 0)
    return pl.pallas_call(
        functools.partial(_nsa_window_kernel, t=t, sub=sub),
        grid=(b, rr, s // t),
        in_specs=[
            pl.BlockSpec((1, t, GROUP_LANES), lambda i, r, j: (i, j, r)),
            pl.BlockSpec((1, WINDOW, GROUP_LANES), lambda i, r, j: (i, prev(j), k_col)),
            pl.BlockSpec((1, t, GROUP_LANES), lambda i, r, j: (i, j, k_col)),
            pl.BlockSpec((1, WINDOW, GROUP_LANES), lambda i, r, j: (i, prev(j), v_col)),
            pl.BlockSpec((1, t, GROUP_LANES), lambda i, r, j: (i, j, v_col)),
            pl.BlockSpec((WINDOW, LANES), lambda i, r, j: (prev(j), 0)),
            pl.BlockSpec((t, LANES), lambda i, r, j: (j, 0)),
            pl.BlockSpec((1, 1, t, LANES), lambda i, r, j: (r, i, j, 0)),
            pl.BlockSpec((1, NSA_KV_GROUPS, 1, LANES), lambda i, r, j: (r, 0, 0, 0)),
        ],
        out_specs=pl.BlockSpec((1, t, GROUP_LANES), lambda i, r, j: (i, j, r)),
        out_shape=jax.ShapeDtypeStruct((b, s, rr * GROUP_LANES), BF16),
        compiler_params=_cparams(("arbitrary", "arbitrary", "arbitrary")),
        name="nsa_window",
    )(q, kv, kv, kv, kv, key_aug, key_aug, gates, slope_rows)


def _nsa_key_aug(s):
    pos = np.arange(s)
    aug = np.zeros((s, LANES), np.float32)
    aug[pos, pos // SLC_BLOCK] = 1.0
    aug[:, AUG_POS_LANE:AUG_POS_LANE + 3] = (pos // 64)[:, None]
    aug[:, AUG_POS_LANE + 3:AUG_POS_LANE + 6] = (pos % 64)[:, None]
    return jnp.asarray(aug, BF16)


def _nsa_slope_rows(slopes_l2):
    rr, gg = slopes_l2.shape
    hi, mid, lo = _split3(jnp.asarray(slopes_l2, F32))
    pieces = jnp.stack([hi, mid, lo], axis=-1).astype(F32)
    rows = jnp.zeros((rr, gg, 1, LANES), F32)
    rows = rows.at[:, :, 0, AUG_POS_LANE:AUG_POS_LANE + 3].set(pieces * 64.0)
    rows = rows.at[:, :, 0, AUG_POS_LANE + 3:AUG_POS_LANE + 6].set(pieces)
    return rows.astype(BF16)


def _pad_lanes(a, width=LANES):
    return jnp.pad(a, ((0, 0), (0, width - a.shape[1])))


def _fox_layer(x2d, b, s, w_in, b_f, w_out, ln_g, ln_b):
    d = x2d.shape[1]
    attn = w_out.shape[0]
    n_heads = attn // HEAD_DIM
    scale = HEAD_DIM ** -0.5 * LOG2E
    col_scale = jnp.concatenate([jnp.full((attn,), scale, F32), jnp.ones((2 * attn,), F32)])[None, :]
    qkv, z = _project(x2d, w_in[:, :3 * attn].astype(BF16), col_scale,
                      _pad_lanes(w_in[:, 3 * attn:])[None], _pad_lanes(b_f[None, :])[None], False, True)
    gate_bias = _gate_cumsum(z.reshape(b, s, LANES), n_heads)
    o = _fox_attention(qkv.reshape(b, s, 3 * attn), gate_bias, n_heads)
    return _out_ln([o.reshape(b * s, attn)], w_out.astype(BF16), x2d,
                   ln_g.reshape(1, d), ln_b.reshape(1, d))


def _selection_weights_t(nc, n_slc):
    cs = np.arange(nc)[:, None] * CMP_STRIDE
    ce = cs + CMP_BLOCK
    ss = np.arange(n_slc)[None, :] * SLC_BLOCK
    se = ss + SLC_BLOCK
    w = np.clip(np.minimum(ce, se) - np.maximum(cs, ss), 0, None) / CMP_STRIDE
    w[nc - 1, :] = 0.0
    return jnp.asarray(w.T, BF16)


def _nsa_layer(x2d, b, s, w_kv, cmp_pe, cmp_w1, cmp_b1, cmp_w2, w_q, b_g, w_out, ln_g, ln_b):
    d = x2d.shape[1]
    gg, rr = NSA_KV_GROUPS, NSA_HEADS_PER_GROUP
    attn = gg * rr * HEAD_DIM
    n_heads = gg * rr
    wq = w_q[:, :attn].reshape(d, gg, rr, HEAD_DIM).transpose(0, 2, 1, 3).reshape(d, attn)
    wgate = w_q[:, attn:].reshape(d, gg, rr, 3).transpose(2, 0, 3, 1).reshape(rr, d, 3 * gg)
    bgate = b_g.reshape(gg, rr, 3).transpose(1, 2, 0).reshape(rr, 1, 3 * gg)
    wgate = jnp.pad(wgate, ((0, 0), (0, 0), (0, LANES - 3 * gg)))
    bgate = jnp.pad(bgate, ((0, 0), (0, 0), (0, LANES - 3 * gg)))
    wo = w_out.reshape(gg, rr, HEAD_DIM, d).transpose(1, 0, 2, 3).reshape(attn, d)
    w_all = jnp.concatenate([wq, w_kv], axis=1).astype(BF16)
    col_scale = jnp.concatenate([jnp.full((attn,), HEAD_DIM ** -0.5 * LOG2E, F32),
                                 jnp.ones((w_kv.shape[1],), F32)])[None, :]
    qkv, gates = _project(x2d, w_all, col_scale, wgate, bgate, True, False)
    width = qkv.shape[1]
    qkv = qkv.reshape(b, s, width)
    gates = gates.reshape(rr, b, s, LANES)

    nc = s // CMP_STRIDE
    half = CMP_STRIDE * HEAD_DIM
    raw = qkv[:, :, attn:attn + 2 * GROUP_LANES]
    chunks = raw.reshape(b, nc, CMP_STRIDE, 2, gg, HEAD_DIM).transpose(0, 3, 4, 1, 2, 5)
    chunks = chunks.reshape(b, 2, gg, nc, half)
    pe_flat = cmp_pe.reshape(2, 2, half)
    hid = cmp_w1.shape[2]
    w2_placed = jnp.zeros((2, gg, hid, GROUP_LANES), F32)
    for g in range(gg):
        w2_placed = w2_placed.at[:, g, :, g * HEAD_DIM:(g + 1) * HEAD_DIM].set(cmp_w2)
    cmp_kv = _compress(chunks, pe_flat, cmp_w1.astype(BF16), cmp_b1.reshape(2, 1, hid),
                       w2_placed.astype(BF16))

    slopes_l2 = (2.0 ** (-8.0 * np.arange(1, n_heads + 1) / n_heads) * LOG2E).reshape(gg, rr).T
    slopes_rep = jnp.asarray(np.broadcast_to(
        slopes_l2[:, :, None, None], (rr, gg, 1, nc)).astype(np.float32))
    n_slc = s // SLC_BLOCK
    assert n_slc <= AUG_POS_LANE, "selection flags must fit below the position lanes"
    o_c, sel_q = _nsa_compressed(qkv, cmp_kv, gates, slopes_rep, _selection_weights_t(nc, n_slc))
    key_aug = _nsa_key_aug(s)
    slope_rows = _nsa_slope_rows(slopes_l2)
    base = attn // GROUP_LANES
    o_s = _nsa_flash(qkv, qkv, base + 2, base + 3, key_aug, gates, slope_rows, sel_q)
    o_w = _nsa_window(qkv, qkv, base + 4, base + 5, key_aug, gates, slope_rows)
    n = b * s
    return _out_ln([o_c.reshape(n, attn), o_s.reshape(n, attn), o_w.reshape(n, attn)],
                   wo.astype(BF16), x2d, ln_g.reshape(1, d), ln_b.reshape(1, d))


def kernel(x, p, fox_w_in, fox_b_f, fox_w_out, nsa_w_kv, cmp_pe, cmp_w1, cmp_b1, cmp_w2, nsa_w_q, nsa_b_g, nsa_w_out, ln_g, ln_b, moe_w_router, moe_b_router, moe_w_gate, moe_w_up, moe_w_down, shared_w_gate, shared_w_up, shared_w_down, ple_w_proj, ple_w_gate, ple_b_gate):
    b, s, d = x.shape
    n = b * s
    depth = p.shape[0]
    n_a = depth // 2
    h = x.reshape(n, d)
    for i in range(depth):
        if i < n_a:
            h1, h1b = _fox_layer(h, b, s, fox_w_in[i], fox_b_f[i], fox_w_out[i], ln_g[i, 0], ln_b[i, 0])
        else:
            j = i - n_a
            h1, h1b = _nsa_layer(h, b, s, nsa_w_kv, cmp_pe, cmp_w1, cmp_b1, cmp_w2,
                                 nsa_w_q[j], nsa_b_g[j], nsa_w_out[j], ln_g[i, 0], ln_b[i, 0])
        h = _moe_layer(h1, h1b, p.reshape(depth, n, -1), i, moe_w_router[i], moe_b_router[i],
                       moe_w_gate, moe_w_up, moe_w_down,
                       shared_w_gate[i], shared_w_up[i], shared_w_down[i],
                       ple_w_proj[i], ple_w_gate[i], ple_b_gate[i],
                       ln_g[i, 1], ln_b[i, 1], ln_g[i, 2], ln_b[i, 2])
    return h.reshape(b, s, d)
```

```python
import functools

import numpy as np
import jax
import jax.numpy as jnp
from jax import lax
from jax.experimental import pallas as pl
from jax.experimental.pallas import tpu as pltpu

F32 = jnp.float32
BF16 = jnp.bfloat16
NEG = -1e30

HEAD_DIM = 64
LANES = 128
NSA_KV_GROUPS = 4
NSA_HEADS_PER_GROUP = 4
GROUP_LANES = NSA_KV_GROUPS * HEAD_DIM
CMP_BLOCK = 32
CMP_STRIDE = 16
SLC_BLOCK = 64
SLC_TOPK = 16
WINDOW = 512
EXPERT_TOPK = 8
EXPERT_GROUPS = 8
EXPERT_TOPK_GROUPS = 4
ROUTED_SCALE = 2.5
LN_EPS = 1e-5
DEPTH = 2
DEEPNORM_ALPHA = (2.0 * DEPTH) ** 0.25
VMEM_LIMIT = 52 * 1024 * 1024

ROW_TILE = 512
FOX_TILE = 1024
FOX_SUB = 128
NSA_Q_TILE = 256
NSA_SEL_TILE = 1024
NSA_WIN_TILE = 1024
NSA_SUB = 256
ROUTER_TILE = 512
EXPERT_ROW_BLOCK = 512
EXPERT_CHUNKS = 8
COMBINE_CHUNKS = 4


def _cparams(sem):
    return pltpu.CompilerParams(dimension_semantics=sem, vmem_limit_bytes=VMEM_LIMIT)


def _nt(a, b):
    return lax.dot_general(a, b, (((1,), (1,)), ((), ())), preferred_element_type=F32)


def _dot(a, b):
    return jnp.dot(a, b, preferred_element_type=F32)


def _split2(a):
    hi = a.astype(BF16)
    lo = (a - hi.astype(F32)).astype(BF16)
    return hi, lo


def _split3(a):
    hi = a.astype(BF16)
    r = a - hi.astype(F32)
    mid = r.astype(BF16)
    lo = (r - mid.astype(F32)).astype(BF16)
    return hi, mid, lo


def _sigmoid(x):
    return 1.0 / (1.0 + jnp.exp(-x))


def _layer_norm(z, g, b):
    mu = jnp.mean(z, axis=-1, keepdims=True)
    zc = z - mu
    var = jnp.mean(zc * zc, axis=-1, keepdims=True)
    return zc * lax.rsqrt(var + LN_EPS) * g + b


def _proj_kernel(x_ref, w_ref, cs_ref, wg_ref, bg_ref, o_ref, g_ref, *, tn, gate_sigmoid, gate_split):
    x = x_ref[...]
    xh = x.astype(BF16)
    m_out = w_ref.shape[1]
    for j in range(m_out // tn):
        sl = slice(j * tn, (j + 1) * tn)
        y = _dot(xh, w_ref[:, sl])
        o_ref[:, sl] = (y * cs_ref[:, sl]).astype(o_ref.dtype)
    xl = (x - xh.astype(F32)).astype(BF16)
    for i in range(wg_ref.shape[0]):
        wh, wl = _split2(wg_ref[i])
        g = _dot(xh, wh) + bg_ref[i]
        if gate_split:
            g = g + _dot(xh, wl) + _dot(xl, wh)
        if gate_sigmoid:
            g = _sigmoid(g)
        g_ref[i] = g


def _project(x2d, w_bf16, col_scale, w_gate, b_gate, gate_sigmoid, gate_split):
    n, k = x2d.shape
    m_out = w_bf16.shape[1]
    n_g = w_gate.shape[0]
    tm = min(ROW_TILE, n)
    return pl.pallas_call(
        functools.partial(_proj_kernel, tn=512, gate_sigmoid=gate_sigmoid, gate_split=gate_split),
        grid=(n // tm,),
        in_specs=[
            pl.BlockSpec((tm, k), lambda i: (i, 0)),
            pl.BlockSpec((k, m_out), lambda i: (0, 0)),
            pl.BlockSpec((1, m_out), lambda i: (0, 0)),
            pl.BlockSpec((n_g, k, LANES), lambda i: (0, 0, 0)),
            pl.BlockSpec((n_g, 1, LANES), lambda i: (0, 0, 0)),
        ],
        out_specs=[
            pl.BlockSpec((tm, m_out), lambda i: (i, 0)),
            pl.BlockSpec((n_g, tm, LANES), lambda i: (0, i, 0)),
        ],
        out_shape=[
            jax.ShapeDtypeStruct((n, m_out), BF16),
            jax.ShapeDtypeStruct((n_g, n, LANES), F32),
        ],
        compiler_params=_cparams(("arbitrary",)),
        name="project",
    )(x2d, w_bf16, col_scale, w_gate, b_gate)


LOG2E = 1.4426950408889634
GATE_PIECE_STRIDE = 16


def _gate_cumsum_kernel(z_ref, o_ref, carry_ref, *, ts, n_heads):
    @pl.when(pl.program_id(1) == 0)
    def _():
        carry_ref[...] = jnp.zeros_like(carry_ref)

    z = z_ref[0]
    log_f = jnp.minimum(z, 0.0) - jnp.log(1.0 + jnp.exp(-jnp.abs(z)))
    row = lax.broadcasted_iota(jnp.int32, (ts, ts), 0)
    col = lax.broadcasted_iota(jnp.int32, (ts, ts), 1)
    tri = jnp.where(col <= row, 1.0, 0.0).astype(BF16)
    hi, mid, lo = _split3(log_f)
    cs = _dot(tri, hi) + _dot(tri, mid) + _dot(tri, lo) + carry_ref[...]
    carry_ref[...] = cs[ts - 1:ts, :]
    lane = lax.broadcasted_iota(jnp.int32, (1, LANES), 1)
    bias = jnp.where(lane < n_heads, cs * (-LOG2E), 0.0)
    hi, mid, lo = _split3(bias)
    pieces = (hi.astype(F32) + pltpu.roll(mid.astype(F32), GATE_PIECE_STRIDE, 1)
              + pltpu.roll(lo.astype(F32), 2 * GATE_PIECE_STRIDE, 1))
    o_ref[0] = (pieces + pltpu.roll(pieces, HEAD_DIM, 1)).astype(o_ref.dtype)


def _gate_cumsum(z, n_heads):
    b, s, _ = z.shape
    ts = min(256, s)
    return pl.pallas_call(
        functools.partial(_gate_cumsum_kernel, ts=ts, n_heads=n_heads),
        grid=(b, s // ts),
        in_specs=[pl.BlockSpec((1, ts, LANES), lambda i, j: (i, j, 0))],
        out_specs=pl.BlockSpec((1, ts, LANES), lambda i, j: (i, j, 0)),
        out_shape=jax.ShapeDtypeStruct((b, s, LANES), BF16),
        scratch_shapes=[pltpu.VMEM((1, LANES), F32)],
        compiler_params=_cparams(("arbitrary", "arbitrary")),
        name="gate_cumsum",
    )(z)


def _flash_items(items):
    outs = []
    for key, s, v, m_prev, acc_prev in items:
        m_new = jnp.maximum(m_prev, jnp.max(s, axis=1, keepdims=True))
        alpha = jnp.exp2(m_prev - m_new)
        p = jnp.concatenate(
            [jnp.exp2(s[:, j * LANES:(j + 1) * LANES] - m_new[:, :LANES])
             for j in range(s.shape[1] // LANES)], axis=1)
        reps = acc_prev.shape[1] // LANES
        alpha = alpha if reps == 1 else jnp.concatenate([alpha] * reps, axis=1)
        outs.append((key, m_new, alpha * acc_prev + _dot(p.astype(BF16), v)))
    return outs


def _fox_kernel(qt_ref, kt_ref, q_ref, k_ref, v_ref, c_ref, o_ref, m_ref, acc_ref, *, t, sub):
    step = pl.program_id(2)
    pair = pl.program_id(1)
    qi = qt_ref[step]
    ki = kt_ref[step]

    @pl.when(ki == 0)
    def _():
        m_ref[...] = jnp.full_like(m_ref, NEG)
        acc_ref[...] = jnp.zeros_like(acc_ref)

    lane = lax.broadcasted_iota(jnp.int32, (1, LANES), 1)
    low = lane < HEAD_DIM

    def body(diag):
        q = q_ref[0]
        k = k_ref[0]
        v = v_ref[0]
        c = c_ref[0]
        items = []
        for h in range(2):
            own = low if h == 0 else jnp.logical_not(low)
            base = (1 - h) * HEAD_DIM + 2 * pair + h
            ones_at = ((lane == base) | (lane == base + GATE_PIECE_STRIDE)
                       | (lane == base + 2 * GATE_PIECE_STRIDE))
            qh = jnp.where(own, q, jnp.where(ones_at, 1.0, 0.0).astype(BF16))
            kh = jnp.where(own, k, c)
            vh = jnp.where(own, v, jnp.ones_like(v))
            for qs in range(t // sub):
                rows = slice(qs * sub, (qs + 1) * sub)
                nk = (qs + 1) * sub if diag else t
                s = _nt(qh[rows], kh[:nk])
                if diag:
                    r_i = lax.broadcasted_iota(jnp.int32, (sub, sub), 0)
                    c_i = lax.broadcasted_iota(jnp.int32, (sub, sub), 1)
                    last = jnp.where(c_i <= r_i, s[:, nk - sub:], NEG)
                    s = last if nk == sub else jnp.concatenate([s[:, :nk - sub], last], axis=1)
                items.append(((h, rows), s, vh[:nk], m_ref[h, rows, :], acc_ref[h, rows, :]))
        for (h, rows), m_new, acc_new in _flash_items(items):
            acc_ref[h, rows, :] = acc_new
            m_ref[h, rows, :] = m_new

    @pl.when(ki < qi)
    def _():
        body(False)

    @pl.when(ki == qi)
    def _():
        body(True)
        a0 = acc_ref[0]
        a1 = acc_ref[1]
        o0 = a0 / pltpu.roll(a0, HEAD_DIM, 1)
        o1 = a1 / pltpu.roll(a1, HEAD_DIM, 1)
        o_ref[0] = jnp.where(low, o0, o1).astype(o_ref.dtype)


def _fox_attention(qkv, gate_bias, n_heads):
    b, s, _ = qkv.shape
    t = min(FOX_TILE, s)
    nq = s // t
    pairs = n_heads // 2
    qt = np.concatenate([np.full((i + 1,), i, np.int32) for i in range(nq)])
    kt = np.concatenate([np.arange(i + 1, dtype=np.int32) for i in range(nq)])
    grid_spec = pltpu.PrefetchScalarGridSpec(
        num_scalar_prefetch=2,
        grid=(b, pairs, len(qt)),
        in_specs=[
            pl.BlockSpec((1, t, LANES), lambda i, j, u, qt, kt: (i, qt[u], j)),
            pl.BlockSpec((1, t, LANES), lambda i, j, u, qt, kt: (i, kt[u], pairs + j)),
            pl.BlockSpec((1, t, LANES), lambda i, j, u, qt, kt: (i, kt[u], 2 * pairs + j)),
            pl.BlockSpec((1, t, LANES), lambda i, j, u, qt, kt: (i, kt[u], 0)),
        ],
        out_specs=pl.BlockSpec((1, t, LANES), lambda i, j, u, qt, kt: (i, qt[u], j)),
        scratch_shapes=[
            pltpu.VMEM((2, t, LANES), F32),
            pltpu.VMEM((2, t, LANES), F32),
        ],
    )
    return pl.pallas_call(
        functools.partial(_fox_kernel, t=t, sub=min(FOX_SUB, t)),
        grid_spec=grid_spec,
        out_shape=jax.ShapeDtypeStruct((b, s, n_heads * HEAD_DIM), BF16),
        compiler_params=_cparams(("arbitrary", "arbitrary", "arbitrary")),
        name="fox_attention",
    )(jnp.asarray(qt), jnp.asarray(kt), qkv, qkv, qkv, gate_bias)


def _out_ln_kernel(*refs, n_in):
    o_refs = refs[:n_in]
    w_ref, x_ref, g_ref, b_ref, y_ref, yb_ref = refs[n_in:]
    if n_in == 1:
        o = o_refs[0][...]
    else:
        o = o_refs[0][...].astype(F32)
        for r in o_refs[1:]:
            o = o + r[...].astype(F32)
        o = o.astype(BF16)
    z = DEEPNORM_ALPHA * x_ref[...] + _dot(o, w_ref[...])
    y = _layer_norm(z, g_ref[...], b_ref[...])
    y_ref[...] = y
    yb_ref[...] = y.astype(BF16)


def _out_ln(o_list, w_bf16, x2d, g, b):
    n, d = x2d.shape
    k = w_bf16.shape[0]
    tm = min(ROW_TILE, n)
    n_in = len(o_list)
    row = lambda i: (i, 0)
    fixed = lambda i: (0, 0)
    return pl.pallas_call(
        functools.partial(_out_ln_kernel, n_in=n_in),
        grid=(n // tm,),
        in_specs=[pl.BlockSpec((tm, k), row)] * n_in + [
            pl.BlockSpec((k, d), fixed),
            pl.BlockSpec((tm, d), row),
            pl.BlockSpec((1, d), fixed),
            pl.BlockSpec((1, d), fixed),
        ],
        out_specs=[pl.BlockSpec((tm, d), row), pl.BlockSpec((tm, d), row)],
        out_shape=[jax.ShapeDtypeStruct((n, d), F32), jax.ShapeDtypeStruct((n, d), BF16)],
        compiler_params=_cparams(("arbitrary",)),
        name="out_ln",
    )(*o_list, w_bf16, x2d, g, b)


def _router_kernel(x_ref, wt_ref, b_ref, idx_ref, w_ref, rank_ref, cnt_ref, carry_ref, *, tn, n_exp):
    @pl.when(pl.program_id(0) == 0)
    def _():
        carry_ref[...] = jnp.zeros_like(carry_ref)

    xh, xl = _split2(x_ref[...])
    wh, wl = _split2(wt_ref[...])
    logits = _nt(wh, xh) + _nt(wh, xl) + _nt(wl, xh)
    scores = _sigmoid(logits)
    biased = scores + b_ref[...]

    per_grp = n_exp // EXPERT_GROUPS
    blocks, gscore = [], []
    for g in range(EXPERT_GROUPS):
        blk = biased[g * per_grp:(g + 1) * per_grp, :]
        m1 = jnp.max(blk, axis=0, keepdims=True)
        eq = blk == m1
        n_eq = jnp.sum(jnp.where(eq, 1.0, 0.0), axis=0, keepdims=True)
        m2 = jnp.max(jnp.where(eq, -3e38, blk), axis=0, keepdims=True)
        blocks.append(blk)
        gscore.append(m1 + jnp.where(n_eq >= 2.0, m1, m2))
    cand = []
    for g in range(EXPERT_GROUPS):
        beaten = jnp.zeros_like(gscore[g])
        for o in range(EXPERT_GROUPS):
            if o == g:
                continue
            wins = (gscore[o] >= gscore[g]) if o < g else (gscore[o] > gscore[g])
            beaten = beaten + jnp.where(wins, 1.0, 0.0)
        cand.append(jnp.where(beaten < float(EXPERT_TOPK_GROUPS), blocks[g], NEG))
    cand = jnp.concatenate(cand, axis=0)

    erow = lax.broadcasted_iota(jnp.int32, (n_exp, tn), 0).astype(F32)
    hot, sel_idx, sel_score = [], [], []
    member = jnp.zeros((n_exp, tn), F32)
    for _ in range(EXPERT_TOPK):
        m = jnp.max(cand, axis=0, keepdims=True)
        first = jnp.min(jnp.where(cand == m, erow, float(n_exp)), axis=0, keepdims=True)
        onehot = erow == first
        sel_idx.append(first)
        sel_score.append(jnp.sum(jnp.where(onehot, scores, 0.0), axis=0, keepdims=True))
        cand = jnp.where(onehot, -3e38, cand)
        member = member + jnp.where(onehot, 1.0, 0.0)
        hot.append(onehot)
    total = sel_score[0]
    for sc in sel_score[1:]:
        total = total + sc

    trow = lax.broadcasted_iota(jnp.int32, (tn, tn), 0)
    tcol = lax.broadcasted_iota(jnp.int32, (tn, tn), 1)
    before = jnp.where(trow < tcol, 1.0, 0.0).astype(BF16)
    prior = _dot(member.astype(BF16), before) + carry_ref[...]
    ranks = [jnp.sum(jnp.where(hot[k], prior, 0.0), axis=0, keepdims=True) for k in range(EXPERT_TOPK)]

    idx_ref[...] = jnp.concatenate(sel_idx, axis=0).astype(jnp.int32)
    w_ref[...] = jnp.concatenate([sc / total * ROUTED_SCALE for sc in sel_score], axis=0)
    rank_ref[...] = jnp.concatenate(ranks, axis=0).astype(jnp.int32)
    carry_ref[...] = carry_ref[...] + jnp.sum(member, axis=1, keepdims=True)
    cnt_ref[...] = carry_ref[...]


def _route(x2d, w_router_t, b_router):
    n, d = x2d.shape
    n_exp = w_router_t.shape[0]
    tn = min(ROUTER_TILE, n)
    col = lambda i: (0, i)
    return pl.pallas_call(
        functools.partial(_router_kernel, tn=tn, n_exp=n_exp),
        grid=(n // tn,),
        in_specs=[
            pl.BlockSpec((tn, d), lambda i: (i, 0)),
            pl.BlockSpec((n_exp, d), lambda i: (0, 0)),
            pl.BlockSpec((n_exp, 1), lambda i: (0, 0)),
        ],
        out_specs=[
            pl.BlockSpec((EXPERT_TOPK, tn), col),
            pl.BlockSpec((EXPERT_TOPK, tn), col),
            pl.BlockSpec((EXPERT_TOPK, tn), col),
            pl.BlockSpec((n_exp, 1), lambda i: (0, 0)),
        ],
        out_shape=[
            jax.ShapeDtypeStruct((EXPERT_TOPK, n), jnp.int32),
            jax.ShapeDtypeStruct((EXPERT_TOPK, n), F32),
            jax.ShapeDtypeStruct((EXPERT_TOPK, n), jnp.int32),
            jax.ShapeDtypeStruct((n_exp, 1), F32),
        ],
        scratch_shapes=[pltpu.VMEM((n_exp, 1), F32)],
        compiler_params=_cparams(("arbitrary",)),
        name="router",
    )(x2d, w_router_t, b_router)


def _experts_kernel(be_ref, first_ref, nused_ref, x_ref, wg_ref, wu_ref, wd_ref, *rest):
    y_ref, wg_s, wu_s, wd_s = rest[-4:]
    blk = pl.program_id(0)

    @pl.when(blk < nused_ref[0])
    def _():
        @pl.when((first_ref[blk] == 1) | (blk == 0))
        def _():
            wg_s[...] = wg_ref[0, 0].astype(BF16)
            wu_s[...] = wu_ref[0, 0].astype(BF16)
            wd_s[...] = wd_ref[0, 0].astype(BF16)

        x = x_ref[...]
        gate = _dot(x, wg_s[...])
        up = _dot(x, wu_s[...])
        h = gate * _sigmoid(gate) * up
        y_ref[...] = _dot(h.astype(BF16), wd_s[...]).astype(y_ref.dtype)

    @pl.when(blk >= nused_ref[0])
    def _():
        y_ref[...] = jnp.zeros_like(y_ref)


def _experts(xg, blk_expert, blk_first, n_used, w_gate, w_up, w_down, layer, y_prev, first_block, n_rows):
    d = xg.shape[1]
    hdim = w_gate.shape[3]
    tm = EXPERT_ROW_BLOCK
    n_blocks = xg.shape[0] // tm
    extra_specs, extra_args, aliases = [], [], {}
    if y_prev is not None:
        extra_specs, extra_args, aliases = [pl.BlockSpec(memory_space=pl.ANY)], [y_prev], {7: 0}
    grid_spec = pltpu.PrefetchScalarGridSpec(
        num_scalar_prefetch=3,
        grid=(n_blocks,),
        in_specs=[
            pl.BlockSpec((tm, d), lambda i, be, fi, nu: (i, 0)),
            pl.BlockSpec((1, 1, d, hdim), lambda i, be, fi, nu: (layer, be[i], 0, 0)),
            pl.BlockSpec((1, 1, d, hdim), lambda i, be, fi, nu: (layer, be[i], 0, 0)),
            pl.BlockSpec((1, 1, hdim, d), lambda i, be, fi, nu: (layer, be[i], 0, 0)),
        ] + extra_specs,
        out_specs=pl.BlockSpec((tm, d), lambda i, be, fi, nu: (i + first_block, 0)),
        scratch_shapes=[
            pltpu.VMEM((d, hdim), BF16),
            pltpu.VMEM((d, hdim), BF16),
            pltpu.VMEM((hdim, d), BF16),
        ],
    )
    return pl.pallas_call(
        _experts_kernel,
        grid_spec=grid_spec,
        out_shape=jax.ShapeDtypeStruct((n_rows, d), BF16),
        input_output_aliases=aliases,
        compiler_params=_cparams(("arbitrary",)),
        name="experts",
    )(blk_expert, blk_first, n_used, xg, w_gate, w_up, w_down, *extra_args)


def _moe_tail_kernel(x_ref, r_ref, rw_ref, p_ref, sg_ref, su_ref, sd_ref, pg_ref, pb_ref, pp_ref,
                     g1_ref, b1_ref, g2_ref, b2_ref, *rest):
    y_ref = rest[-1]
    x = x_ref[...]
    xb = x.astype(BF16)
    gate = _dot(xb, sg_ref[...])
    up = _dot(xb, su_ref[...])
    shared = _dot((gate * _sigmoid(gate) * up).astype(BF16), sd_ref[...])
    rw = rw_ref[...]
    routed = r_ref[0].astype(F32) * rw[:, 0:1]
    for k in range(1, r_ref.shape[0]):
        routed = routed + r_ref[k].astype(F32) * rw[:, k:k + 1]
    x2 = _layer_norm(DEEPNORM_ALPHA * x + (routed + shared), g1_ref[...], b1_ref[...])
    ple_gate = _sigmoid(_dot(x2.astype(BF16), pg_ref[...]) + pb_ref[...])
    ple = ple_gate * _dot(p_ref[0].astype(BF16), pp_ref[...])
    y_ref[...] = _layer_norm(DEEPNORM_ALPHA * x2 + ple, g2_ref[...], b2_ref[...])


def _moe_tail(x2d, routed, routed_w, p_all, layer, sg, su, sd, pg, pb, pp, g1, b1, g2, b2, y_prev, first_tile):
    n, d = x2d.shape
    tm = min(ROW_TILE, n)
    topk = routed.shape[0]
    row = lambda i: (i + first_tile, 0)
    fixed = lambda i: (0, 0)
    full = lambda a: pl.BlockSpec(a.shape, fixed)
    extra_specs, extra_args, aliases = [], [], {}
    if y_prev is not None:
        extra_specs, extra_args, aliases = [pl.BlockSpec(memory_space=pl.ANY)], [y_prev], {14: 0}
    return pl.pallas_call(
        _moe_tail_kernel,
        grid=(routed.shape[1] // tm,),
        in_specs=[
            pl.BlockSpec((tm, d), row),
            pl.BlockSpec((topk, tm, d), lambda i: (0, i, 0)),
            pl.BlockSpec((tm, topk), row),
            pl.BlockSpec((1, tm, p_all.shape[2]), lambda i: (layer, i + first_tile, 0)),
            full(sg), full(su), full(sd), full(pg), full(pb), full(pp),
            full(g1), full(b1), full(g2), full(b2),
        ] + extra_specs,
        out_specs=pl.BlockSpec((tm, d), row),
        out_shape=jax.ShapeDtypeStruct((n, d), F32),
        input_output_aliases=aliases,
        compiler_params=_cparams(("arbitrary",)),
        name="moe_tail",
    )(x2d, routed, routed_w, p_all, sg, su, sd, pg, pb, pp, g1, b1, g2, b2, *extra_args)


def _moe_layer(x1, x1b, p_all, layer, w_router, b_router, w_gate, w_up, w_down, sg, su, sd,
               ple_proj, ple_gate, ple_bias, g1, b1, g2, b2):
    n, d = x1.shape
    n_exp = w_router.shape[1]
    idx, wts, rank, counts = _route(x1, w_router.T, b_router.reshape(n_exp, 1))
    tm = EXPERT_ROW_BLOCK
    counts = counts[:, 0].astype(jnp.int32)
    padded = (counts + tm - 1) // tm * tm
    pad_end = jnp.cumsum(padded)
    pad_start = pad_end - padded
    n_blocks = (n * EXPERT_TOPK + n_exp * (tm - 1)) // tm + 1
    n_blocks = -(-n_blocks // EXPERT_CHUNKS) * EXPERT_CHUNKS
    n_rows = n_blocks * tm
    experts = jnp.arange(n_exp, dtype=jnp.int32)
    start_of = jnp.sum(jnp.where(idx[:, :, None] == experts, pad_start, 0), axis=-1)
    pos = start_of + rank
    tok = jnp.broadcast_to(jnp.arange(n, dtype=jnp.int32)[None, :], pos.shape)
    filler = jnp.arange(tm, dtype=jnp.int32)[None, :]
    filler_key = jnp.where(filler < (padded - counts)[:, None],
                           (pad_start + counts)[:, None] + filler, jnp.int32(2 ** 30))
    n_tail = max(n_rows - (n * EXPERT_TOPK + n_exp * tm), 0)
    keys = jnp.concatenate([pos.reshape(-1), filler_key.reshape(-1),
                            jnp.full((n_tail,), 2 ** 30, jnp.int32)])
    vals = jnp.concatenate([tok.reshape(-1), jnp.arange(n_exp * tm + n_tail, dtype=jnp.int32) % n])
    tok_pad = lax.sort((keys, vals), num_keys=1, is_stable=False)[1][:n_rows]
    blk_start = jnp.arange(n_blocks, dtype=jnp.int32) * tm
    blk_expert = jnp.minimum(jnp.sum((pad_end[None, :] <= blk_start[:, None]).astype(jnp.int32), axis=1),
                             n_exp - 1)
    blk_first = jnp.concatenate([jnp.ones((1,), jnp.int32),
                                 (blk_expert[1:] != blk_expert[:-1]).astype(jnp.int32)])
    n_used = (pad_end[-1:] // tm).astype(jnp.int32)
    per = n_blocks // EXPERT_CHUNKS
    y = None
    for c in range(EXPERT_CHUNKS):
        blocks = slice(c * per, (c + 1) * per)
        xg = x1b.at[tok_pad[c * per * tm:(c + 1) * per * tm]].get(mode="promise_in_bounds")
        y = _experts(xg, blk_expert[blocks], blk_first[blocks], jnp.clip(n_used - c * per, 0, per),
                     w_gate, w_up, w_down, layer, y, c * per, n_rows)
    tail_weights = (sg.astype(BF16), su.astype(BF16), sd.astype(BF16), ple_gate.astype(BF16),
                    ple_bias.reshape(1, d), ple_proj.astype(BF16),
                    g1.reshape(1, d), b1.reshape(1, d), g2.reshape(1, d), b2.reshape(1, d))
    wts_t = wts.T
    rows = n // COMBINE_CHUNKS
    out = None
    for c in range(COMBINE_CHUNKS):
        routed = y.at[pos[:, c * rows:(c + 1) * rows]].get(mode="promise_in_bounds")
        out = _moe_tail(x1, routed, wts_t, p_all, layer, *tail_weights, out, c * rows // min(ROW_TILE, n))
    return out


def _gelu_tanh(x):
    return 0.5 * x * (1.0 + jnp.tanh(0.7978845608028654 * (x + 0.044715 * (x * x * x))))


def _compress_kernel(c_ref, pe_ref, w1_ref, b1_ref, w2_ref, o_ref, *, nc, half):
    out = jnp.zeros((nc, GROUP_LANES), F32)
    for g in range(NSA_KV_GROUPS):
        c = c_ref[0, 0, g].astype(F32)
        a = _dot((c + pe_ref[0, 0:1, :]).astype(BF16), w1_ref[0, :half, :])
        bm = _dot((c + pe_ref[0, 1:2, :]).astype(BF16), w1_ref[0, half:, :])
        h = _gelu_tanh(a + pltpu.roll(bm, nc - 1, 0) + b1_ref[0])
        out = out + _dot(h.astype(BF16), w2_ref[0, g])
    o_ref[0, 0] = out.astype(o_ref.dtype)


def _compress(chunks, pe_flat, w1, b1, w2_placed):
    b, _, _, nc, half = chunks.shape
    hid = w1.shape[2]
    return pl.pallas_call(
        functools.partial(_compress_kernel, nc=nc, half=half),
        grid=(b, 2),
        in_specs=[
            pl.BlockSpec((1, 1, NSA_KV_GROUPS, nc, half), lambda i, j: (i, j, 0, 0, 0)),
            pl.BlockSpec((1, 2, half), lambda i, j: (j, 0, 0)),
            pl.BlockSpec((1, 2 * half, hid), lambda i, j: (j, 0, 0)),
            pl.BlockSpec((1, 1, hid), lambda i, j: (j, 0, 0)),
            pl.BlockSpec((1, NSA_KV_GROUPS, hid, GROUP_LANES), lambda i, j: (j, 0, 0, 0)),
        ],
        out_specs=pl.BlockSpec((1, 1, nc, GROUP_LANES), lambda i, j: (i, j, 0, 0)),
        out_shape=jax.ShapeDtypeStruct((b, 2, nc, GROUP_LANES), BF16),
        compiler_params=_cparams(("arbitrary", "arbitrary")),
        name="compress",
    )(chunks, pe_flat, w1, b1, w2_placed)


def _group_mask(g):
    lane = lax.broadcasted_iota(jnp.int32, (1, GROUP_LANES), 1)
    return (lane >= g * HEAD_DIM) & (lane < (g + 1) * HEAD_DIM)


def _nsa_cmp_kernel(q_ref, kc_ref, vc_ref, gate_ref, slope_ref, wt_ref, o_ref, sel_ref,
                    *, tq, nc, n_slc):
    qi = pl.program_id(1)
    q0 = qi * tq
    kc = kc_ref[0, 0]
    vc = vc_ref[0, 0]
    t_pos = q0 + lax.broadcasted_iota(jnp.int32, (tq, nc), 0)
    cmp_end = lax.broadcasted_iota(jnp.int32, (tq, nc), 1) * CMP_STRIDE + (CMP_BLOCK - 1)
    visible = cmp_end <= t_pos
    end_rel = (lax.broadcasted_iota(jnp.int32, (1, nc), 1) * CMP_STRIDE + (CMP_BLOCK - 1) - q0).astype(F32)
    v_aug = [jnp.where(_group_mask(g), vc, jnp.ones_like(vc)) for g in range(NSA_KV_GROUPS)]
    psum = [jnp.zeros((tq, nc), F32) for _ in range(NSA_KV_GROUPS)]
    for r in range(NSA_HEADS_PER_GROUP):
        q = q_ref[0, :, r * GROUP_LANES:(r + 1) * GROUP_LANES]
        gates = gate_ref[r, 0]
        out = jnp.zeros((tq, GROUP_LANES), F32)
        for g in range(NSA_KV_GROUPS):
            in_grp = _group_mask(g)
            qm = jnp.where(in_grp, q, jnp.zeros_like(q))
            s = _nt(qm, kc) + slope_ref[r, g, :, :nc] * end_rel
            s = jnp.where(visible, s, NEG)
            m = jnp.max(s, axis=1, keepdims=True)
            p = jnp.where(visible, jnp.exp2(s - m), 0.0)
            o_g = _dot(p.astype(BF16), v_aug[g])
            other = (1 - g // 2) * LANES
            l = o_g[:, other:other + LANES]
            inv = jnp.where(l > 0.0, 1.0 / l, 0.0)
            inv2 = jnp.concatenate([inv, inv], axis=1)
            psum[g] = psum[g] + (p * inv2[:, :nc] if nc == GROUP_LANES else p * inv[:, :1])
            out = out + jnp.where(in_grp, o_g * (inv2 * gates[:, g:g + 1]), 0.0)
        o_ref[0, :, r * GROUP_LANES:(r + 1) * GROUP_LANES] = out.astype(o_ref.dtype)

    if True:
        blk = lax.broadcasted_iota(jnp.int32, (n_slc, tq), 0)
        t_row = q0 + lax.broadcasted_iota(jnp.int32, (n_slc, tq), 1)
        cur = t_row // SLC_BLOCK
        forced = (blk == 0) | (blk == cur) | (blk == cur - 1)
        vis = blk * SLC_BLOCK <= t_row
        wt = wt_ref[...]
        sub8 = lax.broadcasted_iota(jnp.int32, (8, tq), 0)
        pad_rows = jnp.zeros((LANES - n_slc, tq), F32)
        for g in range(NSA_KV_GROUPS):
            hi, mid, lo = _split3(psum[g])
            imp = _nt(wt, hi) + _nt(wt, mid) + _nt(wt, lo)
            score = jnp.where(forced, 1e9, jnp.where(vis, imp, NEG))
            beaten = jnp.zeros((n_slc, tq), F32)
            for j in range(n_slc):
                vj = score[j:j + 1, :]
                lo_r, hi_r = (j // 8) * 8, (j // 8) * 8 + 8
                parts = []
                if lo_r > 0:
                    parts.append(jnp.where(vj > score[:lo_r], 1.0, 0.0))
                own = score[lo_r:hi_r]
                parts.append(jnp.where(sub8 > (j % 8), jnp.where(vj >= own, 1.0, 0.0),
                                       jnp.where(vj > own, 1.0, 0.0)))
                if hi_r < n_slc:
                    parts.append(jnp.where(vj >= score[hi_r:], 1.0, 0.0))
                beaten = beaten + jnp.concatenate(parts, axis=0)
            keep = (beaten < float(min(SLC_TOPK, n_slc))) & (score > 0.5 * NEG)
            flags = jnp.concatenate([jnp.where(keep, 0.0, NEG), pad_rows], axis=0)
            sel_ref[0, g] = flags.T.astype(sel_ref.dtype)


def _nsa_compressed(q, cmp_kv, gates_c, slopes_rep, w_sel_t):
    b, s, _ = q.shape
    nc = cmp_kv.shape[2]
    n_slc = s // SLC_BLOCK
    tq = min(NSA_Q_TILE, s)
    rr = NSA_HEADS_PER_GROUP
    return pl.pallas_call(
        functools.partial(_nsa_cmp_kernel, tq=tq, nc=nc, n_slc=n_slc),
        grid=(b, s // tq),
        in_specs=[
            pl.BlockSpec((1, tq, rr * GROUP_LANES), lambda i, j: (i, j, 0)),
            pl.BlockSpec((1, 1, nc, GROUP_LANES), lambda i, j: (i, 0, 0, 0)),
            pl.BlockSpec((1, 1, nc, GROUP_LANES), lambda i, j: (i, 1, 0, 0)),
            pl.BlockSpec((rr, 1, tq, LANES), lambda i, j: (0, i, j, 0)),
            pl.BlockSpec((rr, NSA_KV_GROUPS, 1, slopes_rep.shape[3]), lambda i, j: (0, 0, 0, 0)),
            pl.BlockSpec((n_slc, nc), lambda i, j: (0, 0)),
        ],
        out_specs=[
            pl.BlockSpec((1, tq, rr * GROUP_LANES), lambda i, j: (i, j, 0)),
            pl.BlockSpec((1, NSA_KV_GROUPS, tq, LANES), lambda i, j: (i, 0, j, 0)),
        ],
        out_shape=[
            jax.ShapeDtypeStruct((b, s, rr * GROUP_LANES), BF16),
            jax.ShapeDtypeStruct((b, NSA_KV_GROUPS, s, LANES), BF16),
        ],
        compiler_params=_cparams(("arbitrary", "arbitrary")),
        name="nsa_compressed",
    )(q, cmp_kv, cmp_kv, gates_c, slopes_rep, w_sel_t)


AUG_POS_LANE = 64
MODE_FULL, MODE_LOWER, MODE_UPPER = 0, 1, 2


def _nsa_flash_kernel(qt_ref, kt_ref, fl_ref, *refs, t, sub, use_sel):
    if use_sel:
        q_ref, k_ref, v_ref, ka_ref, gate_ref, slope_ref, sel_ref, o_ref, m_ref, acc_ref = refs
    else:
        q_ref, k_ref, v_ref, ka_ref, gate_ref, slope_ref, o_ref, m_ref, acc_ref = refs
    step = pl.program_id(2)
    flags = fl_ref[step]
    mode = flags >> 2

    @pl.when((flags & 1) != 0)
    def _():
        m_ref[...] = jnp.full_like(m_ref, NEG)
        acc_ref[...] = jnp.zeros_like(acc_ref)

    lane = lax.broadcasted_iota(jnp.int32, (1, LANES), 1)

    def body(mode_static):
        q = q_ref[0]
        k = k_ref[0]
        v = v_ref[0]
        ka = ka_ref[...]
        k_aug = (jnp.concatenate([k[:, :LANES], ka], axis=1), jnp.concatenate([ka, k[:, LANES:]], axis=1))
        items = []
        for g in range(NSA_KV_GROUPS):
            half = g // 2
            in_grp = (lane >= (g % 2) * HEAD_DIM) & (lane < (g % 2 + 1) * HEAD_DIM)
            q_own = jnp.where(in_grp, q[:, half * LANES:(half + 1) * LANES], jnp.zeros((t, LANES), BF16))
            q_extra = jnp.broadcast_to(slope_ref[0, g], (t, LANES))
            if use_sel:
                q_extra = jnp.where(lane < AUG_POS_LANE, sel_ref[0, g], q_extra)
            qg = jnp.concatenate([q_own, q_extra] if half == 0 else [q_extra, q_own], axis=1)
            vg = jnp.where(_group_mask(g), v, jnp.ones_like(v))
            for qs in range(t // sub):
                rows = slice(qs * sub, (qs + 1) * sub)
                if mode_static == MODE_FULL:
                    keys = slice(0, t)
                elif mode_static == MODE_LOWER:
                    keys = slice(0, (qs + 1) * sub)
                else:
                    keys = slice(qs * sub, t)
                s = _nt(qg[rows], k_aug[half][keys])
                if mode_static != MODE_FULL:
                    r_i = lax.broadcasted_iota(jnp.int32, (sub, sub), 0)
                    c_i = lax.broadcasted_iota(jnp.int32, (sub, sub), 1)
                    if mode_static == MODE_LOWER:
                        edge = jnp.where(c_i <= r_i, s[:, -sub:], NEG)
                        s = edge if s.shape[1] == sub else jnp.concatenate([s[:, :-sub], edge], axis=1)
                    else:
                        edge = jnp.where(c_i > r_i, s[:, :sub], NEG)
                        s = edge if s.shape[1] == sub else jnp.concatenate([edge, s[:, sub:]], axis=1)
                items.append(((g, rows), s, vg[keys], m_ref[g, rows, :], acc_ref[g, rows, :]))
        for (g, rows), m_new, acc_new in _flash_items(items):
            acc_ref[g, rows, :] = acc_new
            m_ref[g, rows, :] = m_new

    for mode_static in ((MODE_FULL, MODE_LOWER) if use_sel else (MODE_LOWER, MODE_UPPER)):
        @pl.when(mode == mode_static)
        def _(mode_static=mode_static):
            body(mode_static)

    @pl.when((flags & 2) != 0)
    def _():
        gates = gate_ref[0, 0]
        out = jnp.zeros((t, GROUP_LANES), F32)
        for g in range(NSA_KV_GROUPS):
            acc = acc_ref[g]
            denom = pltpu.roll(acc, LANES, 1)
            col = (1 if use_sel else 2) * NSA_KV_GROUPS + g
            out = out + jnp.where(_group_mask(g), acc * (gates[:, col:col + 1] / denom), 0.0)
        o_ref[0] = out.astype(o_ref.dtype)


def _nsa_flash_tables(s, t, windowed):
    qt, kt, fl = [], [], []
    for qi in range(s // t):
        if windowed:
            tiles = [(qi - 1, MODE_UPPER)] if qi > 0 else []
            tiles.append((qi, MODE_LOWER))
        else:
            tiles = [(ki, MODE_FULL) for ki in range(qi)] + [(qi, MODE_LOWER)]
        for n, (ki, mode) in enumerate(tiles):
            qt.append(qi)
            kt.append(ki)
            fl.append((1 if n == 0 else 0) | (2 if n == len(tiles) - 1 else 0) | (mode << 2))
    return tuple(jnp.asarray(np.asarray(a, np.int32)) for a in (qt, kt, fl))


def _nsa_flash(q, kv, k_col, v_col, key_aug, gates, slope_rows, sel_q=None):
    b, s, _ = q.shape
    use_sel = sel_q is not None
    t = min(NSA_SEL_TILE if use_sel else NSA_WIN_TILE, s)
    assert use_sel or t == WINDOW, "the window branch assumes tile == WINDOW"
    qt, kt, fl = _nsa_flash_tables(s, t, not use_sel)
    rr = NSA_HEADS_PER_GROUP
    in_specs = [
        pl.BlockSpec((1, t, GROUP_LANES), lambda i, r, u, qt, kt, fl: (i, qt[u], r)),
        pl.BlockSpec((1, t, GROUP_LANES), lambda i, r, u, qt, kt, fl: (i, kt[u], k_col)),
        pl.BlockSpec((1, t, GROUP_LANES), lambda i, r, u, qt, kt, fl: (i, kt[u], v_col)),
        pl.BlockSpec((t, LANES), lambda i, r, u, qt, kt, fl: (kt[u], 0)),
        pl.BlockSpec((1, 1, t, LANES), lambda i, r, u, qt, kt, fl: (r, i, qt[u], 0)),
        pl.BlockSpec((1, NSA_KV_GROUPS, 1, LANES), lambda i, r, u, qt, kt, fl: (r, 0, 0, 0)),
    ]
    args = [q, kv, kv, key_aug, gates, slope_rows]
    if use_sel:
        in_specs.append(pl.BlockSpec((1, NSA_KV_GROUPS, t, LANES), lambda i, r, u, qt, kt, fl: (i, 0, qt[u], 0)))
        args.append(sel_q)
    grid_spec = pltpu.PrefetchScalarGridSpec(
        num_scalar_prefetch=3,
        grid=(b, rr, int(qt.shape[0])),
        in_specs=in_specs,
        out_specs=pl.BlockSpec((1, t, GROUP_LANES), lambda i, r, u, qt, kt, fl: (i, qt[u], r)),
        scratch_shapes=[
            pltpu.VMEM((NSA_KV_GROUPS, t, LANES), F32),
            pltpu.VMEM((NSA_KV_GROUPS, t, GROUP_LANES), F32),
        ],
    )
    return pl.pallas_call(
        functools.partial(_nsa_flash_kernel, t=t, sub=min(NSA_SUB, t), use_sel=use_sel),
        grid_spec=grid_spec,
        out_shape=jax.ShapeDtypeStruct((b, s, rr * GROUP_LANES), BF16),
        compiler_params=_cparams(("arbitrary", "arbitrary", "arbitrary")),
        name="nsa_selected" if use_sel else "nsa_window",
    )(qt, kt, fl, *args)


def _nsa_window_kernel(q_ref, kp_ref, kc_ref, vp_ref, vc_ref, kap_ref, kac_ref, gate_ref, slope_ref, o_ref,
                       *, t, sub):
    has_prev = pl.program_id(2) > 0
    lane = lax.broadcasted_iota(jnp.int32, (1, LANES), 1)
    q = q_ref[0]
    k = jnp.concatenate([kp_ref[0], kc_ref[0]], axis=0)
    v = jnp.concatenate([vp_ref[0], vc_ref[0]], axis=0)
    ka = jnp.concatenate([kap_ref[...], kac_ref[...]], axis=0)
    k_aug = (jnp.concatenate([k[:, :LANES], ka], axis=1), jnp.concatenate([ka, k[:, LANES:]], axis=1))
    gates = gate_ref[0, 0]
    span = WINDOW + sub
    r_i = lax.broadcasted_iota(jnp.int32, (sub, sub), 0)
    c_i = lax.broadcasted_iota(jnp.int32, (sub, sub), 1)
    out_rows = [jnp.zeros((sub, GROUP_LANES), F32) for _ in range(t // sub)]
    for g in range(NSA_KV_GROUPS):
        half = g // 2
        in_grp = (lane >= (g % 2) * HEAD_DIM) & (lane < (g % 2 + 1) * HEAD_DIM)
        q_own = jnp.where(in_grp, q[:, half * LANES:(half + 1) * LANES], jnp.zeros((t, LANES), BF16))
        q_extra = jnp.broadcast_to(slope_ref[0, g], (t, LANES))
        qg = jnp.concatenate([q_own, q_extra] if half == 0 else [q_extra, q_own], axis=1)
        vg = jnp.where(_group_mask(g), v, jnp.ones_like(v))
        col = 2 * NSA_KV_GROUPS + g
        for qs in range(t // sub):
            rows = slice(qs * sub, (qs + 1) * sub)
            keys = slice(qs * sub, qs * sub + span)
            s = _nt(qg[rows], k_aug[half][keys])
            chunks = [s[:, j * sub:(j + 1) * sub] for j in range(span // sub)]
            chunks[0] = jnp.where(c_i > r_i, chunks[0], NEG)
            chunks[-1] = jnp.where(c_i <= r_i, chunks[-1], NEG)
            for j in range(len(chunks) - 1):
                if qs * sub + j * sub < WINDOW:
                    chunks[j] = jnp.where(has_prev, chunks[j], NEG)
            m = chunks[0]
            for c in chunks[1:]:
                m = jnp.maximum(m, c)
            m = jnp.max(m, axis=1, keepdims=True)
            p = jnp.concatenate([jnp.exp2(c - m) for c in chunks], axis=1)
            acc = _dot(p.astype(BF16), vg[keys])
            denom = pltpu.roll(acc, LANES, 1)
            out_rows[qs] = out_rows[qs] + jnp.where(
                _group_mask(g), acc * (gates[rows, col:col + 1] / denom), 0.0)
    o_ref[0] = jnp.concatenate(out_rows, axis=0).astype(o_ref.dtype)


def _nsa_window(q, kv, k_col, v_col, key_aug, gates, slope_rows):
    b, s, _ = q.shape
    t = min(NSA_WIN_TILE, s)
    sub = min(NSA_SUB, t)
    assert WINDOW % sub == 0 and t % WINDOW == 0
    rr = NSA_HEADS_PER_GROUP
    per = t // WINDOW
    prev = lambda j: jnp.maximum(j * per - 1, 0)
    return pl.pallas_call(
        functools.partial(_nsa_window_kernel, t=t, sub=sub),
        grid=(b, rr, s // t),
        in_specs=[
            pl.BlockSpec((1, t, GROUP_LANES), lambda i, r, j: (i, j, r)),
            pl.BlockSpec((1, WINDOW, GROUP_LANES), lambda i, r, j: (i, prev(j), k_col)),
            pl.BlockSpec((1, t, GROUP_LANES), lambda i, r, j: (i, j, k_col)),
            pl.BlockSpec((1, WINDOW, GROUP_LANES), lambda i, r, j: (i, prev(j), v_col)),
            pl.BlockSpec((1, t, GROUP_LANES), lambda i, r, j: (i, j, v_col)),
            pl.BlockSpec((WINDOW, LANES), lambda i, r, j: (prev(j), 0)),
            pl.BlockSpec((t, LANES), lambda i, r, j: (j, 0)),
            pl.BlockSpec((1, 1, t, LANES), lambda i, r, j: (r, i, j, 0)),
            pl.BlockSpec((1, NSA_KV_GROUPS, 1, LANES), lambda i, r, j: (r, 0, 0, 0)),
        ],
        out_specs=pl.BlockSpec((1, t, GROUP_LANES), lambda i, r, j: (i, j, r)),
        out_shape=jax.ShapeDtypeStruct((b, s, rr * GROUP_LANES), BF16),
        compiler_params=_cparams(("arbitrary", "arbitrary", "arbitrary")),
        name="nsa_window",
    )(q, kv, kv, kv, kv, key_aug, key_aug, gates, slope_rows)


def _nsa_key_aug(s):
    pos = np.arange(s)
    aug = np.zeros((s, LANES), np.float32)
    aug[pos, pos // SLC_BLOCK] = 1.0
    aug[:, AUG_POS_LANE:AUG_POS_LANE + 3] = (pos // 64)[:, None]
    aug[:, AUG_POS_LANE + 3:AUG_POS_LANE + 6] = (pos % 64)[:, None]
    return jnp.asarray(aug, BF16)


def _nsa_slope_rows(slopes_l2):
    rr, gg = slopes_l2.shape
    hi, mid, lo = _split3(jnp.asarray(slopes_l2, F32))
    pieces = jnp.stack([hi, mid, lo], axis=-1).astype(F32)
    rows = jnp.zeros((rr, gg, 1, LANES), F32)
    rows = rows.at[:, :, 0, AUG_POS_LANE:AUG_POS_LANE + 3].set(pieces * 64.0)
    rows = rows.at[:, :, 0, AUG_POS_LANE + 3:AUG_POS_LANE + 6].set(pieces)
    return rows.astype(BF16)


def _pad_lanes(a, width=LANES):
    return jnp.pad(a, ((0, 0), (0, width - a.shape[1])))


def _fox_layer(x2d, b, s, w_in, b_f, w_out, ln_g, ln_b):
    d = x2d.shape[1]
    attn = w_out.shape[0]
    n_heads = attn // HEAD_DIM
    scale = HEAD_DIM ** -0.5 * LOG2E
    col_scale = jnp.concatenate([jnp.full((attn,), scale, F32), jnp.ones((2 * attn,), F32)])[None, :]
    qkv, z = _project(x2d, w_in[:, :3 * attn].astype(BF16), col_scale,
                      _pad_lanes(w_in[:, 3 * attn:])[None], _pad_lanes(b_f[None, :])[None], False, True)
    gate_bias = _gate_cumsum(z.reshape(b, s, LANES), n_heads)
    o = _fox_attention(qkv.reshape(b, s, 3 * attn), gate_bias, n_heads)
    return _out_ln([o.reshape(b * s, attn)], w_out.astype(BF16), x2d,
                   ln_g.reshape(1, d), ln_b.reshape(1, d))


def _selection_weights_t(nc, n_slc):
    cs = np.arange(nc)[:, None] * CMP_STRIDE
    ce = cs + CMP_BLOCK
    ss = np.arange(n_slc)[None, :] * SLC_BLOCK
    se = ss + SLC_BLOCK
    w = np.clip(np.minimum(ce, se) - np.maximum(cs, ss), 0, None) / CMP_STRIDE
    w[nc - 1, :] = 0.0
    return jnp.asarray(w.T, BF16)


def _nsa_layer(x2d, b, s, w_kv, cmp_pe, cmp_w1, cmp_b1, cmp_w2, w_q, b_g, w_out, ln_g, ln_b):
    d = x2d.shape[1]
    gg, rr = NSA_KV_GROUPS, NSA_HEADS_PER_GROUP
    attn = gg * rr * HEAD_DIM
    n_heads = gg * rr
    wq = w_q[:, :attn].reshape(d, gg, rr, HEAD_DIM).transpose(0, 2, 1, 3).reshape(d, attn)
    wgate = w_q[:, attn:].reshape(d, gg, rr, 3).transpose(2, 0, 3, 1).reshape(rr, d, 3 * gg)
    bgate = b_g.reshape(gg, rr, 3).transpose(1, 2, 0).reshape(rr, 1, 3 * gg)
    wgate = jnp.pad(wgate, ((0, 0), (0, 0), (0, LANES - 3 * gg)))
    bgate = jnp.pad(bgate, ((0, 0), (0, 0), (0, LANES - 3 * gg)))
    wo = w_out.reshape(gg, rr, HEAD_DIM, d).transpose(1, 0, 2, 3).reshape(attn, d)
    w_all = jnp.concatenate([wq, w_kv], axis=1).astype(BF16)
    col_scale = jnp.concatenate([jnp.full((attn,), HEAD_DIM ** -0.5 * LOG2E, F32),
                                 jnp.ones((w_kv.shape[1],), F32)])[None, :]
    qkv, gates = _project(x2d, w_all, col_scale, wgate, bgate, True, False)
    width = qkv.shape[1]
    qkv = qkv.reshape(b, s, width)
    gates = gates.reshape(rr, b, s, LANES)

    nc = s // CMP_STRIDE
    half = CMP_STRIDE * HEAD_DIM
    raw = qkv[:, :, attn:attn + 2 * GROUP_LANES]
    chunks = raw.reshape(b, nc, CMP_STRIDE, 2, gg, HEAD_DIM).transpose(0, 3, 4, 1, 2, 5)
    chunks = chunks.reshape(b, 2, gg, nc, half)
    pe_flat = cmp_pe.reshape(2, 2, half)
    hid = cmp_w1.shape[2]
    w2_placed = jnp.zeros((2, gg, hid, GROUP_LANES), F32)
    for g in range(gg):
        w2_placed = w2_placed.at[:, g, :, g * HEAD_DIM:(g + 1) * HEAD_DIM].set(cmp_w2)
    cmp_kv = _compress(chunks, pe_flat, cmp_w1.astype(BF16), cmp_b1.reshape(2, 1, hid),
                       w2_placed.astype(BF16))

    slopes_l2 = (2.0 ** (-8.0 * np.arange(1, n_heads + 1) / n_heads) * LOG2E).reshape(gg, rr).T
    slopes_rep = jnp.asarray(np.broadcast_to(
        slopes_l2[:, :, None, None], (rr, gg, 1, nc)).astype(np.float32))
    n_slc = s // SLC_BLOCK
    assert n_slc <= AUG_POS_LANE, "selection flags must fit below the position lanes"
    o_c, sel_q = _nsa_compressed(qkv, cmp_kv, gates, slopes_rep, _selection_weights_t(nc, n_slc))
    key_aug = _nsa_key_aug(s)
    slope_rows = _nsa_slope_rows(slopes_l2)
    base = attn // GROUP_LANES
    o_s = _nsa_flash(qkv, qkv, base + 2, base + 3, key_aug, gates, slope_rows, sel_q)
    o_w = _nsa_window(qkv, qkv, base + 4, base + 5, key_aug, gates, slope_rows)
    n = b * s
    return _out_ln([o_c.reshape(n, attn), o_s.reshape(n, attn), o_w.reshape(n, attn)],
                   wo.astype(BF16), x2d, ln_g.reshape(1, d), ln_b.reshape(1, d))


def kernel(x, p, fox_w_in, fox_b_f, fox_w_out, nsa_w_kv, cmp_pe, cmp_w1, cmp_b1, cmp_w2, nsa_w_q, nsa_b_g, nsa_w_out, ln_g, ln_b, moe_w_router, moe_b_router, moe_w_gate, moe_w_up, moe_w_down, shared_w_gate, shared_w_up, shared_w_down, ple_w_proj, ple_w_gate, ple_b_gate):
    b, s, d = x.shape
    n = b * s
    depth = p.shape[0]
    n_a = depth // 2
    h = x.reshape(n, d)
    for i in range(depth):
        if i < n_a:
            h1, h1b = _fox_layer(h, b, s, fox_w_in[i], fox_b_f[i], fox_w_out[i], ln_g[i, 0], ln_b[i, 0])
        else:
            j = i - n_a
            h1, h1b = _nsa_layer(h, b, s, nsa_w_kv, cmp_pe, cmp_w1, cmp_b1, cmp_w2,
                                 nsa_w_q[j], nsa_b_g[j], nsa_w_out[j], ln_g[i, 0], ln_b[i, 0])
        h = _moe_layer(h1, h1b, p.reshape(depth, n, -1), i, moe_w_router[i], moe_b_router[i],
                       moe_w_gate, moe_w_up, moe_w_down,
                       shared_w_gate[i], shared_w_up[i], shared_w_down[i],
                       ple_w_proj[i], ple_w_gate[i], ple_b_gate[i],
                       ln_g[i, 1], ln_b[i, 1], ln_g[i, 2], ln_b[i, 2])
    return h.reshape(b, s, d)
```

```python
import functools

import numpy as np
import jax
import jax.numpy as jnp
from jax import lax
from jax.experimental import pallas as pl
from jax.experimental.pallas import tpu as pltpu

F32 = jnp.float32
BF16 = jnp.bfloat16
NEG = -1e30

HEAD_DIM = 64
LANES = 128
NSA_KV_GROUPS = 4
NSA_HEADS_PER_GROUP = 4
GROUP_LANES = NSA_KV_GROUPS * HEAD_DIM
CMP_BLOCK = 32
CMP_STRIDE = 16
SLC_BLOCK = 64
SLC_TOPK = 16
WINDOW = 512
EXPERT_TOPK = 8
EXPERT_GROUPS = 8
EXPERT_TOPK_GROUPS = 4
ROUTED_SCALE = 2.5
LN_EPS = 1e-5
DEPTH = 2
DEEPNORM_ALPHA = (2.0 * DEPTH) ** 0.25
VMEM_LIMIT = 52 * 1024 * 1024

ROW_TILE = 512
FOX_TILE = 2048
FOX_SUB = 128
NSA_Q_TILE = 256
NSA_SEL_TILE = 1024
NSA_WIN_TILE = 1024
NSA_SUB = 256
ROUTER_TILE = 512
EXPERT_ROW_BLOCK = 512
EXPERT_CHUNKS = 8


def _cparams(sem):
    return pltpu.CompilerParams(dimension_semantics=sem, vmem_limit_bytes=VMEM_LIMIT)


def _nt(a, b):
    return lax.dot_general(a, b, (((1,), (1,)), ((), ())), preferred_element_type=F32)


def _dot(a, b):
    return jnp.dot(a, b, preferred_element_type=F32)


def _split2(a):
    hi = a.astype(BF16)
    lo = (a - hi.astype(F32)).astype(BF16)
    return hi, lo


def _split3(a):
    hi = a.astype(BF16)
    r = a - hi.astype(F32)
    mid = r.astype(BF16)
    lo = (r - mid.astype(F32)).astype(BF16)
    return hi, mid, lo


def _sigmoid(x):
    return 1.0 / (1.0 + jnp.exp(-x))


def _layer_norm(z, g, b):
    mu = jnp.mean(z, axis=-1, keepdims=True)
    zc = z - mu
    var = jnp.mean(zc * zc, axis=-1, keepdims=True)
    return zc * lax.rsqrt(var + LN_EPS) * g + b


def _proj_kernel(x_ref, w_ref, cs_ref, wg_ref, bg_ref, o_ref, g_ref, *, tn, gate_sigmoid, gate_split):
    x = x_ref[...]
    xh = x.astype(BF16)
    m_out = w_ref.shape[1]
    for j in range(m_out // tn):
        sl = slice(j * tn, (j + 1) * tn)
        y = _dot(xh, w_ref[:, sl])
        o_ref[:, sl] = (y * cs_ref[:, sl]).astype(o_ref.dtype)
    xl = (x - xh.astype(F32)).astype(BF16)
    for i in range(wg_ref.shape[0]):
        wh, wl = _split2(wg_ref[i])
        g = _dot(xh, wh) + bg_ref[i]
        if gate_split:
            g = g + _dot(xh, wl) + _dot(xl, wh)
        if gate_sigmoid:
            g = _sigmoid(g)
        g_ref[i] = g


def _project(x2d, w_bf16, col_scale, w_gate, b_gate, gate_sigmoid, gate_split):
    n, k = x2d.shape
    m_out = w_bf16.shape[1]
    n_g = w_gate.shape[0]
    tm = min(ROW_TILE, n)
    return pl.pallas_call(
        functools.partial(_proj_kernel, tn=512, gate_sigmoid=gate_sigmoid, gate_split=gate_split),
        grid=(n // tm,),
        in_specs=[
            pl.BlockSpec((tm, k), lambda i: (i, 0)),
            pl.BlockSpec((k, m_out), lambda i: (0, 0)),
            pl.BlockSpec((1, m_out), lambda i: (0, 0)),
            pl.BlockSpec((n_g, k, LANES), lambda i: (0, 0, 0)),
            pl.BlockSpec((n_g, 1, LANES), lambda i: (0, 0, 0)),
        ],
        out_specs=[
            pl.BlockSpec((tm, m_out), lambda i: (i, 0)),
            pl.BlockSpec((n_g, tm, LANES), lambda i: (0, i, 0)),
        ],
        out_shape=[
            jax.ShapeDtypeStruct((n, m_out), BF16),
            jax.ShapeDtypeStruct((n_g, n, LANES), F32),
        ],
        compiler_params=_cparams(("arbitrary",)),
        name="project",
    )(x2d, w_bf16, col_scale, w_gate, b_gate)


LOG2E = 1.4426950408889634
GATE_PIECE_STRIDE = 16


def _gate_cumsum_kernel(z_ref, o_ref, carry_ref, *, ts, n_heads):
    @pl.when(pl.program_id(1) == 0)
    def _():
        carry_ref[...] = jnp.zeros_like(carry_ref)

    z = z_ref[0]
    log_f = jnp.minimum(z, 0.0) - jnp.log(1.0 + jnp.exp(-jnp.abs(z)))
    row = lax.broadcasted_iota(jnp.int32, (ts, ts), 0)
    col = lax.broadcasted_iota(jnp.int32, (ts, ts), 1)
    tri = jnp.where(col <= row, 1.0, 0.0).astype(BF16)
    hi, mid, lo = _split3(log_f)
    cs = _dot(tri, hi) + _dot(tri, mid) + _dot(tri, lo) + carry_ref[...]
    carry_ref[...] = cs[ts - 1:ts, :]
    lane = lax.broadcasted_iota(jnp.int32, (1, LANES), 1)
    bias = jnp.where(lane < n_heads, cs * (-LOG2E), 0.0)
    hi, mid, lo = _split3(bias)
    pieces = (hi.astype(F32) + pltpu.roll(mid.astype(F32), GATE_PIECE_STRIDE, 1)
              + pltpu.roll(lo.astype(F32), 2 * GATE_PIECE_STRIDE, 1))
    o_ref[0] = (pieces + pltpu.roll(pieces, HEAD_DIM, 1)).astype(o_ref.dtype)


def _gate_cumsum(z, n_heads):
    b, s, _ = z.shape
    ts = min(256, s)
    return pl.pallas_call(
        functools.partial(_gate_cumsum_kernel, ts=ts, n_heads=n_heads),
        grid=(b, s // ts),
        in_specs=[pl.BlockSpec((1, ts, LANES), lambda i, j: (i, j, 0))],
        out_specs=pl.BlockSpec((1, ts, LANES), lambda i, j: (i, j, 0)),
        out_shape=jax.ShapeDtypeStruct((b, s, LANES), BF16),
        scratch_shapes=[pltpu.VMEM((1, LANES), F32)],
        compiler_params=_cparams(("arbitrary", "arbitrary")),
        name="gate_cumsum",
    )(z)


def _flash_items(items):
    outs = []
    for key, s, v, m_prev, acc_prev in items:
        m_new = jnp.maximum(m_prev, jnp.max(s, axis=1, keepdims=True))
        alpha = jnp.exp2(m_prev - m_new)
        p = jnp.concatenate(
            [jnp.exp2(s[:, j * LANES:(j + 1) * LANES] - m_new[:, :LANES])
             for j in range(s.shape[1] // LANES)], axis=1)
        reps = acc_prev.shape[1] // LANES
        alpha = alpha if reps == 1 else jnp.concatenate([alpha] * reps, axis=1)
        outs.append((key, m_new, alpha * acc_prev + _dot(p.astype(BF16), v)))
    return outs


def _fox_kernel(qt_ref, kt_ref, q_ref, k_ref, v_ref, c_ref, o_ref, m_ref, acc_ref, *, t, sub):
    step = pl.program_id(2)
    pair = pl.program_id(1)
    qi = qt_ref[step]
    ki = kt_ref[step]

    @pl.when(ki == 0)
    def _():
        m_ref[...] = jnp.full_like(m_ref, NEG)
        acc_ref[...] = jnp.zeros_like(acc_ref)

    lane = lax.broadcasted_iota(jnp.int32, (1, LANES), 1)
    low = lane < HEAD_DIM

    def body(diag):
        q = q_ref[0]
        k = k_ref[0]
        v = v_ref[0]
        c = c_ref[0]
        items = []
        for h in range(2):
            own = low if h == 0 else jnp.logical_not(low)
            base = (1 - h) * HEAD_DIM + 2 * pair + h
            ones_at = ((lane == base) | (lane == base + GATE_PIECE_STRIDE)
                       | (lane == base + 2 * GATE_PIECE_STRIDE))
            qh = jnp.where(own, q, jnp.where(ones_at, 1.0, 0.0).astype(BF16))
            kh = jnp.where(own, k, c)
            vh = jnp.where(own, v, jnp.ones_like(v))
            for qs in range(t // sub):
                rows = slice(qs * sub, (qs + 1) * sub)
                nk = (qs + 1) * sub if diag else t
                s = _nt(qh[rows], kh[:nk])
                if diag:
                    r_i = lax.broadcasted_iota(jnp.int32, (sub, sub), 0)
                    c_i = lax.broadcasted_iota(jnp.int32, (sub, sub), 1)
                    last = jnp.where(c_i <= r_i, s[:, nk - sub:], NEG)
                    s = last if nk == sub else jnp.concatenate([s[:, :nk - sub], last], axis=1)
                items.append(((h, rows), s, vh[:nk], m_ref[h, rows, :], acc_ref[h, rows, :]))
        for (h, rows), m_new, acc_new in _flash_items(items):
            acc_ref[h, rows, :] = acc_new
            m_ref[h, rows, :] = m_new

    @pl.when(ki < qi)
    def _():
        body(False)

    @pl.when(ki == qi)
    def _():
        body(True)
        a0 = acc_ref[0]
        a1 = acc_ref[1]
        o0 = a0 / pltpu.roll(a0, HEAD_DIM, 1)
        o1 = a1 / pltpu.roll(a1, HEAD_DIM, 1)
        o_ref[0] = jnp.where(low, o0, o1).astype(o_ref.dtype)


def _fox_attention(qkv, gate_bias, n_heads):
    b, s, _ = qkv.shape
    t = min(FOX_TILE, s)
    nq = s // t
    pairs = n_heads // 2
    qt = np.concatenate([np.full((i + 1,), i, np.int32) for i in range(nq)])
    kt = np.concatenate([np.arange(i + 1, dtype=np.int32) for i in range(nq)])
    grid_spec = pltpu.PrefetchScalarGridSpec(
        num_scalar_prefetch=2,
        grid=(b, pairs, len(qt)),
        in_specs=[
            pl.BlockSpec((1, t, LANES), lambda i, j, u, qt, kt: (i, qt[u], j)),
            pl.BlockSpec((1, t, LANES), lambda i, j, u, qt, kt: (i, kt[u], pairs + j)),
            pl.BlockSpec((1, t, LANES), lambda i, j, u, qt, kt: (i, kt[u], 2 * pairs + j)),
            pl.BlockSpec((1, t, LANES), lambda i, j, u, qt, kt: (i, kt[u], 0)),
        ],
        out_specs=pl.BlockSpec((1, t, LANES), lambda i, j, u, qt, kt: (i, qt[u], j)),
        scratch_shapes=[
            pltpu.VMEM((2, t, LANES), F32),
            pltpu.VMEM((2, t, LANES), F32),
        ],
    )
    return pl.pallas_call(
        functools.partial(_fox_kernel, t=t, sub=min(FOX_SUB, t)),
        grid_spec=grid_spec,
        out_shape=jax.ShapeDtypeStruct((b, s, n_heads * HEAD_DIM), BF16),
        compiler_params=_cparams(("arbitrary", "arbitrary", "arbitrary")),
        name="fox_attention",
    )(jnp.asarray(qt), jnp.asarray(kt), qkv, qkv, qkv, gate_bias)


def _out_ln_kernel(*refs, n_in):
    o_refs = refs[:n_in]
    w_ref, x_ref, g_ref, b_ref, y_ref, yb_ref = refs[n_in:]
    if n_in == 1:
        o = o_refs[0][...]
    else:
        o = o_refs[0][...].astype(F32)
        for r in o_refs[1:]:
            o = o + r[...].astype(F32)
        o = o.astype(BF16)
    z = DEEPNORM_ALPHA * x_ref[...] + _dot(o, w_ref[...])
    y = _layer_norm(z, g_ref[...], b_ref[...])
    y_ref[...] = y
    yb_ref[...] = y.astype(BF16)


def _out_ln(o_list, w_bf16, x2d, g, b):
    n, d = x2d.shape
    k = w_bf16.shape[0]
    tm = min(ROW_TILE, n)
    n_in = len(o_list)
    row = lambda i: (i, 0)
    fixed = lambda i: (0, 0)
    return pl.pallas_call(
        functools.partial(_out_ln_kernel, n_in=n_in),
        grid=(n // tm,),
        in_specs=[pl.BlockSpec((tm, k), row)] * n_in + [
            pl.BlockSpec((k, d), fixed),
            pl.BlockSpec((tm, d), row),
            pl.BlockSpec((1, d), fixed),
            pl.BlockSpec((1, d), fixed),
        ],
        out_specs=[pl.BlockSpec((tm, d), row), pl.BlockSpec((tm, d), row)],
        out_shape=[jax.ShapeDtypeStruct((n, d), F32), jax.ShapeDtypeStruct((n, d), BF16)],
        compiler_params=_cparams(("arbitrary",)),
        name="out_ln",
    )(*o_list, w_bf16, x2d, g, b)


def _router_kernel(x_ref, wt_ref, b_ref, idx_ref, w_ref, rank_ref, cnt_ref, carry_ref, *, tn, n_exp):
    @pl.when(pl.program_id(0) == 0)
    def _():
        carry_ref[...] = jnp.zeros_like(carry_ref)

    xh, xl = _split2(x_ref[...])
    wh, wl = _split2(wt_ref[...])
    logits = _nt(wh, xh) + _nt(wh, xl) + _nt(wl, xh)
    scores = _sigmoid(logits)
    biased = scores + b_ref[...]

    per_grp = n_exp // EXPERT_GROUPS
    blocks, gscore = [], []
    for g in range(EXPERT_GROUPS):
        blk = biased[g * per_grp:(g + 1) * per_grp, :]
        m1 = jnp.max(blk, axis=0, keepdims=True)
        eq = blk == m1
        n_eq = jnp.sum(jnp.where(eq, 1.0, 0.0), axis=0, keepdims=True)
        m2 = jnp.max(jnp.where(eq, -3e38, blk), axis=0, keepdims=True)
        blocks.append(blk)
        gscore.append(m1 + jnp.where(n_eq >= 2.0, m1, m2))
    cand = []
    for g in range(EXPERT_GROUPS):
        beaten = jnp.zeros_like(gscore[g])
        for o in range(EXPERT_GROUPS):
            if o == g:
                continue
            wins = (gscore[o] >= gscore[g]) if o < g else (gscore[o] > gscore[g])
            beaten = beaten + jnp.where(wins, 1.0, 0.0)
        cand.append(jnp.where(beaten < float(EXPERT_TOPK_GROUPS), blocks[g], NEG))
    cand = jnp.concatenate(cand, axis=0)

    erow = lax.broadcasted_iota(jnp.int32, (n_exp, tn), 0).astype(F32)
    hot, sel_idx, sel_score = [], [], []
    member = jnp.zeros((n_exp, tn), F32)
    for _ in range(EXPERT_TOPK):
        m = jnp.max(cand, axis=0, keepdims=True)
        first = jnp.min(jnp.where(cand == m, erow, float(n_exp)), axis=0, keepdims=True)
        onehot = erow == first
        sel_idx.append(first)
        sel_score.append(jnp.sum(jnp.where(onehot, scores, 0.0), axis=0, keepdims=True))
        cand = jnp.where(onehot, -3e38, cand)
        member = member + jnp.where(onehot, 1.0, 0.0)
        hot.append(onehot)
    total = sel_score[0]
    for sc in sel_score[1:]:
        total = total + sc

    trow = lax.broadcasted_iota(jnp.int32, (tn, tn), 0)
    tcol = lax.broadcasted_iota(jnp.int32, (tn, tn), 1)
    before = jnp.where(trow < tcol, 1.0, 0.0).astype(BF16)
    prior = _dot(member.astype(BF16), before) + carry_ref[...]
    ranks = [jnp.sum(jnp.where(hot[k], prior, 0.0), axis=0, keepdims=True) for k in range(EXPERT_TOPK)]

    idx_ref[...] = jnp.concatenate(sel_idx, axis=0).astype(jnp.int32)
    w_ref[...] = jnp.concatenate([sc / total * ROUTED_SCALE for sc in sel_score], axis=0)
    rank_ref[...] = jnp.concatenate(ranks, axis=0).astype(jnp.int32)
    carry_ref[...] = carry_ref[...] + jnp.sum(member, axis=1, keepdims=True)
    cnt_ref[...] = carry_ref[...]


def _route(x2d, w_router_t, b_router):
    n, d = x2d.shape
    n_exp = w_router_t.shape[0]
    tn = min(ROUTER_TILE, n)
    col = lambda i: (0, i)
    return pl.pallas_call(
        functools.partial(_router_kernel, tn=tn, n_exp=n_exp),
        grid=(n // tn,),
        in_specs=[
            pl.BlockSpec((tn, d), lambda i: (i, 0)),
            pl.BlockSpec((n_exp, d), lambda i: (0, 0)),
            pl.BlockSpec((n_exp, 1), lambda i: (0, 0)),
        ],
        out_specs=[
            pl.BlockSpec((EXPERT_TOPK, tn), col),
            pl.BlockSpec((EXPERT_TOPK, tn), col),
            pl.BlockSpec((EXPERT_TOPK, tn), col),
            pl.BlockSpec((n_exp, 1), lambda i: (0, 0)),
        ],
        out_shape=[
            jax.ShapeDtypeStruct((EXPERT_TOPK, n), jnp.int32),
            jax.ShapeDtypeStruct((EXPERT_TOPK, n), F32),
            jax.ShapeDtypeStruct((EXPERT_TOPK, n), jnp.int32),
            jax.ShapeDtypeStruct((n_exp, 1), F32),
        ],
        scratch_shapes=[pltpu.VMEM((n_exp, 1), F32)],
        compiler_params=_cparams(("arbitrary",)),
        name="router",
    )(x2d, w_router_t, b_router)


def _experts_kernel(be_ref, first_ref, nused_ref, x_ref, wg_ref, wu_ref, wd_ref, *rest):
    y_ref, wg_s, wu_s, wd_s = rest[-4:]
    blk = pl.program_id(0)

    @pl.when(blk < nused_ref[0])
    def _():
        @pl.when((first_ref[blk] == 1) | (blk == 0))
        def _():
            wg_s[...] = wg_ref[0, 0].astype(BF16)
            wu_s[...] = wu_ref[0, 0].astype(BF16)
            wd_s[...] = wd_ref[0, 0].astype(BF16)

        x = x_ref[...]
        gate = _dot(x, wg_s[...])
        up = _dot(x, wu_s[...])
        h = gate * _sigmoid(gate) * up
        y_ref[...] = _dot(h.astype(BF16), wd_s[...]).astype(y_ref.dtype)

    @pl.when(blk >= nused_ref[0])
    def _():
        y_ref[...] = jnp.zeros_like(y_ref)


def _experts(xg, blk_expert, blk_first, n_used, w_gate, w_up, w_down, layer, y_prev, first_block, n_rows):
    d = xg.shape[1]
    hdim = w_gate.shape[3]
    tm = EXPERT_ROW_BLOCK
    n_blocks = xg.shape[0] // tm
    extra_specs, extra_args, aliases = [], [], {}
    if y_prev is not None:
        extra_specs, extra_args, aliases = [pl.BlockSpec(memory_space=pl.ANY)], [y_prev], {7: 0}
    grid_spec = pltpu.PrefetchScalarGridSpec(
        num_scalar_prefetch=3,
        grid=(n_blocks,),
        in_specs=[
            pl.BlockSpec((tm, d), lambda i, be, fi, nu: (i, 0)),
            pl.BlockSpec((1, 1, d, hdim), lambda i, be, fi, nu: (layer, be[i], 0, 0)),
            pl.BlockSpec((1, 1, d, hdim), lambda i, be, fi, nu: (layer, be[i], 0, 0)),
            pl.BlockSpec((1, 1, hdim, d), lambda i, be, fi, nu: (layer, be[i], 0, 0)),
        ] + extra_specs,
        out_specs=pl.BlockSpec((tm, d), lambda i, be, fi, nu: (i + first_block, 0)),
        scratch_shapes=[
            pltpu.VMEM((d, hdim), BF16),
            pltpu.VMEM((d, hdim), BF16),
            pltpu.VMEM((hdim, d), BF16),
        ],
    )
    return pl.pallas_call(
        _experts_kernel,
        grid_spec=grid_spec,
        out_shape=jax.ShapeDtypeStruct((n_rows, d), BF16),
        input_output_aliases=aliases,
        compiler_params=_cparams(("arbitrary",)),
        name="experts",
    )(blk_expert, blk_first, n_used, xg, w_gate, w_up, w_down, *extra_args)


def _moe_tail_kernel(x_ref, r_ref, rw_ref, p_ref, sg_ref, su_ref, sd_ref, pg_ref, pb_ref, pp_ref,
                     g1_ref, b1_ref, g2_ref, b2_ref, y_ref):
    x = x_ref[...]
    xb = x.astype(BF16)
    gate = _dot(xb, sg_ref[...])
    up = _dot(xb, su_ref[...])
    shared = _dot((gate * _sigmoid(gate) * up).astype(BF16), sd_ref[...])
    rw = rw_ref[...]
    routed = r_ref[0].astype(F32) * rw[:, 0:1]
    for k in range(1, r_ref.shape[0]):
        routed = routed + r_ref[k].astype(F32) * rw[:, k:k + 1]
    x2 = _layer_norm(DEEPNORM_ALPHA * x + (routed + shared), g1_ref[...], b1_ref[...])
    ple_gate = _sigmoid(_dot(x2.astype(BF16), pg_ref[...]) + pb_ref[...])
    ple = ple_gate * _dot(p_ref[0].astype(BF16), pp_ref[...])
    y_ref[...] = _layer_norm(DEEPNORM_ALPHA * x2 + ple, g2_ref[...], b2_ref[...])


def _moe_tail(x2d, routed, routed_w, p_all, layer, sg, su, sd, pg, pb, pp, g1, b1, g2, b2):
    n, d = x2d.shape
    tm = min(ROW_TILE, n)
    topk = routed.shape[0]
    row = lambda i: (i, 0)
    fixed = lambda i: (0, 0)
    full = lambda a: pl.BlockSpec(a.shape, fixed)
    return pl.pallas_call(
        _moe_tail_kernel,
        grid=(n // tm,),
        in_specs=[
            pl.BlockSpec((tm, d), row),
            pl.BlockSpec((topk, tm, d), lambda i: (0, i, 0)),
            pl.BlockSpec((tm, topk), row),
            pl.BlockSpec((1, tm, p_all.shape[2]), lambda i: (layer, i, 0)),
            full(sg), full(su), full(sd), full(pg), full(pb), full(pp),
            full(g1), full(b1), full(g2), full(b2),
        ],
        out_specs=pl.BlockSpec((tm, d), row),
        out_shape=jax.ShapeDtypeStruct((n, d), F32),
        compiler_params=_cparams(("arbitrary",)),
        name="moe_tail",
    )(x2d, routed, routed_w, p_all, sg, su, sd, pg, pb, pp, g1, b1, g2, b2)


def _moe_layer(x1, x1b, p_all, layer, w_router, b_router, w_gate, w_up, w_down, sg, su, sd,
               ple_proj, ple_gate, ple_bias, g1, b1, g2, b2):
    n, d = x1.shape
    n_exp = w_router.shape[1]
    idx, wts, rank, counts = _route(x1, w_router.T, b_router.reshape(n_exp, 1))
    tm = EXPERT_ROW_BLOCK
    counts = counts[:, 0].astype(jnp.int32)
    padded = (counts + tm - 1) // tm * tm
    pad_end = jnp.cumsum(padded)
    pad_start = pad_end - padded
    n_blocks = (n * EXPERT_TOPK + n_exp * (tm - 1)) // tm + 1
    n_blocks = -(-n_blocks // EXPERT_CHUNKS) * EXPERT_CHUNKS
    n_rows = n_blocks * tm
    experts = jnp.arange(n_exp, dtype=jnp.int32)
    start_of = jnp.sum(jnp.where(idx[:, :, None] == experts, pad_start, 0), axis=-1)
    pos = start_of + rank
    tok = jnp.broadcast_to(jnp.arange(n, dtype=jnp.int32)[None, :], pos.shape)
    filler = jnp.arange(tm, dtype=jnp.int32)[None, :]
    filler_key = jnp.where(filler < (padded - counts)[:, None],
                           (pad_start + counts)[:, None] + filler, jnp.int32(2 ** 30))
    n_tail = max(n_rows - (n * EXPERT_TOPK + n_exp * tm), 0)
    keys = jnp.concatenate([pos.reshape(-1), filler_key.reshape(-1),
                            jnp.full((n_tail,), 2 ** 30, jnp.int32)])
    vals = jnp.concatenate([tok.reshape(-1), jnp.arange(n_exp * tm + n_tail, dtype=jnp.int32) % n])
    tok_pad = lax.sort((keys, vals), num_keys=1, is_stable=False)[1][:n_rows]
    blk_start = jnp.arange(n_blocks, dtype=jnp.int32) * tm
    blk_expert = jnp.minimum(jnp.sum((pad_end[None, :] <= blk_start[:, None]).astype(jnp.int32), axis=1),
                             n_exp - 1)
    blk_first = jnp.concatenate([jnp.ones((1,), jnp.int32),
                                 (blk_expert[1:] != blk_expert[:-1]).astype(jnp.int32)])
    n_used = (pad_end[-1:] // tm).astype(jnp.int32)
    per = n_blocks // EXPERT_CHUNKS
    y = None
    for c in range(EXPERT_CHUNKS):
        blocks = slice(c * per, (c + 1) * per)
        xg = x1b.at[tok_pad[c * per * tm:(c + 1) * per * tm]].get(mode="promise_in_bounds")
        y = _experts(xg, blk_expert[blocks], blk_first[blocks], jnp.clip(n_used - c * per, 0, per),
                     w_gate, w_up, w_down, layer, y, c * per, n_rows)
    routed = y.at[pos].get(mode="promise_in_bounds")
    return _moe_tail(x1, routed, wts.T, p_all, layer, sg.astype(BF16), su.astype(BF16), sd.astype(BF16),
                     ple_gate.astype(BF16), ple_bias.reshape(1, d), ple_proj.astype(BF16),
                     g1.reshape(1, d), b1.reshape(1, d), g2.reshape(1, d), b2.reshape(1, d))


def _gelu_tanh(x):
    return 0.5 * x * (1.0 + jnp.tanh(0.7978845608028654 * (x + 0.044715 * (x * x * x))))


def _compress_kernel(c_ref, pe_ref, w1_ref, b1_ref, w2_ref, o_ref, *, nc, half):
    out = jnp.zeros((nc, GROUP_LANES), F32)
    for g in range(NSA_KV_GROUPS):
        c = c_ref[0, 0, g].astype(F32)
        a = _dot((c + pe_ref[0, 0:1, :]).astype(BF16), w1_ref[0, :half, :])
        bm = _dot((c + pe_ref[0, 1:2, :]).astype(BF16), w1_ref[0, half:, :])
        h = _gelu_tanh(a + pltpu.roll(bm, nc - 1, 0) + b1_ref[0])
        out = out + _dot(h.astype(BF16), w2_ref[0, g])
    o_ref[0, 0] = out.astype(o_ref.dtype)


def _compress(chunks, pe_flat, w1, b1, w2_placed):
    b, _, _, nc, half = chunks.shape
    hid = w1.shape[2]
    return pl.pallas_call(
        functools.partial(_compress_kernel, nc=nc, half=half),
        grid=(b, 2),
        in_specs=[
            pl.BlockSpec((1, 1, NSA_KV_GROUPS, nc, half), lambda i, j: (i, j, 0, 0, 0)),
            pl.BlockSpec((1, 2, half), lambda i, j: (j, 0, 0)),
            pl.BlockSpec((1, 2 * half, hid), lambda i, j: (j, 0, 0)),
            pl.BlockSpec((1, 1, hid), lambda i, j: (j, 0, 0)),
            pl.BlockSpec((1, NSA_KV_GROUPS, hid, GROUP_LANES), lambda i, j: (j, 0, 0, 0)),
        ],
        out_specs=pl.BlockSpec((1, 1, nc, GROUP_LANES), lambda i, j: (i, j, 0, 0)),
        out_shape=jax.ShapeDtypeStruct((b, 2, nc, GROUP_LANES), BF16),
        compiler_params=_cparams(("arbitrary", "arbitrary")),
        name="compress",
    )(chunks, pe_flat, w1, b1, w2_placed)


def _group_mask(g):
    lane = lax.broadcasted_iota(jnp.int32, (1, GROUP_LANES), 1)
    return (lane >= g * HEAD_DIM) & (lane < (g + 1) * HEAD_DIM)


def _topk_rank(score):
    n, t = score.shape
    sub8 = lax.broadcasted_iota(jnp.int32, (8, t), 0)
    ahead = jnp.zeros((n, t), F32)
    for j in range(n):
        vj = score[j:j + 1, :]
        lo_r, hi_r = (j // 8) * 8, (j // 8) * 8 + 8
        parts = []
        if lo_r > 0:
            parts.append(jnp.where(vj > score[:lo_r], 1.0, 0.0))
        own = score[lo_r:hi_r]
        parts.append(jnp.where(sub8 > (j % 8), jnp.where(vj >= own, 1.0, 0.0),
                               jnp.where(vj > own, 1.0, 0.0)))
        if hi_r < n:
            parts.append(jnp.where(vj >= score[hi_r:], 1.0, 0.0))
        ahead = ahead + jnp.concatenate(parts, axis=0)
    return ahead


def _nsa_cmp_kernel(q_ref, kc_ref, vc_ref, gate_ref, slope_ref, wt_ref, o_ref, sel_ref,
                    *, tq, nc, n_slc):
    qi = pl.program_id(1)
    q0 = qi * tq
    kc = kc_ref[0, 0]
    vc = vc_ref[0, 0]
    t_pos = q0 + lax.broadcasted_iota(jnp.int32, (tq, nc), 0)
    cmp_end = lax.broadcasted_iota(jnp.int32, (tq, nc), 1) * CMP_STRIDE + (CMP_BLOCK - 1)
    visible = cmp_end <= t_pos
    end_rel = (lax.broadcasted_iota(jnp.int32, (1, nc), 1) * CMP_STRIDE + (CMP_BLOCK - 1) - q0).astype(F32)
    v_aug = [jnp.where(_group_mask(g), vc, jnp.ones_like(vc)) for g in range(NSA_KV_GROUPS)]
    psum = [jnp.zeros((tq, nc), F32) for _ in range(NSA_KV_GROUPS)]
    for r in range(NSA_HEADS_PER_GROUP):
        q = q_ref[0, :, r * GROUP_LANES:(r + 1) * GROUP_LANES]
        gates = gate_ref[r, 0]
        out = jnp.zeros((tq, GROUP_LANES), F32)
        for g in range(NSA_KV_GROUPS):
            in_grp = _group_mask(g)
            qm = jnp.where(in_grp, q, jnp.zeros_like(q))
            s = _nt(qm, kc) + slope_ref[r, g, :, :nc] * end_rel
            s = jnp.where(visible, s, NEG)
            m = jnp.max(s, axis=1, keepdims=True)
            p = jnp.where(visible, jnp.exp2(s - m), 0.0)
            o_g = _dot(p.astype(BF16), v_aug[g])
            other = (1 - g // 2) * LANES
            l = o_g[:, other:other + LANES]
            inv = jnp.where(l > 0.0, 1.0 / l, 0.0)
            inv2 = jnp.concatenate([inv, inv], axis=1)
            psum[g] = psum[g] + (p * inv2[:, :nc] if nc == GROUP_LANES else p * inv[:, :1])
            out = out + jnp.where(in_grp, o_g * (inv2 * gates[:, g:g + 1]), 0.0)
        o_ref[0, :, r * GROUP_LANES:(r + 1) * GROUP_LANES] = out.astype(o_ref.dtype)

    blk = lax.broadcasted_iota(jnp.int32, (n_slc, tq), 0)
    t_row = q0 + lax.broadcasted_iota(jnp.int32, (n_slc, tq), 1)
    cur = t_row // SLC_BLOCK
    forced = (blk == 0) | (blk == cur) | (blk == cur - 1)
    vis = blk * SLC_BLOCK <= t_row
    wt = wt_ref[...]
    pad_rows = jnp.zeros((LANES - n_slc, tq), F32)
    for g in range(NSA_KV_GROUPS):
        hi, mid, lo = _split3(psum[g])
        imp = _nt(wt, hi) + _nt(wt, mid) + _nt(wt, lo)
        score = jnp.where(forced, 1e9, jnp.where(vis, imp, NEG))
        keep = (_topk_rank(score) < float(min(SLC_TOPK, n_slc))) & (score > 0.5 * NEG)
        flags = jnp.concatenate([jnp.where(keep, 0.0, NEG), pad_rows], axis=0)
        sel_ref[0, g] = flags.T.astype(sel_ref.dtype)


def _nsa_compressed(q, cmp_kv, gates_c, slopes_rep, w_sel_t):
    b, s, _ = q.shape
    nc = cmp_kv.shape[2]
    n_slc = s // SLC_BLOCK
    tq = min(NSA_Q_TILE, s)
    rr = NSA_HEADS_PER_GROUP
    return pl.pallas_call(
        functools.partial(_nsa_cmp_kernel, tq=tq, nc=nc, n_slc=n_slc),
        grid=(b, s // tq),
        in_specs=[
            pl.BlockSpec((1, tq, rr * GROUP_LANES), lambda i, j: (i, j, 0)),
            pl.BlockSpec((1, 1, nc, GROUP_LANES), lambda i, j: (i, 0, 0, 0)),
            pl.BlockSpec((1, 1, nc, GROUP_LANES), lambda i, j: (i, 1, 0, 0)),
            pl.BlockSpec((rr, 1, tq, LANES), lambda i, j: (0, i, j, 0)),
            pl.BlockSpec((rr, NSA_KV_GROUPS, 1, slopes_rep.shape[3]), lambda i, j: (0, 0, 0, 0)),
            pl.BlockSpec((n_slc, nc), lambda i, j: (0, 0)),
        ],
        out_specs=[
            pl.BlockSpec((1, tq, rr * GROUP_LANES), lambda i, j: (i, j, 0)),
            pl.BlockSpec((1, NSA_KV_GROUPS, tq, LANES), lambda i, j: (i, 0, j, 0)),
        ],
        out_shape=[
            jax.ShapeDtypeStruct((b, s, rr * GROUP_LANES), BF16),
            jax.ShapeDtypeStruct((b, NSA_KV_GROUPS, s, LANES), BF16),
        ],
        compiler_params=_cparams(("arbitrary", "arbitrary")),
        name="nsa_compressed",
    )(q, cmp_kv, cmp_kv, gates_c, slopes_rep, w_sel_t)


AUG_POS_LANE = 64
MODE_FULL, MODE_LOWER, MODE_UPPER = 0, 1, 2


def _nsa_flash_kernel(qt_ref, kt_ref, fl_ref, *refs, t, sub, use_sel):
    if use_sel:
        q_ref, k_ref, v_ref, ka_ref, gate_ref, slope_ref, sel_ref, o_ref, m_ref, acc_ref = refs
    else:
        q_ref, k_ref, v_ref, ka_ref, gate_ref, slope_ref, o_ref, m_ref, acc_ref = refs
    step = pl.program_id(2)
    flags = fl_ref[step]
    mode = flags >> 2

    @pl.when((flags & 1) != 0)
    def _():
        m_ref[...] = jnp.full_like(m_ref, NEG)
        acc_ref[...] = jnp.zeros_like(acc_ref)

    lane = lax.broadcasted_iota(jnp.int32, (1, LANES), 1)

    def body(mode_static):
        q = q_ref[0]
        k = k_ref[0]
        v = v_ref[0]
        ka = ka_ref[...]
        k_aug = (jnp.concatenate([k[:, :LANES], ka], axis=1), jnp.concatenate([ka, k[:, LANES:]], axis=1))
        items = []
        for g in range(NSA_KV_GROUPS):
            half = g // 2
            in_grp = (lane >= (g % 2) * HEAD_DIM) & (lane < (g % 2 + 1) * HEAD_DIM)
            q_own = jnp.where(in_grp, q[:, half * LANES:(half + 1) * LANES], jnp.zeros((t, LANES), BF16))
            q_extra = jnp.broadcast_to(slope_ref[0, g], (t, LANES))
            if use_sel:
                q_extra = jnp.where(lane < AUG_POS_LANE, sel_ref[0, g], q_extra)
            qg = jnp.concatenate([q_own, q_extra] if half == 0 else [q_extra, q_own], axis=1)
            vg = jnp.where(_group_mask(g), v, jnp.ones_like(v))
            for qs in range(t // sub):
                rows = slice(qs * sub, (qs + 1) * sub)
                if mode_static == MODE_FULL:
                    keys = slice(0, t)
                elif mode_static == MODE_LOWER:
                    keys = slice(0, (qs + 1) * sub)
                else:
                    keys = slice(qs * sub, t)
                s = _nt(qg[rows], k_aug[half][keys])
                if mode_static != MODE_FULL:
                    r_i = lax.broadcasted_iota(jnp.int32, (sub, sub), 0)
                    c_i = lax.broadcasted_iota(jnp.int32, (sub, sub), 1)
                    if mode_static == MODE_LOWER:
                        edge = jnp.where(c_i <= r_i, s[:, -sub:], NEG)
                        s = edge if s.shape[1] == sub else jnp.concatenate([s[:, :-sub], edge], axis=1)
                    else:
                        edge = jnp.where(c_i > r_i, s[:, :sub], NEG)
                        s = edge if s.shape[1] == sub else jnp.concatenate([edge, s[:, sub:]], axis=1)
                items.append(((g, rows), s, vg[keys], m_ref[g, rows, :], acc_ref[g, rows, :]))
        for (g, rows), m_new, acc_new in _flash_items(items):
            acc_ref[g, rows, :] = acc_new
            m_ref[g, rows, :] = m_new

    for mode_static in ((MODE_FULL, MODE_LOWER) if use_sel else (MODE_LOWER, MODE_UPPER)):
        @pl.when(mode == mode_static)
        def _(mode_static=mode_static):
            body(mode_static)

    @pl.when((flags & 2) != 0)
    def _():
        gates = gate_ref[0, 0]
        out = jnp.zeros((t, GROUP_LANES), F32)
        for g in range(NSA_KV_GROUPS):
            acc = acc_ref[g]
            denom = pltpu.roll(acc, LANES, 1)
            col = (1 if use_sel else 2) * NSA_KV_GROUPS + g
            out = out + jnp.where(_group_mask(g), acc * (gates[:, col:col + 1] / denom), 0.0)
        o_ref[0] = out.astype(o_ref.dtype)


def _nsa_flash_tables(s, t, windowed):
    qt, kt, fl = [], [], []
    for qi in range(s // t):
        if windowed:
            tiles = [(qi - 1, MODE_UPPER)] if qi > 0 else []
            tiles.append((qi, MODE_LOWER))
        else:
            tiles = [(ki, MODE_FULL) for ki in range(qi)] + [(qi, MODE_LOWER)]
        for n, (ki, mode) in enumerate(tiles):
            qt.append(qi)
            kt.append(ki)
            fl.append((1 if n == 0 else 0) | (2 if n == len(tiles) - 1 else 0) | (mode << 2))
    return tuple(jnp.asarray(np.asarray(a, np.int32)) for a in (qt, kt, fl))


def _nsa_flash(q, kv, k_col, v_col, key_aug, gates, slope_rows, sel_q=None):
    b, s, _ = q.shape
    use_sel = sel_q is not None
    t = min(NSA_SEL_TILE if use_sel else NSA_WIN_TILE, s)
    assert use_sel or t == WINDOW, "the window branch assumes tile == WINDOW"
    qt, kt, fl = _nsa_flash_tables(s, t, not use_sel)
    rr = NSA_HEADS_PER_GROUP
    in_specs = [
        pl.BlockSpec((1, t, GROUP_LANES), lambda i, r, u, qt, kt, fl: (i, qt[u], r)),
        pl.BlockSpec((1, t, GROUP_LANES), lambda i, r, u, qt, kt, fl: (i, kt[u], k_col)),
        pl.BlockSpec((1, t, GROUP_LANES), lambda i, r, u, qt, kt, fl: (i, kt[u], v_col)),
        pl.BlockSpec((t, LANES), lambda i, r, u, qt, kt, fl: (kt[u], 0)),
        pl.BlockSpec((1, 1, t, LANES), lambda i, r, u, qt, kt, fl: (r, i, qt[u], 0)),
        pl.BlockSpec((1, NSA_KV_GROUPS, 1, LANES), lambda i, r, u, qt, kt, fl: (r, 0, 0, 0)),
    ]
    args = [q, kv, kv, key_aug, gates, slope_rows]
    if use_sel:
        in_specs.append(pl.BlockSpec((1, NSA_KV_GROUPS, t, LANES), lambda i, r, u, qt, kt, fl: (i, 0, qt[u], 0)))
        args.append(sel_q)
    grid_spec = pltpu.PrefetchScalarGridSpec(
        num_scalar_prefetch=3,
        grid=(b, rr, int(qt.shape[0])),
        in_specs=in_specs,
        out_specs=pl.BlockSpec((1, t, GROUP_LANES), lambda i, r, u, qt, kt, fl: (i, qt[u], r)),
        scratch_shapes=[
            pltpu.VMEM((NSA_KV_GROUPS, t, LANES), F32),
            pltpu.VMEM((NSA_KV_GROUPS, t, GROUP_LANES), F32),
        ],
    )
    return pl.pallas_call(
        functools.partial(_nsa_flash_kernel, t=t, sub=min(NSA_SUB, t), use_sel=use_sel),
        grid_spec=grid_spec,
        out_shape=jax.ShapeDtypeStruct((b, s, rr * GROUP_LANES), BF16),
        compiler_params=_cparams(("arbitrary", "arbitrary", "arbitrary")),
        name="nsa_selected" if use_sel else "nsa_window",
    )(qt, kt, fl, *args)


def _nsa_window_kernel(q_ref, kp_ref, kc_ref, vp_ref, vc_ref, kap_ref, kac_ref, gate_ref, slope_ref, o_ref,
                       *, t, sub):
    has_prev = pl.program_id(2) > 0
    lane = lax.broadcasted_iota(jnp.int32, (1, LANES), 1)
    q = q_ref[0]
    k = jnp.concatenate([kp_ref[0], kc_ref[0]], axis=0)
    v = jnp.concatenate([vp_ref[0], vc_ref[0]], axis=0)
    ka = jnp.concatenate([kap_ref[...], kac_ref[...]], axis=0)
    k_aug = (jnp.concatenate([k[:, :LANES], ka], axis=1), jnp.concatenate([ka, k[:, LANES:]], axis=1))
    gates = gate_ref[0, 0]
    span = WINDOW + sub
    r_i = lax.broadcasted_iota(jnp.int32, (sub, sub), 0)
    c_i = lax.broadcasted_iota(jnp.int32, (sub, sub), 1)
    out_rows = [jnp.zeros((sub, GROUP_LANES), F32) for _ in range(t // sub)]
    for g in range(NSA_KV_GROUPS):
        half = g // 2
        in_grp = (lane >= (g % 2) * HEAD_DIM) & (lane < (g % 2 + 1) * HEAD_DIM)
        q_own = jnp.where(in_grp, q[:, half * LANES:(half + 1) * LANES], jnp.zeros((t, LANES), BF16))
        q_extra = jnp.broadcast_to(slope_ref[0, g], (t, LANES))
        qg = jnp.concatenate([q_own, q_extra] if half == 0 else [q_extra, q_own], axis=1)
        vg = jnp.where(_group_mask(g), v, jnp.ones_like(v))
        col = 2 * NSA_KV_GROUPS + g
        for qs in range(t // sub):
            rows = slice(qs * sub, (qs + 1) * sub)
            keys = slice(qs * sub, qs * sub + span)
            s = _nt(qg[rows], k_aug[half][keys])
            chunks = [s[:, j * sub:(j + 1) * sub] for j in range(span // sub)]
            chunks[0] = jnp.where(c_i > r_i, chunks[0], NEG)
            chunks[-1] = jnp.where(c_i <= r_i, chunks[-1], NEG)
            for j in range(len(chunks) - 1):
                if qs * sub + j * sub < WINDOW:
                    chunks[j] = jnp.where(has_prev, chunks[j], NEG)
            m = chunks[0]
            for c in chunks[1:]:
                m = jnp.maximum(m, c)
            m = jnp.max(m, axis=1, keepdims=True)
            p = jnp.concatenate([jnp.exp2(c - m) for c in chunks], axis=1)
            acc = _dot(p.astype(BF16), vg[keys])
            denom = pltpu.roll(acc, LANES, 1)
            out_rows[qs] = out_rows[qs] + jnp.where(
                _group_mask(g), acc * (gates[rows, col:col + 1] / denom), 0.0)
    o_ref[0] = jnp.concatenate(out_rows, axis=0).astype(o_ref.dtype)


def _nsa_window(q, kv, k_col, v_col, key_aug, gates, slope_rows):
    b, s, _ = q.shape
    t = min(NSA_WIN_TILE, s)
    sub = min(NSA_SUB, t)
    assert WINDOW % sub == 0 and t % WINDOW == 0
    rr = NSA_HEADS_PER_GROUP
    per = t // WINDOW
    prev = lambda j: jnp.maximum(j * per - 1, 0)
    return pl.pallas_call(
        functools.partial(_nsa_window_kernel, t=t, sub=sub),
        grid=(b, rr, s // t),
        in_specs=[
            pl.BlockSpec((1, t, GROUP_LANES), lambda i, r, j: (i, j, r)),
            pl.BlockSpec((1, WINDOW, GROUP_LANES), lambda i, r, j: (i, prev(j), k_col)),
            pl.BlockSpec((1, t, GROUP_LANES), lambda i, r, j: (i, j, k_col)),
            pl.BlockSpec((1, WINDOW, GROUP_LANES), lambda i, r, j: (i, prev(j), v_col)),
            pl.BlockSpec((1, t, GROUP_LANES), lambda i, r, j: (i, j, v_col)),
            pl.BlockSpec((WINDOW, LANES), lambda i, r, j: (prev(j), 0)),
            pl.BlockSpec((t, LANES), lambda i, r, j: (j, 0)),
            pl.BlockSpec((1, 1, t, LANES), lambda i, r, j: (r, i, j, 0)),
            pl.BlockSpec((1, NSA_KV_GROUPS, 1, LANES), lambda i, r, j: (r, 0, 0, 0)),
        ],
        out_specs=pl.BlockSpec((1, t, GROUP_LANES), lambda i, r, j: (i, j, r)),
        out_shape=jax.ShapeDtypeStruct((b, s, rr * GROUP_LANES), BF16),
        compiler_params=_cparams(("arbitrary", "arbitrary", "arbitrary")),
        name="nsa_window",
    )(q, kv, kv, kv, kv, key_aug, key_aug, gates, slope_rows)


def _nsa_key_aug(s):
    pos = np.arange(s)
    aug = np.zeros((s, LANES), np.float32)
    aug[pos, pos // SLC_BLOCK] = 1.0
    aug[:, AUG_POS_LANE:AUG_POS_LANE + 3] = (pos // 64)[:, None]
    aug[:, AUG_POS_LANE + 3:AUG_POS_LANE + 6] = (pos % 64)[:, None]
    return jnp.asarray(aug, BF16)


def _nsa_slope_rows(slopes_l2):
    rr, gg = slopes_l2.shape
    hi, mid, lo = _split3(jnp.asarray(slopes_l2, F32))
    pieces = jnp.stack([hi, mid, lo], axis=-1).astype(F32)
    rows = jnp.zeros((rr, gg, 1, LANES), F32)
    rows = rows.at[:, :, 0, AUG_POS_LANE:AUG_POS_LANE + 3].set(pieces * 64.0)
    rows = rows.at[:, :, 0, AUG_POS_LANE + 3:AUG_POS_LANE + 6].set(pieces)
    return rows.astype(BF16)


def _pad_lanes(a, width=LANES):
    return jnp.pad(a, ((0, 0), (0, width - a.shape[1])))


def _fox_layer(x2d, b, s, w_in, b_f, w_out, ln_g, ln_b):
    d = x2d.shape[1]
    attn = w_out.shape[0]
    n_heads = attn // HEAD_DIM
    scale = HEAD_DIM ** -0.5 * LOG2E
    col_scale = jnp.concatenate([jnp.full((attn,), scale, F32), jnp.ones((2 * attn,), F32)])[None, :]
    qkv, z = _project(x2d, w_in[:, :3 * attn].astype(BF16), col_scale,
                      _pad_lanes(w_in[:, 3 * attn:])[None], _pad_lanes(b_f[None, :])[None], False, True)
    gate_bias = _gate_cumsum(z.reshape(b, s, LANES), n_heads)
    o = _fox_attention(qkv.reshape(b, s, 3 * attn), gate_bias, n_heads)
    return _out_ln([o.reshape(b * s, attn)], w_out.astype(BF16), x2d,
                   ln_g.reshape(1, d), ln_b.reshape(1, d))


def _selection_weights_t(nc, n_slc):
    cs = np.arange(nc)[:, None] * CMP_STRIDE
    ce = cs + CMP_BLOCK
    ss = np.arange(n_slc)[None, :] * SLC_BLOCK
    se = ss + SLC_BLOCK
    w = np.clip(np.minimum(ce, se) - np.maximum(cs, ss), 0, None) / CMP_STRIDE
    w[nc - 1, :] = 0.0
    return jnp.asarray(w.T, BF16)


def _nsa_layer(x2d, b, s, w_kv, cmp_pe, cmp_w1, cmp_b1, cmp_w2, w_q, b_g, w_out, ln_g, ln_b):
    d = x2d.shape[1]
    gg, rr = NSA_KV_GROUPS, NSA_HEADS_PER_GROUP
    attn = gg * rr * HEAD_DIM
    n_heads = gg * rr
    wq = w_q[:, :attn].reshape(d, gg, rr, HEAD_DIM).transpose(0, 2, 1, 3).reshape(d, attn)
    wgate = w_q[:, attn:].reshape(d, gg, rr, 3).transpose(2, 0, 3, 1).reshape(rr, d, 3 * gg)
    bgate = b_g.reshape(gg, rr, 3).transpose(1, 2, 0).reshape(rr, 1, 3 * gg)
    wgate = jnp.pad(wgate, ((0, 0), (0, 0), (0, LANES - 3 * gg)))
    bgate = jnp.pad(bgate, ((0, 0), (0, 0), (0, LANES - 3 * gg)))
    wo = w_out.reshape(gg, rr, HEAD_DIM, d).transpose(1, 0, 2, 3).reshape(attn, d)
    w_all = jnp.concatenate([wq, w_kv], axis=1).astype(BF16)
    col_scale = jnp.concatenate([jnp.full((attn,), HEAD_DIM ** -0.5 * LOG2E, F32),
                                 jnp.ones((w_kv.shape[1],), F32)])[None, :]
    qkv, gates = _project(x2d, w_all, col_scale, wgate, bgate, True, False)
    width = qkv.shape[1]
    qkv = qkv.reshape(b, s, width)
    gates = gates.reshape(rr, b, s, LANES)

    nc = s // CMP_STRIDE
    half = CMP_STRIDE * HEAD_DIM
    raw = qkv[:, :, attn:attn + 2 * GROUP_LANES]
    chunks = raw.reshape(b, nc, CMP_STRIDE, 2, gg, HEAD_DIM).transpose(0, 3, 4, 1, 2, 5)
    chunks = chunks.reshape(b, 2, gg, nc, half)
    pe_flat = cmp_pe.reshape(2, 2, half)
    hid = cmp_w1.shape[2]
    w2_placed = jnp.zeros((2, gg, hid, GROUP_LANES), F32)
    for g in range(gg):
        w2_placed = w2_placed.at[:, g, :, g * HEAD_DIM:(g + 1) * HEAD_DIM].set(cmp_w2)
    cmp_kv = _compress(chunks, pe_flat, cmp_w1.astype(BF16), cmp_b1.reshape(2, 1, hid),
                       w2_placed.astype(BF16))

    slopes_l2 = (2.0 ** (-8.0 * np.arange(1, n_heads + 1) / n_heads) * LOG2E).reshape(gg, rr).T
    slopes_rep = jnp.asarray(np.broadcast_to(
        slopes_l2[:, :, None, None], (rr, gg, 1, nc)).astype(np.float32))
    n_slc = s // SLC_BLOCK
    assert n_slc <= AUG_POS_LANE, "selection flags must fit below the position lanes"
    o_c, sel_q = _nsa_compressed(qkv, cmp_kv, gates, slopes_rep, _selection_weights_t(nc, n_slc))
    key_aug = _nsa_key_aug(s)
    slope_rows = _nsa_slope_rows(slopes_l2)
    base = attn // GROUP_LANES
    o_s = _nsa_flash(qkv, qkv, base + 2, base + 3, key_aug, gates, slope_rows, sel_q)
    o_w = _nsa_window(qkv, qkv, base + 4, base + 5, key_aug, gates, slope_rows)
    n = b * s
    return _out_ln([o_c.reshape(n, attn), o_s.reshape(n, attn), o_w.reshape(n, attn)],
                   wo.astype(BF16), x2d, ln_g.reshape(1, d), ln_b.reshape(1, d))


def kernel(x, p, fox_w_in, fox_b_f, fox_w_out, nsa_w_kv, cmp_pe, cmp_w1, cmp_b1, cmp_w2, nsa_w_q, nsa_b_g, nsa_w_out, ln_g, ln_b, moe_w_router, moe_b_router, moe_w_gate, moe_w_up, moe_w_down, shared_w_gate, shared_w_up, shared_w_down, ple_w_proj, ple_w_gate, ple_b_gate):
    b, s, d = x.shape
    n = b * s
    depth = p.shape[0]
    n_a = depth // 2
    h = x.reshape(n, d)
    for i in range(depth):
        if i < n_a:
            h1, h1b = _fox_layer(h, b, s, fox_w_in[i], fox_b_f[i], fox_w_out[i], ln_g[i, 0], ln_b[i, 0])
        else:
            j = i - n_a
            h1, h1b = _nsa_layer(h, b, s, nsa_w_kv, cmp_pe, cmp_w1, cmp_b1, cmp_w2,
                                 nsa_w_q[j], nsa_b_g[j], nsa_w_out[j], ln_g[i, 0], ln_b[i, 0])
        h = _moe_layer(h1, h1b, p.reshape(depth, n, -1), i, moe_w_router[i], moe_b_router[i],
                       moe_w_gate, moe_w_up, moe_w_down,
                       shared_w_gate[i], shared_w_up[i], shared_w_down[i],
                       ple_w_proj[i], ple_w_gate[i], ple_b_gate[i],
                       ln_g[i, 1], ln_b[i, 1], ln_g[i, 2], ln_b[i, 2])
    return h.reshape(b, s, d)
```

```python
import functools

import numpy as np
import jax
import jax.numpy as jnp
from jax import lax
from jax.experimental import pallas as pl
from jax.experimental.pallas import tpu as pltpu

F32 = jnp.float32
BF16 = jnp.bfloat16
NEG = -1e30

HEAD_DIM = 64
LANES = 128
NSA_KV_GROUPS = 4
NSA_HEADS_PER_GROUP = 4
GROUP_LANES = NSA_KV_GROUPS * HEAD_DIM
CMP_BLOCK = 32
CMP_STRIDE = 16
SLC_BLOCK = 64
SLC_TOPK = 16
WINDOW = 512
EXPERT_TOPK = 8
EXPERT_GROUPS = 8
EXPERT_TOPK_GROUPS = 4
ROUTED_SCALE = 2.5
LN_EPS = 1e-5
DEPTH = 2
DEEPNORM_ALPHA = (2.0 * DEPTH) ** 0.25
VMEM_LIMIT = 52 * 1024 * 1024

ROW_TILE = 512
FOX_TILE = 2048
FOX_SUB = 128
NSA_Q_TILE = 256
NSA_SEL_TILE = 1024
NSA_WIN_TILE = 1024
NSA_SUB = 256
ROUTER_TILE = 512
EXPERT_ROW_BLOCK = 512
EXPERT_CHUNKS = 8


def _cparams(sem):
    return pltpu.CompilerParams(dimension_semantics=sem, vmem_limit_bytes=VMEM_LIMIT)


def _nt(a, b):
    return lax.dot_general(a, b, (((1,), (1,)), ((), ())), preferred_element_type=F32)


def _dot(a, b):
    return jnp.dot(a, b, preferred_element_type=F32)


def _split2(a):
    hi = a.astype(BF16)
    lo = (a - hi.astype(F32)).astype(BF16)
    return hi, lo


def _split3(a):
    hi = a.astype(BF16)
    r = a - hi.astype(F32)
    mid = r.astype(BF16)
    lo = (r - mid.astype(F32)).astype(BF16)
    return hi, mid, lo


def _sigmoid(x):
    return 1.0 / (1.0 + jnp.exp(-x))


def _layer_norm(z, g, b):
    mu = jnp.mean(z, axis=-1, keepdims=True)
    zc = z - mu
    var = jnp.mean(zc * zc, axis=-1, keepdims=True)
    return zc * lax.rsqrt(var + LN_EPS) * g + b


def _proj_kernel(x_ref, w_ref, cs_ref, wg_ref, bg_ref, o_ref, g_ref, *, tn, gate_sigmoid, gate_split):
    x = x_ref[...]
    xh = x.astype(BF16)
    m_out = w_ref.shape[1]
    for j in range(m_out // tn):
        sl = slice(j * tn, (j + 1) * tn)
        y = _dot(xh, w_ref[:, sl])
        o_ref[:, sl] = (y * cs_ref[:, sl]).astype(o_ref.dtype)
    xl = (x - xh.astype(F32)).astype(BF16)
    for i in range(wg_ref.shape[0]):
        wh, wl = _split2(wg_ref[i])
        g = _dot(xh, wh) + bg_ref[i]
        if gate_split:
            g = g + _dot(xh, wl) + _dot(xl, wh)
        if gate_sigmoid:
            g = _sigmoid(g)
        g_ref[i] = g


def _project(x2d, w_bf16, col_scale, w_gate, b_gate, gate_sigmoid, gate_split):
    n, k = x2d.shape
    m_out = w_bf16.shape[1]
    n_g = w_gate.shape[0]
    tm = min(ROW_TILE, n)
    return pl.pallas_call(
        functools.partial(_proj_kernel, tn=512, gate_sigmoid=gate_sigmoid, gate_split=gate_split),
        grid=(n // tm,),
        in_specs=[
            pl.BlockSpec((tm, k), lambda i: (i, 0)),
            pl.BlockSpec((k, m_out), lambda i: (0, 0)),
            pl.BlockSpec((1, m_out), lambda i: (0, 0)),
            pl.BlockSpec((n_g, k, LANES), lambda i: (0, 0, 0)),
            pl.BlockSpec((n_g, 1, LANES), lambda i: (0, 0, 0)),
        ],
        out_specs=[
            pl.BlockSpec((tm, m_out), lambda i: (i, 0)),
            pl.BlockSpec((n_g, tm, LANES), lambda i: (0, i, 0)),
        ],
        out_shape=[
            jax.ShapeDtypeStruct((n, m_out), BF16),
            jax.ShapeDtypeStruct((n_g, n, LANES), F32),
        ],
        compiler_params=_cparams(("arbitrary",)),
        name="project",
    )(x2d, w_bf16, col_scale, w_gate, b_gate)


LOG2E = 1.4426950408889634
GATE_PIECE_STRIDE = 16


def _gate_cumsum_kernel(z_ref, o_ref, carry_ref, *, ts, n_heads):
    @pl.when(pl.program_id(1) == 0)
    def _():
        carry_ref[...] = jnp.zeros_like(carry_ref)

    z = z_ref[0]
    log_f = jnp.minimum(z, 0.0) - jnp.log(1.0 + jnp.exp(-jnp.abs(z)))
    row = lax.broadcasted_iota(jnp.int32, (ts, ts), 0)
    col = lax.broadcasted_iota(jnp.int32, (ts, ts), 1)
    tri = jnp.where(col <= row, 1.0, 0.0).astype(BF16)
    hi, mid, lo = _split3(log_f)
    cs = _dot(tri, hi) + _dot(tri, mid) + _dot(tri, lo) + carry_ref[...]
    carry_ref[...] = cs[ts - 1:ts, :]
    lane = lax.broadcasted_iota(jnp.int32, (1, LANES), 1)
    bias = jnp.where(lane < n_heads, cs * (-LOG2E), 0.0)
    hi, mid, lo = _split3(bias)
    pieces = (hi.astype(F32) + pltpu.roll(mid.astype(F32), GATE_PIECE_STRIDE, 1)
              + pltpu.roll(lo.astype(F32), 2 * GATE_PIECE_STRIDE, 1))
    o_ref[0] = (pieces + pltpu.roll(pieces, HEAD_DIM, 1)).astype(o_ref.dtype)


def _gate_cumsum(z, n_heads):
    b, s, _ = z.shape
    ts = min(256, s)
    return pl.pallas_call(
        functools.partial(_gate_cumsum_kernel, ts=ts, n_heads=n_heads),
        grid=(b, s // ts),
        in_specs=[pl.BlockSpec((1, ts, LANES), lambda i, j: (i, j, 0))],
        out_specs=pl.BlockSpec((1, ts, LANES), lambda i, j: (i, j, 0)),
        out_shape=jax.ShapeDtypeStruct((b, s, LANES), BF16),
        scratch_shapes=[pltpu.VMEM((1, LANES), F32)],
        compiler_params=_cparams(("arbitrary", "arbitrary")),
        name="gate_cumsum",
    )(z)


def _flash_items(items):
    outs = []
    for key, s, v, m_prev, acc_prev in items:
        m_new = jnp.maximum(m_prev, jnp.max(s, axis=1, keepdims=True))
        alpha = jnp.exp2(m_prev - m_new)
        p = jnp.concatenate(
            [jnp.exp2(s[:, j * LANES:(j + 1) * LANES] - m_new[:, :LANES])
             for j in range(s.shape[1] // LANES)], axis=1)
        reps = acc_prev.shape[1] // LANES
        alpha = alpha if reps == 1 else jnp.concatenate([alpha] * reps, axis=1)
        outs.append((key, m_new, alpha * acc_prev + _dot(p.astype(BF16), v)))
    return outs


def _fox_kernel(qt_ref, kt_ref, q_ref, k_ref, v_ref, c_ref, o_ref, m_ref, acc_ref, *, t, sub):
    step = pl.program_id(2)
    pair = pl.program_id(1)
    qi = qt_ref[step]
    ki = kt_ref[step]

    @pl.when(ki == 0)
    def _():
        m_ref[...] = jnp.full_like(m_ref, NEG)
        acc_ref[...] = jnp.zeros_like(acc_ref)

    lane = lax.broadcasted_iota(jnp.int32, (1, LANES), 1)
    low = lane < HEAD_DIM

    def body(diag):
        q = q_ref[0]
        k = k_ref[0]
        v = v_ref[0]
        c = c_ref[0]
        items = []
        for h in range(2):
            own = low if h == 0 else jnp.logical_not(low)
            base = (1 - h) * HEAD_DIM + 2 * pair + h
            ones_at = ((lane == base) | (lane == base + GATE_PIECE_STRIDE)
                       | (lane == base + 2 * GATE_PIECE_STRIDE))
            qh = jnp.where(own, q, jnp.where(ones_at, 1.0, 0.0).astype(BF16))
            kh = jnp.where(own, k, c)
            vh = jnp.where(own, v, jnp.ones_like(v))
            for qs in range(t // sub):
                rows = slice(qs * sub, (qs + 1) * sub)
                nk = (qs + 1) * sub if diag else t
                s = _nt(qh[rows], kh[:nk])
                if diag:
                    r_i = lax.broadcasted_iota(jnp.int32, (sub, sub), 0)
                    c_i = lax.broadcasted_iota(jnp.int32, (sub, sub), 1)
                    last = jnp.where(c_i <= r_i, s[:, nk - sub:], NEG)
                    s = last if nk == sub else jnp.concatenate([s[:, :nk - sub], last], axis=1)
                items.append(((h, rows), s, vh[:nk], m_ref[h, rows, :], acc_ref[h, rows, :]))
        for (h, rows), m_new, acc_new in _flash_items(items):
            acc_ref[h, rows, :] = acc_new
            m_ref[h, rows, :] = m_new

    @pl.when(ki < qi)
    def _():
        body(False)

    @pl.when(ki == qi)
    def _():
        body(True)
        a0 = acc_ref[0]
        a1 = acc_ref[1]
        o0 = a0 / pltpu.roll(a0, HEAD_DIM, 1)
        o1 = a1 / pltpu.roll(a1, HEAD_DIM, 1)
        o_ref[0] = jnp.where(low, o0, o1).astype(o_ref.dtype)


def _fox_attention(qkv, gate_bias, n_heads):
    b, s, _ = qkv.shape
    t = min(FOX_TILE, s)
    nq = s // t
    pairs = n_heads // 2
    qt = np.concatenate([np.full((i + 1,), i, np.int32) for i in range(nq)])
    kt = np.concatenate([np.arange(i + 1, dtype=np.int32) for i in range(nq)])
    grid_spec = pltpu.PrefetchScalarGridSpec(
        num_scalar_prefetch=2,
        grid=(b, pairs, len(qt)),
        in_specs=[
            pl.BlockSpec((1, t, LANES), lambda i, j, u, qt, kt: (i, qt[u], j)),
            pl.BlockSpec((1, t, LANES), lambda i, j, u, qt, kt: (i, kt[u], pairs + j)),
            pl.BlockSpec((1, t, LANES), lambda i, j, u, qt, kt: (i, kt[u], 2 * pairs + j)),
            pl.BlockSpec((1, t, LANES), lambda i, j, u, qt, kt: (i, kt[u], 0)),
        ],
        out_specs=pl.BlockSpec((1, t, LANES), lambda i, j, u, qt, kt: (i, qt[u], j)),
        scratch_shapes=[
            pltpu.VMEM((2, t, LANES), F32),
            pltpu.VMEM((2, t, LANES), F32),
        ],
    )
    return pl.pallas_call(
        functools.partial(_fox_kernel, t=t, sub=min(FOX_SUB, t)),
        grid_spec=grid_spec,
        out_shape=jax.ShapeDtypeStruct((b, s, n_heads * HEAD_DIM), BF16),
        compiler_params=_cparams(("arbitrary", "arbitrary", "arbitrary")),
        name="fox_attention",
    )(jnp.asarray(qt), jnp.asarray(kt), qkv, qkv, qkv, gate_bias)


def _out_ln_kernel(*refs, n_in):
    o_refs = refs[:n_in]
    w_ref, x_ref, g_ref, b_ref, y_ref, yb_ref = refs[n_in:]
    if n_in == 1:
        o = o_refs[0][...]
    else:
        o = o_refs[0][...].astype(F32)
        for r in o_refs[1:]:
            o = o + r[...].astype(F32)
        o = o.astype(BF16)
    z = DEEPNORM_ALPHA * x_ref[...] + _dot(o, w_ref[...])
    y = _layer_norm(z, g_ref[...], b_ref[...])
    y_ref[...] = y
    yb_ref[...] = y.astype(BF16)


def _out_ln(o_list, w_bf16, x2d, g, b):
    n, d = x2d.shape
    k = w_bf16.shape[0]
    tm = min(ROW_TILE, n)
    n_in = len(o_list)
    row = lambda i: (i, 0)
    fixed = lambda i: (0, 0)
    return pl.pallas_call(
        functools.partial(_out_ln_kernel, n_in=n_in),
        grid=(n // tm,),
        in_specs=[pl.BlockSpec((tm, k), row)] * n_in + [
            pl.BlockSpec((k, d), fixed),
            pl.BlockSpec((tm, d), row),
            pl.BlockSpec((1, d), fixed),
            pl.BlockSpec((1, d), fixed),
        ],
        out_specs=[pl.BlockSpec((tm, d), row), pl.BlockSpec((tm, d), row)],
        out_shape=[jax.ShapeDtypeStruct((n, d), F32), jax.ShapeDtypeStruct((n, d), BF16)],
        compiler_params=_cparams(("arbitrary",)),
        name="out_ln",
    )(*o_list, w_bf16, x2d, g, b)


def _router_kernel(x_ref, wt_ref, b_ref, idx_ref, w_ref, rank_ref, cnt_ref, carry_ref, *, tn, n_exp):
    @pl.when(pl.program_id(0) == 0)
    def _():
        carry_ref[...] = jnp.zeros_like(carry_ref)

    xh, xl = _split2(x_ref[...])
    wh, wl = _split2(wt_ref[...])
    logits = _nt(wh, xh) + _nt(wh, xl) + _nt(wl, xh)
    scores = _sigmoid(logits)
    biased = scores + b_ref[...]

    per_grp = n_exp // EXPERT_GROUPS
    blocks, gscore = [], []
    for g in range(EXPERT_GROUPS):
        blk = biased[g * per_grp:(g + 1) * per_grp, :]
        m1 = jnp.max(blk, axis=0, keepdims=True)
        eq = blk == m1
        n_eq = jnp.sum(jnp.where(eq, 1.0, 0.0), axis=0, keepdims=True)
        m2 = jnp.max(jnp.where(eq, -3e38, blk), axis=0, keepdims=True)
        blocks.append(blk)
        gscore.append(m1 + jnp.where(n_eq >= 2.0, m1, m2))
    cand = []
    for g in range(EXPERT_GROUPS):
        beaten = jnp.zeros_like(gscore[g])
        for o in range(EXPERT_GROUPS):
            if o == g:
                continue
            wins = (gscore[o] >= gscore[g]) if o < g else (gscore[o] > gscore[g])
            beaten = beaten + jnp.where(wins, 1.0, 0.0)
        cand.append(jnp.where(beaten < float(EXPERT_TOPK_GROUPS), blocks[g], NEG))
    cand = jnp.concatenate(cand, axis=0)

    erow = lax.broadcasted_iota(jnp.int32, (n_exp, tn), 0).astype(F32)
    hot, sel_idx, sel_score = [], [], []
    member = jnp.zeros((n_exp, tn), F32)
    for _ in range(EXPERT_TOPK):
        m = jnp.max(cand, axis=0, keepdims=True)
        first = jnp.min(jnp.where(cand == m, erow, float(n_exp)), axis=0, keepdims=True)
        onehot = erow == first
        sel_idx.append(first)
        sel_score.append(jnp.sum(jnp.where(onehot, scores, 0.0), axis=0, keepdims=True))
        cand = jnp.where(onehot, -3e38, cand)
        member = member + jnp.where(onehot, 1.0, 0.0)
        hot.append(onehot)
    total = sel_score[0]
    for sc in sel_score[1:]:
        total = total + sc

    trow = lax.broadcasted_iota(jnp.int32, (tn, tn), 0)
    tcol = lax.broadcasted_iota(jnp.int32, (tn, tn), 1)
    before = jnp.where(trow < tcol, 1.0, 0.0).astype(BF16)
    prior = _dot(member.astype(BF16), before) + carry_ref[...]
    ranks = [jnp.sum(jnp.where(hot[k], prior, 0.0), axis=0, keepdims=True) for k in range(EXPERT_TOPK)]

    idx_ref[...] = jnp.concatenate(sel_idx, axis=0).astype(jnp.int32)
    w_ref[...] = jnp.concatenate([sc / total * ROUTED_SCALE for sc in sel_score], axis=0)
    rank_ref[...] = jnp.concatenate(ranks, axis=0).astype(jnp.int32)
    carry_ref[...] = carry_ref[...] + jnp.sum(member, axis=1, keepdims=True)
    cnt_ref[...] = carry_ref[...]


def _route(x2d, w_router_t, b_router):
    n, d = x2d.shape
    n_exp = w_router_t.shape[0]
    tn = min(ROUTER_TILE, n)
    col = lambda i: (0, i)
    return pl.pallas_call(
        functools.partial(_router_kernel, tn=tn, n_exp=n_exp),
        grid=(n // tn,),
        in_specs=[
            pl.BlockSpec((tn, d), lambda i: (i, 0)),
            pl.BlockSpec((n_exp, d), lambda i: (0, 0)),
            pl.BlockSpec((n_exp, 1), lambda i: (0, 0)),
        ],
        out_specs=[
            pl.BlockSpec((EXPERT_TOPK, tn), col),
            pl.BlockSpec((EXPERT_TOPK, tn), col),
            pl.BlockSpec((EXPERT_TOPK, tn), col),
            pl.BlockSpec((n_exp, 1), lambda i: (0, 0)),
        ],
        out_shape=[
            jax.ShapeDtypeStruct((EXPERT_TOPK, n), jnp.int32),
            jax.ShapeDtypeStruct((EXPERT_TOPK, n), F32),
            jax.ShapeDtypeStruct((EXPERT_TOPK, n), jnp.int32),
            jax.ShapeDtypeStruct((n_exp, 1), F32),
        ],
        scratch_shapes=[pltpu.VMEM((n_exp, 1), F32)],
        compiler_params=_cparams(("arbitrary",)),
        name="router",
    )(x2d, w_router_t, b_router)


def _experts_kernel(be_ref, first_ref, nused_ref, x_ref, wg_ref, wu_ref, wd_ref, *rest):
    y_ref, wg_s, wu_s, wd_s = rest[-4:]
    blk = pl.program_id(0)

    @pl.when(blk < nused_ref[0])
    def _():
        @pl.when((first_ref[blk] == 1) | (blk == 0))
        def _():
            wg_s[...] = wg_ref[0, 0].astype(BF16)
            wu_s[...] = wu_ref[0, 0].astype(BF16)
            wd_s[...] = wd_ref[0, 0].astype(BF16)

        x = x_ref[...]
        gate = _dot(x, wg_s[...])
        up = _dot(x, wu_s[...])
        h = gate * _sigmoid(gate) * up
        y_ref[...] = _dot(h.astype(BF16), wd_s[...]).astype(y_ref.dtype)

    @pl.when(blk >= nused_ref[0])
    def _():
        y_ref[...] = jnp.zeros_like(y_ref)


def _experts(xg, blk_expert, blk_first, n_used, w_gate, w_up, w_down, layer, y_prev, first_block, n_rows):
    d = xg.shape[1]
    hdim = w_gate.shape[3]
    tm = EXPERT_ROW_BLOCK
    n_blocks = xg.shape[0] // tm
    extra_specs, extra_args, aliases = [], [], {}
    if y_prev is not None:
        extra_specs, extra_args, aliases = [pl.BlockSpec(memory_space=pl.ANY)], [y_prev], {7: 0}
    grid_spec = pltpu.PrefetchScalarGridSpec(
        num_scalar_prefetch=3,
        grid=(n_blocks,),
        in_specs=[
            pl.BlockSpec((tm, d), lambda i, be, fi, nu: (i, 0)),
            pl.BlockSpec((1, 1, d, hdim), lambda i, be, fi, nu: (layer, be[i], 0, 0)),
            pl.BlockSpec((1, 1, d, hdim), lambda i, be, fi, nu: (layer, be[i], 0, 0)),
            pl.BlockSpec((1, 1, hdim, d), lambda i, be, fi, nu: (layer, be[i], 0, 0)),
        ] + extra_specs,
        out_specs=pl.BlockSpec((tm, d), lambda i, be, fi, nu: (i + first_block, 0)),
        scratch_shapes=[
            pltpu.VMEM((d, hdim), BF16),
            pltpu.VMEM((d, hdim), BF16),
            pltpu.VMEM((hdim, d), BF16),
        ],
    )
    return pl.pallas_call(
        _experts_kernel,
        grid_spec=grid_spec,
        out_shape=jax.ShapeDtypeStruct((n_rows, d), BF16),
        input_output_aliases=aliases,
        compiler_params=_cparams(("arbitrary",)),
        name="experts",
    )(blk_expert, blk_first, n_used, xg, w_gate, w_up, w_down, *extra_args)


def _moe_tail_kernel(x_ref, r_ref, rw_ref, p_ref, sg_ref, su_ref, sd_ref, pg_ref, pb_ref, pp_ref,
                     g1_ref, b1_ref, g2_ref, b2_ref, y_ref):
    x = x_ref[...]
    xb = x.astype(BF16)
    gate = _dot(xb, sg_ref[...])
    up = _dot(xb, su_ref[...])
    shared = _dot((gate * _sigmoid(gate) * up).astype(BF16), sd_ref[...])
    rw = rw_ref[...]
    routed = r_ref[0].astype(F32) * rw[:, 0:1]
    for k in range(1, r_ref.shape[0]):
        routed = routed + r_ref[k].astype(F32) * rw[:, k:k + 1]
    x2 = _layer_norm(DEEPNORM_ALPHA * x + (routed + shared), g1_ref[...], b1_ref[...])
    ple_gate = _sigmoid(_dot(x2.astype(BF16), pg_ref[...]) + pb_ref[...])
    ple = ple_gate * _dot(p_ref[0].astype(BF16), pp_ref[...])
    y_ref[...] = _layer_norm(DEEPNORM_ALPHA * x2 + ple, g2_ref[...], b2_ref[...])


def _moe_tail(x2d, routed, routed_w, p_all, layer, sg, su, sd, pg, pb, pp, g1, b1, g2, b2):
    n, d = x2d.shape
    tm = min(ROW_TILE, n)
    topk = routed.shape[0]
    row = lambda i: (i, 0)
    fixed = lambda i: (0, 0)
    full = lambda a: pl.BlockSpec(a.shape, fixed)
    return pl.pallas_call(
        _moe_tail_kernel,
        grid=(n // tm,),
        in_specs=[
            pl.BlockSpec((tm, d), row),
            pl.BlockSpec((topk, tm, d), lambda i: (0, i, 0)),
            pl.BlockSpec((tm, topk), row),
            pl.BlockSpec((1, tm, p_all.shape[2]), lambda i: (layer, i, 0)),
            full(sg), full(su), full(sd), full(pg), full(pb), full(pp),
            full(g1), full(b1), full(g2), full(b2),
        ],
        out_specs=pl.BlockSpec((tm, d), row),
        out_shape=jax.ShapeDtypeStruct((n, d), F32),
        compiler_params=_cparams(("arbitrary",)),
        name="moe_tail",
    )(x2d, routed, routed_w, p_all, sg, su, sd, pg, pb, pp, g1, b1, g2, b2)


def _moe_layer(x1, x1b, p_all, layer, w_router, b_router, w_gate, w_up, w_down, sg, su, sd,
               ple_proj, ple_gate, ple_bias, g1, b1, g2, b2):
    n, d = x1.shape
    n_exp = w_router.shape[1]
    idx, wts, rank, counts = _route(x1, w_router.T, b_router.reshape(n_exp, 1))
    tm = EXPERT_ROW_BLOCK
    counts = counts[:, 0].astype(jnp.int32)
    padded = (counts + tm - 1) // tm * tm
    pad_end = jnp.cumsum(padded)
    pad_start = pad_end - padded
    n_blocks = (n * EXPERT_TOPK + n_exp * (tm - 1)) // tm + 1
    n_blocks = -(-n_blocks // EXPERT_CHUNKS) * EXPERT_CHUNKS
    n_rows = n_blocks * tm
    experts = jnp.arange(n_exp, dtype=jnp.int32)
    start_of = jnp.sum(jnp.where(idx[:, :, None] == experts, pad_start, 0), axis=-1)
    pos = start_of + rank
    stride = n + tm
    past_end = 2 ** 30
    assert n_exp * stride <= past_end and past_end + n_rows + n_exp * tm < 2 ** 31
    tok = jnp.arange(n, dtype=jnp.int32)[None, :]
    filler = jnp.arange(tm, dtype=jnp.int32)[None, :]
    filler_key = jnp.where(filler < (padded - counts)[:, None],
                           experts[:, None] * stride + n + filler,
                           past_end + experts[:, None] * tm + filler)
    n_tail = max(n_rows - (n * EXPERT_TOPK + n_exp * tm), 0)
    keys = jnp.concatenate([(idx * stride + tok).reshape(-1), filler_key.reshape(-1),
                            past_end + n_exp * tm + jnp.arange(n_tail, dtype=jnp.int32)])
    keys = lax.sort(keys, is_stable=False)[:n_rows]
    slot = keys % stride
    tok_pad = jnp.where(keys >= past_end, (keys - past_end) % n,
                        jnp.where(slot < n, slot, (keys // stride * tm + slot - n) % n))
    blk_start = jnp.arange(n_blocks, dtype=jnp.int32) * tm
    blk_expert = jnp.minimum(jnp.sum((pad_end[None, :] <= blk_start[:, None]).astype(jnp.int32), axis=1),
                             n_exp - 1)
    blk_first = jnp.concatenate([jnp.ones((1,), jnp.int32),
                                 (blk_expert[1:] != blk_expert[:-1]).astype(jnp.int32)])
    n_used = (pad_end[-1:] // tm).astype(jnp.int32)
    per = n_blocks // EXPERT_CHUNKS
    y = None
    for c in range(EXPERT_CHUNKS):
        blocks = slice(c * per, (c + 1) * per)
        xg = x1b.at[tok_pad[c * per * tm:(c + 1) * per * tm]].get(mode="promise_in_bounds")
        y = _experts(xg, blk_expert[blocks], blk_first[blocks], jnp.clip(n_used - c * per, 0, per),
                     w_gate, w_up, w_down, layer, y, c * per, n_rows)
    routed = y.at[pos].get(mode="promise_in_bounds")
    return _moe_tail(x1, routed, wts.T, p_all, layer, sg.astype(BF16), su.astype(BF16), sd.astype(BF16),
                     ple_gate.astype(BF16), ple_bias.reshape(1, d), ple_proj.astype(BF16),
                     g1.reshape(1, d), b1.reshape(1, d), g2.reshape(1, d), b2.reshape(1, d))


def _gelu_tanh(x):
    return 0.5 * x * (1.0 + jnp.tanh(0.7978845608028654 * (x + 0.044715 * (x * x * x))))


def _compress_kernel(c_ref, pe_ref, w1_ref, b1_ref, w2_ref, o_ref, *, nc, half):
    out = jnp.zeros((nc, GROUP_LANES), F32)
    for g in range(NSA_KV_GROUPS):
        c = c_ref[0, 0, g].astype(F32)
        a = _dot((c + pe_ref[0, 0:1, :]).astype(BF16), w1_ref[0, :half, :])
        bm = _dot((c + pe_ref[0, 1:2, :]).astype(BF16), w1_ref[0, half:, :])
        h = _gelu_tanh(a + pltpu.roll(bm, nc - 1, 0) + b1_ref[0])
        out = out + _dot(h.astype(BF16), w2_ref[0, g])
    o_ref[0, 0] = out.astype(o_ref.dtype)


def _compress(chunks, pe_flat, w1, b1, w2_placed):
    b, _, _, nc, half = chunks.shape
    hid = w1.shape[2]
    return pl.pallas_call(
        functools.partial(_compress_kernel, nc=nc, half=half),
        grid=(b, 2),
        in_specs=[
            pl.BlockSpec((1, 1, NSA_KV_GROUPS, nc, half), lambda i, j: (i, j, 0, 0, 0)),
            pl.BlockSpec((1, 2, half), lambda i, j: (j, 0, 0)),
            pl.BlockSpec((1, 2 * half, hid), lambda i, j: (j, 0, 0)),
            pl.BlockSpec((1, 1, hid), lambda i, j: (j, 0, 0)),
            pl.BlockSpec((1, NSA_KV_GROUPS, hid, GROUP_LANES), lambda i, j: (j, 0, 0, 0)),
        ],
        out_specs=pl.BlockSpec((1, 1, nc, GROUP_LANES), lambda i, j: (i, j, 0, 0)),
        out_shape=jax.ShapeDtypeStruct((b, 2, nc, GROUP_LANES), BF16),
        compiler_params=_cparams(("arbitrary", "arbitrary")),
        name="compress",
    )(chunks, pe_flat, w1, b1, w2_placed)


def _group_mask(g):
    lane = lax.broadcasted_iota(jnp.int32, (1, GROUP_LANES), 1)
    return (lane >= g * HEAD_DIM) & (lane < (g + 1) * HEAD_DIM)


def _topk_rank(score):
    n, t = score.shape
    sub8 = lax.broadcasted_iota(jnp.int32, (8, t), 0)
    ahead = jnp.zeros((n, t), F32)
    for j in range(n):
        vj = score[j:j + 1, :]
        lo_r, hi_r = (j // 8) * 8, (j // 8) * 8 + 8
        parts = []
        if lo_r > 0:
            parts.append(jnp.where(vj > score[:lo_r], 1.0, 0.0))
        own = score[lo_r:hi_r]
        parts.append(jnp.where(sub8 > (j % 8), jnp.where(vj >= own, 1.0, 0.0),
                               jnp.where(vj > own, 1.0, 0.0)))
        if hi_r < n:
            parts.append(jnp.where(vj >= score[hi_r:], 1.0, 0.0))
        ahead = ahead + jnp.concatenate(parts, axis=0)
    return ahead


def _nsa_cmp_kernel(q_ref, kc_ref, vc_ref, gate_ref, slope_ref, wt_ref, o_ref, sel_ref,
                    *, tq, nc, n_slc):
    qi = pl.program_id(1)
    q0 = qi * tq
    kc = kc_ref[0, 0]
    vc = vc_ref[0, 0]
    t_pos = q0 + lax.broadcasted_iota(jnp.int32, (tq, nc), 0)
    cmp_end = lax.broadcasted_iota(jnp.int32, (tq, nc), 1) * CMP_STRIDE + (CMP_BLOCK - 1)
    visible = cmp_end <= t_pos
    end_rel = (lax.broadcasted_iota(jnp.int32, (1, nc), 1) * CMP_STRIDE + (CMP_BLOCK - 1) - q0).astype(F32)
    v_aug = [jnp.where(_group_mask(g), vc, jnp.ones_like(vc)) for g in range(NSA_KV_GROUPS)]
    psum = [jnp.zeros((tq, nc), F32) for _ in range(NSA_KV_GROUPS)]
    for r in range(NSA_HEADS_PER_GROUP):
        q = q_ref[0, :, r * GROUP_LANES:(r + 1) * GROUP_LANES]
        gates = gate_ref[r, 0]
        out = jnp.zeros((tq, GROUP_LANES), F32)
        for g in range(NSA_KV_GROUPS):
            in_grp = _group_mask(g)
            qm = jnp.where(in_grp, q, jnp.zeros_like(q))
            s = _nt(qm, kc) + slope_ref[r, g, :, :nc] * end_rel
            s = jnp.where(visible, s, NEG)
            m = jnp.max(s, axis=1, keepdims=True)
            p = jnp.where(visible, jnp.exp2(s - m), 0.0)
            o_g = _dot(p.astype(BF16), v_aug[g])
            other = (1 - g // 2) * LANES
            l = o_g[:, other:other + LANES]
            inv = jnp.where(l > 0.0, 1.0 / l, 0.0)
            inv2 = jnp.concatenate([inv, inv], axis=1)
            psum[g] = psum[g] + (p * inv2[:, :nc] if nc == GROUP_LANES else p * inv[:, :1])
            out = out + jnp.where(in_grp, o_g * (inv2 * gates[:, g:g + 1]), 0.0)
        o_ref[0, :, r * GROUP_LANES:(r + 1) * GROUP_LANES] = out.astype(o_ref.dtype)

    blk = lax.broadcasted_iota(jnp.int32, (n_slc, tq), 0)
    t_row = q0 + lax.broadcasted_iota(jnp.int32, (n_slc, tq), 1)
    cur = t_row // SLC_BLOCK
    forced = (blk == 0) | (blk == cur) | (blk == cur - 1)
    vis = blk * SLC_BLOCK <= t_row
    wt = wt_ref[...]
    pad_rows = jnp.zeros((LANES - n_slc, tq), F32)
    for g in range(NSA_KV_GROUPS):
        hi, mid, lo = _split3(psum[g])
        imp = _nt(wt, hi) + _nt(wt, mid) + _nt(wt, lo)
        score = jnp.where(forced, 1e9, jnp.where(vis, imp, NEG))
        keep = (_topk_rank(score) < float(min(SLC_TOPK, n_slc))) & (score > 0.5 * NEG)
        flags = jnp.concatenate([jnp.where(keep, 0.0, NEG), pad_rows], axis=0)
        sel_ref[0, g] = flags.T.astype(sel_ref.dtype)


def _nsa_compressed(q, cmp_kv, gates_c, slopes_rep, w_sel_t):
    b, s, _ = q.shape
    nc = cmp_kv.shape[2]
    n_slc = s // SLC_BLOCK
    tq = min(NSA_Q_TILE, s)
    rr = NSA_HEADS_PER_GROUP
    return pl.pallas_call(
        functools.partial(_nsa_cmp_kernel, tq=tq, nc=nc, n_slc=n_slc),
        grid=(b, s // tq),
        in_specs=[
            pl.BlockSpec((1, tq, rr * GROUP_LANES), lambda i, j: (i, j, 0)),
            pl.BlockSpec((1, 1, nc, GROUP_LANES), lambda i, j: (i, 0, 0, 0)),
            pl.BlockSpec((1, 1, nc, GROUP_LANES), lambda i, j: (i, 1, 0, 0)),
            pl.BlockSpec((rr, 1, tq, LANES), lambda i, j: (0, i, j, 0)),
            pl.BlockSpec((rr, NSA_KV_GROUPS, 1, slopes_rep.shape[3]), lambda i, j: (0, 0, 0, 0)),
            pl.BlockSpec((n_slc, nc), lambda i, j: (0, 0)),
        ],
        out_specs=[
            pl.BlockSpec((1, tq, rr * GROUP_LANES), lambda i, j: (i, j, 0)),
            pl.BlockSpec((1, NSA_KV_GROUPS, tq, LANES), lambda i, j: (i, 0, j, 0)),
        ],
        out_shape=[
            jax.ShapeDtypeStruct((b, s, rr * GROUP_LANES), BF16),
            jax.ShapeDtypeStruct((b, NSA_KV_GROUPS, s, LANES), BF16),
        ],
        compiler_params=_cparams(("arbitrary", "arbitrary")),
        name="nsa_compressed",
    )(q, cmp_kv, cmp_kv, gates_c, slopes_rep, w_sel_t)


AUG_POS_LANE = 64


def _nsa_query_aug(q, slope_row, sel, g, lane):
    t = q.shape[0]
    half = g // 2
    in_grp = (lane >= (g % 2) * HEAD_DIM) & (lane < (g % 2 + 1) * HEAD_DIM)
    q_own = jnp.where(in_grp, q[:, half * LANES:(half + 1) * LANES], jnp.zeros((t, LANES), BF16))
    q_extra = jnp.broadcast_to(slope_row, (t, LANES))
    if sel is not None:
        q_extra = jnp.where(lane < AUG_POS_LANE, sel, q_extra)
    return jnp.concatenate([q_own, q_extra] if half == 0 else [q_extra, q_own], axis=1)


def _nsa_key_operands(k, ka):
    return (jnp.concatenate([k[:, :LANES], ka], axis=1), jnp.concatenate([ka, k[:, LANES:]], axis=1))


def _nsa_selected_kernel(qt_ref, kt_ref, q_ref, k_ref, v_ref, ka_ref, gate_ref, slope_ref, sel_ref, o_ref,
                         m_ref, acc_ref, *, t, sub):
    step = pl.program_id(2)
    qi = qt_ref[step]
    ki = kt_ref[step]

    @pl.when(ki == 0)
    def _():
        m_ref[...] = jnp.full_like(m_ref, NEG)
        acc_ref[...] = jnp.zeros_like(acc_ref)

    lane = lax.broadcasted_iota(jnp.int32, (1, LANES), 1)

    def body(diag):
        q = q_ref[0]
        v = v_ref[0]
        k_aug = _nsa_key_operands(k_ref[0], ka_ref[...])
        items = []
        for g in range(NSA_KV_GROUPS):
            qg = _nsa_query_aug(q, slope_ref[0, g], sel_ref[0, g], g, lane)
            vg = jnp.where(_group_mask(g), v, jnp.ones_like(v))
            for qs in range(t // sub):
                rows = slice(qs * sub, (qs + 1) * sub)
                keys = slice(0, (qs + 1) * sub if diag else t)
                s = _nt(qg[rows], k_aug[g // 2][keys])
                if diag:
                    r_i = lax.broadcasted_iota(jnp.int32, (sub, sub), 0)
                    c_i = lax.broadcasted_iota(jnp.int32, (sub, sub), 1)
                    edge = jnp.where(c_i <= r_i, s[:, -sub:], NEG)
                    s = edge if s.shape[1] == sub else jnp.concatenate([s[:, :-sub], edge], axis=1)
                items.append(((g, rows), s, vg[keys], m_ref[g, rows, :], acc_ref[g, rows, :]))
        for (g, rows), m_new, acc_new in _flash_items(items):
            acc_ref[g, rows, :] = acc_new
            m_ref[g, rows, :] = m_new

    @pl.when(ki < qi)
    def _():
        body(False)

    @pl.when(ki == qi)
    def _():
        body(True)
        gates = gate_ref[0, 0]
        out = jnp.zeros((t, GROUP_LANES), F32)
        for g in range(NSA_KV_GROUPS):
            acc = acc_ref[g]
            denom = pltpu.roll(acc, LANES, 1)
            col = NSA_KV_GROUPS + g
            out = out + jnp.where(_group_mask(g), acc * (gates[:, col:col + 1] / denom), 0.0)
        o_ref[0] = out.astype(o_ref.dtype)


def _nsa_selected(q, kv, k_col, v_col, key_aug, gates, slope_rows, sel_q):
    b, s, _ = q.shape
    t = min(NSA_SEL_TILE, s)
    nq = s // t
    qt = np.concatenate([np.full((i + 1,), i, np.int32) for i in range(nq)])
    kt = np.concatenate([np.arange(i + 1, dtype=np.int32) for i in range(nq)])
    rr = NSA_HEADS_PER_GROUP
    grid_spec = pltpu.PrefetchScalarGridSpec(
        num_scalar_prefetch=2,
        grid=(b, rr, len(qt)),
        in_specs=[
            pl.BlockSpec((1, t, GROUP_LANES), lambda i, r, u, qt, kt: (i, qt[u], r)),
            pl.BlockSpec((1, t, GROUP_LANES), lambda i, r, u, qt, kt: (i, kt[u], k_col)),
            pl.BlockSpec((1, t, GROUP_LANES), lambda i, r, u, qt, kt: (i, kt[u], v_col)),
            pl.BlockSpec((t, LANES), lambda i, r, u, qt, kt: (kt[u], 0)),
            pl.BlockSpec((1, 1, t, LANES), lambda i, r, u, qt, kt: (r, i, qt[u], 0)),
            pl.BlockSpec((1, NSA_KV_GROUPS, 1, LANES), lambda i, r, u, qt, kt: (r, 0, 0, 0)),
            pl.BlockSpec((1, NSA_KV_GROUPS, t, LANES), lambda i, r, u, qt, kt: (i, 0, qt[u], 0)),
        ],
        out_specs=pl.BlockSpec((1, t, GROUP_LANES), lambda i, r, u, qt, kt: (i, qt[u], r)),
        scratch_shapes=[
            pltpu.VMEM((NSA_KV_GROUPS, t, LANES), F32),
            pltpu.VMEM((NSA_KV_GROUPS, t, GROUP_LANES), F32),
        ],
    )
    return pl.pallas_call(
        functools.partial(_nsa_selected_kernel, t=t, sub=min(NSA_SUB, t)),
        grid_spec=grid_spec,
        out_shape=jax.ShapeDtypeStruct((b, s, rr * GROUP_LANES), BF16),
        compiler_params=_cparams(("arbitrary", "arbitrary", "arbitrary")),
        name="nsa_selected",
    )(jnp.asarray(qt), jnp.asarray(kt), q, kv, kv, key_aug, gates, slope_rows, sel_q)


def _nsa_window_kernel(q_ref, kp_ref, kc_ref, vp_ref, vc_ref, kap_ref, kac_ref, gate_ref, slope_ref, o_ref,
                       *, t, sub):
    has_prev = pl.program_id(2) > 0
    lane = lax.broadcasted_iota(jnp.int32, (1, LANES), 1)
    q = q_ref[0]
    k = jnp.concatenate([kp_ref[0], kc_ref[0]], axis=0)
    v = jnp.concatenate([vp_ref[0], vc_ref[0]], axis=0)
    k_aug = _nsa_key_operands(k, jnp.concatenate([kap_ref[...], kac_ref[...]], axis=0))
    gates = gate_ref[0, 0]
    span = WINDOW + sub
    r_i = lax.broadcasted_iota(jnp.int32, (sub, sub), 0)
    c_i = lax.broadcasted_iota(jnp.int32, (sub, sub), 1)
    out_rows = [jnp.zeros((sub, GROUP_LANES), F32) for _ in range(t // sub)]
    for g in range(NSA_KV_GROUPS):
        half = g // 2
        qg = _nsa_query_aug(q, slope_ref[0, g], None, g, lane)
        vg = jnp.where(_group_mask(g), v, jnp.ones_like(v))
        col = 2 * NSA_KV_GROUPS + g
        for qs in range(t // sub):
            rows = slice(qs * sub, (qs + 1) * sub)
            keys = slice(qs * sub, qs * sub + span)
            s = _nt(qg[rows], k_aug[half][keys])
            chunks = [s[:, j * sub:(j + 1) * sub] for j in range(span // sub)]
            chunks[0] = jnp.where(c_i > r_i, chunks[0], NEG)
            chunks[-1] = jnp.where(c_i <= r_i, chunks[-1], NEG)
            for j in range(len(chunks) - 1):
                if qs * sub + j * sub < WINDOW:
                    chunks[j] = jnp.where(has_prev, chunks[j], NEG)
            m = chunks[0]
            for c in chunks[1:]:
                m = jnp.maximum(m, c)
            m = jnp.max(m, axis=1, keepdims=True)
            p = jnp.concatenate([jnp.exp2(c - m) for c in chunks], axis=1)
            acc = _dot(p.astype(BF16), vg[keys])
            denom = pltpu.roll(acc, LANES, 1)
            out_rows[qs] = out_rows[qs] + jnp.where(
                _group_mask(g), acc * (gates[rows, col:col + 1] / denom), 0.0)
    o_ref[0] = jnp.concatenate(out_rows, axis=0).astype(o_ref.dtype)


def _nsa_window(q, kv, k_col, v_col, key_aug, gates, slope_rows):
    b, s, _ = q.shape
    t = min(NSA_WIN_TILE, s)
    sub = min(NSA_SUB, t)
    assert WINDOW % sub == 0 and t % WINDOW == 0
    rr = NSA_HEADS_PER_GROUP
    per = t // WINDOW
    prev = lambda j: jnp.maximum(j * per - 1, 0)
    return pl.pallas_call(
        functools.partial(_nsa_window_kernel, t=t, sub=sub),
        grid=(b, rr, s // t),
        in_specs=[
            pl.BlockSpec((1, t, GROUP_LANES), lambda i, r, j: (i, j, r)),
            pl.BlockSpec((1, WINDOW, GROUP_LANES), lambda i, r, j: (i, prev(j), k_col)),
            pl.BlockSpec((1, t, GROUP_LANES), lambda i, r, j: (i, j, k_col)),
            pl.BlockSpec((1, WINDOW, GROUP_LANES), lambda i, r, j: (i, prev(j), v_col)),
            pl.BlockSpec((1, t, GROUP_LANES), lambda i, r, j: (i, j, v_col)),
            pl.BlockSpec((WINDOW, LANES), lambda i, r, j: (prev(j), 0)),
            pl.BlockSpec((t, LANES), lambda i, r, j: (j, 0)),
            pl.BlockSpec((1, 1, t, LANES), lambda i, r, j: (r, i, j, 0)),
            pl.BlockSpec((1, NSA_KV_GROUPS, 1, LANES), lambda i, r, j: (r, 0, 0, 0)),
        ],
        out_specs=pl.BlockSpec((1, t, GROUP_LANES), lambda i, r, j: (i, j, r)),
        out_shape=jax.ShapeDtypeStruct((b, s, rr * GROUP_LANES), BF16),
        compiler_params=_cparams(("arbitrary", "arbitrary", "arbitrary")),
        name="nsa_window",
    )(q, kv, kv, kv, kv, key_aug, key_aug, gates, slope_rows)


def _nsa_key_aug(s):
    pos = np.arange(s)
    aug = np.zeros((s, LANES), np.float32)
    aug[pos, pos // SLC_BLOCK] = 1.0
    aug[:, AUG_POS_LANE:AUG_POS_LANE + 3] = (pos // 64)[:, None]
    aug[:, AUG_POS_LANE + 3:AUG_POS_LANE + 6] = (pos % 64)[:, None]
    return jnp.asarray(aug, BF16)


def _nsa_slope_rows(slopes_l2):
    rr, gg = slopes_l2.shape
    hi, mid, lo = _split3(jnp.asarray(slopes_l2, F32))
    pieces = jnp.stack([hi, mid, lo], axis=-1).astype(F32)
    rows = jnp.zeros((rr, gg, 1, LANES), F32)
    rows = rows.at[:, :, 0, AUG_POS_LANE:AUG_POS_LANE + 3].set(pieces * 64.0)
    rows = rows.at[:, :, 0, AUG_POS_LANE + 3:AUG_POS_LANE + 6].set(pieces)
    return rows.astype(BF16)


def _pad_lanes(a, width=LANES):
    return jnp.pad(a, ((0, 0), (0, width - a.shape[1])))


def _fox_layer(x2d, b, s, w_in, b_f, w_out, ln_g, ln_b):
    d = x2d.shape[1]
    attn = w_out.shape[0]
    n_heads = attn // HEAD_DIM
    assert n_heads <= GATE_PIECE_STRIDE, "gate bias pieces of different heads would share lanes"
    scale = HEAD_DIM ** -0.5 * LOG2E
    col_scale = jnp.concatenate([jnp.full((attn,), scale, F32), jnp.ones((2 * attn,), F32)])[None, :]
    qkv, z = _project(x2d, w_in[:, :3 * attn].astype(BF16), col_scale,
                      _pad_lanes(w_in[:, 3 * attn:])[None], _pad_lanes(b_f[None, :])[None], False, True)
    gate_bias = _gate_cumsum(z.reshape(b, s, LANES), n_heads)
    o = _fox_attention(qkv.reshape(b, s, 3 * attn), gate_bias, n_heads)
    return _out_ln([o.reshape(b * s, attn)], w_out.astype(BF16), x2d,
                   ln_g.reshape(1, d), ln_b.reshape(1, d))


def _selection_weights_t(nc, n_slc):
    cs = np.arange(nc)[:, None] * CMP_STRIDE
    ce = cs + CMP_BLOCK
    ss = np.arange(n_slc)[None, :] * SLC_BLOCK
    se = ss + SLC_BLOCK
    w = np.clip(np.minimum(ce, se) - np.maximum(cs, ss), 0, None) / CMP_STRIDE
    w[nc - 1, :] = 0.0
    return jnp.asarray(w.T, BF16)


def _nsa_layer(x2d, b, s, w_kv, cmp_pe, cmp_w1, cmp_b1, cmp_w2, w_q, b_g, w_out, ln_g, ln_b):
    d = x2d.shape[1]
    gg, rr = NSA_KV_GROUPS, NSA_HEADS_PER_GROUP
    attn = gg * rr * HEAD_DIM
    n_heads = gg * rr
    wq = w_q[:, :attn].reshape(d, gg, rr, HEAD_DIM).transpose(0, 2, 1, 3).reshape(d, attn)
    wgate = w_q[:, attn:].reshape(d, gg, rr, 3).transpose(2, 0, 3, 1).reshape(rr, d, 3 * gg)
    bgate = b_g.reshape(gg, rr, 3).transpose(1, 2, 0).reshape(rr, 1, 3 * gg)
    wgate = jnp.pad(wgate, ((0, 0), (0, 0), (0, LANES - 3 * gg)))
    bgate = jnp.pad(bgate, ((0, 0), (0, 0), (0, LANES - 3 * gg)))
    wo = w_out.reshape(gg, rr, HEAD_DIM, d).transpose(1, 0, 2, 3).reshape(attn, d)
    w_all = jnp.concatenate([wq, w_kv], axis=1).astype(BF16)
    col_scale = jnp.concatenate([jnp.full((attn,), HEAD_DIM ** -0.5 * LOG2E, F32),
                                 jnp.ones((w_kv.shape[1],), F32)])[None, :]
    qkv, gates = _project(x2d, w_all, col_scale, wgate, bgate, True, False)
    width = qkv.shape[1]
    qkv = qkv.reshape(b, s, width)
    gates = gates.reshape(rr, b, s, LANES)

    nc = s // CMP_STRIDE
    half = CMP_STRIDE * HEAD_DIM
    raw = qkv[:, :, attn:attn + 2 * GROUP_LANES]
    chunks = raw.reshape(b, nc, CMP_STRIDE, 2, gg, HEAD_DIM).transpose(0, 3, 4, 1, 2, 5)
    chunks = chunks.reshape(b, 2, gg, nc, half)
    pe_flat = cmp_pe.reshape(2, 2, half)
    hid = cmp_w1.shape[2]
    w2_placed = jnp.zeros((2, gg, hid, GROUP_LANES), F32)
    for g in range(gg):
        w2_placed = w2_placed.at[:, g, :, g * HEAD_DIM:(g + 1) * HEAD_DIM].set(cmp_w2)
    cmp_kv = _compress(chunks, pe_flat, cmp_w1.astype(BF16), cmp_b1.reshape(2, 1, hid),
                       w2_placed.astype(BF16))

    slopes_l2 = (2.0 ** (-8.0 * np.arange(1, n_heads + 1) / n_heads) * LOG2E).reshape(gg, rr).T
    slopes_rep = jnp.asarray(np.broadcast_to(
        slopes_l2[:, :, None, None], (rr, gg, 1, nc)).astype(np.float32))
    n_slc = s // SLC_BLOCK
    assert n_slc <= AUG_POS_LANE, "selection flags must fit below the position lanes"
    o_c, sel_q = _nsa_compressed(qkv, cmp_kv, gates, slopes_rep, _selection_weights_t(nc, n_slc))
    key_aug = _nsa_key_aug(s)
    slope_rows = _nsa_slope_rows(slopes_l2)
    base = attn // GROUP_LANES
    o_s = _nsa_selected(qkv, qkv, base + 2, base + 3, key_aug, gates, slope_rows, sel_q)
    o_w = _nsa_window(qkv, qkv, base + 4, base + 5, key_aug, gates, slope_rows)
    n = b * s
    return _out_ln([o_c.reshape(n, attn), o_s.reshape(n, attn), o_w.reshape(n, attn)],
                   wo.astype(BF16), x2d, ln_g.reshape(1, d), ln_b.reshape(1, d))


def kernel(x, p, fox_w_in, fox_b_f, fox_w_out, nsa_w_kv, cmp_pe, cmp_w1, cmp_b1, cmp_w2, nsa_w_q, nsa_b_g, nsa_w_out, ln_g, ln_b, moe_w_router, moe_b_router, moe_w_gate, moe_w_up, moe_w_down, shared_w_gate, shared_w_up, shared_w_down, ple_w_proj, ple_w_gate, ple_b_gate):
    b, s, d = x.shape
    n = b * s
    depth = p.shape[0]
    n_a = depth // 2
    h = x.reshape(n, d)
    for i in range(depth):
        if i < n_a:
            h1, h1b = _fox_layer(h, b, s, fox_w_in[i], fox_b_f[i], fox_w_out[i], ln_g[i, 0], ln_b[i, 0])
        else:
            j = i - n_a
            h1, h1b = _nsa_layer(h, b, s, nsa_w_kv, cmp_pe, cmp_w1, cmp_b1, cmp_w2,
                                 nsa_w_q[j], nsa_b_g[j], nsa_w_out[j], ln_g[i, 0], ln_b[i, 0])
        h = _moe_layer(h1, h1b, p.reshape(depth, n, -1), i, moe_w_router[i], moe_b_router[i],
                       moe_w_gate, moe_w_up, moe_w_down,
                       shared_w_gate[i], shared_w_up[i], shared_w_down[i],
                       ple_w_proj[i], ple_w_gate[i], ple_b_gate[i],
                       ln_g[i, 1], ln_b[i, 1], ln_g[i, 2], ln_b[i, 2])
    return h.reshape(b, s, d)
```

```python
import functools

import numpy as np
import jax
import jax.numpy as jnp
from jax import lax
from jax.experimental import pallas as pl
from jax.experimental.pallas import tpu as pltpu

F32 = jnp.float32
BF16 = jnp.bfloat16
NEG = -1e30

HEAD_DIM = 64
LANES = 128
NSA_KV_GROUPS = 4
NSA_HEADS_PER_GROUP = 4
GROUP_LANES = NSA_KV_GROUPS * HEAD_DIM
CMP_BLOCK = 32
CMP_STRIDE = 16
SLC_BLOCK = 64
SLC_TOPK = 16
WINDOW = 512
EXPERT_TOPK = 8
EXPERT_GROUPS = 8
EXPERT_TOPK_GROUPS = 4
ROUTED_SCALE = 2.5
LN_EPS = 1e-5
DEPTH = 2
DEEPNORM_ALPHA = (2.0 * DEPTH) ** 0.25
VMEM_LIMIT = 52 * 1024 * 1024

ROW_TILE = 512
FOX_TILE = 2048
FOX_SUB = 128
NSA_Q_TILE = 256
NSA_SEL_TILE = 1024
NSA_WIN_TILE = 1024
NSA_SUB = 256
ROUTER_TILE = 512
EXPERT_ROW_BLOCK = 512
EXPERT_CHUNKS = 8


def _cparams(sem):
    return pltpu.CompilerParams(dimension_semantics=sem, vmem_limit_bytes=VMEM_LIMIT)


def _nt(a, b):
    return lax.dot_general(a, b, (((1,), (1,)), ((), ())), preferred_element_type=F32)


def _dot(a, b):
    return jnp.dot(a, b, preferred_element_type=F32)


def _split2(a):
    hi = a.astype(BF16)
    lo = (a - hi.astype(F32)).astype(BF16)
    return hi, lo


def _split3(a):
    hi = a.astype(BF16)
    r = a - hi.astype(F32)
    mid = r.astype(BF16)
    lo = (r - mid.astype(F32)).astype(BF16)
    return hi, mid, lo


def _sigmoid(x):
    return 1.0 / (1.0 + jnp.exp(-x))


def _layer_norm(z, g, b):
    mu = jnp.mean(z, axis=-1, keepdims=True)
    zc = z - mu
    var = jnp.mean(zc * zc, axis=-1, keepdims=True)
    return zc * lax.rsqrt(var + LN_EPS) * g + b


def _proj_kernel(x_ref, w_ref, cs_ref, wg_ref, bg_ref, o_ref, g_ref, *, tn, gate_sigmoid, gate_split):
    x = x_ref[...]
    xh = x.astype(BF16)
    m_out = w_ref.shape[1]
    for j in range(m_out // tn):
        sl = slice(j * tn, (j + 1) * tn)
        y = _dot(xh, w_ref[:, sl])
        o_ref[:, sl] = (y * cs_ref[:, sl]).astype(o_ref.dtype)
    xl = (x - xh.astype(F32)).astype(BF16)
    for i in range(wg_ref.shape[0]):
        wh, wl = _split2(wg_ref[i])
        g = _dot(xh, wh) + bg_ref[i]
        if gate_split:
            g = g + _dot(xh, wl) + _dot(xl, wh)
        if gate_sigmoid:
            g = _sigmoid(g)
        g_ref[i] = g


def _project(x2d, w_bf16, col_scale, w_gate, b_gate, gate_sigmoid, gate_split):
    n, k = x2d.shape
    m_out = w_bf16.shape[1]
    n_g = w_gate.shape[0]
    tm = min(ROW_TILE, n)
    return pl.pallas_call(
        functools.partial(_proj_kernel, tn=512, gate_sigmoid=gate_sigmoid, gate_split=gate_split),
        grid=(n // tm,),
        in_specs=[
            pl.BlockSpec((tm, k), lambda i: (i, 0)),
            pl.BlockSpec((k, m_out), lambda i: (0, 0)),
            pl.BlockSpec((1, m_out), lambda i: (0, 0)),
            pl.BlockSpec((n_g, k, LANES), lambda i: (0, 0, 0)),
            pl.BlockSpec((n_g, 1, LANES), lambda i: (0, 0, 0)),
        ],
        out_specs=[
            pl.BlockSpec((tm, m_out), lambda i: (i, 0)),
            pl.BlockSpec((n_g, tm, LANES), lambda i: (0, i, 0)),
        ],
        out_shape=[
            jax.ShapeDtypeStruct((n, m_out), BF16),
            jax.ShapeDtypeStruct((n_g, n, LANES), F32),
        ],
        compiler_params=_cparams(("arbitrary",)),
        name="project",
    )(x2d, w_bf16, col_scale, w_gate, b_gate)


LOG2E = 1.4426950408889634
GATE_PIECE_STRIDE = 16


def _gate_cumsum_kernel(z_ref, o_ref, carry_ref, *, ts, n_heads):
    @pl.when(pl.program_id(1) == 0)
    def _():
        carry_ref[...] = jnp.zeros_like(carry_ref)

    z = z_ref[0]
    log_f = jnp.minimum(z, 0.0) - jnp.log(1.0 + jnp.exp(-jnp.abs(z)))
    row = lax.broadcasted_iota(jnp.int32, (ts, ts), 0)
    col = lax.broadcasted_iota(jnp.int32, (ts, ts), 1)
    tri = jnp.where(col <= row, 1.0, 0.0).astype(BF16)
    hi, mid, lo = _split3(log_f)
    cs = _dot(tri, hi) + _dot(tri, mid) + _dot(tri, lo) + carry_ref[...]
    carry_ref[...] = cs[ts - 1:ts, :]
    lane = lax.broadcasted_iota(jnp.int32, (1, LANES), 1)
    bias = jnp.where(lane < n_heads, cs * (-LOG2E), 0.0)
    hi, mid, lo = _split3(bias)
    pieces = (hi.astype(F32) + pltpu.roll(mid.astype(F32), GATE_PIECE_STRIDE, 1)
              + pltpu.roll(lo.astype(F32), 2 * GATE_PIECE_STRIDE, 1))
    o_ref[0] = (pieces + pltpu.roll(pieces, HEAD_DIM, 1)).astype(o_ref.dtype)


def _gate_cumsum(z, n_heads):
    b, s, _ = z.shape
    ts = min(256, s)
    return pl.pallas_call(
        functools.partial(_gate_cumsum_kernel, ts=ts, n_heads=n_heads),
        grid=(b, s // ts),
        in_specs=[pl.BlockSpec((1, ts, LANES), lambda i, j: (i, j, 0))],
        out_specs=pl.BlockSpec((1, ts, LANES), lambda i, j: (i, j, 0)),
        out_shape=jax.ShapeDtypeStruct((b, s, LANES), BF16),
        scratch_shapes=[pltpu.VMEM((1, LANES), F32)],
        compiler_params=_cparams(("arbitrary", "arbitrary")),
        name="gate_cumsum",
    )(z)


def _flash_items(items):
    outs = []
    for key, s, v, m_prev, acc_prev in items:
        m_new = jnp.maximum(m_prev, jnp.max(s, axis=1, keepdims=True))
        alpha = jnp.exp2(m_prev - m_new)
        p = jnp.concatenate(
            [jnp.exp2(s[:, j * LANES:(j + 1) * LANES] - m_new[:, :LANES])
             for j in range(s.shape[1] // LANES)], axis=1)
        reps = acc_prev.shape[1] // LANES
        alpha = alpha if reps == 1 else jnp.concatenate([alpha] * reps, axis=1)
        outs.append((key, m_new, alpha * acc_prev + _dot(p.astype(BF16), v)))
    return outs


def _fox_kernel(qt_ref, kt_ref, q_ref, k_ref, v_ref, c_ref, o_ref, m_ref, acc_ref, *, t, sub):
    step = pl.program_id(2)
    pair = pl.program_id(1)
    qi = qt_ref[step]
    ki = kt_ref[step]

    @pl.when(ki == 0)
    def _():
        m_ref[...] = jnp.full_like(m_ref, NEG)
        acc_ref[...] = jnp.zeros_like(acc_ref)

    lane = lax.broadcasted_iota(jnp.int32, (1, LANES), 1)
    low = lane < HEAD_DIM

    def body(diag):
        q = q_ref[0]
        k = k_ref[0]
        v = v_ref[0]
        c = c_ref[0]
        items = []
        for h in range(2):
            own = low if h == 0 else jnp.logical_not(low)
            base = (1 - h) * HEAD_DIM + 2 * pair + h
            ones_at = ((lane == base) | (lane == base + GATE_PIECE_STRIDE)
                       | (lane == base + 2 * GATE_PIECE_STRIDE))
            qh = jnp.where(own, q, jnp.where(ones_at, 1.0, 0.0).astype(BF16))
            kh = jnp.where(own, k, c)
            vh = jnp.where(own, v, jnp.ones_like(v))
            for qs in range(t // sub):
                rows = slice(qs * sub, (qs + 1) * sub)
                nk = (qs + 1) * sub if diag else t
                s = _nt(qh[rows], kh[:nk])
                if diag:
                    r_i = lax.broadcasted_iota(jnp.int32, (sub, sub), 0)
                    c_i = lax.broadcasted_iota(jnp.int32, (sub, sub), 1)
                    last = jnp.where(c_i <= r_i, s[:, nk - sub:], NEG)
                    s = last if nk == sub else jnp.concatenate([s[:, :nk - sub], last], axis=1)
                items.append(((h, rows), s, vh[:nk], m_ref[h, rows, :], acc_ref[h, rows, :]))
        for (h, rows), m_new, acc_new in _flash_items(items):
            acc_ref[h, rows, :] = acc_new
            m_ref[h, rows, :] = m_new

    @pl.when(ki < qi)
    def _():
        body(False)

    @pl.when(ki == qi)
    def _():
        body(True)
        a0 = acc_ref[0]
        a1 = acc_ref[1]
        o0 = a0 / pltpu.roll(a0, HEAD_DIM, 1)
        o1 = a1 / pltpu.roll(a1, HEAD_DIM, 1)
        o_ref[0] = jnp.where(low, o0, o1).astype(o_ref.dtype)


def _fox_attention(qkv, gate_bias, n_heads):
    b, s, _ = qkv.shape
    t = min(FOX_TILE, s)
    nq = s // t
    pairs = n_heads // 2
    qt = np.concatenate([np.full((i + 1,), i, np.int32) for i in range(nq)])
    kt = np.concatenate([np.arange(i + 1, dtype=np.int32) for i in range(nq)])
    grid_spec = pltpu.PrefetchScalarGridSpec(
        num_scalar_prefetch=2,
        grid=(b, pairs, len(qt)),
        in_specs=[
            pl.BlockSpec((1, t, LANES), lambda i, j, u, qt, kt: (i, qt[u], j)),
            pl.BlockSpec((1, t, LANES), lambda i, j, u, qt, kt: (i, kt[u], pairs + j)),
            pl.BlockSpec((1, t, LANES), lambda i, j, u, qt, kt: (i, kt[u], 2 * pairs + j)),
            pl.BlockSpec((1, t, LANES), lambda i, j, u, qt, kt: (i, kt[u], 0)),
        ],
        out_specs=pl.BlockSpec((1, t, LANES), lambda i, j, u, qt, kt: (i, qt[u], j)),
        scratch_shapes=[
            pltpu.VMEM((2, t, LANES), F32),
            pltpu.VMEM((2, t, LANES), F32),
        ],
    )
    return pl.pallas_call(
        functools.partial(_fox_kernel, t=t, sub=min(FOX_SUB, t)),
        grid_spec=grid_spec,
        out_shape=jax.ShapeDtypeStruct((b, s, n_heads * HEAD_DIM), BF16),
        compiler_params=_cparams(("arbitrary", "arbitrary", "arbitrary")),
        name="fox_attention",
    )(jnp.asarray(qt), jnp.asarray(kt), qkv, qkv, qkv, gate_bias)


def _out_ln_kernel(*refs, n_in):
    o_refs = refs[:n_in]
    w_ref, x_ref, g_ref, b_ref, y_ref, yb_ref = refs[n_in:]
    if n_in == 1:
        o = o_refs[0][...]
    else:
        o = o_refs[0][...].astype(F32)
        for r in o_refs[1:]:
            o = o + r[...].astype(F32)
        o = o.astype(BF16)
    z = DEEPNORM_ALPHA * x_ref[...] + _dot(o, w_ref[...])
    y = _layer_norm(z, g_ref[...], b_ref[...])
    y_ref[...] = y
    yb_ref[...] = y.astype(BF16)


def _out_ln(o_list, w_bf16, x2d, g, b):
    n, d = x2d.shape
    k = w_bf16.shape[0]
    tm = min(ROW_TILE, n)
    n_in = len(o_list)
    row = lambda i: (i, 0)
    fixed = lambda i: (0, 0)
    return pl.pallas_call(
        functools.partial(_out_ln_kernel, n_in=n_in),
        grid=(n // tm,),
        in_specs=[pl.BlockSpec((tm, k), row)] * n_in + [
            pl.BlockSpec((k, d), fixed),
            pl.BlockSpec((tm, d), row),
            pl.BlockSpec((1, d), fixed),
            pl.BlockSpec((1, d), fixed),
        ],
        out_specs=[pl.BlockSpec((tm, d), row), pl.BlockSpec((tm, d), row)],
        out_shape=[jax.ShapeDtypeStruct((n, d), F32), jax.ShapeDtypeStruct((n, d), BF16)],
        compiler_params=_cparams(("arbitrary",)),
        name="out_ln",
    )(*o_list, w_bf16, x2d, g, b)


def _router_kernel(x_ref, wt_ref, b_ref, idx_ref, w_ref, rank_ref, cnt_ref, carry_ref, *, tn, n_exp):
    @pl.when(pl.program_id(0) == 0)
    def _():
        carry_ref[...] = jnp.zeros_like(carry_ref)

    xh, xl = _split2(x_ref[...])
    wh, wl = _split2(wt_ref[...])
    logits = _nt(wh, xh) + _nt(wh, xl) + _nt(wl, xh)
    scores = _sigmoid(logits)
    biased = scores + b_ref[...]

    per_grp = n_exp // EXPERT_GROUPS
    blocks, gscore = [], []
    for g in range(EXPERT_GROUPS):
        blk = biased[g * per_grp:(g + 1) * per_grp, :]
        m1 = jnp.max(blk, axis=0, keepdims=True)
        eq = blk == m1
        n_eq = jnp.sum(jnp.where(eq, 1.0, 0.0), axis=0, keepdims=True)
        m2 = jnp.max(jnp.where(eq, -3e38, blk), axis=0, keepdims=True)
        blocks.append(blk)
        gscore.append(m1 + jnp.where(n_eq >= 2.0, m1, m2))
    cand = []
    for g in range(EXPERT_GROUPS):
        beaten = jnp.zeros_like(gscore[g])
        for o in range(EXPERT_GROUPS):
            if o == g:
                continue
            wins = (gscore[o] >= gscore[g]) if o < g else (gscore[o] > gscore[g])
            beaten = beaten + jnp.where(wins, 1.0, 0.0)
        cand.append(jnp.where(beaten < float(EXPERT_TOPK_GROUPS), blocks[g], NEG))
    cand = jnp.concatenate(cand, axis=0)

    erow = lax.broadcasted_iota(jnp.int32, (n_exp, tn), 0).astype(F32)
    hot, sel_idx, sel_score = [], [], []
    member = jnp.zeros((n_exp, tn), F32)
    for _ in range(EXPERT_TOPK):
        m = jnp.max(cand, axis=0, keepdims=True)
        first = jnp.min(jnp.where(cand == m, erow, float(n_exp)), axis=0, keepdims=True)
        onehot = erow == first
        sel_idx.append(first)
        sel_score.append(jnp.sum(jnp.where(onehot, scores, 0.0), axis=0, keepdims=True))
        cand = jnp.where(onehot, -3e38, cand)
        member = member + jnp.where(onehot, 1.0, 0.0)
        hot.append(onehot)
    total = sel_score[0]
    for sc in sel_score[1:]:
        total = total + sc

    trow = lax.broadcasted_iota(jnp.int32, (tn, tn), 0)
    tcol = lax.broadcasted_iota(jnp.int32, (tn, tn), 1)
    before = jnp.where(trow < tcol, 1.0, 0.0).astype(BF16)
    prior = _dot(member.astype(BF16), before) + carry_ref[...]
    ranks = [jnp.sum(jnp.where(hot[k], prior, 0.0), axis=0, keepdims=True) for k in range(EXPERT_TOPK)]

    idx_ref[...] = jnp.concatenate(sel_idx, axis=0).astype(jnp.int32)
    w_ref[...] = jnp.concatenate([sc / total * ROUTED_SCALE for sc in sel_score], axis=0)
    rank_ref[...] = jnp.concatenate(ranks, axis=0).astype(jnp.int32)
    carry_ref[...] = carry_ref[...] + jnp.sum(member, axis=1, keepdims=True)
    cnt_ref[...] = carry_ref[...]


def _route(x2d, w_router_t, b_router):
    n, d = x2d.shape
    n_exp = w_router_t.shape[0]
    tn = min(ROUTER_TILE, n)
    col = lambda i: (0, i)
    return pl.pallas_call(
        functools.partial(_router_kernel, tn=tn, n_exp=n_exp),
        grid=(n // tn,),
        in_specs=[
            pl.BlockSpec((tn, d), lambda i: (i, 0)),
            pl.BlockSpec((n_exp, d), lambda i: (0, 0)),
            pl.BlockSpec((n_exp, 1), lambda i: (0, 0)),
        ],
        out_specs=[
            pl.BlockSpec((EXPERT_TOPK, tn), col),
            pl.BlockSpec((EXPERT_TOPK, tn), col),
            pl.BlockSpec((EXPERT_TOPK, tn), col),
            pl.BlockSpec((n_exp, 1), lambda i: (0, 0)),
        ],
        out_shape=[
            jax.ShapeDtypeStruct((EXPERT_TOPK, n), jnp.int32),
            jax.ShapeDtypeStruct((EXPERT_TOPK, n), F32),
            jax.ShapeDtypeStruct((EXPERT_TOPK, n), jnp.int32),
            jax.ShapeDtypeStruct((n_exp, 1), F32),
        ],
        scratch_shapes=[pltpu.VMEM((n_exp, 1), F32)],
        compiler_params=_cparams(("arbitrary",)),
        name="router",
    )(x2d, w_router_t, b_router)


def _experts_kernel(be_ref, first_ref, nused_ref, x_ref, wg_ref, wu_ref, wd_ref, *rest):
    y_ref, wg_s, wu_s, wd_s = rest[-4:]
    blk = pl.program_id(0)

    @pl.when(blk < nused_ref[0])
    def _():
        @pl.when((first_ref[blk] == 1) | (blk == 0))
        def _():
            wg_s[...] = wg_ref[0, 0].astype(BF16)
            wu_s[...] = wu_ref[0, 0].astype(BF16)
            wd_s[...] = wd_ref[0, 0].astype(BF16)

        x = x_ref[...]
        gate = _dot(x, wg_s[...])
        up = _dot(x, wu_s[...])
        h = gate * _sigmoid(gate) * up
        y_ref[...] = _dot(h.astype(BF16), wd_s[...]).astype(y_ref.dtype)

    @pl.when(blk >= nused_ref[0])
    def _():
        y_ref[...] = jnp.zeros_like(y_ref)


def _experts(xg, blk_expert, blk_first, n_used, w_gate, w_up, w_down, layer, y_prev, first_block, n_rows):
    d = xg.shape[1]
    hdim = w_gate.shape[3]
    tm = EXPERT_ROW_BLOCK
    n_blocks = xg.shape[0] // tm
    extra_specs, extra_args, aliases = [], [], {}
    if y_prev is not None:
        extra_specs, extra_args, aliases = [pl.BlockSpec(memory_space=pl.ANY)], [y_prev], {7: 0}
    grid_spec = pltpu.PrefetchScalarGridSpec(
        num_scalar_prefetch=3,
        grid=(n_blocks,),
        in_specs=[
            pl.BlockSpec((tm, d), lambda i, be, fi, nu: (i, 0)),
            pl.BlockSpec((1, 1, d, hdim), lambda i, be, fi, nu: (layer, be[i], 0, 0)),
            pl.BlockSpec((1, 1, d, hdim), lambda i, be, fi, nu: (layer, be[i], 0, 0)),
            pl.BlockSpec((1, 1, hdim, d), lambda i, be, fi, nu: (layer, be[i], 0, 0)),
        ] + extra_specs,
        out_specs=pl.BlockSpec((tm, d), lambda i, be, fi, nu: (i + first_block, 0)),
        scratch_shapes=[
            pltpu.VMEM((d, hdim), BF16),
            pltpu.VMEM((d, hdim), BF16),
            pltpu.VMEM((hdim, d), BF16),
        ],
    )
    return pl.pallas_call(
        _experts_kernel,
        grid_spec=grid_spec,
        out_shape=jax.ShapeDtypeStruct((n_rows, d), BF16),
        input_output_aliases=aliases,
        compiler_params=_cparams(("arbitrary",)),
        name="experts",
    )(blk_expert, blk_first, n_used, xg, w_gate, w_up, w_down, *extra_args)


def _moe_tail_kernel(x_ref, r_ref, rw_ref, p_ref, sg_ref, su_ref, sd_ref, pg_ref, pb_ref, pp_ref,
                     g1_ref, b1_ref, g2_ref, b2_ref, y_ref):
    x = x_ref[...]
    xb = x.astype(BF16)
    gate = _dot(xb, sg_ref[...])
    up = _dot(xb, su_ref[...])
    shared = _dot((gate * _sigmoid(gate) * up).astype(BF16), sd_ref[...])
    rw = rw_ref[...]
    routed = r_ref[0].astype(F32) * rw[:, 0:1]
    for k in range(1, r_ref.shape[0]):
        routed = routed + r_ref[k].astype(F32) * rw[:, k:k + 1]
    x2 = _layer_norm(DEEPNORM_ALPHA * x + (routed + shared), g1_ref[...], b1_ref[...])
    ple_gate = _sigmoid(_dot(x2.astype(BF16), pg_ref[...]) + pb_ref[...])
    ple = ple_gate * _dot(p_ref[0].astype(BF16), pp_ref[...])
    y_ref[...] = _layer_norm(DEEPNORM_ALPHA * x2 + ple, g2_ref[...], b2_ref[...])


def _moe_tail(x2d, routed, routed_w, p_all, layer, sg, su, sd, pg, pb, pp, g1, b1, g2, b2):
    n, d = x2d.shape
    tm = min(ROW_TILE, n)
    topk = routed.shape[0]
    row = lambda i: (i, 0)
    fixed = lambda i: (0, 0)
    full = lambda a: pl.BlockSpec(a.shape, fixed)
    return pl.pallas_call(
        _moe_tail_kernel,
        grid=(n // tm,),
        in_specs=[
            pl.BlockSpec((tm, d), row),
            pl.BlockSpec((topk, tm, d), lambda i: (0, i, 0)),
            pl.BlockSpec((tm, topk), row),
            pl.BlockSpec((1, tm, p_all.shape[2]), lambda i: (layer, i, 0)),
            full(sg), full(su), full(sd), full(pg), full(pb), full(pp),
            full(g1), full(b1), full(g2), full(b2),
        ],
        out_specs=pl.BlockSpec((tm, d), row),
        out_shape=jax.ShapeDtypeStruct((n, d), F32),
        compiler_params=_cparams(("arbitrary",)),
        name="moe_tail",
    )(x2d, routed, routed_w, p_all, sg, su, sd, pg, pb, pp, g1, b1, g2, b2)


def _moe_layer(x1, x1b, p_all, layer, w_router, b_router, w_gate, w_up, w_down, sg, su, sd,
               ple_proj, ple_gate, ple_bias, g1, b1, g2, b2):
    n, d = x1.shape
    n_exp = w_router.shape[1]
    idx, wts, rank, counts = _route(x1, w_router.T, b_router.reshape(n_exp, 1))
    tm = EXPERT_ROW_BLOCK
    counts = counts[:, 0].astype(jnp.int32)
    padded = (counts + tm - 1) // tm * tm
    pad_end = jnp.cumsum(padded)
    pad_start = pad_end - padded
    n_blocks = (n * EXPERT_TOPK + n_exp * (tm - 1)) // tm + 1
    n_blocks = -(-n_blocks // EXPERT_CHUNKS) * EXPERT_CHUNKS
    n_rows = n_blocks * tm
    experts = jnp.arange(n_exp, dtype=jnp.int32)
    start_of = jnp.sum(jnp.where(idx[:, :, None] == experts, pad_start, 0), axis=-1)
    pos = start_of + rank
    stride = n + tm
    past_end = 2 ** 30
    assert n_exp * stride <= past_end and past_end + n_rows + n_exp * tm < 2 ** 31
    tok = jnp.arange(n, dtype=jnp.int32)[None, :]
    filler = jnp.arange(tm, dtype=jnp.int32)[None, :]
    filler_key = jnp.where(filler < (padded - counts)[:, None],
                           experts[:, None] * stride + n + filler,
                           past_end + experts[:, None] * tm + filler)
    n_tail = max(n_rows - (n * EXPERT_TOPK + n_exp * tm), 0)
    keys = jnp.concatenate([(idx * stride + tok).reshape(-1), filler_key.reshape(-1),
                            past_end + n_exp * tm + jnp.arange(n_tail, dtype=jnp.int32)])
    keys = lax.sort(keys, is_stable=False)[:n_rows]
    slot = keys % stride
    tok_pad = jnp.where(keys >= past_end, (keys - past_end) % n,
                        jnp.where(slot < n, slot, (keys // stride * tm + slot - n) % n))
    blk_start = jnp.arange(n_blocks, dtype=jnp.int32) * tm
    blk_expert = jnp.minimum(jnp.sum((pad_end[None, :] <= blk_start[:, None]).astype(jnp.int32), axis=1),
                             n_exp - 1)
    blk_first = jnp.concatenate([jnp.ones((1,), jnp.int32),
                                 (blk_expert[1:] != blk_expert[:-1]).astype(jnp.int32)])
    n_used = (pad_end[-1:] // tm).astype(jnp.int32)
    per = n_blocks // EXPERT_CHUNKS
    y = None
    for c in range(EXPERT_CHUNKS):
        blocks = slice(c * per, (c + 1) * per)
        xg = x1b.at[tok_pad[c * per * tm:(c + 1) * per * tm]].get(mode="promise_in_bounds")
        y = _experts(xg, blk_expert[blocks], blk_first[blocks], jnp.clip(n_used - c * per, 0, per),
                     w_gate, w_up, w_down, layer, y, c * per, n_rows)
    routed = y.at[pos].get(mode="promise_in_bounds")
    return _moe_tail(x1, routed, wts.T, p_all, layer, sg.astype(BF16), su.astype(BF16), sd.astype(BF16),
                     ple_gate.astype(BF16), ple_bias.reshape(1, d), ple_proj.astype(BF16),
                     g1.reshape(1, d), b1.reshape(1, d), g2.reshape(1, d), b2.reshape(1, d))


def _gelu_tanh(x):
    return 0.5 * x * (1.0 + jnp.tanh(0.7978845608028654 * (x + 0.044715 * (x * x * x))))


def _compress_kernel(c_ref, pe_ref, w1_ref, b1_ref, w2_ref, o_ref, *, nc, half):
    out = jnp.zeros((nc, GROUP_LANES), F32)
    for g in range(NSA_KV_GROUPS):
        c = c_ref[0, 0, g].astype(F32)
        a = _dot((c + pe_ref[0, 0:1, :]).astype(BF16), w1_ref[0, :half, :])
        bm = _dot((c + pe_ref[0, 1:2, :]).astype(BF16), w1_ref[0, half:, :])
        h = _gelu_tanh(a + pltpu.roll(bm, nc - 1, 0) + b1_ref[0])
        out = out + _dot(h.astype(BF16), w2_ref[0, g])
    o_ref[0, 0] = out.astype(o_ref.dtype)


def _compress(chunks, pe_flat, w1, b1, w2_placed):
    b, _, _, nc, half = chunks.shape
    hid = w1.shape[2]
    return pl.pallas_call(
        functools.partial(_compress_kernel, nc=nc, half=half),
        grid=(b, 2),
        in_specs=[
            pl.BlockSpec((1, 1, NSA_KV_GROUPS, nc, half), lambda i, j: (i, j, 0, 0, 0)),
            pl.BlockSpec((1, 2, half), lambda i, j: (j, 0, 0)),
            pl.BlockSpec((1, 2 * half, hid), lambda i, j: (j, 0, 0)),
            pl.BlockSpec((1, 1, hid), lambda i, j: (j, 0, 0)),
            pl.BlockSpec((1, NSA_KV_GROUPS, hid, GROUP_LANES), lambda i, j: (j, 0, 0, 0)),
        ],
        out_specs=pl.BlockSpec((1, 1, nc, GROUP_LANES), lambda i, j: (i, j, 0, 0)),
        out_shape=jax.ShapeDtypeStruct((b, 2, nc, GROUP_LANES), BF16),
        compiler_params=_cparams(("arbitrary", "arbitrary")),
        name="compress",
    )(chunks, pe_flat, w1, b1, w2_placed)


def _group_mask(g):
    lane = lax.broadcasted_iota(jnp.int32, (1, GROUP_LANES), 1)
    return (lane >= g * HEAD_DIM) & (lane < (g + 1) * HEAD_DIM)


def _topk_rank(score):
    n, t = score.shape
    sub8 = lax.broadcasted_iota(jnp.int32, (8, t), 0)
    ahead = jnp.zeros((n, t), F32)
    for j in range(n):
        vj = score[j:j + 1, :]
        lo_r, hi_r = (j // 8) * 8, (j // 8) * 8 + 8
        parts = []
        if lo_r > 0:
            parts.append(jnp.where(vj > score[:lo_r], 1.0, 0.0))
        own = score[lo_r:hi_r]
        parts.append(jnp.where(sub8 > (j % 8), jnp.where(vj >= own, 1.0, 0.0),
                               jnp.where(vj > own, 1.0, 0.0)))
        if hi_r < n:
            parts.append(jnp.where(vj >= score[hi_r:], 1.0, 0.0))
        ahead = ahead + jnp.concatenate(parts, axis=0)
    return ahead


def _nsa_cmp_kernel(q_ref, kc_ref, vc_ref, gate_ref, slope_ref, wt_ref, o_ref, sel_ref,
                    *, tq, nc, n_slc):
    qi = pl.program_id(1)
    q0 = qi * tq
    kc = kc_ref[0, 0]
    vc = vc_ref[0, 0]
    t_pos = q0 + lax.broadcasted_iota(jnp.int32, (tq, nc), 0)
    cmp_end = lax.broadcasted_iota(jnp.int32, (tq, nc), 1) * CMP_STRIDE + (CMP_BLOCK - 1)
    visible = cmp_end <= t_pos
    end_rel = (lax.broadcasted_iota(jnp.int32, (1, nc), 1) * CMP_STRIDE + (CMP_BLOCK - 1) - q0).astype(F32)
    v_aug = [jnp.where(_group_mask(g), vc, jnp.ones_like(vc)) for g in range(NSA_KV_GROUPS)]
    psum = [jnp.zeros((tq, nc), F32) for _ in range(NSA_KV_GROUPS)]
    for r in range(NSA_HEADS_PER_GROUP):
        q = q_ref[0, :, r * GROUP_LANES:(r + 1) * GROUP_LANES]
        gates = gate_ref[r, 0]
        out = jnp.zeros((tq, GROUP_LANES), F32)
        for g in range(NSA_KV_GROUPS):
            in_grp = _group_mask(g)
            qm = jnp.where(in_grp, q, jnp.zeros_like(q))
            s = _nt(qm, kc) + slope_ref[r, g, :, :nc] * end_rel
            s = jnp.where(visible, s, NEG)
            m = jnp.max(s, axis=1, keepdims=True)
            p = jnp.where(visible, jnp.exp2(s - m), 0.0)
            o_g = _dot(p.astype(BF16), v_aug[g])
            other = (1 - g // 2) * LANES
            l = o_g[:, other:other + LANES]
            inv = jnp.where(l > 0.0, 1.0 / l, 0.0)
            inv2 = jnp.concatenate([inv, inv], axis=1)
            psum[g] = psum[g] + (p * inv2[:, :nc] if nc == GROUP_LANES else p * inv[:, :1])
            out = out + jnp.where(in_grp, o_g * (inv2 * gates[:, g:g + 1]), 0.0)
        o_ref[0, :, r * GROUP_LANES:(r + 1) * GROUP_LANES] = out.astype(o_ref.dtype)

    blk = lax.broadcasted_iota(jnp.int32, (n_slc, tq), 0)
    t_row = q0 + lax.broadcasted_iota(jnp.int32, (n_slc, tq), 1)
    cur = t_row // SLC_BLOCK
    forced = (blk == 0) | (blk == cur) | (blk == cur - 1)
    vis = blk * SLC_BLOCK <= t_row
    wt = wt_ref[...]
    pad_rows = jnp.zeros((LANES - n_slc, tq), F32)
    for g in range(NSA_KV_GROUPS):
        hi, mid, lo = _split3(psum[g])
        imp = _nt(wt, hi) + _nt(wt, mid) + _nt(wt, lo)
        score = jnp.where(forced, 1e9, jnp.where(vis, imp, NEG))
        keep = (_topk_rank(score) < float(min(SLC_TOPK, n_slc))) & (score > 0.5 * NEG)
        flags = jnp.concatenate([jnp.where(keep, 0.0, NEG), pad_rows], axis=0)
        sel_ref[0, g] = flags.T.astype(sel_ref.dtype)


def _nsa_compressed(q, cmp_kv, gates_c, slopes_rep, w_sel_t):
    b, s, _ = q.shape
    nc = cmp_kv.shape[2]
    n_slc = s // SLC_BLOCK
    tq = min(NSA_Q_TILE, s)
    rr = NSA_HEADS_PER_GROUP
    return pl.pallas_call(
        functools.partial(_nsa_cmp_kernel, tq=tq, nc=nc, n_slc=n_slc),
        grid=(b, s // tq),
        in_specs=[
            pl.BlockSpec((1, tq, rr * GROUP_LANES), lambda i, j: (i, j, 0)),
            pl.BlockSpec((1, 1, nc, GROUP_LANES), lambda i, j: (i, 0, 0, 0)),
            pl.BlockSpec((1, 1, nc, GROUP_LANES), lambda i, j: (i, 1, 0, 0)),
            pl.BlockSpec((rr, 1, tq, LANES), lambda i, j: (0, i, j, 0)),
            pl.BlockSpec((rr, NSA_KV_GROUPS, 1, slopes_rep.shape[3]), lambda i, j: (0, 0, 0, 0)),
            pl.BlockSpec((n_slc, nc), lambda i, j: (0, 0)),
        ],
        out_specs=[
            pl.BlockSpec((1, tq, rr * GROUP_LANES), lambda i, j: (i, j, 0)),
            pl.BlockSpec((1, NSA_KV_GROUPS, tq, LANES), lambda i, j: (i, 0, j, 0)),
        ],
        out_shape=[
            jax.ShapeDtypeStruct((b, s, rr * GROUP_LANES), BF16),
            jax.ShapeDtypeStruct((b, NSA_KV_GROUPS, s, LANES), BF16),
        ],
        compiler_params=_cparams(("arbitrary", "arbitrary")),
        name="nsa_compressed",
    )(q, cmp_kv, cmp_kv, gates_c, slopes_rep, w_sel_t)


AUG_POS_LANE = 64


def _nsa_query_aug(q, slope_row, sel, g, lane):
    t = q.shape[0]
    half = g // 2
    in_grp = (lane >= (g % 2) * HEAD_DIM) & (lane < (g % 2 + 1) * HEAD_DIM)
    q_own = jnp.where(in_grp, q[:, half * LANES:(half + 1) * LANES], jnp.zeros((t, LANES), BF16))
    q_extra = jnp.broadcast_to(slope_row, (t, LANES))
    if sel is not None:
        q_extra = jnp.where(lane < AUG_POS_LANE, sel, q_extra)
    return jnp.concatenate([q_own, q_extra] if half == 0 else [q_extra, q_own], axis=1)


def _nsa_key_operands(k, ka):
    return (jnp.concatenate([k[:, :LANES], ka], axis=1), jnp.concatenate([ka, k[:, LANES:]], axis=1))


def _nsa_selected_kernel(qt_ref, kt_ref, q_ref, k_ref, v_ref, ka_ref, gate_ref, slope_ref, sel_ref, o_ref,
                         m_ref, acc_ref, *, t, sub):
    step = pl.program_id(2)
    qi = qt_ref[step]
    ki = kt_ref[step]

    @pl.when(ki == 0)
    def _():
        m_ref[...] = jnp.full_like(m_ref, NEG)
        acc_ref[...] = jnp.zeros_like(acc_ref)

    lane = lax.broadcasted_iota(jnp.int32, (1, LANES), 1)

    def body(diag):
        q = q_ref[0]
        v = v_ref[0]
        k_aug = _nsa_key_operands(k_ref[0], ka_ref[...])
        items = []
        for g in range(NSA_KV_GROUPS):
            qg = _nsa_query_aug(q, slope_ref[0, g], sel_ref[0, g], g, lane)
            vg = jnp.where(_group_mask(g), v, jnp.ones_like(v))
            for qs in range(t // sub):
                rows = slice(qs * sub, (qs + 1) * sub)
                keys = slice(0, (qs + 1) * sub if diag else t)
                s = _nt(qg[rows], k_aug[g // 2][keys])
                if diag:
                    r_i = lax.broadcasted_iota(jnp.int32, (sub, sub), 0)
                    c_i = lax.broadcasted_iota(jnp.int32, (sub, sub), 1)
                    edge = jnp.where(c_i <= r_i, s[:, -sub:], NEG)
                    s = edge if s.shape[1] == sub else jnp.concatenate([s[:, :-sub], edge], axis=1)
                items.append(((g, rows), s, vg[keys], m_ref[g, rows, :], acc_ref[g, rows, :]))
        for (g, rows), m_new, acc_new in _flash_items(items):
            acc_ref[g, rows, :] = acc_new
            m_ref[g, rows, :] = m_new

    @pl.when(ki < qi)
    def _():
        body(False)

    @pl.when(ki == qi)
    def _():
        body(True)
        gates = gate_ref[0, 0]
        out = jnp.zeros((t, GROUP_LANES), F32)
        for g in range(NSA_KV_GROUPS):
            acc = acc_ref[g]
            denom = pltpu.roll(acc, LANES, 1)
            col = NSA_KV_GROUPS + g
            out = out + jnp.where(_group_mask(g), acc * (gates[:, col:col + 1] / denom), 0.0)
        o_ref[0] = out.astype(o_ref.dtype)


def _nsa_selected(q, kv, k_col, v_col, key_aug, gates, slope_rows, sel_q):
    b, s, _ = q.shape
    t = min(NSA_SEL_TILE, s)
    nq = s // t
    qt = np.concatenate([np.full((i + 1,), i, np.int32) for i in range(nq)])
    kt = np.concatenate([np.arange(i + 1, dtype=np.int32) for i in range(nq)])
    rr = NSA_HEADS_PER_GROUP
    grid_spec = pltpu.PrefetchScalarGridSpec(
        num_scalar_prefetch=2,
        grid=(b, rr, len(qt)),
        in_specs=[
            pl.BlockSpec((1, t, GROUP_LANES), lambda i, r, u, qt, kt: (i, qt[u], r)),
            pl.BlockSpec((1, t, GROUP_LANES), lambda i, r, u, qt, kt: (i, kt[u], k_col)),
            pl.BlockSpec((1, t, GROUP_LANES), lambda i, r, u, qt, kt: (i, kt[u], v_col)),
            pl.BlockSpec((t, LANES), lambda i, r, u, qt, kt: (kt[u], 0)),
            pl.BlockSpec((1, 1, t, LANES), lambda i, r, u, qt, kt: (r, i, qt[u], 0)),
            pl.BlockSpec((1, NSA_KV_GROUPS, 1, LANES), lambda i, r, u, qt, kt: (r, 0, 0, 0)),
            pl.BlockSpec((1, NSA_KV_GROUPS, t, LANES), lambda i, r, u, qt, kt: (i, 0, qt[u], 0)),
        ],
        out_specs=pl.BlockSpec((1, t, GROUP_LANES), lambda i, r, u, qt, kt: (i, qt[u], r)),
        scratch_shapes=[
            pltpu.VMEM((NSA_KV_GROUPS, t, LANES), F32),
            pltpu.VMEM((NSA_KV_GROUPS, t, GROUP_LANES), F32),
        ],
    )
    return pl.pallas_call(
        functools.partial(_nsa_selected_kernel, t=t, sub=min(NSA_SUB, t)),
        grid_spec=grid_spec,
        out_shape=jax.ShapeDtypeStruct((b, s, rr * GROUP_LANES), BF16),
        compiler_params=_cparams(("arbitrary", "arbitrary", "arbitrary")),
        name="nsa_selected",
    )(jnp.asarray(qt), jnp.asarray(kt), q, kv, kv, key_aug, gates, slope_rows, sel_q)


def _nsa_window_kernel(q_ref, kp_ref, kc_ref, vp_ref, vc_ref, kap_ref, kac_ref, gate_ref, slope_ref, o_ref,
                       *, t, sub):
    has_prev = pl.program_id(2) > 0
    lane = lax.broadcasted_iota(jnp.int32, (1, LANES), 1)
    q = q_ref[0]
    k = jnp.concatenate([kp_ref[0], kc_ref[0]], axis=0)
    v = jnp.concatenate([vp_ref[0], vc_ref[0]], axis=0)
    k_aug = _nsa_key_operands(k, jnp.concatenate([kap_ref[...], kac_ref[...]], axis=0))
    gates = gate_ref[0, 0]
    span = WINDOW + sub
    r_i = lax.broadcasted_iota(jnp.int32, (sub, sub), 0)
    c_i = lax.broadcasted_iota(jnp.int32, (sub, sub), 1)
    out_rows = [jnp.zeros((sub, GROUP_LANES), F32) for _ in range(t // sub)]
    for g in range(NSA_KV_GROUPS):
        half = g // 2
        qg = _nsa_query_aug(q, slope_ref[0, g], None, g, lane)
        vg = jnp.where(_group_mask(g), v, jnp.ones_like(v))
        col = 2 * NSA_KV_GROUPS + g
        for qs in range(t // sub):
            rows = slice(qs * sub, (qs + 1) * sub)
            keys = slice(qs * sub, qs * sub + span)
            s = _nt(qg[rows], k_aug[half][keys])
            chunks = [s[:, j * sub:(j + 1) * sub] for j in range(span // sub)]
            chunks[0] = jnp.where(c_i > r_i, chunks[0], NEG)
            chunks[-1] = jnp.where(c_i <= r_i, chunks[-1], NEG)
            for j in range(len(chunks) - 1):
                if qs * sub + j * sub < WINDOW:
                    chunks[j] = jnp.where(has_prev, chunks[j], NEG)
            m = chunks[0]
            for c in chunks[1:]:
                m = jnp.maximum(m, c)
            m = jnp.max(m, axis=1, keepdims=True)
            p = jnp.concatenate([jnp.exp2(c - m) for c in chunks], axis=1)
            acc = _dot(p.astype(BF16), vg[keys])
            denom = pltpu.roll(acc, LANES, 1)
            out_rows[qs] = out_rows[qs] + jnp.where(
                _group_mask(g), acc * (gates[rows, col:col + 1] / denom), 0.0)
    o_ref[0] = jnp.concatenate(out_rows, axis=0).astype(o_ref.dtype)


def _nsa_window(q, kv, k_col, v_col, key_aug, gates, slope_rows):
    b, s, _ = q.shape
    t = min(NSA_WIN_TILE, s)
    sub = min(NSA_SUB, t)
    assert WINDOW % sub == 0 and t % WINDOW == 0
    rr = NSA_HEADS_PER_GROUP
    per = t // WINDOW
    prev = lambda j: jnp.maximum(j * per - 1, 0)
    return pl.pallas_call(
        functools.partial(_nsa_window_kernel, t=t, sub=sub),
        grid=(b, rr, s // t),
        in_specs=[
            pl.BlockSpec((1, t, GROUP_LANES), lambda i, r, j: (i, j, r)),
            pl.BlockSpec((1, WINDOW, GROUP_LANES), lambda i, r, j: (i, prev(j), k_col)),
            pl.BlockSpec((1, t, GROUP_LANES), lambda i, r, j: (i, j, k_col)),
            pl.BlockSpec((1, WINDOW, GROUP_LANES), lambda i, r, j: (i, prev(j), v_col)),
            pl.BlockSpec((1, t, GROUP_LANES), lambda i, r, j: (i, j, v_col)),
            pl.BlockSpec((WINDOW, LANES), lambda i, r, j: (prev(j), 0)),
            pl.BlockSpec((t, LANES), lambda i, r, j: (j, 0)),
            pl.BlockSpec((1, 1, t, LANES), lambda i, r, j: (r, i, j, 0)),
            pl.BlockSpec((1, NSA_KV_GROUPS, 1, LANES), lambda i, r, j: (r, 0, 0, 0)),
        ],
        out_specs=pl.BlockSpec((1, t, GROUP_LANES), lambda i, r, j: (i, j, r)),
        out_shape=jax.ShapeDtypeStruct((b, s, rr * GROUP_LANES), BF16),
        compiler_params=_cparams(("arbitrary", "arbitrary", "arbitrary")),
        name="nsa_window",
    )(q, kv, kv, kv, kv, key_aug, key_aug, gates, slope_rows)


def _nsa_key_aug(s):
    pos = np.arange(s)
    aug = np.zeros((s, LANES), np.float32)
    aug[pos, pos // SLC_BLOCK] = 1.0
    aug[:, AUG_POS_LANE:AUG_POS_LANE + 3] = (pos // 64)[:, None]
    aug[:, AUG_POS_LANE + 3:AUG_POS_LANE + 6] = (pos % 64)[:, None]
    return jnp.asarray(aug, BF16)


def _nsa_slope_rows(slopes_l2):
    rr, gg = slopes_l2.shape
    hi, mid, lo = _split3(jnp.asarray(slopes_l2, F32))
    pieces = jnp.stack([hi, mid, lo], axis=-1).astype(F32)
    rows = jnp.zeros((rr, gg, 1, LANES), F32)
    rows = rows.at[:, :, 0, AUG_POS_LANE:AUG_POS_LANE + 3].set(pieces * 64.0)
    rows = rows.at[:, :, 0, AUG_POS_LANE + 3:AUG_POS_LANE + 6].set(pieces)
    return rows.astype(BF16)


def _pad_lanes(a, width=LANES):
    return jnp.pad(a, ((0, 0), (0, width - a.shape[1])))


def _fox_layer(x2d, b, s, w_in, b_f, w_out, ln_g, ln_b):
    d = x2d.shape[1]
    attn = w_out.shape[0]
    n_heads = attn // HEAD_DIM
    assert n_heads <= GATE_PIECE_STRIDE, "gate bias pieces of different heads would share lanes"
    scale = HEAD_DIM ** -0.5 * LOG2E
    col_scale = jnp.concatenate([jnp.full((attn,), scale, F32), jnp.ones((2 * attn,), F32)])[None, :]
    qkv, z = _project(x2d, w_in[:, :3 * attn].astype(BF16), col_scale,
                      _pad_lanes(w_in[:, 3 * attn:])[None], _pad_lanes(b_f[None, :])[None], False, True)
    gate_bias = _gate_cumsum(z.reshape(b, s, LANES), n_heads)
    o = _fox_attention(qkv.reshape(b, s, 3 * attn), gate_bias, n_heads)
    return _out_ln([o.reshape(b * s, attn)], w_out.astype(BF16), x2d,
                   ln_g.reshape(1, d), ln_b.reshape(1, d))


def _selection_weights_t(nc, n_slc):
    cs = np.arange(nc)[:, None] * CMP_STRIDE
    ce = cs + CMP_BLOCK
    ss = np.arange(n_slc)[None, :] * SLC_BLOCK
    se = ss + SLC_BLOCK
    w = np.clip(np.minimum(ce, se) - np.maximum(cs, ss), 0, None) / CMP_STRIDE
    w[nc - 1, :] = 0.0
    return jnp.asarray(w.T, BF16)


def _nsa_layer(x2d, b, s, w_kv, cmp_pe, cmp_w1, cmp_b1, cmp_w2, w_q, b_g, w_out, ln_g, ln_b):
    d = x2d.shape[1]
    gg, rr = NSA_KV_GROUPS, NSA_HEADS_PER_GROUP
    attn = gg * rr * HEAD_DIM
    n_heads = gg * rr
    wq = w_q[:, :attn].reshape(d, gg, rr, HEAD_DIM).transpose(0, 2, 1, 3).reshape(d, attn)
    wgate = w_q[:, attn:].reshape(d, gg, rr, 3).transpose(2, 0, 3, 1).reshape(rr, d, 3 * gg)
    bgate = b_g.reshape(gg, rr, 3).transpose(1, 2, 0).reshape(rr, 1, 3 * gg)
    wgate = jnp.pad(wgate, ((0, 0), (0, 0), (0, LANES - 3 * gg)))
    bgate = jnp.pad(bgate, ((0, 0), (0, 0), (0, LANES - 3 * gg)))
    wo = w_out.reshape(gg, rr, HEAD_DIM, d).transpose(1, 0, 2, 3).reshape(attn, d)
    w_all = jnp.concatenate([wq, w_kv], axis=1).astype(BF16)
    col_scale = jnp.concatenate([jnp.full((attn,), HEAD_DIM ** -0.5 * LOG2E, F32),
                                 jnp.ones((w_kv.shape[1],), F32)])[None, :]
    qkv, gates = _project(x2d, w_all, col_scale, wgate, bgate, True, False)
    width = qkv.shape[1]
    qkv = qkv.reshape(b, s, width)
    gates = gates.reshape(rr, b, s, LANES)

    nc = s // CMP_STRIDE
    half = CMP_STRIDE * HEAD_DIM
    raw = qkv[:, :, attn:attn + 2 * GROUP_LANES]
    chunks = raw.reshape(b, nc, CMP_STRIDE, 2, gg, HEAD_DIM).transpose(0, 3, 4, 1, 2, 5)
    chunks = chunks.reshape(b, 2, gg, nc, half)
    pe_flat = cmp_pe.reshape(2, 2, half)
    hid = cmp_w1.shape[2]
    w2_placed = jnp.zeros((2, gg, hid, GROUP_LANES), F32)
    for g in range(gg):
        w2_placed = w2_placed.at[:, g, :, g * HEAD_DIM:(g + 1) * HEAD_DIM].set(cmp_w2)
    cmp_kv = _compress(chunks, pe_flat, cmp_w1.astype(BF16), cmp_b1.reshape(2, 1, hid),
                       w2_placed.astype(BF16))

    slopes_l2 = (2.0 ** (-8.0 * np.arange(1, n_heads + 1) / n_heads) * LOG2E).reshape(gg, rr).T
    slopes_rep = jnp.asarray(np.broadcast_to(
        slopes_l2[:, :, None, None], (rr, gg, 1, nc)).astype(np.float32))
    n_slc = s // SLC_BLOCK
    assert n_slc <= AUG_POS_LANE, "selection flags must fit below the position lanes"
    o_c, sel_q = _nsa_compressed(qkv, cmp_kv, gates, slopes_rep, _selection_weights_t(nc, n_slc))
    key_aug = _nsa_key_aug(s)
    slope_rows = _nsa_slope_rows(slopes_l2)
    base = attn // GROUP_LANES
    o_s = _nsa_selected(qkv, qkv, base + 2, base + 3, key_aug, gates, slope_rows, sel_q)
    o_w = _nsa_window(qkv, qkv, base + 4, base + 5, key_aug, gates, slope_rows)
    n = b * s
    return _out_ln([o_c.reshape(n, attn), o_s.reshape(n, attn), o_w.reshape(n, attn)],
                   wo.astype(BF16), x2d, ln_g.reshape(1, d), ln_b.reshape(1, d))


def kernel(x, p, fox_w_in, fox_b_f, fox_w_out, nsa_w_kv, cmp_pe, cmp_w1, cmp_b1, cmp_w2, nsa_w_q, nsa_b_g, nsa_w_out, ln_g, ln_b, moe_w_router, moe_b_router, moe_w_gate, moe_w_up, moe_w_down, shared_w_gate, shared_w_up, shared_w_down, ple_w_proj, ple_w_gate, ple_b_gate):
    b, s, d = x.shape
    n = b * s
    depth = p.shape[0]
    assert depth == DEPTH, "DEEPNORM_ALPHA is derived from DEPTH"
    n_a = depth // 2
    h = x.reshape(n, d)
    for i in range(depth):
        if i < n_a:
            h1, h1b = _fox_layer(h, b, s, fox_w_in[i], fox_b_f[i], fox_w_out[i], ln_g[i, 0], ln_b[i, 0])
        else:
            j = i - n_a
            h1, h1b = _nsa_layer(h, b, s, nsa_w_kv, cmp_pe, cmp_w1, cmp_b1, cmp_w2,
                                 nsa_w_q[j], nsa_b_g[j], nsa_w_out[j], ln_g[i, 0], ln_b[i, 0])
        h = _moe_layer(h1, h1b, p.reshape(depth, n, -1), i, moe_w_router[i], moe_b_router[i],
                       moe_w_gate, moe_w_up, moe_w_down,
                       shared_w_gate[i], shared_w_up[i], shared_w_down[i],
                       ple_w_proj[i], ple_w_gate[i], ple_b_gate[i],
                       ln_g[i, 1], ln_b[i, 1], ln_g[i, 2], ln_b[i, 2])
    return h.reshape(b, s, d)
```

```python
import functools

import numpy as np
import jax
import jax.numpy as jnp
from jax import lax
from jax.experimental import pallas as pl
from jax.experimental.pallas import tpu as pltpu

F32 = jnp.float32
BF16 = jnp.bfloat16
NEG = -1e30

HEAD_DIM = 64
LANES = 128
NSA_KV_GROUPS = 4
NSA_HEADS_PER_GROUP = 4
GROUP_LANES = NSA_KV_GROUPS * HEAD_DIM
CMP_BLOCK = 32
CMP_STRIDE = 16
SLC_BLOCK = 64
SLC_TOPK = 16
WINDOW = 512
EXPERT_TOPK = 8
EXPERT_GROUPS = 8
EXPERT_TOPK_GROUPS = 4
ROUTED_SCALE = 2.5
LN_EPS = 1e-5
DEPTH = 2
DEEPNORM_ALPHA = (2.0 * DEPTH) ** 0.25
VMEM_LIMIT = 52 * 1024 * 1024

ROW_TILE = 512
FOX_TILE = 2048
FOX_SUB = 128
NSA_Q_TILE = 256
NSA_SEL_TILE = 1024
NSA_WIN_TILE = 2048
NSA_SUB = 256
ROUTER_TILE = 512
EXPERT_ROW_BLOCK = 512
EXPERT_CHUNKS = 8


def _cparams(sem):
    return pltpu.CompilerParams(dimension_semantics=sem, vmem_limit_bytes=VMEM_LIMIT)


def _nt(a, b):
    return lax.dot_general(a, b, (((1,), (1,)), ((), ())), preferred_element_type=F32)


def _dot(a, b):
    return jnp.dot(a, b, preferred_element_type=F32)


def _split2(a):
    hi = a.astype(BF16)
    lo = (a - hi.astype(F32)).astype(BF16)
    return hi, lo


def _split3(a):
    hi = a.astype(BF16)
    r = a - hi.astype(F32)
    mid = r.astype(BF16)
    lo = (r - mid.astype(F32)).astype(BF16)
    return hi, mid, lo


def _sigmoid(x):
    return 1.0 / (1.0 + jnp.exp(-x))


def _layer_norm(z, g, b):
    mu = jnp.mean(z, axis=-1, keepdims=True)
    zc = z - mu
    var = jnp.mean(zc * zc, axis=-1, keepdims=True)
    return zc * lax.rsqrt(var + LN_EPS) * g + b


def _proj_kernel(x_ref, w_ref, cs_ref, wg_ref, bg_ref, o_ref, g_ref, *, tn, gate_sigmoid, gate_split):
    x = x_ref[...]
    xh = x.astype(BF16)
    m_out = w_ref.shape[1]
    for j in range(m_out // tn):
        sl = slice(j * tn, (j + 1) * tn)
        y = _dot(xh, w_ref[:, sl])
        o_ref[:, sl] = (y * cs_ref[:, sl]).astype(o_ref.dtype)
    xl = (x - xh.astype(F32)).astype(BF16)
    for i in range(wg_ref.shape[0]):
        wh, wl = _split2(wg_ref[i])
        g = _dot(xh, wh) + bg_ref[i]
        if gate_split:
            g = g + _dot(xh, wl) + _dot(xl, wh)
        if gate_sigmoid:
            g = _sigmoid(g)
        g_ref[i] = g


def _project(x2d, w_bf16, col_scale, w_gate, b_gate, gate_sigmoid, gate_split):
    n, k = x2d.shape
    m_out = w_bf16.shape[1]
    n_g = w_gate.shape[0]
    tm = min(ROW_TILE, n)
    return pl.pallas_call(
        functools.partial(_proj_kernel, tn=512, gate_sigmoid=gate_sigmoid, gate_split=gate_split),
        grid=(n // tm,),
        in_specs=[
            pl.BlockSpec((tm, k), lambda i: (i, 0)),
            pl.BlockSpec((k, m_out), lambda i: (0, 0)),
            pl.BlockSpec((1, m_out), lambda i: (0, 0)),
            pl.BlockSpec((n_g, k, LANES), lambda i: (0, 0, 0)),
            pl.BlockSpec((n_g, 1, LANES), lambda i: (0, 0, 0)),
        ],
        out_specs=[
            pl.BlockSpec((tm, m_out), lambda i: (i, 0)),
            pl.BlockSpec((n_g, tm, LANES), lambda i: (0, i, 0)),
        ],
        out_shape=[
            jax.ShapeDtypeStruct((n, m_out), BF16),
            jax.ShapeDtypeStruct((n_g, n, LANES), F32),
        ],
        compiler_params=_cparams(("arbitrary",)),
        name="project",
    )(x2d, w_bf16, col_scale, w_gate, b_gate)


LOG2E = 1.4426950408889634
GATE_PIECE_STRIDE = 16


def _gate_cumsum_kernel(z_ref, o_ref, carry_ref, *, ts, n_heads):
    @pl.when(pl.program_id(1) == 0)
    def _():
        carry_ref[...] = jnp.zeros_like(carry_ref)

    z = z_ref[0]
    log_f = jnp.minimum(z, 0.0) - jnp.log(1.0 + jnp.exp(-jnp.abs(z)))
    row = lax.broadcasted_iota(jnp.int32, (ts, ts), 0)
    col = lax.broadcasted_iota(jnp.int32, (ts, ts), 1)
    tri = jnp.where(col <= row, 1.0, 0.0).astype(BF16)
    hi, mid, lo = _split3(log_f)
    cs = _dot(tri, hi) + _dot(tri, mid) + _dot(tri, lo) + carry_ref[...]
    carry_ref[...] = cs[ts - 1:ts, :]
    lane = lax.broadcasted_iota(jnp.int32, (1, LANES), 1)
    bias = jnp.where(lane < n_heads, cs * (-LOG2E), 0.0)
    hi, mid, lo = _split3(bias)
    pieces = (hi.astype(F32) + pltpu.roll(mid.astype(F32), GATE_PIECE_STRIDE, 1)
              + pltpu.roll(lo.astype(F32), 2 * GATE_PIECE_STRIDE, 1))
    o_ref[0] = (pieces + pltpu.roll(pieces, HEAD_DIM, 1)).astype(o_ref.dtype)


def _gate_cumsum(z, n_heads):
    b, s, _ = z.shape
    ts = min(256, s)
    return pl.pallas_call(
        functools.partial(_gate_cumsum_kernel, ts=ts, n_heads=n_heads),
        grid=(b, s // ts),
        in_specs=[pl.BlockSpec((1, ts, LANES), lambda i, j: (i, j, 0))],
        out_specs=pl.BlockSpec((1, ts, LANES), lambda i, j: (i, j, 0)),
        out_shape=jax.ShapeDtypeStruct((b, s, LANES), BF16),
        scratch_shapes=[pltpu.VMEM((1, LANES), F32)],
        compiler_params=_cparams(("arbitrary", "arbitrary")),
        name="gate_cumsum",
    )(z)


def _flash_items(items):
    outs = []
    for key, s, v, m_prev, acc_prev in items:
        m_new = jnp.maximum(m_prev, jnp.max(s, axis=1, keepdims=True))
        alpha = jnp.exp2(m_prev - m_new)
        p = jnp.concatenate(
            [jnp.exp2(s[:, j * LANES:(j + 1) * LANES] - m_new[:, :LANES])
             for j in range(s.shape[1] // LANES)], axis=1)
        reps = acc_prev.shape[1] // LANES
        alpha = alpha if reps == 1 else jnp.concatenate([alpha] * reps, axis=1)
        outs.append((key, m_new, alpha * acc_prev + _dot(p.astype(BF16), v)))
    return outs


def _fox_kernel(qt_ref, kt_ref, q_ref, k_ref, v_ref, c_ref, o_ref, m_ref, acc_ref, *, t, sub):
    step = pl.program_id(2)
    pair = pl.program_id(1)
    qi = qt_ref[step]
    ki = kt_ref[step]

    @pl.when(ki == 0)
    def _():
        m_ref[...] = jnp.full_like(m_ref, NEG)
        acc_ref[...] = jnp.zeros_like(acc_ref)

    lane = lax.broadcasted_iota(jnp.int32, (1, LANES), 1)
    low = lane < HEAD_DIM

    def body(diag):
        q = q_ref[0]
        k = k_ref[0]
        v = v_ref[0]
        c = c_ref[0]
        items = []
        for h in range(2):
            own = low if h == 0 else jnp.logical_not(low)
            base = (1 - h) * HEAD_DIM + 2 * pair + h
            ones_at = ((lane == base) | (lane == base + GATE_PIECE_STRIDE)
                       | (lane == base + 2 * GATE_PIECE_STRIDE))
            qh = jnp.where(own, q, jnp.where(ones_at, 1.0, 0.0).astype(BF16))
            kh = jnp.where(own, k, c)
            vh = jnp.where(own, v, jnp.ones_like(v))
            for qs in range(t // sub):
                rows = slice(qs * sub, (qs + 1) * sub)
                nk = (qs + 1) * sub if diag else t
                s = _nt(qh[rows], kh[:nk])
                if diag:
                    r_i = lax.broadcasted_iota(jnp.int32, (sub, sub), 0)
                    c_i = lax.broadcasted_iota(jnp.int32, (sub, sub), 1)
                    last = jnp.where(c_i <= r_i, s[:, nk - sub:], NEG)
                    s = last if nk == sub else jnp.concatenate([s[:, :nk - sub], last], axis=1)
                items.append(((h, rows), s, vh[:nk], m_ref[h, rows, :], acc_ref[h, rows, :]))
        for (h, rows), m_new, acc_new in _flash_items(items):
            acc_ref[h, rows, :] = acc_new
            m_ref[h, rows, :] = m_new

    @pl.when(ki < qi)
    def _():
        body(False)

    @pl.when(ki == qi)
    def _():
        body(True)
        a0 = acc_ref[0]
        a1 = acc_ref[1]
        o0 = a0 / pltpu.roll(a0, HEAD_DIM, 1)
        o1 = a1 / pltpu.roll(a1, HEAD_DIM, 1)
        o_ref[0] = jnp.where(low, o0, o1).astype(o_ref.dtype)


def _fox_attention(qkv, gate_bias, n_heads):
    b, s, _ = qkv.shape
    t = min(FOX_TILE, s)
    nq = s // t
    pairs = n_heads // 2
    qt = np.concatenate([np.full((i + 1,), i, np.int32) for i in range(nq)])
    kt = np.concatenate([np.arange(i + 1, dtype=np.int32) for i in range(nq)])
    grid_spec = pltpu.PrefetchScalarGridSpec(
        num_scalar_prefetch=2,
        grid=(b, pairs, len(qt)),
        in_specs=[
            pl.BlockSpec((1, t, LANES), lambda i, j, u, qt, kt: (i, qt[u], j)),
            pl.BlockSpec((1, t, LANES), lambda i, j, u, qt, kt: (i, kt[u], pairs + j)),
            pl.BlockSpec((1, t, LANES), lambda i, j, u, qt, kt: (i, kt[u], 2 * pairs + j)),
            pl.BlockSpec((1, t, LANES), lambda i, j, u, qt, kt: (i, kt[u], 0)),
        ],
        out_specs=pl.BlockSpec((1, t, LANES), lambda i, j, u, qt, kt: (i, qt[u], j)),
        scratch_shapes=[
            pltpu.VMEM((2, t, LANES), F32),
            pltpu.VMEM((2, t, LANES), F32),
        ],
    )
    return pl.pallas_call(
        functools.partial(_fox_kernel, t=t, sub=min(FOX_SUB, t)),
        grid_spec=grid_spec,
        out_shape=jax.ShapeDtypeStruct((b, s, n_heads * HEAD_DIM), BF16),
        compiler_params=_cparams(("arbitrary", "arbitrary", "arbitrary")),
        name="fox_attention",
    )(jnp.asarray(qt), jnp.asarray(kt), qkv, qkv, qkv, gate_bias)


def _out_ln_kernel(*refs, n_in):
    o_refs = refs[:n_in]
    w_ref, x_ref, g_ref, b_ref, y_ref, yb_ref = refs[n_in:]
    if n_in == 1:
        o = o_refs[0][...]
    else:
        o = o_refs[0][...].astype(F32)
        for r in o_refs[1:]:
            o = o + r[...].astype(F32)
        o = o.astype(BF16)
    z = DEEPNORM_ALPHA * x_ref[...] + _dot(o, w_ref[...])
    y = _layer_norm(z, g_ref[...], b_ref[...])
    y_ref[...] = y
    yb_ref[...] = y.astype(BF16)


def _out_ln(o_list, w_bf16, x2d, g, b):
    n, d = x2d.shape
    k = w_bf16.shape[0]
    tm = min(ROW_TILE, n)
    n_in = len(o_list)
    row = lambda i: (i, 0)
    fixed = lambda i: (0, 0)
    return pl.pallas_call(
        functools.partial(_out_ln_kernel, n_in=n_in),
        grid=(n // tm,),
        in_specs=[pl.BlockSpec((tm, k), row)] * n_in + [
            pl.BlockSpec((k, d), fixed),
            pl.BlockSpec((tm, d), row),
            pl.BlockSpec((1, d), fixed),
            pl.BlockSpec((1, d), fixed),
        ],
        out_specs=[pl.BlockSpec((tm, d), row), pl.BlockSpec((tm, d), row)],
        out_shape=[jax.ShapeDtypeStruct((n, d), F32), jax.ShapeDtypeStruct((n, d), BF16)],
        compiler_params=_cparams(("arbitrary",)),
        name="out_ln",
    )(*o_list, w_bf16, x2d, g, b)


def _router_kernel(x_ref, wt_ref, b_ref, idx_ref, w_ref, rank_ref, cnt_ref, carry_ref, *, tn, n_exp):
    @pl.when(pl.program_id(0) == 0)
    def _():
        carry_ref[...] = jnp.zeros_like(carry_ref)

    xh, xl = _split2(x_ref[...])
    wh, wl = _split2(wt_ref[...])
    logits = _nt(wh, xh) + _nt(wh, xl) + _nt(wl, xh)
    scores = _sigmoid(logits)
    biased = scores + b_ref[...]

    per_grp = n_exp // EXPERT_GROUPS
    blocks, gscore = [], []
    for g in range(EXPERT_GROUPS):
        blk = biased[g * per_grp:(g + 1) * per_grp, :]
        m1 = jnp.max(blk, axis=0, keepdims=True)
        eq = blk == m1
        n_eq = jnp.sum(jnp.where(eq, 1.0, 0.0), axis=0, keepdims=True)
        m2 = jnp.max(jnp.where(eq, -3e38, blk), axis=0, keepdims=True)
        blocks.append(blk)
        gscore.append(m1 + jnp.where(n_eq >= 2.0, m1, m2))
    cand = []
    for g in range(EXPERT_GROUPS):
        beaten = jnp.zeros_like(gscore[g])
        for o in range(EXPERT_GROUPS):
            if o == g:
                continue
            wins = (gscore[o] >= gscore[g]) if o < g else (gscore[o] > gscore[g])
            beaten = beaten + jnp.where(wins, 1.0, 0.0)
        cand.append(jnp.where(beaten < float(EXPERT_TOPK_GROUPS), blocks[g], NEG))
    cand = jnp.concatenate(cand, axis=0)

    erow = lax.broadcasted_iota(jnp.int32, (n_exp, tn), 0).astype(F32)
    hot, sel_idx, sel_score = [], [], []
    member = jnp.zeros((n_exp, tn), F32)
    for _ in range(EXPERT_TOPK):
        m = jnp.max(cand, axis=0, keepdims=True)
        first = jnp.min(jnp.where(cand == m, erow, float(n_exp)), axis=0, keepdims=True)
        onehot = erow == first
        sel_idx.append(first)
        sel_score.append(jnp.sum(jnp.where(onehot, scores, 0.0), axis=0, keepdims=True))
        cand = jnp.where(onehot, -3e38, cand)
        member = member + jnp.where(onehot, 1.0, 0.0)
        hot.append(onehot)
    total = sel_score[0]
    for sc in sel_score[1:]:
        total = total + sc

    trow = lax.broadcasted_iota(jnp.int32, (tn, tn), 0)
    tcol = lax.broadcasted_iota(jnp.int32, (tn, tn), 1)
    before = jnp.where(trow < tcol, 1.0, 0.0).astype(BF16)
    prior = _dot(member.astype(BF16), before) + carry_ref[...]
    ranks = [jnp.sum(jnp.where(hot[k], prior, 0.0), axis=0, keepdims=True) for k in range(EXPERT_TOPK)]

    idx_ref[...] = jnp.concatenate(sel_idx, axis=0).astype(jnp.int32)
    w_ref[...] = jnp.concatenate([sc / total * ROUTED_SCALE for sc in sel_score], axis=0)
    rank_ref[...] = jnp.concatenate(ranks, axis=0).astype(jnp.int32)
    carry_ref[...] = carry_ref[...] + jnp.sum(member, axis=1, keepdims=True)
    cnt_ref[...] = carry_ref[...]


def _route(x2d, w_router_t, b_router):
    n, d = x2d.shape
    n_exp = w_router_t.shape[0]
    tn = min(ROUTER_TILE, n)
    col = lambda i: (0, i)
    return pl.pallas_call(
        functools.partial(_router_kernel, tn=tn, n_exp=n_exp),
        grid=(n // tn,),
        in_specs=[
            pl.BlockSpec((tn, d), lambda i: (i, 0)),
            pl.BlockSpec((n_exp, d), lambda i: (0, 0)),
            pl.BlockSpec((n_exp, 1), lambda i: (0, 0)),
        ],
        out_specs=[
            pl.BlockSpec((EXPERT_TOPK, tn), col),
            pl.BlockSpec((EXPERT_TOPK, tn), col),
            pl.BlockSpec((EXPERT_TOPK, tn), col),
            pl.BlockSpec((n_exp, 1), lambda i: (0, 0)),
        ],
        out_shape=[
            jax.ShapeDtypeStruct((EXPERT_TOPK, n), jnp.int32),
            jax.ShapeDtypeStruct((EXPERT_TOPK, n), F32),
            jax.ShapeDtypeStruct((EXPERT_TOPK, n), jnp.int32),
            jax.ShapeDtypeStruct((n_exp, 1), F32),
        ],
        scratch_shapes=[pltpu.VMEM((n_exp, 1), F32)],
        compiler_params=_cparams(("arbitrary",)),
        name="router",
    )(x2d, w_router_t, b_router)


def _experts_kernel(be_ref, first_ref, nused_ref, x_ref, wg_ref, wu_ref, wd_ref, *rest):
    y_ref, wg_s, wu_s, wd_s = rest[-4:]
    blk = pl.program_id(0)

    @pl.when(blk < nused_ref[0])
    def _():
        @pl.when((first_ref[blk] == 1) | (blk == 0))
        def _():
            wg_s[...] = wg_ref[0, 0].astype(BF16)
            wu_s[...] = wu_ref[0, 0].astype(BF16)
            wd_s[...] = wd_ref[0, 0].astype(BF16)

        x = x_ref[...]
        gate = _dot(x, wg_s[...])
        up = _dot(x, wu_s[...])
        h = gate * _sigmoid(gate) * up
        y_ref[...] = _dot(h.astype(BF16), wd_s[...]).astype(y_ref.dtype)

    @pl.when(blk >= nused_ref[0])
    def _():
        y_ref[...] = jnp.zeros_like(y_ref)


def _experts(xg, blk_expert, blk_first, n_used, w_gate, w_up, w_down, layer, y_prev, first_block, n_rows):
    d = xg.shape[1]
    hdim = w_gate.shape[3]
    tm = EXPERT_ROW_BLOCK
    n_blocks = xg.shape[0] // tm
    extra_specs, extra_args, aliases = [], [], {}
    if y_prev is not None:
        extra_specs, extra_args, aliases = [pl.BlockSpec(memory_space=pl.ANY)], [y_prev], {7: 0}
    grid_spec = pltpu.PrefetchScalarGridSpec(
        num_scalar_prefetch=3,
        grid=(n_blocks,),
        in_specs=[
            pl.BlockSpec((tm, d), lambda i, be, fi, nu: (jnp.minimum(i, jnp.maximum(nu[0] - 1, 0)), 0)),
            pl.BlockSpec((1, 1, d, hdim), lambda i, be, fi, nu: (layer, be[i], 0, 0)),
            pl.BlockSpec((1, 1, d, hdim), lambda i, be, fi, nu: (layer, be[i], 0, 0)),
            pl.BlockSpec((1, 1, hdim, d), lambda i, be, fi, nu: (layer, be[i], 0, 0)),
        ] + extra_specs,
        out_specs=pl.BlockSpec((tm, d), lambda i, be, fi, nu: (i + first_block, 0)),
        scratch_shapes=[
            pltpu.VMEM((d, hdim), BF16),
            pltpu.VMEM((d, hdim), BF16),
            pltpu.VMEM((hdim, d), BF16),
        ],
    )
    return pl.pallas_call(
        _experts_kernel,
        grid_spec=grid_spec,
        out_shape=jax.ShapeDtypeStruct((n_rows, d), BF16),
        input_output_aliases=aliases,
        compiler_params=_cparams(("arbitrary",)),
        name="experts",
    )(blk_expert, blk_first, n_used, xg, w_gate, w_up, w_down, *extra_args)


def _moe_tail_kernel(x_ref, r_ref, rw_ref, p_ref, sg_ref, su_ref, sd_ref, pg_ref, pb_ref, pp_ref,
                     g1_ref, b1_ref, g2_ref, b2_ref, y_ref):
    x = x_ref[...]
    xb = x.astype(BF16)
    gate = _dot(xb, sg_ref[...])
    up = _dot(xb, su_ref[...])
    shared = _dot((gate * _sigmoid(gate) * up).astype(BF16), sd_ref[...])
    rw = rw_ref[...]
    routed = r_ref[0].astype(F32) * rw[:, 0:1]
    for k in range(1, r_ref.shape[0]):
        routed = routed + r_ref[k].astype(F32) * rw[:, k:k + 1]
    x2 = _layer_norm(DEEPNORM_ALPHA * x + (routed + shared), g1_ref[...], b1_ref[...])
    ple_gate = _sigmoid(_dot(x2.astype(BF16), pg_ref[...]) + pb_ref[...])
    ple = ple_gate * _dot(p_ref[0].astype(BF16), pp_ref[...])
    y_ref[...] = _layer_norm(DEEPNORM_ALPHA * x2 + ple, g2_ref[...], b2_ref[...])


def _moe_tail(x2d, routed, routed_w, p_all, layer, sg, su, sd, pg, pb, pp, g1, b1, g2, b2):
    n, d = x2d.shape
    tm = min(ROW_TILE, n)
    topk = routed.shape[0]
    row = lambda i: (i, 0)
    fixed = lambda i: (0, 0)
    full = lambda a: pl.BlockSpec(a.shape, fixed)
    return pl.pallas_call(
        _moe_tail_kernel,
        grid=(n // tm,),
        in_specs=[
            pl.BlockSpec((tm, d), row),
            pl.BlockSpec((topk, tm, d), lambda i: (0, i, 0)),
            pl.BlockSpec((tm, topk), row),
            pl.BlockSpec((1, tm, p_all.shape[2]), lambda i: (layer, i, 0)),
            full(sg), full(su), full(sd), full(pg), full(pb), full(pp),
            full(g1), full(b1), full(g2), full(b2),
        ],
        out_specs=pl.BlockSpec((tm, d), row),
        out_shape=jax.ShapeDtypeStruct((n, d), F32),
        compiler_params=_cparams(("arbitrary",)),
        name="moe_tail",
    )(x2d, routed, routed_w, p_all, sg, su, sd, pg, pb, pp, g1, b1, g2, b2)


def _moe_layer(x1, x1b, p_all, layer, w_router, b_router, w_gate, w_up, w_down, sg, su, sd,
               ple_proj, ple_gate, ple_bias, g1, b1, g2, b2):
    n, d = x1.shape
    n_exp = w_router.shape[1]
    idx, wts, rank, counts = _route(x1, w_router.T, b_router.reshape(n_exp, 1))
    tm = EXPERT_ROW_BLOCK
    counts = counts[:, 0].astype(jnp.int32)
    padded = (counts + tm - 1) // tm * tm
    pad_end = jnp.cumsum(padded)
    pad_start = pad_end - padded
    n_blocks = (n * EXPERT_TOPK + n_exp * (tm - 1)) // tm + 1
    n_blocks = -(-n_blocks // EXPERT_CHUNKS) * EXPERT_CHUNKS
    n_rows = n_blocks * tm
    experts = jnp.arange(n_exp, dtype=jnp.int32)
    start_of = jnp.sum(jnp.where(idx[:, :, None] == experts, pad_start, 0), axis=-1)
    pos = start_of + rank
    stride = n + tm
    past_end = 2 ** 30
    assert n_exp * stride <= past_end and past_end + n_rows + n_exp * tm < 2 ** 31
    tok = jnp.arange(n, dtype=jnp.int32)[None, :]
    filler = jnp.arange(tm, dtype=jnp.int32)[None, :]
    filler_key = jnp.where(filler < (padded - counts)[:, None],
                           experts[:, None] * stride + n + filler,
                           past_end + experts[:, None] * tm + filler)
    n_tail = max(n_rows - (n * EXPERT_TOPK + n_exp * tm), 0)
    keys = jnp.concatenate([(idx * stride + tok).reshape(-1), filler_key.reshape(-1),
                            past_end + n_exp * tm + jnp.arange(n_tail, dtype=jnp.int32)])
    keys = lax.sort(keys, is_stable=False)[:n_rows]
    slot = keys % stride
    tok_pad = jnp.where(keys >= past_end, (keys - past_end) % n,
                        jnp.where(slot < n, slot, (keys // stride * tm + slot - n) % n))
    blk_start = jnp.arange(n_blocks, dtype=jnp.int32) * tm
    blk_expert = jnp.minimum(jnp.sum((pad_end[None, :] <= blk_start[:, None]).astype(jnp.int32), axis=1),
                             n_exp - 1)
    blk_first = jnp.concatenate([jnp.ones((1,), jnp.int32),
                                 (blk_expert[1:] != blk_expert[:-1]).astype(jnp.int32)])
    n_used = (pad_end[-1:] // tm).astype(jnp.int32)
    per = n_blocks // EXPERT_CHUNKS
    y = None
    for c in range(EXPERT_CHUNKS):
        blocks = slice(c * per, (c + 1) * per)
        xg = x1b.at[tok_pad[c * per * tm:(c + 1) * per * tm]].get(mode="promise_in_bounds")
        y = _experts(xg, blk_expert[blocks], blk_first[blocks], jnp.clip(n_used - c * per, 0, per),
                     w_gate, w_up, w_down, layer, y, c * per, n_rows)
    routed = y.at[pos].get(mode="promise_in_bounds")
    return _moe_tail(x1, routed, wts.T, p_all, layer, sg.astype(BF16), su.astype(BF16), sd.astype(BF16),
                     ple_gate.astype(BF16), ple_bias.reshape(1, d), ple_proj.astype(BF16),
                     g1.reshape(1, d), b1.reshape(1, d), g2.reshape(1, d), b2.reshape(1, d))


def _gelu_tanh(x):
    return 0.5 * x * (1.0 + jnp.tanh(0.7978845608028654 * (x + 0.044715 * (x * x * x))))


def _compress_kernel(c_ref, pe_ref, w1_ref, b1_ref, w2_ref, o_ref, *, nc, half):
    out = jnp.zeros((nc, GROUP_LANES), F32)
    for g in range(NSA_KV_GROUPS):
        c = c_ref[0, 0, g].astype(F32)
        a = _dot((c + pe_ref[0, 0:1, :]).astype(BF16), w1_ref[0, :half, :])
        bm = _dot((c + pe_ref[0, 1:2, :]).astype(BF16), w1_ref[0, half:, :])
        h = _gelu_tanh(a + pltpu.roll(bm, nc - 1, 0) + b1_ref[0])
        out = out + _dot(h.astype(BF16), w2_ref[0, g])
    o_ref[0, 0] = out.astype(o_ref.dtype)


def _compress(chunks, pe_flat, w1, b1, w2_placed):
    b, _, _, nc, half = chunks.shape
    hid = w1.shape[2]
    return pl.pallas_call(
        functools.partial(_compress_kernel, nc=nc, half=half),
        grid=(b, 2),
        in_specs=[
            pl.BlockSpec((1, 1, NSA_KV_GROUPS, nc, half), lambda i, j: (i, j, 0, 0, 0)),
            pl.BlockSpec((1, 2, half), lambda i, j: (j, 0, 0)),
            pl.BlockSpec((1, 2 * half, hid), lambda i, j: (j, 0, 0)),
            pl.BlockSpec((1, 1, hid), lambda i, j: (j, 0, 0)),
            pl.BlockSpec((1, NSA_KV_GROUPS, hid, GROUP_LANES), lambda i, j: (j, 0, 0, 0)),
        ],
        out_specs=pl.BlockSpec((1, 1, nc, GROUP_LANES), lambda i, j: (i, j, 0, 0)),
        out_shape=jax.ShapeDtypeStruct((b, 2, nc, GROUP_LANES), BF16),
        compiler_params=_cparams(("arbitrary", "arbitrary")),
        name="compress",
    )(chunks, pe_flat, w1, b1, w2_placed)


def _group_mask(g):
    lane = lax.broadcasted_iota(jnp.int32, (1, GROUP_LANES), 1)
    return (lane >= g * HEAD_DIM) & (lane < (g + 1) * HEAD_DIM)


def _topk_rank(score):
    n, t = score.shape
    sub8 = lax.broadcasted_iota(jnp.int32, (8, t), 0)
    ahead = jnp.zeros((n, t), F32)
    for j in range(n):
        vj = score[j:j + 1, :]
        lo_r, hi_r = (j // 8) * 8, (j // 8) * 8 + 8
        parts = []
        if lo_r > 0:
            parts.append(jnp.where(vj > score[:lo_r], 1.0, 0.0))
        own = score[lo_r:hi_r]
        parts.append(jnp.where(sub8 > (j % 8), jnp.where(vj >= own, 1.0, 0.0),
                               jnp.where(vj > own, 1.0, 0.0)))
        if hi_r < n:
            parts.append(jnp.where(vj >= score[hi_r:], 1.0, 0.0))
        ahead = ahead + jnp.concatenate(parts, axis=0)
    return ahead


def _nsa_cmp_kernel(q_ref, kc_ref, vc_ref, gate_ref, slope_ref, wt_ref, o_ref, sel_ref,
                    *, tq, nc, n_slc):
    qi = pl.program_id(1)
    q0 = qi * tq
    kc = kc_ref[0, 0]
    vc = vc_ref[0, 0]
    t_pos = q0 + lax.broadcasted_iota(jnp.int32, (tq, nc), 0)
    cmp_end = lax.broadcasted_iota(jnp.int32, (tq, nc), 1) * CMP_STRIDE + (CMP_BLOCK - 1)
    visible = cmp_end <= t_pos
    end_rel = (lax.broadcasted_iota(jnp.int32, (1, nc), 1) * CMP_STRIDE + (CMP_BLOCK - 1) - q0).astype(F32)
    v_aug = [jnp.where(_group_mask(g), vc, jnp.ones_like(vc)) for g in range(NSA_KV_GROUPS)]
    psum = [jnp.zeros((tq, nc), F32) for _ in range(NSA_KV_GROUPS)]
    for r in range(NSA_HEADS_PER_GROUP):
        q = q_ref[0, :, r * GROUP_LANES:(r + 1) * GROUP_LANES]
        gates = gate_ref[r, 0]
        out = jnp.zeros((tq, GROUP_LANES), F32)
        for g in range(NSA_KV_GROUPS):
            in_grp = _group_mask(g)
            qm = jnp.where(in_grp, q, jnp.zeros_like(q))
            s = _nt(qm, kc) + slope_ref[r, g, :, :nc] * end_rel
            s = jnp.where(visible, s, NEG)
            m = jnp.max(s, axis=1, keepdims=True)
            p = jnp.where(visible, jnp.exp2(s - m), 0.0)
            o_g = _dot(p.astype(BF16), v_aug[g])
            other = (1 - g // 2) * LANES
            l = o_g[:, other:other + LANES]
            inv = jnp.where(l > 0.0, 1.0 / l, 0.0)
            inv2 = jnp.concatenate([inv, inv], axis=1)
            psum[g] = psum[g] + (p * inv2[:, :nc] if nc == GROUP_LANES else p * inv[:, :1])
            out = out + jnp.where(in_grp, o_g * (inv2 * gates[:, g:g + 1]), 0.0)
        o_ref[0, :, r * GROUP_LANES:(r + 1) * GROUP_LANES] = out.astype(o_ref.dtype)

    blk = lax.broadcasted_iota(jnp.int32, (n_slc, tq), 0)
    t_row = q0 + lax.broadcasted_iota(jnp.int32, (n_slc, tq), 1)
    cur = t_row // SLC_BLOCK
    forced = (blk == 0) | (blk == cur) | (blk == cur - 1)
    vis = blk * SLC_BLOCK <= t_row
    wt = wt_ref[...]
    pad_rows = jnp.zeros((LANES - n_slc, tq), F32)
    for g in range(NSA_KV_GROUPS):
        hi, mid, lo = _split3(psum[g])
        imp = _nt(wt, hi) + _nt(wt, mid) + _nt(wt, lo)
        score = jnp.where(forced, 1e9, jnp.where(vis, imp, NEG))
        keep = (_topk_rank(score) < float(min(SLC_TOPK, n_slc))) & (score > 0.5 * NEG)
        flags = jnp.concatenate([jnp.where(keep, 0.0, NEG), pad_rows], axis=0)
        sel_ref[0, g] = flags.T.astype(sel_ref.dtype)


def _nsa_compressed(q, cmp_kv, gates_c, slopes_rep, w_sel_t):
    b, s, _ = q.shape
    nc = cmp_kv.shape[2]
    n_slc = s // SLC_BLOCK
    tq = min(NSA_Q_TILE, s)
    rr = NSA_HEADS_PER_GROUP
    return pl.pallas_call(
        functools.partial(_nsa_cmp_kernel, tq=tq, nc=nc, n_slc=n_slc),
        grid=(b, s // tq),
        in_specs=[
            pl.BlockSpec((1, tq, rr * GROUP_LANES), lambda i, j: (i, j, 0)),
            pl.BlockSpec((1, 1, nc, GROUP_LANES), lambda i, j: (i, 0, 0, 0)),
            pl.BlockSpec((1, 1, nc, GROUP_LANES), lambda i, j: (i, 1, 0, 0)),
            pl.BlockSpec((rr, 1, tq, LANES), lambda i, j: (0, i, j, 0)),
            pl.BlockSpec((rr, NSA_KV_GROUPS, 1, slopes_rep.shape[3]), lambda i, j: (0, 0, 0, 0)),
            pl.BlockSpec((n_slc, nc), lambda i, j: (0, 0)),
        ],
        out_specs=[
            pl.BlockSpec((1, tq, rr * GROUP_LANES), lambda i, j: (i, j, 0)),
            pl.BlockSpec((1, NSA_KV_GROUPS, tq, LANES), lambda i, j: (i, 0, j, 0)),
        ],
        out_shape=[
            jax.ShapeDtypeStruct((b, s, rr * GROUP_LANES), BF16),
            jax.ShapeDtypeStruct((b, NSA_KV_GROUPS, s, LANES), BF16),
        ],
        compiler_params=_cparams(("arbitrary", "arbitrary")),
        name="nsa_compressed",
    )(q, cmp_kv, cmp_kv, gates_c, slopes_rep, w_sel_t)


AUG_POS_LANE = 64


def _nsa_query_aug(q, slope_row, sel, g, lane):
    t = q.shape[0]
    half = g // 2
    in_grp = (lane >= (g % 2) * HEAD_DIM) & (lane < (g % 2 + 1) * HEAD_DIM)
    q_own = jnp.where(in_grp, q[:, half * LANES:(half + 1) * LANES], jnp.zeros((t, LANES), BF16))
    q_extra = jnp.broadcast_to(slope_row, (t, LANES))
    if sel is not None:
        q_extra = jnp.where(lane < AUG_POS_LANE, sel, q_extra)
    return jnp.concatenate([q_own, q_extra] if half == 0 else [q_extra, q_own], axis=1)


def _nsa_key_operands(k, ka):
    return (jnp.concatenate([k[:, :LANES], ka], axis=1), jnp.concatenate([ka, k[:, LANES:]], axis=1))


def _nsa_selected_kernel(qt_ref, kt_ref, q_ref, k_ref, v_ref, ka_ref, gate_ref, slope_ref, sel_ref, o_ref,
                         m_ref, acc_ref, *, t, sub):
    step = pl.program_id(2)
    qi = qt_ref[step]
    ki = kt_ref[step]

    @pl.when(ki == 0)
    def _():
        m_ref[...] = jnp.full_like(m_ref, NEG)
        acc_ref[...] = jnp.zeros_like(acc_ref)

    lane = lax.broadcasted_iota(jnp.int32, (1, LANES), 1)

    def body(diag):
        q = q_ref[0]
        v = v_ref[0]
        k_aug = _nsa_key_operands(k_ref[0], ka_ref[...])
        items = []
        for g in range(NSA_KV_GROUPS):
            qg = _nsa_query_aug(q, slope_ref[0, g], sel_ref[0, g], g, lane)
            vg = jnp.where(_group_mask(g), v, jnp.ones_like(v))
            for qs in range(t // sub):
                rows = slice(qs * sub, (qs + 1) * sub)
                keys = slice(0, (qs + 1) * sub if diag else t)
                s = _nt(qg[rows], k_aug[g // 2][keys])
                if diag:
                    r_i = lax.broadcasted_iota(jnp.int32, (sub, sub), 0)
                    c_i = lax.broadcasted_iota(jnp.int32, (sub, sub), 1)
                    edge = jnp.where(c_i <= r_i, s[:, -sub:], NEG)
                    s = edge if s.shape[1] == sub else jnp.concatenate([s[:, :-sub], edge], axis=1)
                items.append(((g, rows), s, vg[keys], m_ref[g, rows, :], acc_ref[g, rows, :]))
        for (g, rows), m_new, acc_new in _flash_items(items):
            acc_ref[g, rows, :] = acc_new
            m_ref[g, rows, :] = m_new

    @pl.when(ki < qi)
    def _():
        body(False)

    @pl.when(ki == qi)
    def _():
        body(True)
        gates = gate_ref[0, 0]
        out = jnp.zeros((t, GROUP_LANES), F32)
        for g in range(NSA_KV_GROUPS):
            acc = acc_ref[g]
            denom = pltpu.roll(acc, LANES, 1)
            col = NSA_KV_GROUPS + g
            out = out + jnp.where(_group_mask(g), acc * (gates[:, col:col + 1] / denom), 0.0)
        o_ref[0] = out.astype(o_ref.dtype)


def _nsa_selected(q, kv, k_col, v_col, key_aug, gates, slope_rows, sel_q):
    b, s, _ = q.shape
    t = min(NSA_SEL_TILE, s)
    nq = s // t
    qt = np.concatenate([np.full((i + 1,), i, np.int32) for i in range(nq)])
    kt = np.concatenate([np.arange(i + 1, dtype=np.int32) for i in range(nq)])
    rr = NSA_HEADS_PER_GROUP
    grid_spec = pltpu.PrefetchScalarGridSpec(
        num_scalar_prefetch=2,
        grid=(b, rr, len(qt)),
        in_specs=[
            pl.BlockSpec((1, t, GROUP_LANES), lambda i, r, u, qt, kt: (i, qt[u], r)),
            pl.BlockSpec((1, t, GROUP_LANES), lambda i, r, u, qt, kt: (i, kt[u], k_col)),
            pl.BlockSpec((1, t, GROUP_LANES), lambda i, r, u, qt, kt: (i, kt[u], v_col)),
            pl.BlockSpec((t, LANES), lambda i, r, u, qt, kt: (kt[u], 0)),
            pl.BlockSpec((1, 1, t, LANES), lambda i, r, u, qt, kt: (r, i, qt[u], 0)),
            pl.BlockSpec((1, NSA_KV_GROUPS, 1, LANES), lambda i, r, u, qt, kt: (r, 0, 0, 0)),
            pl.BlockSpec((1, NSA_KV_GROUPS, t, LANES), lambda i, r, u, qt, kt: (i, 0, qt[u], 0)),
        ],
        out_specs=pl.BlockSpec((1, t, GROUP_LANES), lambda i, r, u, qt, kt: (i, qt[u], r)),
        scratch_shapes=[
            pltpu.VMEM((NSA_KV_GROUPS, t, LANES), F32),
            pltpu.VMEM((NSA_KV_GROUPS, t, GROUP_LANES), F32),
        ],
    )
    return pl.pallas_call(
        functools.partial(_nsa_selected_kernel, t=t, sub=min(NSA_SUB, t)),
        grid_spec=grid_spec,
        out_shape=jax.ShapeDtypeStruct((b, s, rr * GROUP_LANES), BF16),
        compiler_params=_cparams(("arbitrary", "arbitrary", "arbitrary")),
        name="nsa_selected",
    )(jnp.asarray(qt), jnp.asarray(kt), q, kv, kv, key_aug, gates, slope_rows, sel_q)


def _nsa_window_kernel(q_ref, kp_ref, kc_ref, vp_ref, vc_ref, kap_ref, kac_ref, gate_ref, slope_ref, o_ref,
                       *, t, sub):
    has_prev = pl.program_id(2) > 0
    lane = lax.broadcasted_iota(jnp.int32, (1, LANES), 1)
    q = q_ref[0]
    k = jnp.concatenate([kp_ref[0], kc_ref[0]], axis=0)
    v = jnp.concatenate([vp_ref[0], vc_ref[0]], axis=0)
    k_aug = _nsa_key_operands(k, jnp.concatenate([kap_ref[...], kac_ref[...]], axis=0))
    gates = gate_ref[0, 0]
    span = WINDOW + sub
    r_i = lax.broadcasted_iota(jnp.int32, (sub, sub), 0)
    c_i = lax.broadcasted_iota(jnp.int32, (sub, sub), 1)
    out_rows = [jnp.zeros((sub, GROUP_LANES), F32) for _ in range(t // sub)]
    for g in range(NSA_KV_GROUPS):
        half = g // 2
        qg = _nsa_query_aug(q, slope_ref[0, g], None, g, lane)
        vg = jnp.where(_group_mask(g), v, jnp.ones_like(v))
        col = 2 * NSA_KV_GROUPS + g
        for qs in range(t // sub):
            rows = slice(qs * sub, (qs + 1) * sub)
            keys = slice(qs * sub, qs * sub + span)
            s = _nt(qg[rows], k_aug[half][keys])
            chunks = [s[:, j * sub:(j + 1) * sub] for j in range(span // sub)]
            chunks[0] = jnp.where(c_i > r_i, chunks[0], NEG)
            chunks[-1] = jnp.where(c_i <= r_i, chunks[-1], NEG)
            for j in range(len(chunks) - 1):
                if qs * sub + j * sub < WINDOW:
                    chunks[j] = jnp.where(has_prev, chunks[j], NEG)
            m = chunks[0]
            for c in chunks[1:]:
                m = jnp.maximum(m, c)
            m = jnp.max(m, axis=1, keepdims=True)
            p = jnp.concatenate([jnp.exp2(c - m) for c in chunks], axis=1)
            acc = _dot(p.astype(BF16), vg[keys])
            denom = pltpu.roll(acc, LANES, 1)
            out_rows[qs] = out_rows[qs] + jnp.where(
                _group_mask(g), acc * (gates[rows, col:col + 1] / denom), 0.0)
    o_ref[0] = jnp.concatenate(out_rows, axis=0).astype(o_ref.dtype)


def _nsa_window(q, kv, k_col, v_col, key_aug, gates, slope_rows):
    b, s, _ = q.shape
    t = min(NSA_WIN_TILE, s)
    sub = min(NSA_SUB, t)
    assert WINDOW % sub == 0 and t % WINDOW == 0
    rr = NSA_HEADS_PER_GROUP
    per = t // WINDOW
    prev = lambda j: jnp.maximum(j * per - 1, 0)
    return pl.pallas_call(
        functools.partial(_nsa_window_kernel, t=t, sub=sub),
        grid=(b, rr, s // t),
        in_specs=[
            pl.BlockSpec((1, t, GROUP_LANES), lambda i, r, j: (i, j, r)),
            pl.BlockSpec((1, WINDOW, GROUP_LANES), lambda i, r, j: (i, prev(j), k_col)),
            pl.BlockSpec((1, t, GROUP_LANES), lambda i, r, j: (i, j, k_col)),
            pl.BlockSpec((1, WINDOW, GROUP_LANES), lambda i, r, j: (i, prev(j), v_col)),
            pl.BlockSpec((1, t, GROUP_LANES), lambda i, r, j: (i, j, v_col)),
            pl.BlockSpec((WINDOW, LANES), lambda i, r, j: (prev(j), 0)),
            pl.BlockSpec((t, LANES), lambda i, r, j: (j, 0)),
            pl.BlockSpec((1, 1, t, LANES), lambda i, r, j: (r, i, j, 0)),
            pl.BlockSpec((1, NSA_KV_GROUPS, 1, LANES), lambda i, r, j: (r, 0, 0, 0)),
        ],
        out_specs=pl.BlockSpec((1, t, GROUP_LANES), lambda i, r, j: (i, j, r)),
        out_shape=jax.ShapeDtypeStruct((b, s, rr * GROUP_LANES), BF16),
        compiler_params=_cparams(("arbitrary", "arbitrary", "arbitrary")),
        name="nsa_window",
    )(q, kv, kv, kv, kv, key_aug, key_aug, gates, slope_rows)


def _nsa_key_aug(s):
    pos = np.arange(s)
    aug = np.zeros((s, LANES), np.float32)
    aug[pos, pos // SLC_BLOCK] = 1.0
    aug[:, AUG_POS_LANE:AUG_POS_LANE + 3] = (pos // 64)[:, None]
    aug[:, AUG_POS_LANE + 3:AUG_POS_LANE + 6] = (pos % 64)[:, None]
    return jnp.asarray(aug, BF16)


def _nsa_slope_rows(slopes_l2):
    rr, gg = slopes_l2.shape
    hi, mid, lo = _split3(jnp.asarray(slopes_l2, F32))
    pieces = jnp.stack([hi, mid, lo], axis=-1).astype(F32)
    rows = jnp.zeros((rr, gg, 1, LANES), F32)
    rows = rows.at[:, :, 0, AUG_POS_LANE:AUG_POS_LANE + 3].set(pieces * 64.0)
    rows = rows.at[:, :, 0, AUG_POS_LANE + 3:AUG_POS_LANE + 6].set(pieces)
    return rows.astype(BF16)


def _pad_lanes(a, width=LANES):
    return jnp.pad(a, ((0, 0), (0, width - a.shape[1])))


def _fox_layer(x2d, b, s, w_in, b_f, w_out, ln_g, ln_b):
    d = x2d.shape[1]
    attn = w_out.shape[0]
    n_heads = attn // HEAD_DIM
    assert n_heads <= GATE_PIECE_STRIDE, "gate bias pieces of different heads would share lanes"
    scale = HEAD_DIM ** -0.5 * LOG2E
    col_scale = jnp.concatenate([jnp.full((attn,), scale, F32), jnp.ones((2 * attn,), F32)])[None, :]
    qkv, z = _project(x2d, w_in[:, :3 * attn].astype(BF16), col_scale,
                      _pad_lanes(w_in[:, 3 * attn:])[None], _pad_lanes(b_f[None, :])[None], False, True)
    gate_bias = _gate_cumsum(z.reshape(b, s, LANES), n_heads)
    o = _fox_attention(qkv.reshape(b, s, 3 * attn), gate_bias, n_heads)
    return _out_ln([o.reshape(b * s, attn)], w_out.astype(BF16), x2d,
                   ln_g.reshape(1, d), ln_b.reshape(1, d))


def _selection_weights_t(nc, n_slc):
    cs = np.arange(nc)[:, None] * CMP_STRIDE
    ce = cs + CMP_BLOCK
    ss = np.arange(n_slc)[None, :] * SLC_BLOCK
    se = ss + SLC_BLOCK
    w = np.clip(np.minimum(ce, se) - np.maximum(cs, ss), 0, None) / CMP_STRIDE
    w[nc - 1, :] = 0.0
    return jnp.asarray(w.T, BF16)


def _nsa_layer(x2d, b, s, w_kv, cmp_pe, cmp_w1, cmp_b1, cmp_w2, w_q, b_g, w_out, ln_g, ln_b):
    d = x2d.shape[1]
    gg, rr = NSA_KV_GROUPS, NSA_HEADS_PER_GROUP
    attn = gg * rr * HEAD_DIM
    n_heads = gg * rr
    wq = w_q[:, :attn].reshape(d, gg, rr, HEAD_DIM).transpose(0, 2, 1, 3).reshape(d, attn)
    wgate = w_q[:, attn:].reshape(d, gg, rr, 3).transpose(2, 0, 3, 1).reshape(rr, d, 3 * gg)
    bgate = b_g.reshape(gg, rr, 3).transpose(1, 2, 0).reshape(rr, 1, 3 * gg)
    wgate = jnp.pad(wgate, ((0, 0), (0, 0), (0, LANES - 3 * gg)))
    bgate = jnp.pad(bgate, ((0, 0), (0, 0), (0, LANES - 3 * gg)))
    wo = w_out.reshape(gg, rr, HEAD_DIM, d).transpose(1, 0, 2, 3).reshape(attn, d)
    w_all = jnp.concatenate([wq, w_kv], axis=1).astype(BF16)
    col_scale = jnp.concatenate([jnp.full((attn,), HEAD_DIM ** -0.5 * LOG2E, F32),
                                 jnp.ones((w_kv.shape[1],), F32)])[None, :]
    qkv, gates = _project(x2d, w_all, col_scale, wgate, bgate, True, False)
    width = qkv.shape[1]
    qkv = qkv.reshape(b, s, width)
    gates = gates.reshape(rr, b, s, LANES)

    nc = s // CMP_STRIDE
    half = CMP_STRIDE * HEAD_DIM
    raw = qkv[:, :, attn:attn + 2 * GROUP_LANES]
    chunks = raw.reshape(b, nc, CMP_STRIDE, 2, gg, HEAD_DIM).transpose(0, 3, 4, 1, 2, 5)
    chunks = chunks.reshape(b, 2, gg, nc, half)
    pe_flat = cmp_pe.reshape(2, 2, half)
    hid = cmp_w1.shape[2]
    w2_placed = jnp.zeros((2, gg, hid, GROUP_LANES), F32)
    for g in range(gg):
        w2_placed = w2_placed.at[:, g, :, g * HEAD_DIM:(g + 1) * HEAD_DIM].set(cmp_w2)
    cmp_kv = _compress(chunks, pe_flat, cmp_w1.astype(BF16), cmp_b1.reshape(2, 1, hid),
                       w2_placed.astype(BF16))

    slopes_l2 = (2.0 ** (-8.0 * np.arange(1, n_heads + 1) / n_heads) * LOG2E).reshape(gg, rr).T
    slopes_rep = jnp.asarray(np.broadcast_to(
        slopes_l2[:, :, None, None], (rr, gg, 1, nc)).astype(np.float32))
    n_slc = s // SLC_BLOCK
    assert n_slc <= AUG_POS_LANE, "selection flags must fit below the position lanes"
    o_c, sel_q = _nsa_compressed(qkv, cmp_kv, gates, slopes_rep, _selection_weights_t(nc, n_slc))
    key_aug = _nsa_key_aug(s)
    slope_rows = _nsa_slope_rows(slopes_l2)
    base = attn // GROUP_LANES
    o_s = _nsa_selected(qkv, qkv, base + 2, base + 3, key_aug, gates, slope_rows, sel_q)
    o_w = _nsa_window(qkv, qkv, base + 4, base + 5, key_aug, gates, slope_rows)
    n = b * s
    return _out_ln([o_c.reshape(n, attn), o_s.reshape(n, attn), o_w.reshape(n, attn)],
                   wo.astype(BF16), x2d, ln_g.reshape(1, d), ln_b.reshape(1, d))


def kernel(x, p, fox_w_in, fox_b_f, fox_w_out, nsa_w_kv, cmp_pe, cmp_w1, cmp_b1, cmp_w2, nsa_w_q, nsa_b_g, nsa_w_out, ln_g, ln_b, moe_w_router, moe_b_router, moe_w_gate, moe_w_up, moe_w_down, shared_w_gate, shared_w_up, shared_w_down, ple_w_proj, ple_w_gate, ple_b_gate):
    b, s, d = x.shape
    n = b * s
    depth = p.shape[0]
    assert depth == DEPTH, "DEEPNORM_ALPHA is derived from DEPTH"
    n_a = depth // 2
    h = x.reshape(n, d)
    for i in range(depth):
        if i < n_a:
            h1, h1b = _fox_layer(h, b, s, fox_w_in[i], fox_b_f[i], fox_w_out[i], ln_g[i, 0], ln_b[i, 0])
        else:
            j = i - n_a
            h1, h1b = _nsa_layer(h, b, s, nsa_w_kv, cmp_pe, cmp_w1, cmp_b1, cmp_w2,
                                 nsa_w_q[j], nsa_b_g[j], nsa_w_out[j], ln_g[i, 0], ln_b[i, 0])
        h = _moe_layer(h1, h1b, p.reshape(depth, n, -1), i, moe_w_router[i], moe_b_router[i],
                       moe_w_gate, moe_w_up, moe_w_down,
                       shared_w_gate[i], shared_w_up[i], shared_w_down[i],
                       ple_w_proj[i], ple_w_gate[i], ple_b_gate[i],
                       ln_g[i, 1], ln_b[i, 1], ln_g[i, 2], ln_b[i, 2])
    return h.reshape(b, s, d)
```

```python
import functools

import numpy as np
import jax
import jax.numpy as jnp
from jax import lax
from jax.experimental import pallas as pl
from jax.experimental.pallas import tpu as pltpu

F32 = jnp.float32
BF16 = jnp.bfloat16
NEG = -1e30

HEAD_DIM = 64
LANES = 128
NSA_KV_GROUPS = 4
NSA_HEADS_PER_GROUP = 4
GROUP_LANES = NSA_KV_GROUPS * HEAD_DIM
CMP_BLOCK = 32
CMP_STRIDE = 16
SLC_BLOCK = 64
SLC_TOPK = 16
WINDOW = 512
EXPERT_TOPK = 8
EXPERT_GROUPS = 8
EXPERT_TOPK_GROUPS = 4
ROUTED_SCALE = 2.5
LN_EPS = 1e-5
DEPTH = 2
DEEPNORM_ALPHA = (2.0 * DEPTH) ** 0.25
VMEM_LIMIT = 52 * 1024 * 1024

ROW_TILE = 512
FOX_TILE = 2048
FOX_SUB = 128
NSA_Q_TILE = 256
NSA_SEL_TILE = 1024
NSA_WIN_TILE = 2048
NSA_SUB = 256
EXPERT_ROW_BLOCK = 512
EXPERT_CHUNKS = 8


def _cparams(sem):
    return pltpu.CompilerParams(dimension_semantics=sem, vmem_limit_bytes=VMEM_LIMIT)


def _nt(a, b):
    return lax.dot_general(a, b, (((1,), (1,)), ((), ())), preferred_element_type=F32)


def _dot(a, b):
    return jnp.dot(a, b, preferred_element_type=F32)


def _split2(a):
    hi = a.astype(BF16)
    lo = (a - hi.astype(F32)).astype(BF16)
    return hi, lo


def _split3(a):
    hi = a.astype(BF16)
    r = a - hi.astype(F32)
    mid = r.astype(BF16)
    lo = (r - mid.astype(F32)).astype(BF16)
    return hi, mid, lo


def _sigmoid(x):
    return 1.0 / (1.0 + jnp.exp(-x))


def _layer_norm(z, g, b):
    mu = jnp.mean(z, axis=-1, keepdims=True)
    zc = z - mu
    var = jnp.mean(zc * zc, axis=-1, keepdims=True)
    return zc * lax.rsqrt(var + LN_EPS) * g + b


def _proj_kernel(x_ref, w_ref, cs_ref, wg_ref, bg_ref, o_ref, g_ref, *, tn, gate_sigmoid, gate_split):
    x = x_ref[...]
    xh = x.astype(BF16)
    m_out = w_ref.shape[1]
    for j in range(m_out // tn):
        sl = slice(j * tn, (j + 1) * tn)
        y = _dot(xh, w_ref[:, sl])
        o_ref[:, sl] = (y * cs_ref[:, sl]).astype(o_ref.dtype)
    xl = (x - xh.astype(F32)).astype(BF16)
    for i in range(wg_ref.shape[0]):
        wh, wl = _split2(wg_ref[i])
        g = _dot(xh, wh) + bg_ref[i]
        if gate_split:
            g = g + _dot(xh, wl) + _dot(xl, wh)
        if gate_sigmoid:
            g = _sigmoid(g)
        g_ref[i] = g


def _project(x2d, w_bf16, col_scale, w_gate, b_gate, gate_sigmoid, gate_split):
    n, k = x2d.shape
    m_out = w_bf16.shape[1]
    n_g = w_gate.shape[0]
    tm = min(ROW_TILE, n)
    return pl.pallas_call(
        functools.partial(_proj_kernel, tn=512, gate_sigmoid=gate_sigmoid, gate_split=gate_split),
        grid=(n // tm,),
        in_specs=[
            pl.BlockSpec((tm, k), lambda i: (i, 0)),
            pl.BlockSpec((k, m_out), lambda i: (0, 0)),
            pl.BlockSpec((1, m_out), lambda i: (0, 0)),
            pl.BlockSpec((n_g, k, LANES), lambda i: (0, 0, 0)),
            pl.BlockSpec((n_g, 1, LANES), lambda i: (0, 0, 0)),
        ],
        out_specs=[
            pl.BlockSpec((tm, m_out), lambda i: (i, 0)),
            pl.BlockSpec((n_g, tm, LANES), lambda i: (0, i, 0)),
        ],
        out_shape=[
            jax.ShapeDtypeStruct((n, m_out), BF16),
            jax.ShapeDtypeStruct((n_g, n, LANES), F32),
        ],
        compiler_params=_cparams(("arbitrary",)),
        name="project",
    )(x2d, w_bf16, col_scale, w_gate, b_gate)


LOG2E = 1.4426950408889634
GATE_PIECE_STRIDE = 16


def _gate_cumsum_kernel(z_ref, o_ref, carry_ref, *, ts, n_heads):
    @pl.when(pl.program_id(1) == 0)
    def _():
        carry_ref[...] = jnp.zeros_like(carry_ref)

    z = z_ref[0]
    log_f = jnp.minimum(z, 0.0) - jnp.log(1.0 + jnp.exp(-jnp.abs(z)))
    row = lax.broadcasted_iota(jnp.int32, (ts, ts), 0)
    col = lax.broadcasted_iota(jnp.int32, (ts, ts), 1)
    tri = jnp.where(col <= row, 1.0, 0.0).astype(BF16)
    hi, mid, lo = _split3(log_f)
    cs = _dot(tri, hi) + _dot(tri, mid) + _dot(tri, lo) + carry_ref[...]
    carry_ref[...] = cs[ts - 1:ts, :]
    lane = lax.broadcasted_iota(jnp.int32, (1, LANES), 1)
    bias = jnp.where(lane < n_heads, cs * (-LOG2E), 0.0)
    hi, mid, lo = _split3(bias)
    pieces = (hi.astype(F32) + pltpu.roll(mid.astype(F32), GATE_PIECE_STRIDE, 1)
              + pltpu.roll(lo.astype(F32), 2 * GATE_PIECE_STRIDE, 1))
    o_ref[0] = (pieces + pltpu.roll(pieces, HEAD_DIM, 1)).astype(o_ref.dtype)


def _gate_cumsum(z, n_heads):
    b, s, _ = z.shape
    ts = min(256, s)
    return pl.pallas_call(
        functools.partial(_gate_cumsum_kernel, ts=ts, n_heads=n_heads),
        grid=(b, s // ts),
        in_specs=[pl.BlockSpec((1, ts, LANES), lambda i, j: (i, j, 0))],
        out_specs=pl.BlockSpec((1, ts, LANES), lambda i, j: (i, j, 0)),
        out_shape=jax.ShapeDtypeStruct((b, s, LANES), BF16),
        scratch_shapes=[pltpu.VMEM((1, LANES), F32)],
        compiler_params=_cparams(("arbitrary", "arbitrary")),
        name="gate_cumsum",
    )(z)


def _flash_items(items):
    outs = []
    for key, s, v, m_prev, acc_prev in items:
        m_new = jnp.maximum(m_prev, jnp.max(s, axis=1, keepdims=True))
        alpha = jnp.exp2(m_prev - m_new)
        p = jnp.concatenate(
            [jnp.exp2(s[:, j * LANES:(j + 1) * LANES] - m_new[:, :LANES])
             for j in range(s.shape[1] // LANES)], axis=1)
        reps = acc_prev.shape[1] // LANES
        alpha = alpha if reps == 1 else jnp.concatenate([alpha] * reps, axis=1)
        outs.append((key, m_new, alpha * acc_prev + _dot(p.astype(BF16), v)))
    return outs


def _fox_kernel(qt_ref, kt_ref, q_ref, k_ref, v_ref, c_ref, o_ref, m_ref, acc_ref, *, t, sub):
    step = pl.program_id(2)
    pair = pl.program_id(1)
    qi = qt_ref[step]
    ki = kt_ref[step]

    @pl.when(ki == 0)
    def _():
        m_ref[...] = jnp.full_like(m_ref, NEG)
        acc_ref[...] = jnp.zeros_like(acc_ref)

    lane = lax.broadcasted_iota(jnp.int32, (1, LANES), 1)
    low = lane < HEAD_DIM

    def body(diag):
        q = q_ref[0]
        k = k_ref[0]
        v = v_ref[0]
        c = c_ref[0]
        items = []
        for h in range(2):
            own = low if h == 0 else jnp.logical_not(low)
            base = (1 - h) * HEAD_DIM + 2 * pair + h
            ones_at = ((lane == base) | (lane == base + GATE_PIECE_STRIDE)
                       | (lane == base + 2 * GATE_PIECE_STRIDE))
            qh = jnp.where(own, q, jnp.where(ones_at, 1.0, 0.0).astype(BF16))
            kh = jnp.where(own, k, c)
            vh = jnp.where(own, v, jnp.ones_like(v))
            for qs in range(t // sub):
                rows = slice(qs * sub, (qs + 1) * sub)
                nk = (qs + 1) * sub if diag else t
                s = _nt(qh[rows], kh[:nk])
                if diag:
                    r_i = lax.broadcasted_iota(jnp.int32, (sub, sub), 0)
                    c_i = lax.broadcasted_iota(jnp.int32, (sub, sub), 1)
                    last = jnp.where(c_i <= r_i, s[:, nk - sub:], NEG)
                    s = last if nk == sub else jnp.concatenate([s[:, :nk - sub], last], axis=1)
                items.append(((h, rows), s, vh[:nk], m_ref[h, rows, :], acc_ref[h, rows, :]))
        for (h, rows), m_new, acc_new in _flash_items(items):
            acc_ref[h, rows, :] = acc_new
            m_ref[h, rows, :] = m_new

    @pl.when(ki < qi)
    def _():
        body(False)

    @pl.when(ki == qi)
    def _():
        body(True)
        a0 = acc_ref[0]
        a1 = acc_ref[1]
        o0 = a0 / pltpu.roll(a0, HEAD_DIM, 1)
        o1 = a1 / pltpu.roll(a1, HEAD_DIM, 1)
        o_ref[0] = jnp.where(low, o0, o1).astype(o_ref.dtype)


def _fox_attention(qkv, gate_bias, n_heads):
    b, s, _ = qkv.shape
    t = min(FOX_TILE, s)
    nq = s // t
    pairs = n_heads // 2
    qt = np.concatenate([np.full((i + 1,), i, np.int32) for i in range(nq)])
    kt = np.concatenate([np.arange(i + 1, dtype=np.int32) for i in range(nq)])
    grid_spec = pltpu.PrefetchScalarGridSpec(
        num_scalar_prefetch=2,
        grid=(b, pairs, len(qt)),
        in_specs=[
            pl.BlockSpec((1, t, LANES), lambda i, j, u, qt, kt: (i, qt[u], j)),
            pl.BlockSpec((1, t, LANES), lambda i, j, u, qt, kt: (i, kt[u], pairs + j)),
            pl.BlockSpec((1, t, LANES), lambda i, j, u, qt, kt: (i, kt[u], 2 * pairs + j)),
            pl.BlockSpec((1, t, LANES), lambda i, j, u, qt, kt: (i, kt[u], 0)),
        ],
        out_specs=pl.BlockSpec((1, t, LANES), lambda i, j, u, qt, kt: (i, qt[u], j)),
        scratch_shapes=[
            pltpu.VMEM((2, t, LANES), F32),
            pltpu.VMEM((2, t, LANES), F32),
        ],
    )
    return pl.pallas_call(
        functools.partial(_fox_kernel, t=t, sub=min(FOX_SUB, t)),
        grid_spec=grid_spec,
        out_shape=jax.ShapeDtypeStruct((b, s, n_heads * HEAD_DIM), BF16),
        compiler_params=_cparams(("arbitrary", "arbitrary", "arbitrary")),
        name="fox_attention",
    )(jnp.asarray(qt), jnp.asarray(kt), qkv, qkv, qkv, gate_bias)


def _out_ln_kernel(*refs, n_in, tm, n_exp):
    o_refs = refs[:n_in]
    w_ref, x_ref, g_ref, b_ref, wt_ref, rb_ref, y_ref, yb_ref = refs[n_in:n_in + 8]
    router_refs = refs[n_in + 8:]
    if n_in == 1:
        o = o_refs[0][...]
    else:
        o = o_refs[0][...].astype(F32)
        for r in o_refs[1:]:
            o = o + r[...].astype(F32)
        o = o.astype(BF16)
    z = DEEPNORM_ALPHA * x_ref[...] + _dot(o, w_ref[...])
    y = _layer_norm(z, g_ref[...], b_ref[...])
    y_ref[...] = y
    yb_ref[...] = y.astype(BF16)
    _router_body(y, wt_ref, rb_ref, *router_refs, tn=tm, n_exp=n_exp)


def _out_ln(o_list, w_bf16, x2d, g, b, w_router_t, b_router):
    n, d = x2d.shape
    k = w_bf16.shape[0]
    n_exp = w_router_t.shape[0]
    tm = min(ROW_TILE, n)
    n_in = len(o_list)
    row = lambda i: (i, 0)
    col = lambda i: (0, i)
    fixed = lambda i: (0, 0)
    return pl.pallas_call(
        functools.partial(_out_ln_kernel, n_in=n_in, tm=tm, n_exp=n_exp),
        grid=(n // tm,),
        in_specs=[pl.BlockSpec((tm, k), row)] * n_in + [
            pl.BlockSpec((k, d), fixed),
            pl.BlockSpec((tm, d), row),
            pl.BlockSpec((1, d), fixed),
            pl.BlockSpec((1, d), fixed),
            pl.BlockSpec((n_exp, d), fixed),
            pl.BlockSpec((n_exp, 1), fixed),
        ],
        out_specs=[pl.BlockSpec((tm, d), row), pl.BlockSpec((tm, d), row),
                   pl.BlockSpec((EXPERT_TOPK, tm), col), pl.BlockSpec((EXPERT_TOPK, tm), col),
                   pl.BlockSpec((EXPERT_TOPK, tm), col), pl.BlockSpec((n_exp, 1), fixed)],
        out_shape=[jax.ShapeDtypeStruct((n, d), F32), jax.ShapeDtypeStruct((n, d), BF16),
                   jax.ShapeDtypeStruct((EXPERT_TOPK, n), jnp.int32),
                   jax.ShapeDtypeStruct((EXPERT_TOPK, n), F32),
                   jax.ShapeDtypeStruct((EXPERT_TOPK, n), jnp.int32),
                   jax.ShapeDtypeStruct((n_exp, 1), F32)],
        scratch_shapes=[pltpu.VMEM((n_exp, 1), F32)],
        compiler_params=_cparams(("arbitrary",)),
        name="out_ln_route",
    )(*o_list, w_bf16, x2d, g, b, w_router_t, b_router)


def _router_body(x, wt_ref, b_ref, idx_ref, w_ref, rank_ref, cnt_ref, carry_ref, *, tn, n_exp):
    @pl.when(pl.program_id(0) == 0)
    def _():
        carry_ref[...] = jnp.zeros_like(carry_ref)

    xh, xl = _split2(x)
    wh, wl = _split2(wt_ref[...])
    logits = _nt(wh, xh) + _nt(wh, xl) + _nt(wl, xh)
    scores = _sigmoid(logits)
    biased = scores + b_ref[...]

    per_grp = n_exp // EXPERT_GROUPS
    blocks, gscore = [], []
    for g in range(EXPERT_GROUPS):
        blk = biased[g * per_grp:(g + 1) * per_grp, :]
        m1 = jnp.max(blk, axis=0, keepdims=True)
        eq = blk == m1
        n_eq = jnp.sum(jnp.where(eq, 1.0, 0.0), axis=0, keepdims=True)
        m2 = jnp.max(jnp.where(eq, -3e38, blk), axis=0, keepdims=True)
        blocks.append(blk)
        gscore.append(m1 + jnp.where(n_eq >= 2.0, m1, m2))
    cand = []
    for g in range(EXPERT_GROUPS):
        beaten = jnp.zeros_like(gscore[g])
        for o in range(EXPERT_GROUPS):
            if o == g:
                continue
            wins = (gscore[o] >= gscore[g]) if o < g else (gscore[o] > gscore[g])
            beaten = beaten + jnp.where(wins, 1.0, 0.0)
        cand.append(jnp.where(beaten < float(EXPERT_TOPK_GROUPS), blocks[g], NEG))
    cand = jnp.concatenate(cand, axis=0)

    erow = lax.broadcasted_iota(jnp.int32, (n_exp, tn), 0).astype(F32)
    hot, sel_idx, sel_score = [], [], []
    member = jnp.zeros((n_exp, tn), F32)
    for _ in range(EXPERT_TOPK):
        m = jnp.max(cand, axis=0, keepdims=True)
        first = jnp.min(jnp.where(cand == m, erow, float(n_exp)), axis=0, keepdims=True)
        onehot = erow == first
        sel_idx.append(first)
        sel_score.append(jnp.sum(jnp.where(onehot, scores, 0.0), axis=0, keepdims=True))
        cand = jnp.where(onehot, -3e38, cand)
        member = member + jnp.where(onehot, 1.0, 0.0)
        hot.append(onehot)
    total = sel_score[0]
    for sc in sel_score[1:]:
        total = total + sc

    trow = lax.broadcasted_iota(jnp.int32, (tn, tn), 0)
    tcol = lax.broadcasted_iota(jnp.int32, (tn, tn), 1)
    before = jnp.where(trow < tcol, 1.0, 0.0).astype(BF16)
    prior = _dot(member.astype(BF16), before) + carry_ref[...]
    ranks = [jnp.sum(jnp.where(hot[k], prior, 0.0), axis=0, keepdims=True) for k in range(EXPERT_TOPK)]

    idx_ref[...] = jnp.concatenate(sel_idx, axis=0).astype(jnp.int32)
    w_ref[...] = jnp.concatenate([sc / total * ROUTED_SCALE for sc in sel_score], axis=0)
    rank_ref[...] = jnp.concatenate(ranks, axis=0).astype(jnp.int32)
    carry_ref[...] = carry_ref[...] + jnp.sum(member, axis=1, keepdims=True)
    cnt_ref[...] = carry_ref[...]


def _experts_kernel(be_ref, first_ref, nused_ref, x_ref, wg_ref, wu_ref, wd_ref, *rest):
    y_ref, wg_s, wu_s, wd_s = rest[-4:]
    blk = pl.program_id(0)

    @pl.when(blk < nused_ref[0])
    def _():
        @pl.when((first_ref[blk] == 1) | (blk == 0))
        def _():
            wg_s[...] = wg_ref[0, 0].astype(BF16)
            wu_s[...] = wu_ref[0, 0].astype(BF16)
            wd_s[...] = wd_ref[0, 0].astype(BF16)

        x = x_ref[...]
        gate = _dot(x, wg_s[...])
        up = _dot(x, wu_s[...])
        h = gate * _sigmoid(gate) * up
        y_ref[...] = _dot(h.astype(BF16), wd_s[...]).astype(y_ref.dtype)

    @pl.when(blk >= nused_ref[0])
    def _():
        y_ref[...] = jnp.zeros_like(y_ref)


def _experts(xg, blk_expert, blk_first, n_used, w_gate, w_up, w_down, layer, y_prev, first_block, n_rows):
    d = xg.shape[1]
    hdim = w_gate.shape[3]
    tm = EXPERT_ROW_BLOCK
    n_blocks = xg.shape[0] // tm
    extra_specs, extra_args, aliases = [], [], {}
    if y_prev is not None:
        extra_specs, extra_args, aliases = [pl.BlockSpec(memory_space=pl.ANY)], [y_prev], {7: 0}
    grid_spec = pltpu.PrefetchScalarGridSpec(
        num_scalar_prefetch=3,
        grid=(n_blocks,),
        in_specs=[
            pl.BlockSpec((tm, d), lambda i, be, fi, nu: (jnp.minimum(i, jnp.maximum(nu[0] - 1, 0)), 0)),
            pl.BlockSpec((1, 1, d, hdim), lambda i, be, fi, nu: (layer, be[i], 0, 0)),
            pl.BlockSpec((1, 1, d, hdim), lambda i, be, fi, nu: (layer, be[i], 0, 0)),
            pl.BlockSpec((1, 1, hdim, d), lambda i, be, fi, nu: (layer, be[i], 0, 0)),
        ] + extra_specs,
        out_specs=pl.BlockSpec((tm, d), lambda i, be, fi, nu: (i + first_block, 0)),
        scratch_shapes=[
            pltpu.VMEM((d, hdim), BF16),
            pltpu.VMEM((d, hdim), BF16),
            pltpu.VMEM((hdim, d), BF16),
        ],
    )
    return pl.pallas_call(
        _experts_kernel,
        grid_spec=grid_spec,
        out_shape=jax.ShapeDtypeStruct((n_rows, d), BF16),
        input_output_aliases=aliases,
        compiler_params=_cparams(("arbitrary",)),
        name="experts",
    )(blk_expert, blk_first, n_used, xg, w_gate, w_up, w_down, *extra_args)


def _moe_tail_kernel(x_ref, r_ref, rw_ref, p_ref, sg_ref, su_ref, sd_ref, pg_ref, pb_ref, pp_ref,
                     g1_ref, b1_ref, g2_ref, b2_ref, y_ref):
    x = x_ref[...]
    xb = x.astype(BF16)
    gate = _dot(xb, sg_ref[...])
    up = _dot(xb, su_ref[...])
    shared = _dot((gate * _sigmoid(gate) * up).astype(BF16), sd_ref[...])
    rw = rw_ref[...]
    routed = r_ref[0].astype(F32) * rw[:, 0:1]
    for k in range(1, r_ref.shape[0]):
        routed = routed + r_ref[k].astype(F32) * rw[:, k:k + 1]
    x2 = _layer_norm(DEEPNORM_ALPHA * x + (routed + shared), g1_ref[...], b1_ref[...])
    ple_gate = _sigmoid(_dot(x2.astype(BF16), pg_ref[...]) + pb_ref[...])
    ple = ple_gate * _dot(p_ref[0].astype(BF16), pp_ref[...])
    y_ref[...] = _layer_norm(DEEPNORM_ALPHA * x2 + ple, g2_ref[...], b2_ref[...])


def _moe_tail(x2d, routed, routed_w, p_all, layer, sg, su, sd, pg, pb, pp, g1, b1, g2, b2):
    n, d = x2d.shape
    tm = min(ROW_TILE, n)
    topk = routed.shape[0]
    row = lambda i: (i, 0)
    fixed = lambda i: (0, 0)
    full = lambda a: pl.BlockSpec(a.shape, fixed)
    return pl.pallas_call(
        _moe_tail_kernel,
        grid=(n // tm,),
        in_specs=[
            pl.BlockSpec((tm, d), row),
            pl.BlockSpec((topk, tm, d), lambda i: (0, i, 0)),
            pl.BlockSpec((tm, topk), row),
            pl.BlockSpec((1, tm, p_all.shape[2]), lambda i: (layer, i, 0)),
            full(sg), full(su), full(sd), full(pg), full(pb), full(pp),
            full(g1), full(b1), full(g2), full(b2),
        ],
        out_specs=pl.BlockSpec((tm, d), row),
        out_shape=jax.ShapeDtypeStruct((n, d), F32),
        compiler_params=_cparams(("arbitrary",)),
        name="moe_tail",
    )(x2d, routed, routed_w, p_all, sg, su, sd, pg, pb, pp, g1, b1, g2, b2)


def _moe_layer(x1, x1b, routing, p_all, layer, w_gate, w_up, w_down, sg, su, sd,
               ple_proj, ple_gate, ple_bias, g1, b1, g2, b2):
    n, d = x1.shape
    idx, wts, rank, counts = routing
    n_exp = counts.shape[0]
    tm = EXPERT_ROW_BLOCK
    counts = counts[:, 0].astype(jnp.int32)
    padded = (counts + tm - 1) // tm * tm
    pad_end = jnp.cumsum(padded)
    pad_start = pad_end - padded
    n_blocks = (n * EXPERT_TOPK + n_exp * (tm - 1)) // tm + 1
    n_blocks = -(-n_blocks // EXPERT_CHUNKS) * EXPERT_CHUNKS
    n_rows = n_blocks * tm
    experts = jnp.arange(n_exp, dtype=jnp.int32)
    start_of = jnp.sum(jnp.where(idx[:, :, None] == experts, pad_start, 0), axis=-1)
    pos = start_of + rank
    stride = n + tm
    past_end = 2 ** 30
    assert n_exp * stride <= past_end and past_end + n_rows + n_exp * tm < 2 ** 31
    tok = jnp.arange(n, dtype=jnp.int32)[None, :]
    filler = jnp.arange(tm, dtype=jnp.int32)[None, :]
    filler_key = jnp.where(filler < (padded - counts)[:, None],
                           experts[:, None] * stride + n + filler,
                           past_end + experts[:, None] * tm + filler)
    n_tail = max(n_rows - (n * EXPERT_TOPK + n_exp * tm), 0)
    keys = jnp.concatenate([(idx * stride + tok).reshape(-1), filler_key.reshape(-1),
                            past_end + n_exp * tm + jnp.arange(n_tail, dtype=jnp.int32)])
    keys = lax.sort(keys, is_stable=False)[:n_rows]
    slot = keys % stride
    tok_pad = jnp.where(keys >= past_end, (keys - past_end) % n,
                        jnp.where(slot < n, slot, (keys // stride * tm + slot - n) % n))
    blk_start = jnp.arange(n_blocks, dtype=jnp.int32) * tm
    blk_expert = jnp.minimum(jnp.sum((pad_end[None, :] <= blk_start[:, None]).astype(jnp.int32), axis=1),
                             n_exp - 1)
    blk_first = jnp.concatenate([jnp.ones((1,), jnp.int32),
                                 (blk_expert[1:] != blk_expert[:-1]).astype(jnp.int32)])
    n_used = (pad_end[-1:] // tm).astype(jnp.int32)
    per = n_blocks // EXPERT_CHUNKS
    y = None
    for c in range(EXPERT_CHUNKS):
        blocks = slice(c * per, (c + 1) * per)
        xg = x1b.at[tok_pad[c * per * tm:(c + 1) * per * tm]].get(mode="promise_in_bounds")
        y = _experts(xg, blk_expert[blocks], blk_first[blocks], jnp.clip(n_used - c * per, 0, per),
                     w_gate, w_up, w_down, layer, y, c * per, n_rows)
    routed = y.at[pos].get(mode="promise_in_bounds")
    return _moe_tail(x1, routed, wts.T, p_all, layer, sg.astype(BF16), su.astype(BF16), sd.astype(BF16),
                     ple_gate.astype(BF16), ple_bias.reshape(1, d), ple_proj.astype(BF16),
                     g1.reshape(1, d), b1.reshape(1, d), g2.reshape(1, d), b2.reshape(1, d))


def _gelu_tanh(x):
    return 0.5 * x * (1.0 + jnp.tanh(0.7978845608028654 * (x + 0.044715 * (x * x * x))))


def _compress_kernel(c_ref, pe_ref, w1_ref, b1_ref, w2_ref, o_ref, *, nc, half):
    out = jnp.zeros((nc, GROUP_LANES), F32)
    for g in range(NSA_KV_GROUPS):
        c = c_ref[0, 0, g].astype(F32)
        a = _dot((c + pe_ref[0, 0:1, :]).astype(BF16), w1_ref[0, :half, :])
        bm = _dot((c + pe_ref[0, 1:2, :]).astype(BF16), w1_ref[0, half:, :])
        h = _gelu_tanh(a + pltpu.roll(bm, nc - 1, 0) + b1_ref[0])
        out = out + _dot(h.astype(BF16), w2_ref[0, g])
    o_ref[0, 0] = out.astype(o_ref.dtype)


def _compress(chunks, pe_flat, w1, b1, w2_placed):
    b, _, _, nc, half = chunks.shape
    hid = w1.shape[2]
    return pl.pallas_call(
        functools.partial(_compress_kernel, nc=nc, half=half),
        grid=(b, 2),
        in_specs=[
            pl.BlockSpec((1, 1, NSA_KV_GROUPS, nc, half), lambda i, j: (i, j, 0, 0, 0)),
            pl.BlockSpec((1, 2, half), lambda i, j: (j, 0, 0)),
            pl.BlockSpec((1, 2 * half, hid), lambda i, j: (j, 0, 0)),
            pl.BlockSpec((1, 1, hid), lambda i, j: (j, 0, 0)),
            pl.BlockSpec((1, NSA_KV_GROUPS, hid, GROUP_LANES), lambda i, j: (j, 0, 0, 0)),
        ],
        out_specs=pl.BlockSpec((1, 1, nc, GROUP_LANES), lambda i, j: (i, j, 0, 0)),
        out_shape=jax.ShapeDtypeStruct((b, 2, nc, GROUP_LANES), BF16),
        compiler_params=_cparams(("arbitrary", "arbitrary")),
        name="compress",
    )(chunks, pe_flat, w1, b1, w2_placed)


def _group_mask(g):
    lane = lax.broadcasted_iota(jnp.int32, (1, GROUP_LANES), 1)
    return (lane >= g * HEAD_DIM) & (lane < (g + 1) * HEAD_DIM)


def _topk_rank(score):
    n, t = score.shape
    sub8 = lax.broadcasted_iota(jnp.int32, (8, t), 0)
    ahead = jnp.zeros((n, t), F32)
    for j in range(n):
        vj = score[j:j + 1, :]
        lo_r, hi_r = (j // 8) * 8, (j // 8) * 8 + 8
        parts = []
        if lo_r > 0:
            parts.append(jnp.where(vj > score[:lo_r], 1.0, 0.0))
        own = score[lo_r:hi_r]
        parts.append(jnp.where(sub8 > (j % 8), jnp.where(vj >= own, 1.0, 0.0),
                               jnp.where(vj > own, 1.0, 0.0)))
        if hi_r < n:
            parts.append(jnp.where(vj >= score[hi_r:], 1.0, 0.0))
        ahead = ahead + jnp.concatenate(parts, axis=0)
    return ahead


def _nsa_cmp_kernel(q_ref, kc_ref, vc_ref, gate_ref, slope_ref, wt_ref, o_ref, sel_ref,
                    *, tq, nc, n_slc):
    qi = pl.program_id(1)
    q0 = qi * tq
    kc = kc_ref[0, 0]
    vc = vc_ref[0, 0]
    t_pos = q0 + lax.broadcasted_iota(jnp.int32, (tq, nc), 0)
    cmp_end = lax.broadcasted_iota(jnp.int32, (tq, nc), 1) * CMP_STRIDE + (CMP_BLOCK - 1)
    visible = cmp_end <= t_pos
    end_rel = (lax.broadcasted_iota(jnp.int32, (1, nc), 1) * CMP_STRIDE + (CMP_BLOCK - 1) - q0).astype(F32)
    v_aug = [jnp.where(_group_mask(g), vc, jnp.ones_like(vc)) for g in range(NSA_KV_GROUPS)]
    psum = [jnp.zeros((tq, nc), F32) for _ in range(NSA_KV_GROUPS)]
    for r in range(NSA_HEADS_PER_GROUP):
        q = q_ref[0, :, r * GROUP_LANES:(r + 1) * GROUP_LANES]
        gates = gate_ref[r, 0]
        out = jnp.zeros((tq, GROUP_LANES), F32)
        for g in range(NSA_KV_GROUPS):
            in_grp = _group_mask(g)
            qm = jnp.where(in_grp, q, jnp.zeros_like(q))
            s = _nt(qm, kc) + slope_ref[r, g, :, :nc] * end_rel
            s = jnp.where(visible, s, NEG)
            m = jnp.max(s, axis=1, keepdims=True)
            p = jnp.where(visible, jnp.exp2(s - m), 0.0)
            o_g = _dot(p.astype(BF16), v_aug[g])
            other = (1 - g // 2) * LANES
            l = o_g[:, other:other + LANES]
            inv = jnp.where(l > 0.0, 1.0 / l, 0.0)
            inv2 = jnp.concatenate([inv, inv], axis=1)
            psum[g] = psum[g] + (p * inv2[:, :nc] if nc == GROUP_LANES else p * inv[:, :1])
            out = out + jnp.where(in_grp, o_g * (inv2 * gates[:, g:g + 1]), 0.0)
        o_ref[0, :, r * GROUP_LANES:(r + 1) * GROUP_LANES] = out.astype(o_ref.dtype)

    blk = lax.broadcasted_iota(jnp.int32, (n_slc, tq), 0)
    t_row = q0 + lax.broadcasted_iota(jnp.int32, (n_slc, tq), 1)
    cur = t_row // SLC_BLOCK
    forced = (blk == 0) | (blk == cur) | (blk == cur - 1)
    vis = blk * SLC_BLOCK <= t_row
    wt = wt_ref[...]
    pad_rows = jnp.zeros((LANES - n_slc, tq), F32)
    for g in range(NSA_KV_GROUPS):
        hi, mid, lo = _split3(psum[g])
        imp = _nt(wt, hi) + _nt(wt, mid) + _nt(wt, lo)
        score = jnp.where(forced, 1e9, jnp.where(vis, imp, NEG))
        keep = (_topk_rank(score) < float(min(SLC_TOPK, n_slc))) & (score > 0.5 * NEG)
        flags = jnp.concatenate([jnp.where(keep, 0.0, NEG), pad_rows], axis=0)
        sel_ref[0, g] = flags.T.astype(sel_ref.dtype)


def _nsa_compressed(q, cmp_kv, gates_c, slopes_rep, w_sel_t):
    b, s, _ = q.shape
    nc = cmp_kv.shape[2]
    n_slc = s // SLC_BLOCK
    tq = min(NSA_Q_TILE, s)
    rr = NSA_HEADS_PER_GROUP
    return pl.pallas_call(
        functools.partial(_nsa_cmp_kernel, tq=tq, nc=nc, n_slc=n_slc),
        grid=(b, s // tq),
        in_specs=[
            pl.BlockSpec((1, tq, rr * GROUP_LANES), lambda i, j: (i, j, 0)),
            pl.BlockSpec((1, 1, nc, GROUP_LANES), lambda i, j: (i, 0, 0, 0)),
            pl.BlockSpec((1, 1, nc, GROUP_LANES), lambda i, j: (i, 1, 0, 0)),
            pl.BlockSpec((rr, 1, tq, LANES), lambda i, j: (0, i, j, 0)),
            pl.BlockSpec((rr, NSA_KV_GROUPS, 1, slopes_rep.shape[3]), lambda i, j: (0, 0, 0, 0)),
            pl.BlockSpec((n_slc, nc), lambda i, j: (0, 0)),
        ],
        out_specs=[
            pl.BlockSpec((1, tq, rr * GROUP_LANES), lambda i, j: (i, j, 0)),
            pl.BlockSpec((1, NSA_KV_GROUPS, tq, LANES), lambda i, j: (i, 0, j, 0)),
        ],
        out_shape=[
            jax.ShapeDtypeStruct((b, s, rr * GROUP_LANES), BF16),
            jax.ShapeDtypeStruct((b, NSA_KV_GROUPS, s, LANES), BF16),
        ],
        compiler_params=_cparams(("arbitrary", "arbitrary")),
        name="nsa_compressed",
    )(q, cmp_kv, cmp_kv, gates_c, slopes_rep, w_sel_t)


AUG_POS_LANE = 64


def _nsa_query_aug(q, slope_row, sel, g, lane):
    t = q.shape[0]
    half = g // 2
    in_grp = (lane >= (g % 2) * HEAD_DIM) & (lane < (g % 2 + 1) * HEAD_DIM)
    q_own = jnp.where(in_grp, q[:, half * LANES:(half + 1) * LANES], jnp.zeros((t, LANES), BF16))
    q_extra = jnp.broadcast_to(slope_row, (t, LANES))
    if sel is not None:
        q_extra = jnp.where(lane < AUG_POS_LANE, sel, q_extra)
    return jnp.concatenate([q_own, q_extra] if half == 0 else [q_extra, q_own], axis=1)


def _nsa_key_operands(k, ka):
    return (jnp.concatenate([k[:, :LANES], ka], axis=1), jnp.concatenate([ka, k[:, LANES:]], axis=1))


def _nsa_selected_kernel(qt_ref, kt_ref, q_ref, k_ref, v_ref, ka_ref, gate_ref, slope_ref, sel_ref, o_ref,
                         m_ref, acc_ref, *, t, sub):
    step = pl.program_id(2)
    qi = qt_ref[step]
    ki = kt_ref[step]

    @pl.when(ki == 0)
    def _():
        m_ref[...] = jnp.full_like(m_ref, NEG)
        acc_ref[...] = jnp.zeros_like(acc_ref)

    lane = lax.broadcasted_iota(jnp.int32, (1, LANES), 1)

    def body(diag):
        q = q_ref[0]
        v = v_ref[0]
        k_aug = _nsa_key_operands(k_ref[0], ka_ref[...])
        items = []
        for g in range(NSA_KV_GROUPS):
            qg = _nsa_query_aug(q, slope_ref[0, g], sel_ref[0, g], g, lane)
            vg = jnp.where(_group_mask(g), v, jnp.ones_like(v))
            for qs in range(t // sub):
                rows = slice(qs * sub, (qs + 1) * sub)
                keys = slice(0, (qs + 1) * sub if diag else t)
                s = _nt(qg[rows], k_aug[g // 2][keys])
                if diag:
                    r_i = lax.broadcasted_iota(jnp.int32, (sub, sub), 0)
                    c_i = lax.broadcasted_iota(jnp.int32, (sub, sub), 1)
                    edge = jnp.where(c_i <= r_i, s[:, -sub:], NEG)
                    s = edge if s.shape[1] == sub else jnp.concatenate([s[:, :-sub], edge], axis=1)
                items.append(((g, rows), s, vg[keys], m_ref[g, rows, :], acc_ref[g, rows, :]))
        for (g, rows), m_new, acc_new in _flash_items(items):
            acc_ref[g, rows, :] = acc_new
            m_ref[g, rows, :] = m_new

    @pl.when(ki < qi)
    def _():
        body(False)

    @pl.when(ki == qi)
    def _():
        body(True)
        gates = gate_ref[0, 0]
        out = jnp.zeros((t, GROUP_LANES), F32)
        for g in range(NSA_KV_GROUPS):
            acc = acc_ref[g]
            denom = pltpu.roll(acc, LANES, 1)
            col = NSA_KV_GROUPS + g
            out = out + jnp.where(_group_mask(g), acc * (gates[:, col:col + 1] / denom), 0.0)
        o_ref[0] = out.astype(o_ref.dtype)


def _nsa_selected(q, kv, k_col, v_col, key_aug, gates, slope_rows, sel_q):
    b, s, _ = q.shape
    t = min(NSA_SEL_TILE, s)
    nq = s // t
    qt = np.concatenate([np.full((i + 1,), i, np.int32) for i in range(nq)])
    kt = np.concatenate([np.arange(i + 1, dtype=np.int32) for i in range(nq)])
    rr = NSA_HEADS_PER_GROUP
    grid_spec = pltpu.PrefetchScalarGridSpec(
        num_scalar_prefetch=2,
        grid=(b, rr, len(qt)),
        in_specs=[
            pl.BlockSpec((1, t, GROUP_LANES), lambda i, r, u, qt, kt: (i, qt[u], r)),
            pl.BlockSpec((1, t, GROUP_LANES), lambda i, r, u, qt, kt: (i, kt[u], k_col)),
            pl.BlockSpec((1, t, GROUP_LANES), lambda i, r, u, qt, kt: (i, kt[u], v_col)),
            pl.BlockSpec((t, LANES), lambda i, r, u, qt, kt: (kt[u], 0)),
            pl.BlockSpec((1, 1, t, LANES), lambda i, r, u, qt, kt: (r, i, qt[u], 0)),
            pl.BlockSpec((1, NSA_KV_GROUPS, 1, LANES), lambda i, r, u, qt, kt: (r, 0, 0, 0)),
            pl.BlockSpec((1, NSA_KV_GROUPS, t, LANES), lambda i, r, u, qt, kt: (i, 0, qt[u], 0)),
        ],
        out_specs=pl.BlockSpec((1, t, GROUP_LANES), lambda i, r, u, qt, kt: (i, qt[u], r)),
        scratch_shapes=[
            pltpu.VMEM((NSA_KV_GROUPS, t, LANES), F32),
            pltpu.VMEM((NSA_KV_GROUPS, t, GROUP_LANES), F32),
        ],
    )
    return pl.pallas_call(
        functools.partial(_nsa_selected_kernel, t=t, sub=min(NSA_SUB, t)),
        grid_spec=grid_spec,
        out_shape=jax.ShapeDtypeStruct((b, s, rr * GROUP_LANES), BF16),
        compiler_params=_cparams(("arbitrary", "arbitrary", "arbitrary")),
        name="nsa_selected",
    )(jnp.asarray(qt), jnp.asarray(kt), q, kv, kv, key_aug, gates, slope_rows, sel_q)


def _nsa_window_kernel(q_ref, kp_ref, kc_ref, vp_ref, vc_ref, kap_ref, kac_ref, gate_ref, slope_ref, o_ref,
                       *, t, sub):
    has_prev = pl.program_id(2) > 0
    lane = lax.broadcasted_iota(jnp.int32, (1, LANES), 1)
    q = q_ref[0]
    k = jnp.concatenate([kp_ref[0], kc_ref[0]], axis=0)
    v = jnp.concatenate([vp_ref[0], vc_ref[0]], axis=0)
    k_aug = _nsa_key_operands(k, jnp.concatenate([kap_ref[...], kac_ref[...]], axis=0))
    gates = gate_ref[0, 0]
    span = WINDOW + sub
    r_i = lax.broadcasted_iota(jnp.int32, (sub, sub), 0)
    c_i = lax.broadcasted_iota(jnp.int32, (sub, sub), 1)
    out_rows = [jnp.zeros((sub, GROUP_LANES), F32) for _ in range(t // sub)]
    for g in range(NSA_KV_GROUPS):
        half = g // 2
        qg = _nsa_query_aug(q, slope_ref[0, g], None, g, lane)
        vg = jnp.where(_group_mask(g), v, jnp.ones_like(v))
        col = 2 * NSA_KV_GROUPS + g
        for qs in range(t // sub):
            rows = slice(qs * sub, (qs + 1) * sub)
            keys = slice(qs * sub, qs * sub + span)
            s = _nt(qg[rows], k_aug[half][keys])
            chunks = [s[:, j * sub:(j + 1) * sub] for j in range(span // sub)]
            chunks[0] = jnp.where(c_i > r_i, chunks[0], NEG)
            chunks[-1] = jnp.where(c_i <= r_i, chunks[-1], NEG)
            for j in range(len(chunks) - 1):
                if qs * sub + j * sub < WINDOW:
                    chunks[j] = jnp.where(has_prev, chunks[j], NEG)
            m = chunks[0]
            for c in chunks[1:]:
                m = jnp.maximum(m, c)
            m = jnp.max(m, axis=1, keepdims=True)
            p = jnp.concatenate([jnp.exp2(c - m) for c in chunks], axis=1)
            acc = _dot(p.astype(BF16), vg[keys])
            denom = pltpu.roll(acc, LANES, 1)
            out_rows[qs] = out_rows[qs] + jnp.where(
                _group_mask(g), acc * (gates[rows, col:col + 1] / denom), 0.0)
    o_ref[0] = jnp.concatenate(out_rows, axis=0).astype(o_ref.dtype)


def _nsa_window(q, kv, k_col, v_col, key_aug, gates, slope_rows):
    b, s, _ = q.shape
    t = min(NSA_WIN_TILE, s)
    sub = min(NSA_SUB, t)
    assert WINDOW % sub == 0 and t % WINDOW == 0
    rr = NSA_HEADS_PER_GROUP
    per = t // WINDOW
    prev = lambda j: jnp.maximum(j * per - 1, 0)
    return pl.pallas_call(
        functools.partial(_nsa_window_kernel, t=t, sub=sub),
        grid=(b, rr, s // t),
        in_specs=[
            pl.BlockSpec((1, t, GROUP_LANES), lambda i, r, j: (i, j, r)),
            pl.BlockSpec((1, WINDOW, GROUP_LANES), lambda i, r, j: (i, prev(j), k_col)),
            pl.BlockSpec((1, t, GROUP_LANES), lambda i, r, j: (i, j, k_col)),
            pl.BlockSpec((1, WINDOW, GROUP_LANES), lambda i, r, j: (i, prev(j), v_col)),
            pl.BlockSpec((1, t, GROUP_LANES), lambda i, r, j: (i, j, v_col)),
            pl.BlockSpec((WINDOW, LANES), lambda i, r, j: (prev(j), 0)),
            pl.BlockSpec((t, LANES), lambda i, r, j: (j, 0)),
            pl.BlockSpec((1, 1, t, LANES), lambda i, r, j: (r, i, j, 0)),
            pl.BlockSpec((1, NSA_KV_GROUPS, 1, LANES), lambda i, r, j: (r, 0, 0, 0)),
        ],
        out_specs=pl.BlockSpec((1, t, GROUP_LANES), lambda i, r, j: (i, j, r)),
        out_shape=jax.ShapeDtypeStruct((b, s, rr * GROUP_LANES), BF16),
        compiler_params=_cparams(("arbitrary", "arbitrary", "arbitrary")),
        name="nsa_window",
    )(q, kv, kv, kv, kv, key_aug, key_aug, gates, slope_rows)


def _nsa_key_aug(s):
    pos = np.arange(s)
    aug = np.zeros((s, LANES), np.float32)
    aug[pos, pos // SLC_BLOCK] = 1.0
    aug[:, AUG_POS_LANE:AUG_POS_LANE + 3] = (pos // 64)[:, None]
    aug[:, AUG_POS_LANE + 3:AUG_POS_LANE + 6] = (pos % 64)[:, None]
    return jnp.asarray(aug, BF16)


def _nsa_slope_rows(slopes_l2):
    rr, gg = slopes_l2.shape
    hi, mid, lo = _split3(jnp.asarray(slopes_l2, F32))
    pieces = jnp.stack([hi, mid, lo], axis=-1).astype(F32)
    rows = jnp.zeros((rr, gg, 1, LANES), F32)
    rows = rows.at[:, :, 0, AUG_POS_LANE:AUG_POS_LANE + 3].set(pieces * 64.0)
    rows = rows.at[:, :, 0, AUG_POS_LANE + 3:AUG_POS_LANE + 6].set(pieces)
    return rows.astype(BF16)


def _pad_lanes(a, width=LANES):
    return jnp.pad(a, ((0, 0), (0, width - a.shape[1])))


def _fox_layer(x2d, b, s, w_in, b_f, w_out, ln_g, ln_b, router):
    d = x2d.shape[1]
    attn = w_out.shape[0]
    n_heads = attn // HEAD_DIM
    assert n_heads <= GATE_PIECE_STRIDE, "gate bias pieces of different heads would share lanes"
    scale = HEAD_DIM ** -0.5 * LOG2E
    col_scale = jnp.concatenate([jnp.full((attn,), scale, F32), jnp.ones((2 * attn,), F32)])[None, :]
    qkv, z = _project(x2d, w_in[:, :3 * attn].astype(BF16), col_scale,
                      _pad_lanes(w_in[:, 3 * attn:])[None], _pad_lanes(b_f[None, :])[None], False, True)
    gate_bias = _gate_cumsum(z.reshape(b, s, LANES), n_heads)
    o = _fox_attention(qkv.reshape(b, s, 3 * attn), gate_bias, n_heads)
    return _out_ln([o.reshape(b * s, attn)], w_out.astype(BF16), x2d,
                   ln_g.reshape(1, d), ln_b.reshape(1, d), *router)


def _selection_weights_t(nc, n_slc):
    cs = np.arange(nc)[:, None] * CMP_STRIDE
    ce = cs + CMP_BLOCK
    ss = np.arange(n_slc)[None, :] * SLC_BLOCK
    se = ss + SLC_BLOCK
    w = np.clip(np.minimum(ce, se) - np.maximum(cs, ss), 0, None) / CMP_STRIDE
    w[nc - 1, :] = 0.0
    return jnp.asarray(w.T, BF16)


def _nsa_layer(x2d, b, s, w_kv, cmp_pe, cmp_w1, cmp_b1, cmp_w2, w_q, b_g, w_out, ln_g, ln_b, router):
    d = x2d.shape[1]
    gg, rr = NSA_KV_GROUPS, NSA_HEADS_PER_GROUP
    attn = gg * rr * HEAD_DIM
    n_heads = gg * rr
    wq = w_q[:, :attn].reshape(d, gg, rr, HEAD_DIM).transpose(0, 2, 1, 3).reshape(d, attn)
    wgate = w_q[:, attn:].reshape(d, gg, rr, 3).transpose(2, 0, 3, 1).reshape(rr, d, 3 * gg)
    bgate = b_g.reshape(gg, rr, 3).transpose(1, 2, 0).reshape(rr, 1, 3 * gg)
    wgate = jnp.pad(wgate, ((0, 0), (0, 0), (0, LANES - 3 * gg)))
    bgate = jnp.pad(bgate, ((0, 0), (0, 0), (0, LANES - 3 * gg)))
    wo = w_out.reshape(gg, rr, HEAD_DIM, d).transpose(1, 0, 2, 3).reshape(attn, d)
    w_all = jnp.concatenate([wq, w_kv], axis=1).astype(BF16)
    col_scale = jnp.concatenate([jnp.full((attn,), HEAD_DIM ** -0.5 * LOG2E, F32),
                                 jnp.ones((w_kv.shape[1],), F32)])[None, :]
    qkv, gates = _project(x2d, w_all, col_scale, wgate, bgate, True, False)
    width = qkv.shape[1]
    qkv = qkv.reshape(b, s, width)
    gates = gates.reshape(rr, b, s, LANES)

    nc = s // CMP_STRIDE
    half = CMP_STRIDE * HEAD_DIM
    raw = qkv[:, :, attn:attn + 2 * GROUP_LANES]
    chunks = raw.reshape(b, nc, CMP_STRIDE, 2, gg, HEAD_DIM).transpose(0, 3, 4, 1, 2, 5)
    chunks = chunks.reshape(b, 2, gg, nc, half)
    pe_flat = cmp_pe.reshape(2, 2, half)
    hid = cmp_w1.shape[2]
    w2_placed = jnp.zeros((2, gg, hid, GROUP_LANES), F32)
    for g in range(gg):
        w2_placed = w2_placed.at[:, g, :, g * HEAD_DIM:(g + 1) * HEAD_DIM].set(cmp_w2)
    cmp_kv = _compress(chunks, pe_flat, cmp_w1.astype(BF16), cmp_b1.reshape(2, 1, hid),
                       w2_placed.astype(BF16))

    slopes_l2 = (2.0 ** (-8.0 * np.arange(1, n_heads + 1) / n_heads) * LOG2E).reshape(gg, rr).T
    slopes_rep = jnp.asarray(np.broadcast_to(
        slopes_l2[:, :, None, None], (rr, gg, 1, nc)).astype(np.float32))
    n_slc = s // SLC_BLOCK
    assert n_slc <= AUG_POS_LANE, "selection flags must fit below the position lanes"
    o_c, sel_q = _nsa_compressed(qkv, cmp_kv, gates, slopes_rep, _selection_weights_t(nc, n_slc))
    key_aug = _nsa_key_aug(s)
    slope_rows = _nsa_slope_rows(slopes_l2)
    base = attn // GROUP_LANES
    o_s = _nsa_selected(qkv, qkv, base + 2, base + 3, key_aug, gates, slope_rows, sel_q)
    o_w = _nsa_window(qkv, qkv, base + 4, base + 5, key_aug, gates, slope_rows)
    n = b * s
    return _out_ln([o_c.reshape(n, attn), o_s.reshape(n, attn), o_w.reshape(n, attn)],
                   wo.astype(BF16), x2d, ln_g.reshape(1, d), ln_b.reshape(1, d), *router)


def kernel(x, p, fox_w_in, fox_b_f, fox_w_out, nsa_w_kv, cmp_pe, cmp_w1, cmp_b1, cmp_w2, nsa_w_q, nsa_b_g, nsa_w_out, ln_g, ln_b, moe_w_router, moe_b_router, moe_w_gate, moe_w_up, moe_w_down, shared_w_gate, shared_w_up, shared_w_down, ple_w_proj, ple_w_gate, ple_b_gate):
    b, s, d = x.shape
    n = b * s
    depth = p.shape[0]
    assert depth == DEPTH, "DEEPNORM_ALPHA is derived from DEPTH"
    n_a = depth // 2
    h = x.reshape(n, d)
    for i in range(depth):
        router = (moe_w_router[i].T, moe_b_router[i].reshape(-1, 1))
        if i < n_a:
            h1, h1b, *routing = _fox_layer(h, b, s, fox_w_in[i], fox_b_f[i], fox_w_out[i],
                                           ln_g[i, 0], ln_b[i, 0], router)
        else:
            j = i - n_a
            h1, h1b, *routing = _nsa_layer(h, b, s, nsa_w_kv, cmp_pe, cmp_w1, cmp_b1, cmp_w2,
                                           nsa_w_q[j], nsa_b_g[j], nsa_w_out[j], ln_g[i, 0], ln_b[i, 0], router)
        h = _moe_layer(h1, h1b, routing, p.reshape(depth, n, -1), i,
                       moe_w_gate, moe_w_up, moe_w_down,
                       shared_w_gate[i], shared_w_up[i], shared_w_down[i],
                       ple_w_proj[i], ple_w_gate[i], ple_b_gate[i],
                       ln_g[i, 1], ln_b[i, 1], ln_g[i, 2], ln_b[i, 2])
    return h.reshape(b, s, d)
```
